```python
import math
import functools
import jax
import jax.numpy as jnp
from jax import lax
import numpy as np

D_MODEL = 1024
BATCH = 8
SEQ = 4096
DEPTH = 2

GRID_W = 64
CTX_LEN = 256
NORM_EPS = 1e-6

S5_WIDTH = D_MODEL // 4
S5_GROUP = 16
S5_GROUPS = S5_WIDTH // S5_GROUP
S5_STATE = 64
S5_MIN_STEP = 1e-3
S5_MAX_STEP = 1e-1
RW_WIDTH = D_MODEL // 2
RW_HEAD = 64
RW_HEADS = RW_WIDTH // RW_HEAD
RW_DECAY_LORA = 64
RW_ICLR_LORA = 64
RW_GATE_LORA = 128
RW_GN_EPS = 64e-5
HY_WIDTH = D_MODEL // 4
HY_ORDER = 2
HY_POS_DIM = 33
HY_FILTER_HIDDEN = 64
HY_DECAY_TARGET = 1e-2
HY_DECAY_PCT_SHORT = 0.3
HY_DECAY_PCT_LONG = 1.5
SHORT_CONV = 3
N_BRANCH = 3
FFN_HIDDEN = 2816
N_EXPERTS = 8
TOP_K = 2
EXPERT_HIDDEN = 3584
MOE_BLOCK = 256
IN_S5 = S5_WIDTH
IN_RW = 3 * RW_WIDTH
IN_LORA = 2 * RW_DECAY_LORA + 2 * RW_ICLR_LORA + RW_GATE_LORA
IN_HY = (HY_ORDER + 1) * HY_WIDTH
IN_GATE = N_BRANCH * D_MODEL
IN_COLS = IN_S5 + IN_RW + IN_LORA + IN_HY + IN_GATE
N_DENSE = (DEPTH + 1) // 2
N_MOE = DEPTH // 2

kernel_name = 'hybrid_s5_rwkv7_hyena_moe_flow_block'


def rms_norm(x, gain):
    xf = x.astype(jnp.float32)
    y = xf * lax.rsqrt(jnp.mean(xf * xf, axis=-1, keepdims=True) + NORM_EPS)
    return (y * gain.astype(jnp.float32)).astype(x.dtype)


def short_conv(x, w, b=None):
    half = SHORT_CONV // 2
    n = x.shape[1]
    xp = jnp.pad(x, ((0, 0), (half, half), (0, 0)))
    y = sum(xp[:, j:j + n] * w[j] for j in range(SHORT_CONV))
    return y if b is None else y + b


def latent_short_conv(x, w, b=None):
    bsz, n_tok, ch = x.shape
    rows = n_tok // GRID_W
    y = short_conv(x.reshape(bsz * rows, GRID_W, ch), w, b)
    return y.reshape(bsz, n_tok, ch)


def s5_discretise(lam_re, lam_im, log_step, b_re, b_im):
    lr = lam_re.astype(jnp.float32)
    li = lam_im.astype(jnp.float32)
    step = jnp.exp(log_step.astype(jnp.float32))[:, None]
    mag = jnp.exp(lr * step)
    ang = li * step
    ab_re, ab_im = mag * jnp.cos(ang), mag * jnp.sin(ang)
    den = lr * lr + li * li
    nr, ni = ab_re - 1.0, ab_im
    q_re = (nr * lr + ni * li) / den
    q_im = (ni * lr - nr * li) / den
    br, bi = b_re.astype(jnp.float32), b_im.astype(jnp.float32)
    bb_re = q_re[..., None] * br - q_im[..., None] * bi
    bb_im = q_re[..., None] * bi + q_im[..., None] * br
    return ab_re, ab_im, bb_re, bb_im


def _complex_affine_combine(e1, e2):
    a1r, a1i, b1r, b1i = e1
    a2r, a2i, b2r, b2i = e2
    return (a2r * a1r - a2i * a1i,
            a2r * a1i + a2i * a1r,
            a2r * b1r - a2i * b1i + b2r,
            a2r * b1i + a2i * b1r + b2i)


def s5_scan(u, ab_re, ab_im, bb_re, bb_im, h0_re, h0_im):
    n = u.shape[1]
    bu_re = jnp.einsum('blgi,gpi->blgp', u, bb_re)
    bu_im = jnp.einsum('blgi,gpi->blgp', u, bb_im)
    bu_re = bu_re.at[:, 0].add(ab_re * h0_re - ab_im * h0_im)
    bu_im = bu_im.at[:, 0].add(ab_re * h0_im + ab_im * h0_re)
    a_re = jnp.broadcast_to(ab_re, (1, n) + ab_re.shape)
    a_im = jnp.broadcast_to(ab_im, (1, n) + ab_im.shape)
    _, _, h_re, h_im = lax.associative_scan(_complex_affine_combine, (a_re, a_im, bu_re, bu_im), axis=1)
    return h_re, h_im


def s5_branch(u, p, init, with_output):
    bsz, n, _ = u.shape
    ug = u.astype(jnp.float32).reshape(bsz, n, S5_GROUPS, S5_GROUP)
    finals, outs = [], []
    for d in range(2):
        ab_re, ab_im, bb_re, bb_im = s5_discretise(p['s5_lam_re'][d], p['s5_lam_im'][d], p['s5_log_step'][d],
                                                   p['s5_b_re'][d], p['s5_b_im'][d])
        u_d = ug if d == 0 else ug[:, ::-1]
        h_re, h_im = s5_scan(u_d, ab_re, ab_im, bb_re, bb_im, init[d][0], init[d][1])
        finals.append((h_re[:, -1], h_im[:, -1]))
        if with_output:
            y_d = (jnp.einsum('blgp,gip->blgi', h_re, p['s5_c_re'][d].astype(jnp.float32))
                   - jnp.einsum('blgp,gip->blgi', h_im, p['s5_c_im'][d].astype(jnp.float32)))
            outs.append(y_d if d == 0 else y_d[:, ::-1])
    if not with_output:
        return None, finals
    y = (outs[0] + outs[1]).reshape(bsz, n, S5_WIDTH) + p['s5_d'] * ug.reshape(bsz, n, S5_WIDTH)
    y = jax.nn.gelu(y, approximate=False)
    return y * jax.nn.sigmoid(y @ p['s5_glu_w']), finals


def rwkv_scan(r, w, k, v, a, b, s0, with_output):
    def step(s, inp):
        r_t, w_t, k_t, v_t, a_t, b_t = inp
        sa = jnp.einsum('bhvk,bhk->bhv', s, a_t)
        s = s * w_t[:, :, None, :] + sa[..., None] * b_t[:, :, None, :] + v_t[..., None] * k_t[:, :, None, :]
        y_t = jnp.einsum('bhvk,bhk->bhv', s, r_t) if with_output else None
        return s, y_t
    xs = tuple(jnp.swapaxes(t, 0, 1) for t in (r, w, k, v, a, b))
    s_fin, ys = lax.scan(step, s0, xs)
    return (jnp.swapaxes(ys, 0, 1) if with_output else None), s_fin


def rwkv_branch(rkv, lora, p, init, with_output):
    bsz, n, _ = rkv.shape

    def heads(t):
        return t.reshape(bsz, n, RW_HEADS, RW_HEAD)

    r, k, v = jnp.split(rkv.astype(jnp.float32), 3, axis=-1)
    lora = lora.astype(jnp.float32)
    c1 = RW_DECAY_LORA
    c2 = 2 * RW_DECAY_LORA
    c3 = c2 + RW_ICLR_LORA
    c4 = c3 + RW_ICLR_LORA
    w_lo = (lora[..., :c1], lora[..., c1:c2])
    a_lo = (lora[..., c2:c3], lora[..., c3:c4])
    kk = heads(k * p['rw_kk'])
    kk = kk * lax.rsqrt(jnp.maximum(jnp.sum(kk * kk, axis=-1, keepdims=True), 1e-24))
    rh, vh = heads(r), heads(v)
    finals, wkv, bonus = [], 0.0, 0.0
    for d in range(2):
        w_log = -jax.nn.softplus(-(p['rw_w0'][d] + jnp.tanh(w_lo[d]) @ p['rw_w2'][d])) - 0.5
        decay = jnp.exp(-jnp.exp(w_log))
        a = jax.nn.sigmoid(p['rw_a0'][d] + a_lo[d] @ p['rw_a2'][d])
        kd = heads(k * (1.0 + (a - 1.0) * p['rw_ka']))
        seq = (rh, heads(decay), kd, vh, -kk, kk * heads(a))
        if d == 1:
            seq = tuple(t[:, ::-1] for t in seq)
        ys, s_fin = rwkv_scan(*seq, init[d], with_output)
        finals.append(s_fin)
        if with_output:
            wkv = wkv + (ys if d == 0 else ys[:, ::-1])
            bonus = bonus + jnp.sum(heads(r * p['rw_rk']) * kd, axis=-1, keepdims=True) * vh
    if not with_output:
        return None, finals
    mu = jnp.mean(wkv, axis=-1, keepdims=True)
    var = jnp.mean(jnp.square(wkv - mu), axis=-1, keepdims=True)
    o = ((wkv - mu) * lax.rsqrt(var + RW_GN_EPS)).reshape(bsz, n, RW_WIDTH) * p['rw_ln_w'] + p['rw_ln_b']
    o = o + bonus.reshape(bsz, n, RW_WIDTH)
    g = jax.nn.sigmoid(lora[..., c4:]) @ p['rw_g2']
    return o * g, finals


def hyena_filter_spectra(n_tok, p):
    bands = (HY_POS_DIM - 1) // 2
    t = jnp.linspace(0.0, 1.0, n_tok, dtype=jnp.float32)[:, None]
    w = (2.0 * math.pi / n_tok) * jnp.arange(n_tok, dtype=jnp.float32)[:, None]
    f = jnp.linspace(1e-4, bands - 1, bands, dtype=jnp.float32)[None, :]
    feats = jnp.concatenate([t, jnp.cos(f * w), -jnp.sin(f * w)], axis=-1)
    h = jnp.sin(p['hy_f_freq1'] * (feats @ p['hy_f_w1'] + p['hy_f_b1']))
    h = jnp.sin(p['hy_f_freq2'] * (h @ p['hy_f_w2'] + p['hy_f_b2']))
    h = (h @ p['hy_f_w3']).astype(jnp.float32).reshape(n_tok, HY_ORDER, 2, HY_WIDTH)
    rates = jnp.abs(jnp.linspace(math.log(HY_DECAY_TARGET) / HY_DECAY_PCT_SHORT,
                                 math.log(HY_DECAY_TARGET) / HY_DECAY_PCT_LONG, HY_WIDTH, dtype=jnp.float32))
    h = h * jnp.exp(-t * rates)[:, None, None, :]
    h_fwd, h_bwd = h[:, :, 0], h[:, :, 1]
    filt = jnp.concatenate([h_fwd, jnp.zeros_like(h_fwd[:1]), h_bwd[:0:-1]], axis=0)
    return jnp.fft.rfft(filt, axis=0)


def long_conv(z, k_spec, bias):
    n = z.shape[1]
    z_spec = jnp.fft.rfft(z, n=2 * n, axis=1)
    y = jnp.fft.irfft(z_spec * k_spec[None], n=2 * n, axis=1)[:, :n]
    return y + z * bias


def hyena_branch(streams, p):
    n = streams.shape[1]
    v, x1, x2 = jnp.split(streams.astype(jnp.float32), 3, axis=-1)
    k_spec = hyena_filter_spectra(n, p)
    z = v
    for o, gate in enumerate((x1, x2)):
        z = gate * long_conv(z, k_spec[:, o], p['hy_bias'][o])
    return z


def merge_branches(y_s5, y_rw, y_hy, gates, p):
    g_s5, g_rw, g_hy = jnp.split(jax.nn.sigmoid(gates.astype(jnp.float32)), N_BRANCH, axis=-1)
    m = g_s5 * (y_s5 @ p['br_s5']) + g_rw * (y_rw @ p['br_rw']) + g_hy * (y_hy @ p['br_hy'])
    return m @ p['out_w']


def token_mixer(h_lat, h_ctx, p, with_ctx_out):
    bsz = h_lat.shape[0]
    cuts = [IN_S5, IN_S5 + IN_RW, IN_S5 + IN_RW + IN_LORA, IN_S5 + IN_RW + IN_LORA + IN_HY]
    ctx_cols = IN_COLS if with_ctx_out else cuts[2]
    u_l, rkv_l, lora_l, hy_l, gate_l = jnp.split(h_lat @ p['in_w'], cuts, axis=-1)
    ctx_parts = jnp.split(h_ctx @ p['in_w'][:, :ctx_cols], cuts if with_ctx_out else cuts[:2], axis=-1)
    zs = jnp.zeros((bsz, S5_GROUPS, S5_STATE), jnp.float32)
    zr = jnp.zeros((bsz, RW_HEADS, RW_HEAD, RW_HEAD), jnp.float32)
    s5_c, s5_state = s5_branch(ctx_parts[0], p, ((zs, zs), (zs, zs)), with_ctx_out)
    rw_c, rw_state = rwkv_branch(short_conv(ctx_parts[1], p['rw_conv_w']), ctx_parts[2], p, (zr, zr), with_ctx_out)
    s5_l, _ = s5_branch(u_l, p, s5_state, True)
    rw_l, _ = rwkv_branch(latent_short_conv(rkv_l, p['rw_conv_w']), lora_l, p, rw_state, True)
    hy_lat = hyena_branch(latent_short_conv(hy_l, p['hy_conv_w'], p['hy_conv_b']), p)
    y_lat = merge_branches(s5_l, rw_l, hy_lat, gate_l, p).astype(h_lat.dtype)
    if not with_ctx_out:
        return y_lat, None
    hy_ctx = hyena_branch(short_conv(ctx_parts[3], p['hy_conv_w'], p['hy_conv_b']), p)
    y_ctx = merge_branches(s5_c, rw_c, hy_ctx, ctx_parts[4], p).astype(h_ctx.dtype)
    return y_lat, y_ctx


def swiglu(h, wg, wu, wd):
    return (jax.nn.silu(h @ wg) * (h @ wu)) @ wd


def moe_swiglu(h, router_w, wg, wu, wd):
    d_model = h.shape[-1]
    tok = h.reshape(-1, d_model)
    n = tok.shape[0]
    n_assign = n * TOP_K
    logits = (tok @ router_w).astype(jnp.float32)
    top_logit, top_e = lax.top_k(logits, TOP_K)
    gate = jax.nn.softmax(top_logit, axis=-1).reshape(-1)
    flat_e = top_e.reshape(-1)
    order = jnp.argsort(flat_e)
    sorted_e = flat_e[order]
    sizes = jnp.zeros((N_EXPERTS,), jnp.int32).at[flat_e].add(1)
    padded = (sizes + MOE_BLOCK - 1) // MOE_BLOCK * MOE_BLOCK
    pad_end = jnp.cumsum(padded)
    pad_start = pad_end - padded
    grp_start = jnp.cumsum(sizes) - sizes
    slot = pad_start[sorted_e] + jnp.arange(n_assign, dtype=jnp.int32) - grp_start[sorted_e]
    n_blocks = -(-n_assign // MOE_BLOCK) + N_EXPERTS
    cap = n_blocks * MOE_BLOCK
    slot_tok = jnp.full((cap,), n, jnp.int32).at[slot].set((order // TOP_K).astype(jnp.int32))
    slot_gate = jnp.zeros((cap,), jnp.float32).at[slot].set(gate[order])
    block_start = jnp.arange(n_blocks, dtype=jnp.int32) * MOE_BLOCK
    block_e = jnp.minimum(jnp.sum(block_start[:, None] >= pad_end[None, :], axis=1), N_EXPERTS - 1)
    tok_pad = jnp.concatenate([tok, jnp.zeros((1, d_model), tok.dtype)], axis=0)
    xb = tok_pad[slot_tok].reshape(n_blocks, MOE_BLOCK, d_model)

    def expert_block(args):
        x_blk, e = args
        return (jax.nn.silu(x_blk @ wg[e]) * (x_blk @ wu[e])) @ wd[e]

    yb = lax.map(expert_block, (xb, block_e)).reshape(cap, d_model)
    yb = yb * slot_gate[:, None].astype(yb.dtype)
    out = jax.ops.segment_sum(yb, slot_tok, num_segments=n + 1)[:n]
    return out.reshape(h.shape)


def setup_inputs(seed: int = 0) -> dict:
    key = jax.random.key(seed)
    counter = [0]

    def nk():
        counter[0] += 1
        return jax.random.fold_in(key, counter[0])

    def nrm(shape, scale=1.0):
        return scale * jax.random.normal(nk(), shape, jnp.float32)

    L, D = DEPTH, D_MODEL
    G, P, GC = S5_GROUPS, S5_STATE, S5_GROUP
    W, R, RG = RW_WIDTH, RW_DECAY_LORA, RW_GATE_LORA
    HW, HH = HY_WIDTH, HY_FILTER_HIDDEN
    inp = {}
    inp['x'] = nrm((BATCH, SEQ, D))
    inp['c'] = nrm((BATCH, D))
    inp['ctx'] = nrm((BATCH, CTX_LEN, D))
    inp['c_ctx'] = nrm((D,))
    inp['mod_w'] = nrm((L, D, 6 * D), 0.5 * D ** -0.5)
    inp['mod_b'] = nrm((L, 6 * D), 0.02)
    inp['norm_g'] = 1.0 + nrm((L, 4, D), 0.02)
    inp['in_w'] = nrm((L, D, IN_COLS), D ** -0.5)
    inp['s5_lam_re'] = -0.5 + nrm((L, 2, G, P), 0.01)
    inp['s5_lam_im'] = math.pi * jnp.arange(P, dtype=jnp.float32) + nrm((L, 2, G, P), 0.01)
    inp['s5_log_step'] = jax.random.uniform(nk(), (L, 2, G), jnp.float32, math.log(S5_MIN_STEP), math.log(S5_MAX_STEP))
    inp['s5_b_re'] = nrm((L, 2, G, P, GC), (2 * GC) ** -0.5)
    inp['s5_b_im'] = nrm((L, 2, G, P, GC), (2 * GC) ** -0.5)
    inp['s5_c_re'] = nrm((L, 2, G, GC, P), (2 * P) ** -0.5)
    inp['s5_c_im'] = nrm((L, 2, G, GC, P), (2 * P) ** -0.5)
    inp['s5_d'] = nrm((L, S5_WIDTH), 0.5)
    inp['s5_glu_w'] = nrm((L, S5_WIDTH, S5_WIDTH), S5_WIDTH ** -0.5)
    inp['rw_conv_w'] = nrm((L, SHORT_CONV, IN_RW), 0.3).at[:, SHORT_CONV // 2].add(1.0)
    inp['rw_w0'] = -6.5 + 5.0 * jnp.linspace(0.0, 1.0, W, dtype=jnp.float32) ** 1.5 + nrm((L, 2, W), 0.1)
    inp['rw_w2'] = nrm((L, 2, R, W), 0.5 * R ** -0.5)
    inp['rw_a0'] = nrm((L, 2, W), 0.1)
    inp['rw_a2'] = nrm((L, 2, RW_ICLR_LORA, W), 0.5 * RW_ICLR_LORA ** -0.5)
    inp['rw_g2'] = nrm((L, RG, W), RG ** -0.5)
    inp['rw_kk'] = 0.85 + nrm((L, W), 0.02)
    inp['rw_ka'] = 1.0 + nrm((L, W), 0.02)
    inp['rw_rk'] = nrm((L, W), 0.1)
    inp['rw_ln_w'] = 1.0 + nrm((L, W), 0.02)
    inp['rw_ln_b'] = nrm((L, W), 0.02)
    inp['hy_conv_w'] = nrm((L, SHORT_CONV, IN_HY), 0.3).at[:, SHORT_CONV // 2].add(1.0)
    inp['hy_conv_b'] = nrm((L, IN_HY), 0.02)
    inp['hy_f_w1'] = nrm((L, HY_POS_DIM, HH), HY_POS_DIM ** -0.5)
    inp['hy_f_b1'] = nrm((L, HH), 0.1)
    inp['hy_f_freq1'] = 1.0 + nrm((L, HH), 0.02)
    inp['hy_f_w2'] = nrm((L, HH, HH), HH ** -0.5)
    inp['hy_f_b2'] = nrm((L, HH), 0.1)
    inp['hy_f_freq2'] = 1.0 + nrm((L, HH), 0.02)
    inp['hy_f_w3'] = nrm((L, HH, HY_ORDER * 2 * HW), 0.1 * HH ** -0.5)
    inp['hy_bias'] = nrm((L, HY_ORDER, HW), 0.5)
    inp['br_s5'] = nrm((L, S5_WIDTH, D), S5_WIDTH ** -0.5)
    inp['br_rw'] = nrm((L, RW_WIDTH, D), RW_WIDTH ** -0.5)
    inp['br_hy'] = nrm((L, HY_WIDTH, D), HY_WIDTH ** -0.5)
    inp['out_w'] = nrm((L, D, D), D ** -0.5)
    inp['ffn_wg'] = nrm((N_DENSE, D, FFN_HIDDEN), D ** -0.5)
    inp['ffn_wu'] = nrm((N_DENSE, D, FFN_HIDDEN), D ** -0.5)
    inp['ffn_wd'] = nrm((N_DENSE, FFN_HIDDEN, D), FFN_HIDDEN ** -0.5)
    inp['moe_router'] = nrm((N_MOE, D, N_EXPERTS), D ** -0.5)
    inp['moe_wg'] = nrm((N_MOE, N_EXPERTS, D, EXPERT_HIDDEN), D ** -0.5)
    inp['moe_wu'] = nrm((N_MOE, N_EXPERTS, D, EXPERT_HIDDEN), D ** -0.5)
    inp['moe_wd'] = nrm((N_MOE, N_EXPERTS, EXPERT_HIDDEN, D), EXPERT_HIDDEN ** -0.5)
    return inp


def reference(x, c, ctx, c_ctx, mod_w, mod_b, norm_g, in_w,
              s5_lam_re, s5_lam_im, s5_log_step, s5_b_re, s5_b_im, s5_c_re, s5_c_im, s5_d, s5_glu_w,
              rw_conv_w, rw_w0, rw_w2, rw_a0, rw_a2, rw_g2, rw_kk, rw_ka, rw_rk, rw_ln_w, rw_ln_b,
              hy_conv_w, hy_conv_b, hy_f_w1, hy_f_b1, hy_f_freq1, hy_f_w2, hy_f_b2, hy_f_freq2, hy_f_w3, hy_bias,
              br_s5, br_rw, br_hy, out_w, ffn_wg, ffn_wu, ffn_wd, moe_router, moe_wg, moe_wu, moe_wd):
    silu_c = jax.nn.silu(c)
    silu_cc = jax.nn.silu(c_ctx)
    for i in range(DEPTH):
        last = i == DEPTH - 1
        p = {
            'in_w': in_w[i],
            's5_lam_re': s5_lam_re[i], 's5_lam_im': s5_lam_im[i], 's5_log_step': s5_log_step[i],
            's5_b_re': s5_b_re[i], 's5_b_im': s5_b_im[i], 's5_c_re': s5_c_re[i], 's5_c_im': s5_c_im[i],
            's5_d': s5_d[i], 's5_glu_w': s5_glu_w[i],
            'rw_conv_w': rw_conv_w[i], 'rw_w0': rw_w0[i], 'rw_w2': rw_w2[i], 'rw_a0': rw_a0[i], 'rw_a2': rw_a2[i],
            'rw_g2': rw_g2[i], 'rw_kk': rw_kk[i], 'rw_ka': rw_ka[i], 'rw_rk': rw_rk[i],
            'rw_ln_w': rw_ln_w[i], 'rw_ln_b': rw_ln_b[i],
            'hy_conv_w': hy_conv_w[i], 'hy_conv_b': hy_conv_b[i], 'hy_f_w1': hy_f_w1[i], 'hy_f_b1': hy_f_b1[i],
            'hy_f_freq1': hy_f_freq1[i], 'hy_f_w2': hy_f_w2[i], 'hy_f_b2': hy_f_b2[i], 'hy_f_freq2': hy_f_freq2[i],
            'hy_f_w3': hy_f_w3[i], 'hy_bias': hy_bias[i],
            'br_s5': br_s5[i], 'br_rw': br_rw[i], 'br_hy': br_hy[i], 'out_w': out_w[i],
        }
        if i % 2 == 0:
            ffn = functools.partial(swiglu, wg=ffn_wg[i // 2], wu=ffn_wu[i // 2], wd=ffn_wd[i // 2])
        else:
            ffn = functools.partial(moe_swiglu, router_w=moe_router[i // 2], wg=moe_wg[i // 2],
                                    wu=moe_wu[i // 2], wd=moe_wd[i // 2])
        ml = jnp.split((silu_c @ mod_w[i] + mod_b[i])[:, None, :], 6, axis=-1)
        mc = jnp.split(silu_cc @ mod_w[i] + mod_b[i], 6, axis=-1)
        g_pre_m, g_post_m, g_pre_f, g_post_f = norm_g[i]
        h_l = rms_norm(x, g_pre_m) * (1 + ml[1]) + ml[0]
        h_c = rms_norm(ctx, g_pre_m) * (1 + mc[1]) + mc[0]
        y_l, y_c = token_mixer(h_l, h_c, p, not last)
        x = x + ml[2] * rms_norm(y_l, g_post_m)
        x = x + ml[5] * rms_norm(ffn(rms_norm(x, g_pre_f) * (1 + ml[4]) + ml[3]), g_post_f)
        if not last:
            ctx = ctx + mc[2] * rms_norm(y_c, g_post_m)
            ctx = ctx + mc[5] * rms_norm(ffn(rms_norm(ctx, g_pre_f) * (1 + mc[4]) + mc[3]), g_post_f)
    return x
```

```python
import math
import functools
import jax
import jax.numpy as jnp
from jax import lax
from jax.experimental import pallas as pl
from jax.experimental.pallas import tpu as pltpu

D_MODEL = 1024
BATCH = 8
SEQ = 4096
DEPTH = 2
GRID_W = 64
CTX_LEN = 256
NORM_EPS = 1e-6
S5_WIDTH = D_MODEL // 4
S5_GROUP = 16
S5_GROUPS = S5_WIDTH // S5_GROUP
S5_STATE = 64
RW_WIDTH = D_MODEL // 2
RW_HEAD = 64
RW_HEADS = RW_WIDTH // RW_HEAD
RW_DECAY_LORA = 64
RW_ICLR_LORA = 64
RW_GATE_LORA = 128
RW_GN_EPS = 64e-5
HY_WIDTH = D_MODEL // 4
HY_ORDER = 2
HY_POS_DIM = 33
HY_FILTER_HIDDEN = 64
HY_DECAY_TARGET = 1e-2
HY_DECAY_PCT_SHORT = 0.3
HY_DECAY_PCT_LONG = 1.5
SHORT_CONV = 3
N_BRANCH = 3
FFN_HIDDEN = 2816
N_EXPERTS = 8
TOP_K = 2
EXPERT_HIDDEN = 3584
MOE_BLOCK = 256
IN_S5 = S5_WIDTH
IN_RW = 3 * RW_WIDTH
IN_LORA = 2 * RW_DECAY_LORA + 2 * RW_ICLR_LORA + RW_GATE_LORA
IN_HY = (HY_ORDER + 1) * HY_WIDTH
IN_GATE = N_BRANCH * D_MODEL
IN_COLS = IN_S5 + IN_RW + IN_LORA + IN_HY + IN_GATE

VMEM_LIMIT = 56 * 1024 * 1024


def _mm_kernel(a_ref, b_ref, o_ref):
    o_ref[...] = jnp.dot(a_ref[...].astype(jnp.bfloat16), b_ref[...],
                         preferred_element_type=jnp.float32)


def _pick(n, cands):
    for c in cands:
        if n % c == 0:
            return c
    return n


def pmatmul(a, b):
    m, k = a.shape
    n = b.shape[1]
    b = b.astype(jnp.bfloat16)
    tm = _pick(m, (512, 256, 128, 8))
    tn = _pick(n, (1024, 768, 512, 384, 256, 128))
    return pl.pallas_call(
        _mm_kernel,
        out_shape=jax.ShapeDtypeStruct((m, n), jnp.float32),
        grid=(n // tn, m // tm),
        in_specs=[pl.BlockSpec((tm, k), lambda j, i: (i, 0)),
                  pl.BlockSpec((k, tn), lambda j, i: (0, j))],
        out_specs=pl.BlockSpec((tm, tn), lambda j, i: (i, j)),
        compiler_params=pltpu.CompilerParams(
            dimension_semantics=("parallel", "parallel"),
            vmem_limit_bytes=VMEM_LIMIT),
        name="mm",
    )(a, b)


def mm(a, b):
    lead = a.shape[:-1]
    return pmatmul(a.reshape(-1, a.shape[-1]), b).reshape(lead + (b.shape[1],))


def rms_norm(x, gain):
    y = x * lax.rsqrt(jnp.mean(x * x, axis=-1, keepdims=True) + NORM_EPS)
    return y * gain


def short_conv(x, w, b=None):
    half = SHORT_CONV // 2
    n = x.shape[1]
    xp = jnp.pad(x, ((0, 0), (half, half), (0, 0)))
    y = sum(xp[:, j:j + n] * w[j] for j in range(SHORT_CONV))
    return y if b is None else y + b


def latent_short_conv(x, w, b=None):
    bsz, n_tok, ch = x.shape
    rows = n_tok // GRID_W
    y = short_conv(x.reshape(bsz * rows, GRID_W, ch), w, b)
    return y.reshape(bsz, n_tok, ch)


def s5_discretise(lam_re, lam_im, log_step, b_re, b_im):
    lr, li = lam_re, lam_im
    step = jnp.exp(log_step)[:, None]
    mag = jnp.exp(lr * step)
    ang = li * step
    ab_re, ab_im = mag * jnp.cos(ang), mag * jnp.sin(ang)
    den = lr * lr + li * li
    nr, ni = ab_re - 1.0, ab_im
    q_re = (nr * lr + ni * li) / den
    q_im = (ni * lr - nr * li) / den
    bb_re = q_re[..., None] * b_re - q_im[..., None] * b_im
    bb_im = q_re[..., None] * b_im + q_im[..., None] * b_re
    return ab_re, ab_im, bb_re, bb_im


def _complex_affine_combine(e1, e2):
    a1r, a1i, b1r, b1i = e1
    a2r, a2i, b2r, b2i = e2
    return (a2r * a1r - a2i * a1i,
            a2r * a1i + a2i * a1r,
            a2r * b1r - a2i * b1i + b2r,
            a2r * b1i + a2i * b1r + b2i)


def s5_scan(u, ab_re, ab_im, bb_re, bb_im, h0_re, h0_im):
    n = u.shape[1]
    hp = lax.Precision.HIGHEST
    bu_re = jnp.einsum('blgi,gpi->blgp', u, bb_re, precision=hp)
    bu_im = jnp.einsum('blgi,gpi->blgp', u, bb_im, precision=hp)
    bu_re = bu_re.at[:, 0].add(ab_re * h0_re - ab_im * h0_im)
    bu_im = bu_im.at[:, 0].add(ab_re * h0_im + ab_im * h0_re)
    a_re = jnp.broadcast_to(ab_re, (1, n) + ab_re.shape)
    a_im = jnp.broadcast_to(ab_im, (1, n) + ab_im.shape)
    _, _, h_re, h_im = lax.associative_scan(_complex_affine_combine, (a_re, a_im, bu_re, bu_im), axis=1)
    return h_re, h_im


def s5_branch(u, p, init, with_output):
    bsz, n, _ = u.shape
    hp = lax.Precision.HIGHEST
    ug = u.reshape(bsz, n, S5_GROUPS, S5_GROUP)
    finals, outs = [], []
    for d in range(2):
        ab_re, ab_im, bb_re, bb_im = s5_discretise(p['s5_lam_re'][d], p['s5_lam_im'][d], p['s5_log_step'][d],
                                                   p['s5_b_re'][d], p['s5_b_im'][d])
        u_d = ug if d == 0 else ug[:, ::-1]
        h_re, h_im = s5_scan(u_d, ab_re, ab_im, bb_re, bb_im, init[d][0], init[d][1])
        finals.append((h_re[:, -1], h_im[:, -1]))
        if with_output:
            y_d = (jnp.einsum('blgp,gip->blgi', h_re, p['s5_c_re'][d], precision=hp)
                   - jnp.einsum('blgp,gip->blgi', h_im, p['s5_c_im'][d], precision=hp))
            outs.append(y_d if d == 0 else y_d[:, ::-1])
    if not with_output:
        return None, finals
    y = (outs[0] + outs[1]).reshape(bsz, n, S5_WIDTH) + p['s5_d'] * ug.reshape(bsz, n, S5_WIDTH)
    y = jax.nn.gelu(y, approximate=False)
    return y * jax.nn.sigmoid(mm(y, p['s5_glu_w'])), finals


def rwkv_scan(r, w, k, v, a, b, s0, with_output):
    def step(s, inp):
        r_t, w_t, k_t, v_t, a_t, b_t = inp
        sa = jnp.sum(s * a_t[:, :, None, :], axis=-1)
        s = s * w_t[:, :, None, :] + sa[..., None] * b_t[:, :, None, :] + v_t[..., None] * k_t[:, :, None, :]
        y_t = jnp.sum(s * r_t[:, :, None, :], axis=-1) if with_output else None
        return s, y_t
    xs = tuple(jnp.swapaxes(t, 0, 1) for t in (r, w, k, v, a, b))
    s_fin, ys = lax.scan(step, s0, xs)
    return (jnp.swapaxes(ys, 0, 1) if with_output else None), s_fin


def rwkv_branch(rkv, lora, p, init, with_output):
    bsz, n, _ = rkv.shape
    hp = lax.Precision.HIGHEST

    def heads(t):
        return t.reshape(bsz, n, RW_HEADS, RW_HEAD)

    r, k, v = jnp.split(rkv, 3, axis=-1)
    c1 = RW_DECAY_LORA
    c2 = 2 * RW_DECAY_LORA
    c3 = c2 + RW_ICLR_LORA
    c4 = c3 + RW_ICLR_LORA
    w_lo = (lora[..., :c1], lora[..., c1:c2])
    a_lo = (lora[..., c2:c3], lora[..., c3:c4])
    kk = heads(k * p['rw_kk'])
    kk = kk * lax.rsqrt(jnp.maximum(jnp.sum(kk * kk, axis=-1, keepdims=True), 1e-24))
    rh, vh = heads(r), heads(v)
    finals, wkv, bonus = [], 0.0, 0.0
    for d in range(2):
        w_log = -jax.nn.softplus(-(p['rw_w0'][d] + mm(jnp.tanh(w_lo[d]), p['rw_w2'][d]))) - 0.5
        decay = jnp.exp(-jnp.exp(w_log))
        a = jax.nn.sigmoid(p['rw_a0'][d] + mm(a_lo[d], p['rw_a2'][d]))
        kd = heads(k * (1.0 + (a - 1.0) * p['rw_ka']))
        seq = (rh, heads(decay), kd, vh, -kk, kk * heads(a))
        if d == 1:
            seq = tuple(t[:, ::-1] for t in seq)
        ys, s_fin = rwkv_scan(*seq, init[d], with_output)
        finals.append(s_fin)
        if with_output:
            wkv = wkv + (ys if d == 0 else ys[:, ::-1])
            bonus = bonus + jnp.sum(heads(r * p['rw_rk']) * kd, axis=-1, keepdims=True) * vh
    if not with_output:
        return None, finals
    mu = jnp.mean(wkv, axis=-1, keepdims=True)
    var = jnp.mean(jnp.square(wkv - mu), axis=-1, keepdims=True)
    o = ((wkv - mu) * lax.rsqrt(var + RW_GN_EPS)).reshape(bsz, n, RW_WIDTH) * p['rw_ln_w'] + p['rw_ln_b']
    o = o + bonus.reshape(bsz, n, RW_WIDTH)
    g = mm(jax.nn.sigmoid(lora[..., c4:]), p['rw_g2'])
    return o * g, finals


def hyena_filter_spectra(n_tok, p):
    hp = lax.Precision.HIGHEST
    bands = (HY_POS_DIM - 1) // 2
    t = jnp.linspace(0.0, 1.0, n_tok, dtype=jnp.float32)[:, None]
    w = (2.0 * math.pi / n_tok) * jnp.arange(n_tok, dtype=jnp.float32)[:, None]
    f = jnp.linspace(1e-4, bands - 1, bands, dtype=jnp.float32)[None, :]
    feats = jnp.concatenate([t, jnp.cos(f * w), -jnp.sin(f * w)], axis=-1)
    h = jnp.sin(p['hy_f_freq1'] * (jnp.dot(feats, p['hy_f_w1'], precision=hp) + p['hy_f_b1']))
    h = jnp.sin(p['hy_f_freq2'] * (jnp.dot(h, p['hy_f_w2'], precision=hp) + p['hy_f_b2']))
    h = jnp.dot(h, p['hy_f_w3'], precision=hp).reshape(n_tok, HY_ORDER, 2, HY_WIDTH)
    rates = jnp.abs(jnp.linspace(math.log(HY_DECAY_TARGET) / HY_DECAY_PCT_SHORT,
                                 math.log(HY_DECAY_TARGET) / HY_DECAY_PCT_LONG, HY_WIDTH, dtype=jnp.float32))
    h = h * jnp.exp(-t * rates)[:, None, None, :]
    h_fwd, h_bwd = h[:, :, 0], h[:, :, 1]
    filt = jnp.concatenate([h_fwd, jnp.zeros_like(h_fwd[:1]), h_bwd[:0:-1]], axis=0)
    return jnp.fft.rfft(filt, axis=0)


def long_conv(z, k_spec, bias):
    n = z.shape[1]
    z_spec = jnp.fft.rfft(z, n=2 * n, axis=1)
    y = jnp.fft.irfft(z_spec * k_spec[None], n=2 * n, axis=1)[:, :n]
    return y + z * bias


def hyena_branch(streams, p):
    n = streams.shape[1]
    v, x1, x2 = jnp.split(streams, 3, axis=-1)
    k_spec = hyena_filter_spectra(n, p)
    z = v
    for o, gate in enumerate((x1, x2)):
        z = gate * long_conv(z, k_spec[:, o], p['hy_bias'][o])
    return z


def merge_branches(y_s5, y_rw, y_hy, gates, p):
    g_s5, g_rw, g_hy = jnp.split(jax.nn.sigmoid(gates), N_BRANCH, axis=-1)
    m = g_s5 * mm(y_s5, p['br_s5']) + g_rw * mm(y_rw, p['br_rw']) + g_hy * mm(y_hy, p['br_hy'])
    return mm(m, p['out_w'])


def token_mixer(h_lat, h_ctx, p, with_ctx_out):
    bsz = h_lat.shape[0]
    cuts = [IN_S5, IN_S5 + IN_RW, IN_S5 + IN_RW + IN_LORA, IN_S5 + IN_RW + IN_LORA + IN_HY]
    edges = [0] + cuts + [IN_COLS]
    w_parts = [p['in_w'][:, edges[i]:edges[i + 1]] for i in range(5)]
    u_l, rkv_l, lora_l, hy_l, gate_l = [mm(h_lat, wp) for wp in w_parts]
    n_ctx_parts = 5 if with_ctx_out else 3
    ctx_parts = [mm(h_ctx, wp) for wp in w_parts[:n_ctx_parts]]
    zs = jnp.zeros((bsz, S5_GROUPS, S5_STATE), jnp.float32)
    zr = jnp.zeros((bsz, RW_HEADS, RW_HEAD, RW_HEAD), jnp.float32)
    s5_c, s5_state = s5_branch(ctx_parts[0], p, ((zs, zs), (zs, zs)), with_ctx_out)
    rw_c, rw_state = rwkv_branch(short_conv(ctx_parts[1], p['rw_conv_w']), ctx_parts[2], p, (zr, zr), with_ctx_out)
    s5_l, _ = s5_branch(u_l, p, s5_state, True)
    rw_l, _ = rwkv_branch(latent_short_conv(rkv_l, p['rw_conv_w']), lora_l, p, rw_state, True)
    hy_lat = hyena_branch(latent_short_conv(hy_l, p['hy_conv_w'], p['hy_conv_b']), p)
    y_lat = merge_branches(s5_l, rw_l, hy_lat, gate_l, p)
    if not with_ctx_out:
        return y_lat, None
    hy_ctx = hyena_branch(short_conv(ctx_parts[3], p['hy_conv_w'], p['hy_conv_b']), p)
    y_ctx = merge_branches(s5_c, rw_c, hy_ctx, ctx_parts[4], p)
    return y_lat, y_ctx


def swiglu(h, wg, wu, wd):
    return mm(jax.nn.silu(mm(h, wg)) * mm(h, wu), wd)


def moe_swiglu(h, router_w, wg, wu, wd):
    d_model = h.shape[-1]
    tok = h.reshape(-1, d_model)
    n = tok.shape[0]
    n_assign = n * TOP_K
    logits = jnp.dot(tok, router_w)
    top_logit, top_e = lax.top_k(logits, TOP_K)
    gate = jax.nn.softmax(top_logit, axis=-1).reshape(-1)
    flat_e = top_e.reshape(-1)
    order = jnp.argsort(flat_e)
    sorted_e = flat_e[order]
    sizes = jnp.zeros((N_EXPERTS,), jnp.int32).at[flat_e].add(1)
    padded = (sizes + MOE_BLOCK - 1) // MOE_BLOCK * MOE_BLOCK
    pad_end = jnp.cumsum(padded)
    pad_start = pad_end - padded
    grp_start = jnp.cumsum(sizes) - sizes
    slot = pad_start[sorted_e] + jnp.arange(n_assign, dtype=jnp.int32) - grp_start[sorted_e]
    n_blocks = -(-n_assign // MOE_BLOCK) + N_EXPERTS
    cap = n_blocks * MOE_BLOCK
    slot_tok = jnp.full((cap,), n, jnp.int32).at[slot].set((order // TOP_K).astype(jnp.int32))
    slot_gate = jnp.zeros((cap,), jnp.float32).at[slot].set(gate[order])
    block_start = jnp.arange(n_blocks, dtype=jnp.int32) * MOE_BLOCK
    block_e = jnp.minimum(jnp.sum(block_start[:, None] >= pad_end[None, :], axis=1), N_EXPERTS - 1)
    tok_pad = jnp.concatenate([tok, jnp.zeros((1, d_model), tok.dtype)], axis=0)
    xb = tok_pad[slot_tok].reshape(n_blocks, MOE_BLOCK, d_model)

    def expert_block(args):
        x_blk, e = args
        return (jax.nn.silu(x_blk @ wg[e]) * (x_blk @ wu[e])) @ wd[e]

    yb = lax.map(expert_block, (xb, block_e)).reshape(cap, d_model)
    yb = yb * slot_gate[:, None].astype(yb.dtype)
    out = jax.ops.segment_sum(yb, slot_tok, num_segments=n + 1)[:n]
    return out.reshape(h.shape)


def kernel(x, c, ctx, c_ctx, mod_w, mod_b, norm_g, in_w, s5_lam_re, s5_lam_im, s5_log_step, s5_b_re, s5_b_im, s5_c_re, s5_c_im, s5_d, s5_glu_w, rw_conv_w, rw_w0, rw_w2, rw_a0, rw_a2, rw_g2, rw_kk, rw_ka, rw_rk, rw_ln_w, rw_ln_b, hy_conv_w, hy_conv_b, hy_f_w1, hy_f_b1, hy_f_freq1, hy_f_w2, hy_f_b2, hy_f_freq2, hy_f_w3, hy_bias, br_s5, br_rw, br_hy, out_w, ffn_wg, ffn_wu, ffn_wd, moe_router, moe_wg, moe_wu, moe_wd):
    silu_c = jax.nn.silu(c)
    silu_cc = jax.nn.silu(c_ctx)
    for i in range(DEPTH):
        last = i == DEPTH - 1
        p = {
            'in_w': in_w[i],
            's5_lam_re': s5_lam_re[i], 's5_lam_im': s5_lam_im[i], 's5_log_step': s5_log_step[i],
            's5_b_re': s5_b_re[i], 's5_b_im': s5_b_im[i], 's5_c_re': s5_c_re[i], 's5_c_im': s5_c_im[i],
            's5_d': s5_d[i], 's5_glu_w': s5_glu_w[i],
            'rw_conv_w': rw_conv_w[i], 'rw_w0': rw_w0[i], 'rw_w2': rw_w2[i], 'rw_a0': rw_a0[i], 'rw_a2': rw_a2[i],
            'rw_g2': rw_g2[i], 'rw_kk': rw_kk[i], 'rw_ka': rw_ka[i], 'rw_rk': rw_rk[i],
            'rw_ln_w': rw_ln_w[i], 'rw_ln_b': rw_ln_b[i],
            'hy_conv_w': hy_conv_w[i], 'hy_conv_b': hy_conv_b[i], 'hy_f_w1': hy_f_w1[i], 'hy_f_b1': hy_f_b1[i],
            'hy_f_freq1': hy_f_freq1[i], 'hy_f_w2': hy_f_w2[i], 'hy_f_b2': hy_f_b2[i], 'hy_f_freq2': hy_f_freq2[i],
            'hy_f_w3': hy_f_w3[i], 'hy_bias': hy_bias[i],
            'br_s5': br_s5[i], 'br_rw': br_rw[i], 'br_hy': br_hy[i], 'out_w': out_w[i],
        }
        if i % 2 == 0:
            ffn = functools.partial(swiglu, wg=ffn_wg[i // 2], wu=ffn_wu[i // 2], wd=ffn_wd[i // 2])
        else:
            ffn = functools.partial(moe_swiglu, router_w=moe_router[i // 2], wg=moe_wg[i // 2],
                                    wu=moe_wu[i // 2], wd=moe_wd[i // 2])
        ml = jnp.split((silu_c @ mod_w[i] + mod_b[i])[:, None, :], 6, axis=-1)
        mc = jnp.split(silu_cc @ mod_w[i] + mod_b[i], 6, axis=-1)
        g_pre_m, g_post_m, g_pre_f, g_post_f = norm_g[i]
        h_l = rms_norm(x, g_pre_m) * (1 + ml[1]) + ml[0]
        h_c = rms_norm(ctx, g_pre_m) * (1 + mc[1]) + mc[0]
        y_l, y_c = token_mixer(h_l, h_c, p, not last)
        x = x + ml[2] * rms_norm(y_l, g_post_m)
        x = x + ml[5] * rms_norm(ffn(rms_norm(x, g_pre_f) * (1 + ml[4]) + ml[3]), g_post_f)
        if not last:
            ctx = ctx + mc[2] * rms_norm(y_c, g_post_m)
            ctx = ctx + mc[5] * rms_norm(ffn(rms_norm(ctx, g_pre_f) * (1 + mc[4]) + mc[3]), g_post_f)
    return x
```

```python
import math
import functools
import numpy as np
import jax
import jax.numpy as jnp
from jax import lax
from jax.experimental import pallas as pl
from jax.experimental.pallas import tpu as pltpu

D_MODEL = 1024
BATCH = 8
SEQ = 4096
DEPTH = 2
GRID_W = 64
CTX_LEN = 256
NORM_EPS = 1e-6
S5_WIDTH = D_MODEL // 4
S5_GROUP = 16
S5_GROUPS = S5_WIDTH // S5_GROUP
S5_STATE = 64
RW_WIDTH = D_MODEL // 2
RW_HEAD = 64
RW_HEADS = RW_WIDTH // RW_HEAD
RW_DECAY_LORA = 64
RW_ICLR_LORA = 64
RW_GATE_LORA = 128
RW_GN_EPS = 64e-5
HY_WIDTH = D_MODEL // 4
HY_ORDER = 2
HY_POS_DIM = 33
HY_FILTER_HIDDEN = 64
HY_DECAY_TARGET = 1e-2
HY_DECAY_PCT_SHORT = 0.3
HY_DECAY_PCT_LONG = 1.5
SHORT_CONV = 3
N_BRANCH = 3
FFN_HIDDEN = 2816
N_EXPERTS = 8
TOP_K = 2
EXPERT_HIDDEN = 3584
MOE_BLOCK = 256
IN_S5 = S5_WIDTH
IN_RW = 3 * RW_WIDTH
IN_LORA = 2 * RW_DECAY_LORA + 2 * RW_ICLR_LORA + RW_GATE_LORA
IN_HY = (HY_ORDER + 1) * HY_WIDTH
IN_GATE = N_BRANCH * D_MODEL
IN_COLS = IN_S5 + IN_RW + IN_LORA + IN_HY + IN_GATE

VMEM_LIMIT = 56 * 1024 * 1024
S5_CHUNK = 64
RW_CHUNK = 64
RW_HG = 4
RW_LANES = RW_HG * RW_HEAD
RW_BLOCK_TOKENS = 512


def _mm_kernel(a_ref, b_ref, o_ref):
    o_ref[...] = jnp.dot(a_ref[...].astype(jnp.bfloat16), b_ref[...],
                         preferred_element_type=jnp.float32)


def _pick(n, cands):
    for c in cands:
        if n % c == 0:
            return c
    return n


def pmatmul(a, b):
    m, k = a.shape
    n = b.shape[1]
    b = b.astype(jnp.bfloat16)
    tm = _pick(m, (512, 256, 128, 8))
    tn = _pick(n, (1024, 768, 512, 384, 256, 128))
    return pl.pallas_call(
        _mm_kernel,
        out_shape=jax.ShapeDtypeStruct((m, n), jnp.float32),
        grid=(n // tn, m // tm),
        in_specs=[pl.BlockSpec((tm, k), lambda j, i: (i, 0)),
                  pl.BlockSpec((k, tn), lambda j, i: (0, j))],
        out_specs=pl.BlockSpec((tm, tn), lambda j, i: (i, j)),
        compiler_params=pltpu.CompilerParams(
            dimension_semantics=("parallel", "parallel"),
            vmem_limit_bytes=VMEM_LIMIT),
        name="mm",
    )(a, b)


def mm(a, b):
    lead = a.shape[:-1]
    return pmatmul(a.reshape(-1, a.shape[-1]), b).reshape(lead + (b.shape[1],))


def rms_norm(x, gain):
    y = x * lax.rsqrt(jnp.mean(x * x, axis=-1, keepdims=True) + NORM_EPS)
    return y * gain


def short_conv(x, w, b=None):
    half = SHORT_CONV // 2
    n = x.shape[1]
    xp = jnp.pad(x, ((0, 0), (half, half), (0, 0)))
    y = sum(xp[:, j:j + n] * w[j] for j in range(SHORT_CONV))
    return y if b is None else y + b


def latent_short_conv(x, w, b=None):
    bsz, n_tok, ch = x.shape
    rows = n_tok // GRID_W
    y = short_conv(x.reshape(bsz * rows, GRID_W, ch), w, b)
    return y.reshape(bsz, n_tok, ch)


def s5_chunk_operators(lam_re, lam_im, log_step, b_re, b_im, c_re, c_im):
    T, G, P, GC = S5_CHUNK, S5_GROUPS, S5_STATE, S5_GROUP
    hp = lax.Precision.HIGHEST
    step = jnp.exp(log_step)[..., None]
    lr, li = lam_re, lam_im
    tau = jnp.arange(T + 1, dtype=jnp.float32)[:, None, None, None]
    mag = jnp.exp(lr * step * tau)
    ang = li * step * tau
    e_re, e_im = mag * jnp.cos(ang), mag * jnp.sin(ang)
    ab_re, ab_im = e_re[1], e_im[1]
    den = lr * lr + li * li
    nr, ni = ab_re - 1.0, ab_im
    q_re = (nr * lr + ni * li) / den
    q_im = (ni * lr - nr * li) / den
    bb_re = q_re[..., None] * b_re - q_im[..., None] * b_im
    bb_im = q_re[..., None] * b_im + q_im[..., None] * b_re
    eb_re = e_re[..., None] * bb_re - e_im[..., None] * bb_im
    eb_im = e_re[..., None] * bb_im + e_im[..., None] * bb_re
    ktau = (jnp.einsum('dgip,tdgpj->tdgij', c_re, eb_re[:T], precision=hp)
            - jnp.einsum('dgip,tdgpj->tdgij', c_im, eb_im[:T], precision=hp))
    t_idx = jnp.arange(T)
    lag = t_idx[:, None] - t_idx[None, :]
    kf = jnp.where((lag >= 0)[:, :, None, None, None], ktau[jnp.clip(lag, 0, T - 1), 0], 0.0)
    kb = jnp.where((lag <= 0)[:, :, None, None, None], ktau[jnp.clip(-lag, 0, T - 1), 1], 0.0)
    kt = jnp.transpose(kf + kb, (2, 1, 4, 0, 3)).reshape(G, T * GC, T * GC)
    wf_re, wf_im = eb_re[T - 1 - t_idx, 0], eb_im[T - 1 - t_idx, 0]
    wb_re, wb_im = eb_re[t_idx, 1], eb_im[t_idx, 1]
    win = jnp.stack([wf_re, wf_im, wb_re, wb_im], axis=0)
    win = jnp.transpose(win, (2, 1, 4, 0, 3)).reshape(G, T * GC, 4 * P)
    ef_re, ef_im = e_re[t_idx + 1, 0], e_im[t_idx + 1, 0]
    eb2_re, eb2_im = e_re[T - t_idx, 1], e_im[T - t_idx, 1]

    def readout(cr, ci, er, ei):
        re = cr[None] * er[:, :, None, :] - ci[None] * ei[:, :, None, :]
        im = -(cr[None] * ei[:, :, None, :] + ci[None] * er[:, :, None, :])
        return re, im

    of_re, of_im = readout(c_re[0], c_im[0], ef_re, ef_im)
    ob_re, ob_im = readout(c_re[1], c_im[1], eb2_re, eb2_im)
    wout = jnp.stack([of_re, of_im, ob_re, ob_im], axis=0)
    wout = jnp.transpose(wout, (2, 0, 4, 1, 3)).reshape(G, 4 * P, T * GC)
    at_re, at_im = e_re[T], e_im[T]
    apow = jnp.concatenate([at_re[0], at_re[0], at_re[1], at_re[1]], axis=-1)[:, None, :]
    aimg = jnp.concatenate([-at_im[0], at_im[0], -at_im[1], at_im[1]], axis=-1)[:, None, :]
    return kt.astype(jnp.bfloat16), win.astype(jnp.bfloat16), wout.astype(jnp.bfloat16), apow, aimg


def _s5_kernel(x_ref, kt_ref, win_ref, wout_ref, apow_ref, aimg_ref, h0_ref, y_ref, hfin_ref, hin_ref, *,
               n_chunks, bsz):
    P2 = 2 * S5_STATE
    xb = x_ref[0].astype(jnp.bfloat16)
    hin_ref[...] = jnp.dot(xb, win_ref[0], preferred_element_type=jnp.float32)
    ap = apow_ref[0]
    ai = aimg_ref[0]
    apf, aif = ap[:, :P2], ai[:, :P2]
    apb, aib = ap[:, P2:], ai[:, P2:]

    def cmul(h, a_p, a_i):
        return h * a_p + pltpu.roll(h, S5_STATE, axis=1) * a_i

    def body(c, carry):
        hf, hb = carry
        rf = pl.ds(pl.multiple_of(c * bsz, bsz), bsz)
        rb = pl.ds(pl.multiple_of((n_chunks - 1 - c) * bsz, bsz), bsz)
        df = hin_ref[rf, :P2]
        db = hin_ref[rb, P2:]
        hin_ref[rf, :P2] = hf
        hin_ref[rb, P2:] = hb
        return cmul(hf, apf, aif) + df, cmul(hb, apb, aib) + db

    h0 = h0_ref[0]
    hf, hb = lax.fori_loop(0, n_chunks, body, (h0[:, :P2], h0[:, P2:]))
    hfin_ref[0, :, :P2] = hf
    hfin_ref[0, :, P2:] = hb
    y = jnp.dot(xb, kt_ref[0], preferred_element_type=jnp.float32)
    y = y + jnp.dot(hin_ref[...].astype(jnp.bfloat16), wout_ref[0], preferred_element_type=jnp.float32)
    y_ref[0] = y


def s5_scan_pallas(u, ops, h0):
    kt, win, wout, apow, aimg = ops
    bsz, n, _ = u.shape
    T, G, P, GC = S5_CHUNK, S5_GROUPS, S5_STATE, S5_GROUP
    nc = n // T
    rows = nc * bsz
    x = jnp.transpose(u.reshape(bsz, nc, T, G, GC), (3, 1, 0, 2, 4)).reshape(G, rows, T * GC)
    y, hfin = pl.pallas_call(
        functools.partial(_s5_kernel, n_chunks=nc, bsz=bsz),
        out_shape=(jax.ShapeDtypeStruct((G, rows, T * GC), jnp.float32),
                   jax.ShapeDtypeStruct((G, bsz, 4 * P), jnp.float32)),
        grid=(G,),
        in_specs=[pl.BlockSpec((1, rows, T * GC), lambda g: (g, 0, 0)),
                  pl.BlockSpec((1, T * GC, T * GC), lambda g: (g, 0, 0)),
                  pl.BlockSpec((1, T * GC, 4 * P), lambda g: (g, 0, 0)),
                  pl.BlockSpec((1, 4 * P, T * GC), lambda g: (g, 0, 0)),
                  pl.BlockSpec((1, 1, 4 * P), lambda g: (g, 0, 0)),
                  pl.BlockSpec((1, 1, 4 * P), lambda g: (g, 0, 0)),
                  pl.BlockSpec((1, bsz, 4 * P), lambda g: (g, 0, 0))],
        out_specs=(pl.BlockSpec((1, rows, T * GC), lambda g: (g, 0, 0)),
                   pl.BlockSpec((1, bsz, 4 * P), lambda g: (g, 0, 0))),
        scratch_shapes=[pltpu.VMEM((rows, 4 * P), jnp.float32)],
        compiler_params=pltpu.CompilerParams(dimension_semantics=("parallel",),
                                             vmem_limit_bytes=VMEM_LIMIT),
        name="s5_scan",
    )(x, kt, win, wout, apow, aimg, h0)
    y = jnp.transpose(y.reshape(G, nc, bsz, T, GC), (2, 1, 3, 0, 4)).reshape(bsz, n, G * GC)
    return y, hfin


def s5_branch(u, u_ctx, p):
    ops = s5_chunk_operators(p['s5_lam_re'], p['s5_lam_im'], p['s5_log_step'], p['s5_b_re'], p['s5_b_im'],
                             p['s5_c_re'], p['s5_c_im'])
    h0 = jnp.zeros((S5_GROUPS, u.shape[0], 4 * S5_STATE), jnp.float32)
    y_ctx, h_ctx = s5_scan_pallas(u_ctx, ops, h0)
    y_lat, _ = s5_scan_pallas(u, ops, h_ctx)

    def post(y, uu):
        y = y + p['s5_d'] * uu
        y = jax.nn.gelu(y, approximate=False)
        return y * jax.nn.sigmoid(mm(y, p['s5_glu_w']))

    return post(y_lat, u), post(y_ctx, u_ctx)


_NT = (((1,), (1,)), ((), ()))

_M_SAME, _M_EYE, _M_STRICT, _M_INCL, _M_LEVEL0 = 0, 1, 2, 4, 6
_N_LEVELS = 6


def rwkv_masks():
    C, n = RW_CHUNK, RW_LANES
    row = np.arange(n)[:, None]
    col = np.arange(n)[None, :]
    same = (row // C) == (col // C)
    out = [same, row == col]
    per_dir = []
    for reverse in (False, True):
        t, j = (row % C, col % C) if not reverse else (col % C, row % C)
        per_dir.append((same & (j < t), same & (j <= t),
                        [same & ((t // s) % 2 == 1) & ((j // s) == (t // s) - 1) for s in (1, 2, 4, 8, 16, 32)]))
    out += [per_dir[0][0], per_dir[1][0], per_dir[0][1], per_dir[1][1]]
    for l in range(_N_LEVELS):
        out += [per_dir[0][2][l], per_dir[1][2][l]]
    tt, jj = np.arange(C)[:, None], np.arange(C)[None, :]
    tri = np.stack([jj <= tt, jj >= tt])
    return jnp.asarray(np.stack(out), jnp.float32), jnp.asarray(tri, jnp.bfloat16)


def _bdot(a, b, dims=None):
    a = a.astype(jnp.bfloat16)
    b = b.astype(jnp.bfloat16)
    if dims is None:
        return jnp.dot(a, b, preferred_element_type=jnp.float32)
    return lax.dot_general(a, b, dims, preferred_element_type=jnp.float32)


def _rwkv_chunk(S, r, lw, k, v, kk, b, m_ref, tri, d):
    C = RW_CHUNK
    same = m_ref[_M_SAME]
    lw_hi = lw.astype(jnp.bfloat16)
    rem = lw - lw_hi.astype(jnp.float32)
    lw_mid = rem.astype(jnp.bfloat16)
    lw_lo = (rem - lw_mid.astype(jnp.float32)).astype(jnp.bfloat16)
    cl = (jnp.dot(tri, lw_hi, preferred_element_type=jnp.float32)
          + jnp.dot(tri, lw_mid, preferred_element_type=jnp.float32)
          + jnp.dot(tri, lw_lo, preferred_element_type=jnp.float32))
    tot = jnp.sum(lw, axis=0, keepdims=True)
    e_neg = jnp.exp(-cl)
    e_end = jnp.exp(tot - cl)
    a_t = -kk * jnp.exp(cl - lw)
    r_t = r * jnp.exp(cl)
    b_t = b * e_neg
    k_t = k * e_neg
    b_h = b * e_end
    k_h = k * e_end
    p_c = jnp.exp(tot)

    def tile(x):
        return jnp.concatenate([x] * RW_HG, axis=0)

    def stack(x):
        return tile(x) * same

    def unstack(z):
        acc = z[0:C]
        for h in range(1, RW_HG):
            acc = acc + z[h * C:(h + 1) * C]
        return acc

    n = RW_LANES
    lhs = jnp.concatenate([stack(a_t), stack(r_t)], axis=0)
    rhs = jnp.concatenate([tile(b_t), tile(k_t)], axis=0)
    amat = _bdot(lhs, rhs, _NT)
    a_ab = amat[:n, :n] * m_ref[_M_STRICT + d]
    a_ak = amat[:n, n:] * m_ref[_M_STRICT + d]
    a_rb = amat[n:, :n] * m_ref[_M_INCL + d]
    a_rk = amat[n:, n:] * m_ref[_M_INCL + d]
    dinv = m_ref[_M_EYE] + a_ab * m_ref[_M_LEVEL0 + d]
    for l in range(1, _N_LEVELS):
        dinv = dinv + _bdot(dinv, _bdot(a_ab * m_ref[_M_LEVEL0 + 2 * l + d], dinv))
    sv = stack(v)
    av = _bdot(jnp.concatenate([a_ak, a_rk], axis=0), sv)
    akv = unstack(av[:n])
    arkv = unstack(av[n:])
    mu = _bdot(dinv, jnp.concatenate([stack(a_t), stack(akv)], axis=1))
    m1 = unstack(mu[:, :n])
    u0 = unstack(mu[:, n:])
    my = _bdot(a_rb, jnp.concatenate([stack(m1), stack(u0)], axis=1))
    m2 = r_t + unstack(my[:, :n])
    y0 = unstack(my[:, n:]) + arkv
    mut = jnp.concatenate([m1, u0], axis=1).T
    gh = _bdot(mut, b_h)
    g = m_ref[_M_EYE] * p_c + gh[:n] * same
    hmat = (gh[n:] + _bdot(v.T, k_h)) * same
    y = _bdot(m2, S, _NT) + y0
    s_new = _bdot(S, g) + hmat
    return s_new, y


def _rwkv_kernel(m_ref, tri_ref, rf_ref, rb_ref, vf_ref, vb_ref, kkf_ref, kkb_ref, lwf_ref, kf_ref, bf_ref,
                 lwb_ref, kb_ref, bb_ref, s0_ref, yf_ref, yb_ref, sfin_ref, sf_scr, sb_scr, *, n_chunks):
    i = pl.program_id(2)
    C = RW_CHUNK

    @pl.when(i == 0)
    def _():
        sf_scr[...] = s0_ref[0, 0, 0]
        sb_scr[...] = s0_ref[0, 1, 0]

    tri_f = tri_ref[0]
    tri_b = tri_ref[1]

    def body(c, carry):
        rf = pl.ds(pl.multiple_of(c * C, C), C)
        rb = pl.ds(pl.multiple_of((n_chunks - 1 - c) * C, C), C)
        s_f, y_f = _rwkv_chunk(sf_scr[...], rf_ref[0, rf, :], lwf_ref[0, rf, :], kf_ref[0, rf, :],
                               vf_ref[0, rf, :], kkf_ref[0, rf, :], bf_ref[0, rf, :], m_ref, tri_f, 0)
        s_b, y_b = _rwkv_chunk(sb_scr[...], rb_ref[0, rb, :], lwb_ref[0, rb, :], kb_ref[0, rb, :],
                               vb_ref[0, rb, :], kkb_ref[0, rb, :], bb_ref[0, rb, :], m_ref, tri_b, 1)
        sf_scr[...] = s_f
        sb_scr[...] = s_b
        yf_ref[0, rf, :] = y_f
        yb_ref[0, rb, :] = y_b
        return carry

    lax.fori_loop(0, n_chunks, body, 0)

    @pl.when(i == pl.num_programs(2) - 1)
    def _():
        sfin_ref[0, 0, 0] = sf_scr[...]
        sfin_ref[0, 1, 0] = sb_scr[...]


def rwkv_scan_pallas(r, v, kk, lw, kd, bvec, s0, masks, tri, block_tokens):
    bsz, n, width = r.shape
    ng = width // RW_LANES
    tb = block_tokens
    nb = n // tb
    fwd = pl.BlockSpec((1, tb, RW_LANES), lambda b, g, i: (b, i, g))
    bwd = pl.BlockSpec((1, tb, RW_LANES), lambda b, g, i: (b, nb - 1 - i, g))
    state_spec = pl.BlockSpec((1, 2, 1, RW_LANES, RW_LANES), lambda b, g, i: (b, 0, g, 0, 0))
    return pl.pallas_call(
        functools.partial(_rwkv_kernel, n_chunks=tb // RW_CHUNK),
        out_shape=(jax.ShapeDtypeStruct((bsz, n, width), jnp.float32),
                   jax.ShapeDtypeStruct((bsz, n, width), jnp.float32),
                   jax.ShapeDtypeStruct(s0.shape, jnp.float32)),
        grid=(bsz, ng, nb),
        in_specs=[pl.BlockSpec(masks.shape, lambda b, g, i: (0, 0, 0)),
                  pl.BlockSpec(tri.shape, lambda b, g, i: (0, 0, 0)),
                  fwd, bwd, fwd, bwd, fwd, bwd,
                  fwd, fwd, fwd, bwd, bwd, bwd, state_spec],
        out_specs=(fwd, bwd, state_spec),
        scratch_shapes=[pltpu.VMEM((RW_LANES, RW_LANES), jnp.float32),
                        pltpu.VMEM((RW_LANES, RW_LANES), jnp.float32)],
        compiler_params=pltpu.CompilerParams(dimension_semantics=("parallel", "parallel", "arbitrary"),
                                             vmem_limit_bytes=VMEM_LIMIT),
        name="rwkv_scan",
    )(masks, tri, r, r, v, v, kk, kk, lw[0], kd[0], bvec[0], lw[1], kd[1], bvec[1], s0)


def rwkv_branch(rkv, lora, p, s0, masks, tri, block_tokens):
    bsz, n, _ = rkv.shape

    def heads(t):
        return t.reshape(bsz, n, RW_HEADS, RW_HEAD)

    def flat(t):
        return t.reshape(bsz, n, RW_WIDTH)

    r, k, v = jnp.split(rkv, 3, axis=-1)
    c1 = RW_DECAY_LORA
    c2 = 2 * RW_DECAY_LORA
    c3 = c2 + RW_ICLR_LORA
    c4 = c3 + RW_ICLR_LORA
    w_lo = (lora[..., :c1], lora[..., c1:c2])
    a_lo = (lora[..., c2:c3], lora[..., c3:c4])
    kk = heads(k * p['rw_kk'])
    kk = flat(kk * lax.rsqrt(jnp.maximum(jnp.sum(kk * kk, axis=-1, keepdims=True), 1e-24)))
    lw, kd, bvec = [], [], []
    bonus = 0.0
    for d in range(2):
        w_log = -jax.nn.softplus(-(p['rw_w0'][d] + mm(jnp.tanh(w_lo[d]), p['rw_w2'][d]))) - 0.5
        lw.append(-jnp.exp(w_log))
        a = jax.nn.sigmoid(p['rw_a0'][d] + mm(a_lo[d], p['rw_a2'][d]))
        kd.append(k * (1.0 + (a - 1.0) * p['rw_ka']))
        bvec.append(kk * a)
        bonus = bonus + jnp.sum(heads(r * p['rw_rk']) * heads(kd[d]), axis=-1, keepdims=True) * heads(v)
    y_f, y_b, s_fin = rwkv_scan_pallas(r, v, kk, lw, kd, bvec, s0, masks, tri, block_tokens)
    wkv = heads(y_f + y_b)
    mu = jnp.mean(wkv, axis=-1, keepdims=True)
    var = jnp.mean(jnp.square(wkv - mu), axis=-1, keepdims=True)
    o = flat((wkv - mu) * lax.rsqrt(var + RW_GN_EPS)) * p['rw_ln_w'] + p['rw_ln_b']
    o = o + flat(bonus)
    g = mm(jax.nn.sigmoid(lora[..., c4:]), p['rw_g2'])
    return o * g, s_fin


def hyena_filter_spectra(n_tok, p):
    hp = lax.Precision.HIGHEST
    bands = (HY_POS_DIM - 1) // 2
    t = jnp.linspace(0.0, 1.0, n_tok, dtype=jnp.float32)[:, None]
    w = (2.0 * math.pi / n_tok) * jnp.arange(n_tok, dtype=jnp.float32)[:, None]
    f = jnp.linspace(1e-4, bands - 1, bands, dtype=jnp.float32)[None, :]
    feats = jnp.concatenate([t, jnp.cos(f * w), -jnp.sin(f * w)], axis=-1)
    h = jnp.sin(p['hy_f_freq1'] * (jnp.dot(feats, p['hy_f_w1'], precision=hp) + p['hy_f_b1']))
    h = jnp.sin(p['hy_f_freq2'] * (jnp.dot(h, p['hy_f_w2'], precision=hp) + p['hy_f_b2']))
    h = jnp.dot(h, p['hy_f_w3'], precision=hp).reshape(n_tok, HY_ORDER, 2, HY_WIDTH)
    rates = jnp.abs(jnp.linspace(math.log(HY_DECAY_TARGET) / HY_DECAY_PCT_SHORT,
                                 math.log(HY_DECAY_TARGET) / HY_DECAY_PCT_LONG, HY_WIDTH, dtype=jnp.float32))
    h = h * jnp.exp(-t * rates)[:, None, None, :]
    h_fwd, h_bwd = h[:, :, 0], h[:, :, 1]
    filt = jnp.concatenate([h_fwd, jnp.zeros_like(h_fwd[:1]), h_bwd[:0:-1]], axis=0)
    return jnp.fft.rfft(filt, axis=0)


def long_conv(z, k_spec, bias):
    n = z.shape[1]
    z_spec = jnp.fft.rfft(z, n=2 * n, axis=1)
    y = jnp.fft.irfft(z_spec * k_spec[None], n=2 * n, axis=1)[:, :n]
    return y + z * bias


def hyena_branch(streams, p):
    n = streams.shape[1]
    v, x1, x2 = jnp.split(streams, 3, axis=-1)
    k_spec = hyena_filter_spectra(n, p)
    z = v
    for o, gate in enumerate((x1, x2)):
        z = gate * long_conv(z, k_spec[:, o], p['hy_bias'][o])
    return z


def merge_branches(y_s5, y_rw, y_hy, gates, p):
    g_s5, g_rw, g_hy = jnp.split(jax.nn.sigmoid(gates), N_BRANCH, axis=-1)
    m = g_s5 * mm(y_s5, p['br_s5']) + g_rw * mm(y_rw, p['br_rw']) + g_hy * mm(y_hy, p['br_hy'])
    return mm(m, p['out_w'])


def token_mixer(h_lat, h_ctx, p, with_ctx_out):
    bsz = h_lat.shape[0]
    cuts = [IN_S5, IN_S5 + IN_RW, IN_S5 + IN_RW + IN_LORA, IN_S5 + IN_RW + IN_LORA + IN_HY]
    edges = [0] + cuts + [IN_COLS]
    w_parts = [p['in_w'][:, edges[i]:edges[i + 1]] for i in range(5)]
    u_l, rkv_l, lora_l, hy_l, gate_l = [mm(h_lat, wp) for wp in w_parts]
    n_ctx_parts = 5 if with_ctx_out else 3
    ctx_parts = [mm(h_ctx, wp) for wp in w_parts[:n_ctx_parts]]
    masks, tri = rwkv_masks()
    zr = jnp.zeros((bsz, 2, RW_WIDTH // RW_LANES, RW_LANES, RW_LANES), jnp.float32)
    s5_l, s5_c = s5_branch(u_l, ctx_parts[0], p)
    rw_c, rw_state = rwkv_branch(short_conv(ctx_parts[1], p['rw_conv_w']), ctx_parts[2], p, zr, masks, tri,
                                 CTX_LEN)
    rw_l, _ = rwkv_branch(latent_short_conv(rkv_l, p['rw_conv_w']), lora_l, p, rw_state, masks, tri,
                          RW_BLOCK_TOKENS)
    hy_lat = hyena_branch(latent_short_conv(hy_l, p['hy_conv_w'], p['hy_conv_b']), p)
    y_lat = merge_branches(s5_l, rw_l, hy_lat, gate_l, p)
    if not with_ctx_out:
        return y_lat, None
    hy_ctx = hyena_branch(short_conv(ctx_parts[3], p['hy_conv_w'], p['hy_conv_b']), p)
    y_ctx = merge_branches(s5_c, rw_c, hy_ctx, ctx_parts[4], p)
    return y_lat, y_ctx


def swiglu(h, wg, wu, wd):
    return mm(jax.nn.silu(mm(h, wg)) * mm(h, wu), wd)


def moe_swiglu(h, router_w, wg, wu, wd):
    d_model = h.shape[-1]
    tok = h.reshape(-1, d_model)
    n = tok.shape[0]
    n_assign = n * TOP_K
    logits = jnp.dot(tok, router_w)
    top_logit, top_e = lax.top_k(logits, TOP_K)
    gate = jax.nn.softmax(top_logit, axis=-1).reshape(-1)
    flat_e = top_e.reshape(-1)
    order = jnp.argsort(flat_e)
    sorted_e = flat_e[order]
    sizes = jnp.zeros((N_EXPERTS,), jnp.int32).at[flat_e].add(1)
    padded = (sizes + MOE_BLOCK - 1) // MOE_BLOCK * MOE_BLOCK
    pad_end = jnp.cumsum(padded)
    pad_start = pad_end - padded
    grp_start = jnp.cumsum(sizes) - sizes
    slot = pad_start[sorted_e] + jnp.arange(n_assign, dtype=jnp.int32) - grp_start[sorted_e]
    n_blocks = -(-n_assign // MOE_BLOCK) + N_EXPERTS
    cap = n_blocks * MOE_BLOCK
    slot_tok = jnp.full((cap,), n, jnp.int32).at[slot].set((order // TOP_K).astype(jnp.int32))
    slot_gate = jnp.zeros((cap,), jnp.float32).at[slot].set(gate[order])
    block_start = jnp.arange(n_blocks, dtype=jnp.int32) * MOE_BLOCK
    block_e = jnp.minimum(jnp.sum(block_start[:, None] >= pad_end[None, :], axis=1), N_EXPERTS - 1)
    tok_pad = jnp.concatenate([tok, jnp.zeros((1, d_model), tok.dtype)], axis=0)
    xb = tok_pad[slot_tok].reshape(n_blocks, MOE_BLOCK, d_model)

    def expert_block(args):
        x_blk, e = args
        return (jax.nn.silu(x_blk @ wg[e]) * (x_blk @ wu[e])) @ wd[e]

    yb = lax.map(expert_block, (xb, block_e)).reshape(cap, d_model)
    yb = yb * slot_gate[:, None].astype(yb.dtype)
    out = jax.ops.segment_sum(yb, slot_tok, num_segments=n + 1)[:n]
    return out.reshape(h.shape)


def kernel(x, c, ctx, c_ctx, mod_w, mod_b, norm_g, in_w, s5_lam_re, s5_lam_im, s5_log_step, s5_b_re, s5_b_im, s5_c_re, s5_c_im, s5_d, s5_glu_w, rw_conv_w, rw_w0, rw_w2, rw_a0, rw_a2, rw_g2, rw_kk, rw_ka, rw_rk, rw_ln_w, rw_ln_b, hy_conv_w, hy_conv_b, hy_f_w1, hy_f_b1, hy_f_freq1, hy_f_w2, hy_f_b2, hy_f_freq2, hy_f_w3, hy_bias, br_s5, br_rw, br_hy, out_w, ffn_wg, ffn_wu, ffn_wd, moe_router, moe_wg, moe_wu, moe_wd):
    silu_c = jax.nn.silu(c)
    silu_cc = jax.nn.silu(c_ctx)
    for i in range(DEPTH):
        last = i == DEPTH - 1
        p = {
            'in_w': in_w[i],
            's5_lam_re': s5_lam_re[i], 's5_lam_im': s5_lam_im[i], 's5_log_step': s5_log_step[i],
            's5_b_re': s5_b_re[i], 's5_b_im': s5_b_im[i], 's5_c_re': s5_c_re[i], 's5_c_im': s5_c_im[i],
            's5_d': s5_d[i], 's5_glu_w': s5_glu_w[i],
            'rw_conv_w': rw_conv_w[i], 'rw_w0': rw_w0[i], 'rw_w2': rw_w2[i], 'rw_a0': rw_a0[i], 'rw_a2': rw_a2[i],
            'rw_g2': rw_g2[i], 'rw_kk': rw_kk[i], 'rw_ka': rw_ka[i], 'rw_rk': rw_rk[i],
            'rw_ln_w': rw_ln_w[i], 'rw_ln_b': rw_ln_b[i],
            'hy_conv_w': hy_conv_w[i], 'hy_conv_b': hy_conv_b[i], 'hy_f_w1': hy_f_w1[i], 'hy_f_b1': hy_f_b1[i],
            'hy_f_freq1': hy_f_freq1[i], 'hy_f_w2': hy_f_w2[i], 'hy_f_b2': hy_f_b2[i], 'hy_f_freq2': hy_f_freq2[i],
            'hy_f_w3': hy_f_w3[i], 'hy_bias': hy_bias[i],
            'br_s5': br_s5[i], 'br_rw': br_rw[i], 'br_hy': br_hy[i], 'out_w': out_w[i],
        }
        if i % 2 == 0:
            ffn = functools.partial(swiglu, wg=ffn_wg[i // 2], wu=ffn_wu[i // 2], wd=ffn_wd[i // 2])
        else:
            ffn = functools.partial(moe_swiglu, router_w=moe_router[i // 2], wg=moe_wg[i // 2],
                                    wu=moe_wu[i // 2], wd=moe_wd[i // 2])
        ml = jnp.split((silu_c @ mod_w[i] + mod_b[i])[:, None, :], 6, axis=-1)
        mc = jnp.split(silu_cc @ mod_w[i] + mod_b[i], 6, axis=-1)
        g_pre_m, g_post_m, g_pre_f, g_post_f = norm_g[i]
        h_l = rms_norm(x, g_pre_m) * (1 + ml[1]) + ml[0]
        h_c = rms_norm(ctx, g_pre_m) * (1 + mc[1]) + mc[0]
        y_l, y_c = token_mixer(h_l, h_c, p, not last)
        x = x + ml[2] * rms_norm(y_l, g_post_m)
        x = x + ml[5] * rms_norm(ffn(rms_norm(x, g_pre_f) * (1 + ml[4]) + ml[3]), g_post_f)
        if not last:
            ctx = ctx + mc[2] * rms_norm(y_c, g_post_m)
            ctx = ctx + mc[5] * rms_norm(ffn(rms_norm(ctx, g_pre_f) * (1 + mc[4]) + mc[3]), g_post_f)
    return x
```

```python
import math
import functools
import numpy as np
import jax
import jax.numpy as jnp
from jax import lax
from jax.experimental import pallas as pl
from jax.experimental.pallas import tpu as pltpu

D_MODEL = 1024
BATCH = 8
SEQ = 4096
DEPTH = 2
GRID_W = 64
CTX_LEN = 256
NORM_EPS = 1e-6
S5_WIDTH = D_MODEL // 4
S5_GROUP = 16
S5_GROUPS = S5_WIDTH // S5_GROUP
S5_STATE = 64
RW_WIDTH = D_MODEL // 2
RW_HEAD = 64
RW_HEADS = RW_WIDTH // RW_HEAD
RW_DECAY_LORA = 64
RW_ICLR_LORA = 64
RW_GATE_LORA = 128
RW_GN_EPS = 64e-5
HY_WIDTH = D_MODEL // 4
HY_ORDER = 2
HY_POS_DIM = 33
HY_FILTER_HIDDEN = 64
HY_DECAY_TARGET = 1e-2
HY_DECAY_PCT_SHORT = 0.3
HY_DECAY_PCT_LONG = 1.5
SHORT_CONV = 3
N_BRANCH = 3
FFN_HIDDEN = 2816
N_EXPERTS = 8
TOP_K = 2
EXPERT_HIDDEN = 3584
MOE_BLOCK = 256
IN_S5 = S5_WIDTH
IN_RW = 3 * RW_WIDTH
IN_LORA = 2 * RW_DECAY_LORA + 2 * RW_ICLR_LORA + RW_GATE_LORA
IN_HY = (HY_ORDER + 1) * HY_WIDTH
IN_GATE = N_BRANCH * D_MODEL
IN_COLS = IN_S5 + IN_RW + IN_LORA + IN_HY + IN_GATE

VMEM_LIMIT = 56 * 1024 * 1024
S5_CHUNK = 64
RW_CHUNK = 64
RW_HG = 4
RW_LANES = RW_HG * RW_HEAD
RW_BLOCK_TOKENS = 512
MOE_HIDDEN_TILES = 2
FFN_BLOCK_ROWS = 256


def _mm_kernel(a_ref, b_ref, o_ref):
    o_ref[...] = jnp.dot(a_ref[...].astype(jnp.bfloat16), b_ref[...],
                         preferred_element_type=jnp.float32)


def _pick(n, cands):
    for c in cands:
        if n % c == 0:
            return c
    return n


def pmatmul(a, b):
    m, k = a.shape
    n = b.shape[1]
    b = b.astype(jnp.bfloat16)
    tm = _pick(m, (512, 256, 128, 8))
    tn = _pick(n, (1024, 768, 512, 384, 256, 128))
    return pl.pallas_call(
        _mm_kernel,
        out_shape=jax.ShapeDtypeStruct((m, n), jnp.float32),
        grid=(n // tn, m // tm),
        in_specs=[pl.BlockSpec((tm, k), lambda j, i: (i, 0)),
                  pl.BlockSpec((k, tn), lambda j, i: (0, j))],
        out_specs=pl.BlockSpec((tm, tn), lambda j, i: (i, j)),
        compiler_params=pltpu.CompilerParams(
            dimension_semantics=("parallel", "parallel"),
            vmem_limit_bytes=VMEM_LIMIT),
        name="mm",
    )(a, b)


def mm(a, b):
    lead = a.shape[:-1]
    return pmatmul(a.reshape(-1, a.shape[-1]), b).reshape(lead + (b.shape[1],))


def rms_norm(x, gain):
    y = x * lax.rsqrt(jnp.mean(x * x, axis=-1, keepdims=True) + NORM_EPS)
    return y * gain


def short_conv(x, w, b=None):
    half = SHORT_CONV // 2
    n = x.shape[1]
    xp = jnp.pad(x, ((0, 0), (half, half), (0, 0)))
    y = sum(xp[:, j:j + n] * w[j] for j in range(SHORT_CONV))
    return y if b is None else y + b


def latent_short_conv(x, w, b=None):
    bsz, n_tok, ch = x.shape
    rows = n_tok // GRID_W
    y = short_conv(x.reshape(bsz * rows, GRID_W, ch), w, b)
    return y.reshape(bsz, n_tok, ch)


def s5_chunk_operators(lam_re, lam_im, log_step, b_re, b_im, c_re, c_im):
    T, G, P, GC = S5_CHUNK, S5_GROUPS, S5_STATE, S5_GROUP
    hp = lax.Precision.HIGHEST
    step = jnp.exp(log_step)[..., None]
    lr, li = lam_re, lam_im
    tau = jnp.arange(T + 1, dtype=jnp.float32)[:, None, None, None]
    mag = jnp.exp(lr * step * tau)
    ang = li * step * tau
    e_re, e_im = mag * jnp.cos(ang), mag * jnp.sin(ang)
    ab_re, ab_im = e_re[1], e_im[1]
    den = lr * lr + li * li
    nr, ni = ab_re - 1.0, ab_im
    q_re = (nr * lr + ni * li) / den
    q_im = (ni * lr - nr * li) / den
    bb_re = q_re[..., None] * b_re - q_im[..., None] * b_im
    bb_im = q_re[..., None] * b_im + q_im[..., None] * b_re
    eb_re = e_re[..., None] * bb_re - e_im[..., None] * bb_im
    eb_im = e_re[..., None] * bb_im + e_im[..., None] * bb_re
    ktau = (jnp.einsum('dgip,tdgpj->tdgij', c_re, eb_re[:T], precision=hp)
            - jnp.einsum('dgip,tdgpj->tdgij', c_im, eb_im[:T], precision=hp))
    t_idx = jnp.arange(T)
    lag = t_idx[:, None] - t_idx[None, :]
    kf = jnp.where((lag >= 0)[:, :, None, None, None], ktau[jnp.clip(lag, 0, T - 1), 0], 0.0)
    kb = jnp.where((lag <= 0)[:, :, None, None, None], ktau[jnp.clip(-lag, 0, T - 1), 1], 0.0)
    kt = jnp.transpose(kf + kb, (2, 1, 4, 0, 3)).reshape(G, T * GC, T * GC)
    wf_re, wf_im = eb_re[T - 1 - t_idx, 0], eb_im[T - 1 - t_idx, 0]
    wb_re, wb_im = eb_re[t_idx, 1], eb_im[t_idx, 1]
    win = jnp.stack([wf_re, wf_im, wb_re, wb_im], axis=0)
    win = jnp.transpose(win, (2, 1, 4, 0, 3)).reshape(G, T * GC, 4 * P)
    ef_re, ef_im = e_re[t_idx + 1, 0], e_im[t_idx + 1, 0]
    eb2_re, eb2_im = e_re[T - t_idx, 1], e_im[T - t_idx, 1]

    def readout(cr, ci, er, ei):
        re = cr[None] * er[:, :, None, :] - ci[None] * ei[:, :, None, :]
        im = -(cr[None] * ei[:, :, None, :] + ci[None] * er[:, :, None, :])
        return re, im

    of_re, of_im = readout(c_re[0], c_im[0], ef_re, ef_im)
    ob_re, ob_im = readout(c_re[1], c_im[1], eb2_re, eb2_im)
    wout = jnp.stack([of_re, of_im, ob_re, ob_im], axis=0)
    wout = jnp.transpose(wout, (2, 0, 4, 1, 3)).reshape(G, 4 * P, T * GC)
    at_re, at_im = e_re[T], e_im[T]
    apow = jnp.concatenate([at_re[0], at_re[0], at_re[1], at_re[1]], axis=-1)[:, None, :]
    aimg = jnp.concatenate([-at_im[0], at_im[0], -at_im[1], at_im[1]], axis=-1)[:, None, :]
    return kt.astype(jnp.bfloat16), win.astype(jnp.bfloat16), wout.astype(jnp.bfloat16), apow, aimg


def _s5_kernel(x_ref, kt_ref, win_ref, wout_ref, apow_ref, aimg_ref, h0_ref, y_ref, hfin_ref, hin_ref, *,
               n_chunks, bsz):
    P2 = 2 * S5_STATE
    xb = x_ref[0].astype(jnp.bfloat16)
    hin_ref[...] = jnp.dot(xb, win_ref[0], preferred_element_type=jnp.float32)
    ap = apow_ref[0]
    ai = aimg_ref[0]
    apf, aif = ap[:, :P2], ai[:, :P2]
    apb, aib = ap[:, P2:], ai[:, P2:]

    def cmul(h, a_p, a_i):
        return h * a_p + pltpu.roll(h, S5_STATE, axis=1) * a_i

    def body(c, carry):
        hf, hb = carry
        rf = pl.ds(pl.multiple_of(c * bsz, bsz), bsz)
        rb = pl.ds(pl.multiple_of((n_chunks - 1 - c) * bsz, bsz), bsz)
        df = hin_ref[rf, :P2]
        db = hin_ref[rb, P2:]
        hin_ref[rf, :P2] = hf
        hin_ref[rb, P2:] = hb
        return cmul(hf, apf, aif) + df, cmul(hb, apb, aib) + db

    h0 = h0_ref[0]
    hf, hb = lax.fori_loop(0, n_chunks, body, (h0[:, :P2], h0[:, P2:]))
    hfin_ref[0, :, :P2] = hf
    hfin_ref[0, :, P2:] = hb
    y = jnp.dot(xb, kt_ref[0], preferred_element_type=jnp.float32)
    y = y + jnp.dot(hin_ref[...].astype(jnp.bfloat16), wout_ref[0], preferred_element_type=jnp.float32)
    y_ref[0] = y


def s5_scan_pallas(u, ops, h0):
    kt, win, wout, apow, aimg = ops
    bsz, n, _ = u.shape
    T, G, P, GC = S5_CHUNK, S5_GROUPS, S5_STATE, S5_GROUP
    nc = n // T
    rows = nc * bsz
    x = jnp.transpose(u.reshape(bsz, nc, T, G, GC), (3, 1, 0, 2, 4)).reshape(G, rows, T * GC)
    y, hfin = pl.pallas_call(
        functools.partial(_s5_kernel, n_chunks=nc, bsz=bsz),
        out_shape=(jax.ShapeDtypeStruct((G, rows, T * GC), jnp.float32),
                   jax.ShapeDtypeStruct((G, bsz, 4 * P), jnp.float32)),
        grid=(G,),
        in_specs=[pl.BlockSpec((1, rows, T * GC), lambda g: (g, 0, 0)),
                  pl.BlockSpec((1, T * GC, T * GC), lambda g: (g, 0, 0)),
                  pl.BlockSpec((1, T * GC, 4 * P), lambda g: (g, 0, 0)),
                  pl.BlockSpec((1, 4 * P, T * GC), lambda g: (g, 0, 0)),
                  pl.BlockSpec((1, 1, 4 * P), lambda g: (g, 0, 0)),
                  pl.BlockSpec((1, 1, 4 * P), lambda g: (g, 0, 0)),
                  pl.BlockSpec((1, bsz, 4 * P), lambda g: (g, 0, 0))],
        out_specs=(pl.BlockSpec((1, rows, T * GC), lambda g: (g, 0, 0)),
                   pl.BlockSpec((1, bsz, 4 * P), lambda g: (g, 0, 0))),
        scratch_shapes=[pltpu.VMEM((rows, 4 * P), jnp.float32)],
        compiler_params=pltpu.CompilerParams(dimension_semantics=("parallel",),
                                             vmem_limit_bytes=VMEM_LIMIT),
        name="s5_scan",
    )(x, kt, win, wout, apow, aimg, h0)
    y = jnp.transpose(y.reshape(G, nc, bsz, T, GC), (2, 1, 3, 0, 4)).reshape(bsz, n, G * GC)
    return y, hfin


def s5_branch(u, u_ctx, p):
    ops = s5_chunk_operators(p['s5_lam_re'], p['s5_lam_im'], p['s5_log_step'], p['s5_b_re'], p['s5_b_im'],
                             p['s5_c_re'], p['s5_c_im'])
    h0 = jnp.zeros((S5_GROUPS, u.shape[0], 4 * S5_STATE), jnp.float32)
    y_ctx, h_ctx = s5_scan_pallas(u_ctx, ops, h0)
    y_lat, _ = s5_scan_pallas(u, ops, h_ctx)

    def post(y, uu):
        y = y + p['s5_d'] * uu
        y = jax.nn.gelu(y, approximate=False)
        return y * jax.nn.sigmoid(mm(y, p['s5_glu_w']))

    return post(y_lat, u), post(y_ctx, u_ctx)


_NT = (((1,), (1,)), ((), ()))

_M_SAME, _M_EYE, _M_STRICT, _M_INCL, _M_LEVEL0 = 0, 1, 2, 4, 6
_N_LEVELS = 6


def rwkv_masks():
    C, n = RW_CHUNK, RW_LANES
    row = np.arange(n)[:, None]
    col = np.arange(n)[None, :]
    same = (row // C) == (col // C)
    out = [same, row == col]
    per_dir = []
    for reverse in (False, True):
        t, j = (row % C, col % C) if not reverse else (col % C, row % C)
        per_dir.append((same & (j < t), same & (j <= t),
                        [same & ((t // s) % 2 == 1) & ((j // s) == (t // s) - 1) for s in (1, 2, 4, 8, 16, 32)]))
    out += [per_dir[0][0], per_dir[1][0], per_dir[0][1], per_dir[1][1]]
    for l in range(_N_LEVELS):
        out += [per_dir[0][2][l], per_dir[1][2][l]]
    tt, jj = np.arange(C)[:, None], np.arange(C)[None, :]
    tri = np.stack([jj <= tt, jj >= tt])
    return jnp.asarray(np.stack(out), jnp.float32), jnp.asarray(tri, jnp.bfloat16)


def _bdot(a, b, dims=None):
    a = a.astype(jnp.bfloat16)
    b = b.astype(jnp.bfloat16)
    if dims is None:
        return jnp.dot(a, b, preferred_element_type=jnp.float32)
    return lax.dot_general(a, b, dims, preferred_element_type=jnp.float32)


def _rwkv_chunk(S, r, lw, k, v, kk, b, m_ref, tri, d):
    C = RW_CHUNK
    same = m_ref[_M_SAME]
    lw_hi = lw.astype(jnp.bfloat16)
    rem = lw - lw_hi.astype(jnp.float32)
    lw_mid = rem.astype(jnp.bfloat16)
    lw_lo = (rem - lw_mid.astype(jnp.float32)).astype(jnp.bfloat16)
    cl = (jnp.dot(tri, lw_hi, preferred_element_type=jnp.float32)
          + jnp.dot(tri, lw_mid, preferred_element_type=jnp.float32)
          + jnp.dot(tri, lw_lo, preferred_element_type=jnp.float32))
    tot = jnp.sum(lw, axis=0, keepdims=True)
    e_neg = jnp.exp(-cl)
    e_end = jnp.exp(tot - cl)
    a_t = -kk * jnp.exp(cl - lw)
    r_t = r * jnp.exp(cl)
    b_t = b * e_neg
    k_t = k * e_neg
    b_h = b * e_end
    k_h = k * e_end
    p_c = jnp.exp(tot)

    def tile(x):
        return jnp.concatenate([x] * RW_HG, axis=0)

    def stack(x):
        return tile(x) * same

    def unstack(z):
        acc = z[0:C]
        for h in range(1, RW_HG):
            acc = acc + z[h * C:(h + 1) * C]
        return acc

    n = RW_LANES
    lhs = jnp.concatenate([stack(a_t), stack(r_t)], axis=0)
    rhs = jnp.concatenate([tile(b_t), tile(k_t)], axis=0)
    amat = _bdot(lhs, rhs, _NT)
    a_ab = amat[:n, :n] * m_ref[_M_STRICT + d]
    a_ak = amat[:n, n:] * m_ref[_M_STRICT + d]
    a_rb = amat[n:, :n] * m_ref[_M_INCL + d]
    a_rk = amat[n:, n:] * m_ref[_M_INCL + d]
    dinv = m_ref[_M_EYE] + a_ab * m_ref[_M_LEVEL0 + d]
    for l in range(1, _N_LEVELS):
        dinv = dinv + _bdot(dinv, _bdot(a_ab * m_ref[_M_LEVEL0 + 2 * l + d], dinv))
    sv = stack(v)
    av = _bdot(jnp.concatenate([a_ak, a_rk], axis=0), sv)
    akv = unstack(av[:n])
    arkv = unstack(av[n:])
    mu = _bdot(dinv, jnp.concatenate([stack(a_t), stack(akv)], axis=1))
    m1 = unstack(mu[:, :n])
    u0 = unstack(mu[:, n:])
    my = _bdot(a_rb, jnp.concatenate([stack(m1), stack(u0)], axis=1))
    m2 = r_t + unstack(my[:, :n])
    y0 = unstack(my[:, n:]) + arkv
    mut = jnp.concatenate([m1, u0], axis=1).T
    gh = _bdot(mut, b_h)
    g = m_ref[_M_EYE] * p_c + gh[:n] * same
    hmat = (gh[n:] + _bdot(v.T, k_h)) * same
    y = _bdot(m2, S, _NT) + y0
    s_new = _bdot(S, g) + hmat
    return s_new, y


def _rwkv_kernel(m_ref, tri_ref, rf_ref, rb_ref, vf_ref, vb_ref, kkf_ref, kkb_ref, lwf_ref, kf_ref, bf_ref,
                 lwb_ref, kb_ref, bb_ref, s0_ref, yf_ref, yb_ref, sfin_ref, sf_scr, sb_scr, *, n_chunks):
    i = pl.program_id(2)
    C = RW_CHUNK

    @pl.when(i == 0)
    def _():
        sf_scr[...] = s0_ref[0, 0, 0]
        sb_scr[...] = s0_ref[0, 1, 0]

    tri_f = tri_ref[0]
    tri_b = tri_ref[1]

    def body(c, carry):
        rf = pl.ds(pl.multiple_of(c * C, C), C)
        rb = pl.ds(pl.multiple_of((n_chunks - 1 - c) * C, C), C)
        s_f, y_f = _rwkv_chunk(sf_scr[...], rf_ref[0, rf, :], lwf_ref[0, rf, :], kf_ref[0, rf, :],
                               vf_ref[0, rf, :], kkf_ref[0, rf, :], bf_ref[0, rf, :], m_ref, tri_f, 0)
        s_b, y_b = _rwkv_chunk(sb_scr[...], rb_ref[0, rb, :], lwb_ref[0, rb, :], kb_ref[0, rb, :],
                               vb_ref[0, rb, :], kkb_ref[0, rb, :], bb_ref[0, rb, :], m_ref, tri_b, 1)
        sf_scr[...] = s_f
        sb_scr[...] = s_b
        yf_ref[0, rf, :] = y_f
        yb_ref[0, rb, :] = y_b
        return carry

    lax.fori_loop(0, n_chunks, body, 0)

    @pl.when(i == pl.num_programs(2) - 1)
    def _():
        sfin_ref[0, 0, 0] = sf_scr[...]
        sfin_ref[0, 1, 0] = sb_scr[...]


def rwkv_scan_pallas(r, v, kk, lw, kd, bvec, s0, masks, tri, block_tokens):
    bsz, n, width = r.shape
    ng = width // RW_LANES
    tb = block_tokens
    nb = n // tb
    fwd = pl.BlockSpec((1, tb, RW_LANES), lambda b, g, i: (b, i, g))
    bwd = pl.BlockSpec((1, tb, RW_LANES), lambda b, g, i: (b, nb - 1 - i, g))
    state_spec = pl.BlockSpec((1, 2, 1, RW_LANES, RW_LANES), lambda b, g, i: (b, 0, g, 0, 0))
    return pl.pallas_call(
        functools.partial(_rwkv_kernel, n_chunks=tb // RW_CHUNK),
        out_shape=(jax.ShapeDtypeStruct((bsz, n, width), jnp.float32),
                   jax.ShapeDtypeStruct((bsz, n, width), jnp.float32),
                   jax.ShapeDtypeStruct(s0.shape, jnp.float32)),
        grid=(bsz, ng, nb),
        in_specs=[pl.BlockSpec(masks.shape, lambda b, g, i: (0, 0, 0)),
                  pl.BlockSpec(tri.shape, lambda b, g, i: (0, 0, 0)),
                  fwd, bwd, fwd, bwd, fwd, bwd,
                  fwd, fwd, fwd, bwd, bwd, bwd, state_spec],
        out_specs=(fwd, bwd, state_spec),
        scratch_shapes=[pltpu.VMEM((RW_LANES, RW_LANES), jnp.float32),
                        pltpu.VMEM((RW_LANES, RW_LANES), jnp.float32)],
        compiler_params=pltpu.CompilerParams(dimension_semantics=("parallel", "parallel", "arbitrary"),
                                             vmem_limit_bytes=VMEM_LIMIT),
        name="rwkv_scan",
    )(masks, tri, r, r, v, v, kk, kk, lw[0], kd[0], bvec[0], lw[1], kd[1], bvec[1], s0)


def rwkv_branch(rkv, lora, p, s0, masks, tri, block_tokens):
    bsz, n, _ = rkv.shape

    def heads(t):
        return t.reshape(bsz, n, RW_HEADS, RW_HEAD)

    def flat(t):
        return t.reshape(bsz, n, RW_WIDTH)

    r, k, v = jnp.split(rkv, 3, axis=-1)
    c1 = RW_DECAY_LORA
    c2 = 2 * RW_DECAY_LORA
    c3 = c2 + RW_ICLR_LORA
    c4 = c3 + RW_ICLR_LORA
    w_lo = (lora[..., :c1], lora[..., c1:c2])
    a_lo = (lora[..., c2:c3], lora[..., c3:c4])
    kk = heads(k * p['rw_kk'])
    kk = flat(kk * lax.rsqrt(jnp.maximum(jnp.sum(kk * kk, axis=-1, keepdims=True), 1e-24)))
    lw, kd, bvec = [], [], []
    bonus = 0.0
    for d in range(2):
        w_log = -jax.nn.softplus(-(p['rw_w0'][d] + mm(jnp.tanh(w_lo[d]), p['rw_w2'][d]))) - 0.5
        lw.append(-jnp.exp(w_log))
        a = jax.nn.sigmoid(p['rw_a0'][d] + mm(a_lo[d], p['rw_a2'][d]))
        kd.append(k * (1.0 + (a - 1.0) * p['rw_ka']))
        bvec.append(kk * a)
        bonus = bonus + jnp.sum(heads(r * p['rw_rk']) * heads(kd[d]), axis=-1, keepdims=True) * heads(v)
    y_f, y_b, s_fin = rwkv_scan_pallas(r, v, kk, lw, kd, bvec, s0, masks, tri, block_tokens)
    wkv = heads(y_f + y_b)
    mu = jnp.mean(wkv, axis=-1, keepdims=True)
    var = jnp.mean(jnp.square(wkv - mu), axis=-1, keepdims=True)
    o = flat((wkv - mu) * lax.rsqrt(var + RW_GN_EPS)) * p['rw_ln_w'] + p['rw_ln_b']
    o = o + flat(bonus)
    g = mm(jax.nn.sigmoid(lora[..., c4:]), p['rw_g2'])
    return o * g, s_fin


def hyena_filter_spectra(n_tok, p):
    hp = lax.Precision.HIGHEST
    bands = (HY_POS_DIM - 1) // 2
    t = jnp.linspace(0.0, 1.0, n_tok, dtype=jnp.float32)[:, None]
    w = (2.0 * math.pi / n_tok) * jnp.arange(n_tok, dtype=jnp.float32)[:, None]
    f = jnp.linspace(1e-4, bands - 1, bands, dtype=jnp.float32)[None, :]
    feats = jnp.concatenate([t, jnp.cos(f * w), -jnp.sin(f * w)], axis=-1)
    h = jnp.sin(p['hy_f_freq1'] * (jnp.dot(feats, p['hy_f_w1'], precision=hp) + p['hy_f_b1']))
    h = jnp.sin(p['hy_f_freq2'] * (jnp.dot(h, p['hy_f_w2'], precision=hp) + p['hy_f_b2']))
    h = jnp.dot(h, p['hy_f_w3'], precision=hp).reshape(n_tok, HY_ORDER, 2, HY_WIDTH)
    rates = jnp.abs(jnp.linspace(math.log(HY_DECAY_TARGET) / HY_DECAY_PCT_SHORT,
                                 math.log(HY_DECAY_TARGET) / HY_DECAY_PCT_LONG, HY_WIDTH, dtype=jnp.float32))
    h = h * jnp.exp(-t * rates)[:, None, None, :]
    h_fwd, h_bwd = h[:, :, 0], h[:, :, 1]
    filt = jnp.concatenate([h_fwd, jnp.zeros_like(h_fwd[:1]), h_bwd[:0:-1]], axis=0)
    return jnp.fft.rfft(filt, axis=0)


def long_conv(z, k_spec, bias):
    n = z.shape[1]
    z_spec = jnp.fft.rfft(z, n=2 * n, axis=1)
    y = jnp.fft.irfft(z_spec * k_spec[None], n=2 * n, axis=1)[:, :n]
    return y + z * bias


def hyena_branch(streams, p):
    n = streams.shape[1]
    v, x1, x2 = jnp.split(streams, 3, axis=-1)
    k_spec = hyena_filter_spectra(n, p)
    z = v
    for o, gate in enumerate((x1, x2)):
        z = gate * long_conv(z, k_spec[:, o], p['hy_bias'][o])
    return z


def merge_branches(y_s5, y_rw, y_hy, gates, p):
    g_s5, g_rw, g_hy = jnp.split(jax.nn.sigmoid(gates), N_BRANCH, axis=-1)
    m = g_s5 * mm(y_s5, p['br_s5']) + g_rw * mm(y_rw, p['br_rw']) + g_hy * mm(y_hy, p['br_hy'])
    return mm(m, p['out_w'])


def token_mixer(h_lat, h_ctx, p, with_ctx_out):
    bsz = h_lat.shape[0]
    cuts = [IN_S5, IN_S5 + IN_RW, IN_S5 + IN_RW + IN_LORA, IN_S5 + IN_RW + IN_LORA + IN_HY]
    edges = [0] + cuts + [IN_COLS]
    w_parts = [p['in_w'][:, edges[i]:edges[i + 1]] for i in range(5)]
    u_l, rkv_l, lora_l, hy_l, gate_l = [mm(h_lat, wp) for wp in w_parts]
    n_ctx_parts = 5 if with_ctx_out else 3
    ctx_parts = [mm(h_ctx, wp) for wp in w_parts[:n_ctx_parts]]
    masks, tri = rwkv_masks()
    zr = jnp.zeros((bsz, 2, RW_WIDTH // RW_LANES, RW_LANES, RW_LANES), jnp.float32)
    s5_l, s5_c = s5_branch(u_l, ctx_parts[0], p)
    rw_c, rw_state = rwkv_branch(short_conv(ctx_parts[1], p['rw_conv_w']), ctx_parts[2], p, zr, masks, tri,
                                 CTX_LEN)
    rw_l, _ = rwkv_branch(latent_short_conv(rkv_l, p['rw_conv_w']), lora_l, p, rw_state, masks, tri,
                          RW_BLOCK_TOKENS)
    hy_lat = hyena_branch(latent_short_conv(hy_l, p['hy_conv_w'], p['hy_conv_b']), p)
    y_lat = merge_branches(s5_l, rw_l, hy_lat, gate_l, p)
    if not with_ctx_out:
        return y_lat, None
    hy_ctx = hyena_branch(short_conv(ctx_parts[3], p['hy_conv_w'], p['hy_conv_b']), p)
    y_ctx = merge_branches(s5_c, rw_c, hy_ctx, ctx_parts[4], p)
    return y_lat, y_ctx


def _rms(x, gain):
    return x * lax.rsqrt(jnp.mean(x * x, axis=-1, keepdims=True) + NORM_EPS) * gain


def _ffn_kernel(x_ref, mod_ref, gain_ref, wg_ref, wu_ref, wd_ref, o_ref):
    x = x_ref[...]
    shift, scale, gate = mod_ref[0, 0:1, :], mod_ref[0, 1:2, :], mod_ref[0, 2:3, :]
    h = (_rms(x, gain_ref[0:1, :]) * (1.0 + scale) + shift).astype(jnp.bfloat16)
    g = jnp.dot(h, wg_ref[...], preferred_element_type=jnp.float32)
    u = jnp.dot(h, wu_ref[...], preferred_element_type=jnp.float32)
    a = (g * jax.nn.sigmoid(g) * u).astype(jnp.bfloat16)
    y = jnp.dot(a, wd_ref[...], preferred_element_type=jnp.float32)
    o_ref[...] = x + gate * _rms(y, gain_ref[1:2, :])


def ffn_block(x, mod, gains, wg, wu, wd):
    bsz, n, d = x.shape
    hid = wg.shape[1]
    tm = min(FFN_BLOCK_ROWS, n)
    per_b = n // tm
    xf = x.reshape(bsz * n, d)
    bmap = (lambda i: (i // per_b, 0, 0)) if mod.shape[0] == bsz else (lambda i: (0, 0, 0))
    out = pl.pallas_call(
        _ffn_kernel,
        out_shape=jax.ShapeDtypeStruct((bsz * n, d), jnp.float32),
        grid=(bsz * per_b,),
        in_specs=[pl.BlockSpec((tm, d), lambda i: (i, 0)),
                  pl.BlockSpec((1, 3, d), bmap),
                  pl.BlockSpec((2, d), lambda i: (0, 0)),
                  pl.BlockSpec((d, hid), lambda i: (0, 0)),
                  pl.BlockSpec((d, hid), lambda i: (0, 0)),
                  pl.BlockSpec((hid, d), lambda i: (0, 0))],
        out_specs=pl.BlockSpec((tm, d), lambda i: (i, 0)),
        compiler_params=pltpu.CompilerParams(dimension_semantics=("parallel",), vmem_limit_bytes=VMEM_LIMIT),
        name="ffn_block",
    )(xf, mod, gains, wg.astype(jnp.bfloat16), wu.astype(jnp.bfloat16), wd.astype(jnp.bfloat16))
    return out.reshape(bsz, n, d)


def _moe_kernel(be_ref, x_ref, gate_ref, wg_ref, wu_ref, wd_ref, o_ref):
    j = pl.program_id(1)
    xb = x_ref[...].astype(jnp.bfloat16)
    g = jnp.dot(xb, wg_ref[0], preferred_element_type=jnp.float32)
    u = jnp.dot(xb, wu_ref[0], preferred_element_type=jnp.float32)
    a = (g * jax.nn.sigmoid(g) * u).astype(jnp.bfloat16)
    y = jnp.dot(a, wd_ref[0], preferred_element_type=jnp.float32)

    @pl.when(j == 0)
    def _():
        o_ref[...] = y

    @pl.when(j > 0)
    def _():
        o_ref[...] += y

    @pl.when(j == pl.num_programs(1) - 1)
    def _():
        o_ref[...] *= gate_ref[...]


def moe_expert_blocks(xb, slot_gate, block_e, wg, wu, wd):
    cap, d = xb.shape
    n_blocks = cap // MOE_BLOCK
    hid = wg.shape[-1]
    nh = MOE_HIDDEN_TILES
    th = hid // nh

    def hidx(i, j):
        return j + (i % 2) * (nh - 1 - 2 * j)

    return pl.pallas_call(
        _moe_kernel,
        out_shape=jax.ShapeDtypeStruct((cap, d), jnp.float32),
        grid_spec=pltpu.PrefetchScalarGridSpec(
            num_scalar_prefetch=1,
            grid=(n_blocks, nh),
            in_specs=[pl.BlockSpec((MOE_BLOCK, d), lambda i, j, be: (i, 0)),
                      pl.BlockSpec((MOE_BLOCK, 1), lambda i, j, be: (i, 0)),
                      pl.BlockSpec((1, d, th), lambda i, j, be: (be[i], 0, hidx(i, j))),
                      pl.BlockSpec((1, d, th), lambda i, j, be: (be[i], 0, hidx(i, j))),
                      pl.BlockSpec((1, th, d), lambda i, j, be: (be[i], hidx(i, j), 0))],
            out_specs=pl.BlockSpec((MOE_BLOCK, d), lambda i, j, be: (i, 0))),
        compiler_params=pltpu.CompilerParams(dimension_semantics=("parallel", "arbitrary"),
                                             vmem_limit_bytes=VMEM_LIMIT),
        name="moe_experts",
    )(block_e, xb, slot_gate.reshape(cap, 1), wg.astype(jnp.bfloat16), wu.astype(jnp.bfloat16),
      wd.astype(jnp.bfloat16))


def moe_swiglu(h, router_w, wg, wu, wd):
    d_model = h.shape[-1]
    tok = h.reshape(-1, d_model)
    n = tok.shape[0]
    n_assign = n * TOP_K
    logits = jnp.dot(tok, router_w)
    top_logit, top_e = lax.top_k(logits, TOP_K)
    gate = jax.nn.softmax(top_logit, axis=-1).reshape(-1)
    flat_e = top_e.reshape(-1)
    order = jnp.argsort(flat_e)
    sorted_e = flat_e[order]
    sizes = jnp.zeros((N_EXPERTS,), jnp.int32).at[flat_e].add(1)
    padded = (sizes + MOE_BLOCK - 1) // MOE_BLOCK * MOE_BLOCK
    pad_end = jnp.cumsum(padded)
    pad_start = pad_end - padded
    grp_start = jnp.cumsum(sizes) - sizes
    slot = pad_start[sorted_e] + jnp.arange(n_assign, dtype=jnp.int32) - grp_start[sorted_e]
    n_blocks = -(-n_assign // MOE_BLOCK) + N_EXPERTS
    cap = n_blocks * MOE_BLOCK
    slot_tok = jnp.full((cap,), n, jnp.int32).at[slot].set((order // TOP_K).astype(jnp.int32))
    slot_gate = jnp.zeros((cap,), jnp.float32).at[slot].set(gate[order])
    block_start = jnp.arange(n_blocks, dtype=jnp.int32) * MOE_BLOCK
    block_e = jnp.minimum(jnp.sum(block_start[:, None] >= pad_end[None, :], axis=1), N_EXPERTS - 1)
    tok_pad = jnp.concatenate([tok, jnp.zeros((1, d_model), tok.dtype)], axis=0)
    xb = tok_pad[slot_tok]
    yb = moe_expert_blocks(xb, slot_gate, block_e.astype(jnp.int32), wg, wu, wd)
    pos = jnp.zeros((n_assign,), jnp.int32).at[order].set(slot).reshape(n, TOP_K)
    out = yb[pos[:, 0]]
    for kk in range(1, TOP_K):
        out = out + yb[pos[:, kk]]
    return out.reshape(h.shape)


def kernel(x, c, ctx, c_ctx, mod_w, mod_b, norm_g, in_w, s5_lam_re, s5_lam_im, s5_log_step, s5_b_re, s5_b_im, s5_c_re, s5_c_im, s5_d, s5_glu_w, rw_conv_w, rw_w0, rw_w2, rw_a0, rw_a2, rw_g2, rw_kk, rw_ka, rw_rk, rw_ln_w, rw_ln_b, hy_conv_w, hy_conv_b, hy_f_w1, hy_f_b1, hy_f_freq1, hy_f_w2, hy_f_b2, hy_f_freq2, hy_f_w3, hy_bias, br_s5, br_rw, br_hy, out_w, ffn_wg, ffn_wu, ffn_wd, moe_router, moe_wg, moe_wu, moe_wd):
    silu_c = jax.nn.silu(c)
    silu_cc = jax.nn.silu(c_ctx)
    for i in range(DEPTH):
        last = i == DEPTH - 1
        p = {
            'in_w': in_w[i],
            's5_lam_re': s5_lam_re[i], 's5_lam_im': s5_lam_im[i], 's5_log_step': s5_log_step[i],
            's5_b_re': s5_b_re[i], 's5_b_im': s5_b_im[i], 's5_c_re': s5_c_re[i], 's5_c_im': s5_c_im[i],
            's5_d': s5_d[i], 's5_glu_w': s5_glu_w[i],
            'rw_conv_w': rw_conv_w[i], 'rw_w0': rw_w0[i], 'rw_w2': rw_w2[i], 'rw_a0': rw_a0[i], 'rw_a2': rw_a2[i],
            'rw_g2': rw_g2[i], 'rw_kk': rw_kk[i], 'rw_ka': rw_ka[i], 'rw_rk': rw_rk[i],
            'rw_ln_w': rw_ln_w[i], 'rw_ln_b': rw_ln_b[i],
            'hy_conv_w': hy_conv_w[i], 'hy_conv_b': hy_conv_b[i], 'hy_f_w1': hy_f_w1[i], 'hy_f_b1': hy_f_b1[i],
            'hy_f_freq1': hy_f_freq1[i], 'hy_f_w2': hy_f_w2[i], 'hy_f_b2': hy_f_b2[i], 'hy_f_freq2': hy_f_freq2[i],
            'hy_f_w3': hy_f_w3[i], 'hy_bias': hy_bias[i],
            'br_s5': br_s5[i], 'br_rw': br_rw[i], 'br_hy': br_hy[i], 'out_w': out_w[i],
        }
        ml = jnp.split((silu_c @ mod_w[i] + mod_b[i])[:, None, :], 6, axis=-1)
        mc = jnp.split(silu_cc @ mod_w[i] + mod_b[i], 6, axis=-1)
        g_pre_m, g_post_m, g_pre_f, g_post_f = norm_g[i]
        if i % 2 == 0:
            def channel_mix(t, m, j=i // 2):
                return ffn_block(t, m, norm_g[i, 2:4], ffn_wg[j], ffn_wu[j], ffn_wd[j])
        else:
            def channel_mix(t, m, j=i // 2):
                h = rms_norm(t, g_pre_f) * (1 + m[:, 1:2]) + m[:, 0:1]
                y = moe_swiglu(h, moe_router[j], moe_wg[j], moe_wu[j], moe_wd[j])
                return t + m[:, 2:3] * rms_norm(y, g_post_f)
        h_l = rms_norm(x, g_pre_m) * (1 + ml[1]) + ml[0]
        h_c = rms_norm(ctx, g_pre_m) * (1 + mc[1]) + mc[0]
        y_l, y_c = token_mixer(h_l, h_c, p, not last)
        x = x + ml[2] * rms_norm(y_l, g_post_m)
        x = channel_mix(x, jnp.concatenate(ml[3:6], axis=1))
        if not last:
            ctx = ctx + mc[2] * rms_norm(y_c, g_post_m)
            ctx = channel_mix(ctx, jnp.stack(mc[3:6])[None])
    return x
```

```python
import math
import functools
import numpy as np
import jax
import jax.numpy as jnp
from jax import lax
from jax.experimental import pallas as pl
from jax.experimental.pallas import tpu as pltpu

D_MODEL = 1024
BATCH = 8
SEQ = 4096
DEPTH = 2
GRID_W = 64
CTX_LEN = 256
NORM_EPS = 1e-6
S5_WIDTH = D_MODEL // 4
S5_GROUP = 16
S5_GROUPS = S5_WIDTH // S5_GROUP
S5_STATE = 64
RW_WIDTH = D_MODEL // 2
RW_HEAD = 64
RW_HEADS = RW_WIDTH // RW_HEAD
RW_DECAY_LORA = 64
RW_ICLR_LORA = 64
RW_GATE_LORA = 128
RW_GN_EPS = 64e-5
HY_WIDTH = D_MODEL // 4
HY_ORDER = 2
HY_POS_DIM = 33
HY_FILTER_HIDDEN = 64
HY_DECAY_TARGET = 1e-2
HY_DECAY_PCT_SHORT = 0.3
HY_DECAY_PCT_LONG = 1.5
SHORT_CONV = 3
N_BRANCH = 3
FFN_HIDDEN = 2816
N_EXPERTS = 8
TOP_K = 2
EXPERT_HIDDEN = 3584
MOE_BLOCK = 256
IN_S5 = S5_WIDTH
IN_RW = 3 * RW_WIDTH
IN_LORA = 2 * RW_DECAY_LORA + 2 * RW_ICLR_LORA + RW_GATE_LORA
IN_HY = (HY_ORDER + 1) * HY_WIDTH
IN_GATE = N_BRANCH * D_MODEL
IN_COLS = IN_S5 + IN_RW + IN_LORA + IN_HY + IN_GATE

VMEM_LIMIT = 56 * 1024 * 1024
S5_CHUNK = 64
RW_CHUNK = 64
RW_HG = 4
RW_LANES = RW_HG * RW_HEAD
RW_BLOCK_TOKENS = 512
MOE_HIDDEN_TILES = 2
FFN_BLOCK_ROWS = 256


def _mm_kernel(a_ref, b_ref, o_ref):
    o_ref[...] = jnp.dot(a_ref[...].astype(jnp.bfloat16), b_ref[...],
                         preferred_element_type=jnp.float32)


def _pick(n, cands):
    for c in cands:
        if n % c == 0:
            return c
    return n


def pmatmul(a, b):
    m, k = a.shape
    n = b.shape[1]
    b = b.astype(jnp.bfloat16)
    tm = _pick(m, (512, 256, 128, 8))
    tn = _pick(n, (1024, 768, 512, 384, 256, 128))
    return pl.pallas_call(
        _mm_kernel,
        out_shape=jax.ShapeDtypeStruct((m, n), jnp.float32),
        grid=(n // tn, m // tm),
        in_specs=[pl.BlockSpec((tm, k), lambda j, i: (i, 0)),
                  pl.BlockSpec((k, tn), lambda j, i: (0, j))],
        out_specs=pl.BlockSpec((tm, tn), lambda j, i: (i, j)),
        compiler_params=pltpu.CompilerParams(
            dimension_semantics=("parallel", "parallel"),
            vmem_limit_bytes=VMEM_LIMIT),
        name="mm",
    )(a, b)


def mm(a, b):
    lead = a.shape[:-1]
    return pmatmul(a.reshape(-1, a.shape[-1]), b).reshape(lead + (b.shape[1],))


def rms_norm(x, gain):
    y = x * lax.rsqrt(jnp.mean(x * x, axis=-1, keepdims=True) + NORM_EPS)
    return y * gain


def short_conv(x, w, b=None):
    half = SHORT_CONV // 2
    n = x.shape[1]
    xp = jnp.pad(x, ((0, 0), (half, half), (0, 0)))
    y = sum(xp[:, j:j + n] * w[j] for j in range(SHORT_CONV))
    return y if b is None else y + b


def latent_short_conv(x, w, b=None):
    bsz, n_tok, ch = x.shape
    rows = n_tok // GRID_W
    y = short_conv(x.reshape(bsz * rows, GRID_W, ch), w, b)
    return y.reshape(bsz, n_tok, ch)


def s5_chunk_operators(lam_re, lam_im, log_step, b_re, b_im, c_re, c_im):
    T, G, P, GC = S5_CHUNK, S5_GROUPS, S5_STATE, S5_GROUP
    hp = lax.Precision.HIGHEST
    step = jnp.exp(log_step)[..., None]
    lr, li = lam_re, lam_im
    tau = jnp.arange(T + 1, dtype=jnp.float32)[:, None, None, None]
    mag = jnp.exp(lr * step * tau)
    ang = li * step * tau
    e_re, e_im = mag * jnp.cos(ang), mag * jnp.sin(ang)
    ab_re, ab_im = e_re[1], e_im[1]
    den = lr * lr + li * li
    nr, ni = ab_re - 1.0, ab_im
    q_re = (nr * lr + ni * li) / den
    q_im = (ni * lr - nr * li) / den
    bb_re = q_re[..., None] * b_re - q_im[..., None] * b_im
    bb_im = q_re[..., None] * b_im + q_im[..., None] * b_re
    eb_re = e_re[..., None] * bb_re - e_im[..., None] * bb_im
    eb_im = e_re[..., None] * bb_im + e_im[..., None] * bb_re
    ktau = (jnp.einsum('dgip,tdgpj->tdgij', c_re, eb_re[:T], precision=hp)
            - jnp.einsum('dgip,tdgpj->tdgij', c_im, eb_im[:T], precision=hp))
    t_idx = jnp.arange(T)
    lag = t_idx[:, None] - t_idx[None, :]
    kf = jnp.where((lag >= 0)[:, :, None, None, None], ktau[jnp.clip(lag, 0, T - 1), 0], 0.0)
    kb = jnp.where((lag <= 0)[:, :, None, None, None], ktau[jnp.clip(-lag, 0, T - 1), 1], 0.0)
    kt = jnp.transpose(kf + kb, (2, 1, 4, 0, 3)).reshape(G, T * GC, T * GC)
    wf_re, wf_im = eb_re[T - 1 - t_idx, 0], eb_im[T - 1 - t_idx, 0]
    wb_re, wb_im = eb_re[t_idx, 1], eb_im[t_idx, 1]
    win = jnp.stack([wf_re, wf_im, wb_re, wb_im], axis=0)
    win = jnp.transpose(win, (2, 1, 4, 0, 3)).reshape(G, T * GC, 4 * P)
    ef_re, ef_im = e_re[t_idx + 1, 0], e_im[t_idx + 1, 0]
    eb2_re, eb2_im = e_re[T - t_idx, 1], e_im[T - t_idx, 1]

    def readout(cr, ci, er, ei):
        re = cr[None] * er[:, :, None, :] - ci[None] * ei[:, :, None, :]
        im = -(cr[None] * ei[:, :, None, :] + ci[None] * er[:, :, None, :])
        return re, im

    of_re, of_im = readout(c_re[0], c_im[0], ef_re, ef_im)
    ob_re, ob_im = readout(c_re[1], c_im[1], eb2_re, eb2_im)
    wout = jnp.stack([of_re, of_im, ob_re, ob_im], axis=0)
    wout = jnp.transpose(wout, (2, 0, 4, 1, 3)).reshape(G, 4 * P, T * GC)
    at_re, at_im = e_re[T], e_im[T]
    apow = jnp.concatenate([at_re[0], at_re[0], at_re[1], at_re[1]], axis=-1)[:, None, :]
    aimg = jnp.concatenate([-at_im[0], at_im[0], -at_im[1], at_im[1]], axis=-1)[:, None, :]
    return kt.astype(jnp.bfloat16), win.astype(jnp.bfloat16), wout.astype(jnp.bfloat16), apow, aimg


def _s5_kernel(x_ref, kt_ref, win_ref, wout_ref, apow_ref, aimg_ref, h0_ref, y_ref, hfin_ref, hin_ref, *,
               n_chunks, bsz):
    P2 = 2 * S5_STATE
    xb = x_ref[0].astype(jnp.bfloat16)
    hin_ref[...] = jnp.dot(xb, win_ref[0], preferred_element_type=jnp.float32)
    ap = apow_ref[0]
    ai = aimg_ref[0]
    apf, aif = ap[:, :P2], ai[:, :P2]
    apb, aib = ap[:, P2:], ai[:, P2:]

    def cmul(h, a_p, a_i):
        return h * a_p + pltpu.roll(h, S5_STATE, axis=1) * a_i

    def body(c, carry):
        hf, hb = carry
        rf = pl.ds(pl.multiple_of(c * bsz, bsz), bsz)
        rb = pl.ds(pl.multiple_of((n_chunks - 1 - c) * bsz, bsz), bsz)
        df = hin_ref[rf, :P2]
        db = hin_ref[rb, P2:]
        hin_ref[rf, :P2] = hf
        hin_ref[rb, P2:] = hb
        return cmul(hf, apf, aif) + df, cmul(hb, apb, aib) + db

    h0 = h0_ref[0]
    hf, hb = lax.fori_loop(0, n_chunks, body, (h0[:, :P2], h0[:, P2:]))
    hfin_ref[0, :, :P2] = hf
    hfin_ref[0, :, P2:] = hb
    y = jnp.dot(xb, kt_ref[0], preferred_element_type=jnp.float32)
    y = y + jnp.dot(hin_ref[...].astype(jnp.bfloat16), wout_ref[0], preferred_element_type=jnp.float32)
    y_ref[0] = y


def s5_scan_pallas(u, ops, h0):
    kt, win, wout, apow, aimg = ops
    bsz, n, _ = u.shape
    T, G, P, GC = S5_CHUNK, S5_GROUPS, S5_STATE, S5_GROUP
    nc = n // T
    rows = nc * bsz
    x = jnp.transpose(u.reshape(bsz, nc, T, G, GC), (3, 1, 0, 2, 4)).reshape(G, rows, T * GC)
    y, hfin = pl.pallas_call(
        functools.partial(_s5_kernel, n_chunks=nc, bsz=bsz),
        out_shape=(jax.ShapeDtypeStruct((G, rows, T * GC), jnp.float32),
                   jax.ShapeDtypeStruct((G, bsz, 4 * P), jnp.float32)),
        grid=(G,),
        in_specs=[pl.BlockSpec((1, rows, T * GC), lambda g: (g, 0, 0)),
                  pl.BlockSpec((1, T * GC, T * GC), lambda g: (g, 0, 0)),
                  pl.BlockSpec((1, T * GC, 4 * P), lambda g: (g, 0, 0)),
                  pl.BlockSpec((1, 4 * P, T * GC), lambda g: (g, 0, 0)),
                  pl.BlockSpec((1, 1, 4 * P), lambda g: (g, 0, 0)),
                  pl.BlockSpec((1, 1, 4 * P), lambda g: (g, 0, 0)),
                  pl.BlockSpec((1, bsz, 4 * P), lambda g: (g, 0, 0))],
        out_specs=(pl.BlockSpec((1, rows, T * GC), lambda g: (g, 0, 0)),
                   pl.BlockSpec((1, bsz, 4 * P), lambda g: (g, 0, 0))),
        scratch_shapes=[pltpu.VMEM((rows, 4 * P), jnp.float32)],
        compiler_params=pltpu.CompilerParams(dimension_semantics=("parallel",),
                                             vmem_limit_bytes=VMEM_LIMIT),
        name="s5_scan",
    )(x, kt, win, wout, apow, aimg, h0)
    y = jnp.transpose(y.reshape(G, nc, bsz, T, GC), (2, 1, 3, 0, 4)).reshape(bsz, n, G * GC)
    return y, hfin


def s5_branch(u, u_ctx, p):
    ops = s5_chunk_operators(p['s5_lam_re'], p['s5_lam_im'], p['s5_log_step'], p['s5_b_re'], p['s5_b_im'],
                             p['s5_c_re'], p['s5_c_im'])
    h0 = jnp.zeros((S5_GROUPS, u.shape[0], 4 * S5_STATE), jnp.float32)
    y_ctx, h_ctx = s5_scan_pallas(u_ctx, ops, h0)
    y_lat, _ = s5_scan_pallas(u, ops, h_ctx)

    def post(y, uu):
        y = y + p['s5_d'] * uu
        y = jax.nn.gelu(y, approximate=False)
        return y * jax.nn.sigmoid(mm(y, p['s5_glu_w']))

    return post(y_lat, u), post(y_ctx, u_ctx)


_NT = (((1,), (1,)), ((), ()))

_M_SAME, _M_EYE, _M_STRICT, _M_INCL, _M_LEVEL0 = 0, 1, 2, 4, 6
_N_LEVELS = 6


def rwkv_masks():
    C, n = RW_CHUNK, RW_LANES
    row = np.arange(n)[:, None]
    col = np.arange(n)[None, :]
    same = (row // C) == (col // C)
    out = [same, row == col]
    per_dir = []
    for reverse in (False, True):
        t, j = (row % C, col % C) if not reverse else (col % C, row % C)
        per_dir.append((same & (j < t), same & (j <= t),
                        [same & ((t // s) % 2 == 1) & ((j // s) == (t // s) - 1) for s in (1, 2, 4, 8, 16, 32)]))
    out += [per_dir[0][0], per_dir[1][0], per_dir[0][1], per_dir[1][1]]
    for l in range(_N_LEVELS):
        out += [per_dir[0][2][l], per_dir[1][2][l]]
    tt, jj = np.arange(C)[:, None], np.arange(C)[None, :]
    tri = np.stack([jj <= tt, jj >= tt])
    return jnp.asarray(np.stack(out), jnp.float32), jnp.asarray(tri, jnp.bfloat16)


def _bdot(a, b, dims=None):
    a = a.astype(jnp.bfloat16)
    b = b.astype(jnp.bfloat16)
    if dims is None:
        return jnp.dot(a, b, preferred_element_type=jnp.float32)
    return lax.dot_general(a, b, dims, preferred_element_type=jnp.float32)


def _rwkv_chunk(S, r, lw, k, v, kk, b, m_ref, tri, d):
    C = RW_CHUNK
    same = m_ref[_M_SAME]
    lw_hi = lw.astype(jnp.bfloat16)
    rem = lw - lw_hi.astype(jnp.float32)
    lw_mid = rem.astype(jnp.bfloat16)
    lw_lo = (rem - lw_mid.astype(jnp.float32)).astype(jnp.bfloat16)
    cl = (jnp.dot(tri, lw_hi, preferred_element_type=jnp.float32)
          + jnp.dot(tri, lw_mid, preferred_element_type=jnp.float32)
          + jnp.dot(tri, lw_lo, preferred_element_type=jnp.float32))
    yield
    tot = jnp.sum(lw, axis=0, keepdims=True)
    e_neg = jnp.exp(-cl)
    e_end = jnp.exp(tot - cl)
    a_t = -kk * jnp.exp(cl - lw)
    r_t = r * jnp.exp(cl)
    b_t = b * e_neg
    k_t = k * e_neg
    b_h = b * e_end
    k_h = k * e_end
    p_c = jnp.exp(tot)

    def tile(x):
        return jnp.concatenate([x] * RW_HG, axis=0)

    def stack(x):
        return tile(x) * same

    def unstack(z):
        acc = z[0:C]
        for h in range(1, RW_HG):
            acc = acc + z[h * C:(h + 1) * C]
        return acc

    n = RW_LANES
    lhs = jnp.concatenate([stack(a_t), stack(r_t)], axis=0)
    rhs = jnp.concatenate([tile(b_t), tile(k_t)], axis=0)
    amat = _bdot(lhs, rhs, _NT)
    yield
    a_ab = amat[:n, :n] * m_ref[_M_STRICT + d]
    a_ak = amat[:n, n:] * m_ref[_M_STRICT + d]
    a_rb = amat[n:, :n] * m_ref[_M_INCL + d]
    a_rk = amat[n:, n:] * m_ref[_M_INCL + d]
    dinv = m_ref[_M_EYE] + a_ab * m_ref[_M_LEVEL0 + d]
    for l in range(1, _N_LEVELS):
        inner = _bdot(a_ab * m_ref[_M_LEVEL0 + 2 * l + d], dinv)
        yield
        dinv = dinv + _bdot(dinv, inner)
        yield
    sv = stack(v)
    av = _bdot(jnp.concatenate([a_ak, a_rk], axis=0), sv)
    yield
    akv = unstack(av[:n])
    arkv = unstack(av[n:])
    mu = _bdot(dinv, jnp.concatenate([stack(a_t), stack(akv)], axis=1))
    yield
    m1 = unstack(mu[:, :n])
    u0 = unstack(mu[:, n:])
    my = _bdot(a_rb, jnp.concatenate([stack(m1), stack(u0)], axis=1))
    yield
    m2 = r_t + unstack(my[:, :n])
    y0 = unstack(my[:, n:]) + arkv
    mut = jnp.concatenate([m1, u0], axis=1).T
    gh = _bdot(mut, b_h)
    yield
    g = m_ref[_M_EYE] * p_c + gh[:n] * same
    hmat = (gh[n:] + _bdot(v.T, k_h)) * same
    y = _bdot(m2, S, _NT) + y0
    s_new = _bdot(S, g) + hmat
    return s_new, y


def _lockstep(gens):
    results = [None] * len(gens)
    active = list(range(len(gens)))
    while active:
        for i in list(active):
            try:
                next(gens[i])
            except StopIteration as stop:
                results[i] = stop.value
                active.remove(i)
    return results


def _rwkv_kernel(m_ref, tri_ref, rf_ref, rb_ref, vf_ref, vb_ref, kkf_ref, kkb_ref, lwf_ref, kf_ref, bf_ref,
                 lwb_ref, kb_ref, bb_ref, s0_ref, yf_ref, yb_ref, sfin_ref, s_scr, *, n_chunks, n_groups):
    i = pl.program_id(1)
    C = RW_CHUNK

    @pl.when(i == 0)
    def _():
        s_scr[...] = s0_ref[0]

    tri_f = tri_ref[0]
    tri_b = tri_ref[1]

    def body(c, carry):
        rf = pl.ds(pl.multiple_of(c * C, C), C)
        rb = pl.ds(pl.multiple_of((n_chunks - 1 - c) * C, C), C)
        gens = []
        for g in range(n_groups):
            ln = slice(g * RW_LANES, (g + 1) * RW_LANES)
            gens.append(_rwkv_chunk(s_scr[0, g], rf_ref[0, rf, ln], lwf_ref[0, rf, ln], kf_ref[0, rf, ln],
                                    vf_ref[0, rf, ln], kkf_ref[0, rf, ln], bf_ref[0, rf, ln], m_ref, tri_f, 0))
            gens.append(_rwkv_chunk(s_scr[1, g], rb_ref[0, rb, ln], lwb_ref[0, rb, ln], kb_ref[0, rb, ln],
                                    vb_ref[0, rb, ln], kkb_ref[0, rb, ln], bb_ref[0, rb, ln], m_ref, tri_b, 1))
        out = _lockstep(gens)
        for g in range(n_groups):
            ln = slice(g * RW_LANES, (g + 1) * RW_LANES)
            (s_f, y_f), (s_b, y_b) = out[2 * g], out[2 * g + 1]
            s_scr[0, g] = s_f
            s_scr[1, g] = s_b
            yf_ref[0, rf, ln] = y_f
            yb_ref[0, rb, ln] = y_b
        return carry

    lax.fori_loop(0, n_chunks, body, 0)

    @pl.when(i == pl.num_programs(1) - 1)
    def _():
        sfin_ref[0] = s_scr[...]


def rwkv_scan_pallas(r, v, kk, lw, kd, bvec, s0, masks, tri, block_tokens):
    bsz, n, width = r.shape
    ng = width // RW_LANES
    tb = block_tokens
    nb = n // tb
    fwd = pl.BlockSpec((1, tb, width), lambda b, i: (b, i, 0))
    bwd = pl.BlockSpec((1, tb, width), lambda b, i: (b, nb - 1 - i, 0))
    state_spec = pl.BlockSpec((1, 2, ng, RW_LANES, RW_LANES), lambda b, i: (b, 0, 0, 0, 0))
    return pl.pallas_call(
        functools.partial(_rwkv_kernel, n_chunks=tb // RW_CHUNK, n_groups=ng),
        out_shape=(jax.ShapeDtypeStruct((bsz, n, width), jnp.float32),
                   jax.ShapeDtypeStruct((bsz, n, width), jnp.float32),
                   jax.ShapeDtypeStruct(s0.shape, jnp.float32)),
        grid=(bsz, nb),
        in_specs=[pl.BlockSpec(masks.shape, lambda b, i: (0, 0, 0)),
                  pl.BlockSpec(tri.shape, lambda b, i: (0, 0, 0)),
                  fwd, bwd, fwd, bwd, fwd, bwd,
                  fwd, fwd, fwd, bwd, bwd, bwd, state_spec],
        out_specs=(fwd, bwd, state_spec),
        scratch_shapes=[pltpu.VMEM((2, ng, RW_LANES, RW_LANES), jnp.float32)],
        compiler_params=pltpu.CompilerParams(dimension_semantics=("parallel", "arbitrary"),
                                             vmem_limit_bytes=VMEM_LIMIT),
        name="rwkv_scan",
    )(masks, tri, r, r, v, v, kk, kk, lw[0], kd[0], bvec[0], lw[1], kd[1], bvec[1], s0)


def rwkv_branch(rkv, lora, p, s0, masks, tri, block_tokens):
    bsz, n, _ = rkv.shape

    def heads(t):
        return t.reshape(bsz, n, RW_HEADS, RW_HEAD)

    def flat(t):
        return t.reshape(bsz, n, RW_WIDTH)

    r, k, v = jnp.split(rkv, 3, axis=-1)
    c1 = RW_DECAY_LORA
    c2 = 2 * RW_DECAY_LORA
    c3 = c2 + RW_ICLR_LORA
    c4 = c3 + RW_ICLR_LORA
    w_lo = (lora[..., :c1], lora[..., c1:c2])
    a_lo = (lora[..., c2:c3], lora[..., c3:c4])
    kk = heads(k * p['rw_kk'])
    kk = flat(kk * lax.rsqrt(jnp.maximum(jnp.sum(kk * kk, axis=-1, keepdims=True), 1e-24)))
    lw, kd, bvec = [], [], []
    bonus = 0.0
    for d in range(2):
        w_log = -jax.nn.softplus(-(p['rw_w0'][d] + mm(jnp.tanh(w_lo[d]), p['rw_w2'][d]))) - 0.5
        lw.append(-jnp.exp(w_log))
        a = jax.nn.sigmoid(p['rw_a0'][d] + mm(a_lo[d], p['rw_a2'][d]))
        kd.append(k * (1.0 + (a - 1.0) * p['rw_ka']))
        bvec.append(kk * a)
        bonus = bonus + jnp.sum(heads(r * p['rw_rk']) * heads(kd[d]), axis=-1, keepdims=True) * heads(v)
    y_f, y_b, s_fin = rwkv_scan_pallas(r, v, kk, lw, kd, bvec, s0, masks, tri, block_tokens)
    wkv = heads(y_f + y_b)
    mu = jnp.mean(wkv, axis=-1, keepdims=True)
    var = jnp.mean(jnp.square(wkv - mu), axis=-1, keepdims=True)
    o = flat((wkv - mu) * lax.rsqrt(var + RW_GN_EPS)) * p['rw_ln_w'] + p['rw_ln_b']
    o = o + flat(bonus)
    g = mm(jax.nn.sigmoid(lora[..., c4:]), p['rw_g2'])
    return o * g, s_fin


def hyena_filter_spectra(n_tok, p):
    hp = lax.Precision.HIGHEST
    bands = (HY_POS_DIM - 1) // 2
    t = jnp.linspace(0.0, 1.0, n_tok, dtype=jnp.float32)[:, None]
    w = (2.0 * math.pi / n_tok) * jnp.arange(n_tok, dtype=jnp.float32)[:, None]
    f = jnp.linspace(1e-4, bands - 1, bands, dtype=jnp.float32)[None, :]
    feats = jnp.concatenate([t, jnp.cos(f * w), -jnp.sin(f * w)], axis=-1)
    h = jnp.sin(p['hy_f_freq1'] * (jnp.dot(feats, p['hy_f_w1'], precision=hp) + p['hy_f_b1']))
    h = jnp.sin(p['hy_f_freq2'] * (jnp.dot(h, p['hy_f_w2'], precision=hp) + p['hy_f_b2']))
    h = jnp.dot(h, p['hy_f_w3'], precision=hp).reshape(n_tok, HY_ORDER, 2, HY_WIDTH)
    rates = jnp.abs(jnp.linspace(math.log(HY_DECAY_TARGET) / HY_DECAY_PCT_SHORT,
                                 math.log(HY_DECAY_TARGET) / HY_DECAY_PCT_LONG, HY_WIDTH, dtype=jnp.float32))
    h = h * jnp.exp(-t * rates)[:, None, None, :]
    h_fwd, h_bwd = h[:, :, 0], h[:, :, 1]
    filt = jnp.concatenate([h_fwd, jnp.zeros_like(h_fwd[:1]), h_bwd[:0:-1]], axis=0)
    return jnp.fft.rfft(filt, axis=0)


def long_conv(z, k_spec, bias):
    n = z.shape[1]
    z_spec = jnp.fft.rfft(z, n=2 * n, axis=1)
    y = jnp.fft.irfft(z_spec * k_spec[None], n=2 * n, axis=1)[:, :n]
    return y + z * bias


def hyena_branch(streams, p):
    n = streams.shape[1]
    v, x1, x2 = jnp.split(streams, 3, axis=-1)
    k_spec = hyena_filter_spectra(n, p)
    z = v
    for o, gate in enumerate((x1, x2)):
        z = gate * long_conv(z, k_spec[:, o], p['hy_bias'][o])
    return z


HY_N = 2 * SEQ
HY_N1 = 64
HY_N2 = 128
HY_NH = HY_N1 // 2
HY_PITCH = 136
HY_LANES = 128


def hyena_dft_tables():
    k1 = np.arange(HY_N1)[:, None]
    n1 = np.arange(HY_NH)[None, :]
    n2 = np.arange(HY_N2)
    ph = -2 * np.pi * (k1 * n1 / HY_N1)[None] - 2 * np.pi * (n2[:, None, None] * k1[None] / HY_N)
    lhs1 = np.concatenate([np.cos(ph), np.sin(ph)], axis=1)
    lhs2 = np.concatenate([np.cos(ph).transpose(0, 2, 1), np.sin(ph).transpose(0, 2, 1)], axis=2) / HY_N
    kk = np.arange(HY_N2)
    ang = -2 * np.pi * np.outer(kk, kk) / HY_N2
    cr, ci = np.cos(ang), np.sin(ang)
    f_fwd = np.block([[cr, -ci], [ci, cr]])
    f_inv = np.block([[cr, ci], [-ci, cr]])

    def as_bf(a):
        return jnp.asarray(a, jnp.float32).astype(jnp.bfloat16)

    return as_bf(lhs1), as_bf(lhs2), as_bf(f_fwd), as_bf(f_inv)


def hyena_spectrum_layout(k_spec):
    w = k_spec.shape[1]
    full = jnp.concatenate([k_spec, jnp.conj(k_spec[-2:0:-1])], axis=0)
    parts = jnp.stack([jnp.real(full), jnp.imag(full)], axis=0).reshape(2, HY_N2, HY_N1, w)
    parts = jnp.transpose(parts, (2, 0, 1, 3)).reshape(HY_N1 * 2 * HY_N2, w)
    return jnp.transpose(parts.reshape(-1, w // HY_LANES, HY_LANES), (1, 0, 2)).astype(jnp.float32)


def _hy_kernel(z_ref, g_ref, k_ref, l1_ref, l2_ref, ff_ref, fi_ref, bias_ref, o_ref, a_ref):
    f32, bf16 = jnp.float32, jnp.bfloat16
    n2n, nh, pitch, half = HY_N2, HY_NH, HY_PITCH, HY_N1

    def stage1(q, c):
        slab = z_ref[0, pl.ds(pl.multiple_of(q * nh, nh), nh), :].astype(bf16)
        a_ref[pl.ds(pl.multiple_of(q * pitch, 8), 2 * half), :] = jnp.dot(l1_ref[q], slab, preferred_element_type=f32)
        return c

    lax.fori_loop(0, n2n, stage1, 0, unroll=8)

    def stage2(k, c):
        a = jnp.concatenate([a_ref[pl.ds(k, n2n, stride=pitch), :],
                             a_ref[pl.ds(half + k, n2n, stride=pitch), :]], axis=0).astype(bf16)
        x = jnp.dot(ff_ref[...], a, preferred_element_type=f32)
        xr, xi = x[:n2n], x[n2n:]
        base = pl.multiple_of(k * 2 * n2n, 2 * n2n)
        kr = k_ref[0, pl.ds(base, n2n), :]
        ki = k_ref[0, pl.ds(base + n2n, n2n), :]
        y = jnp.concatenate([xr * kr - xi * ki, xr * ki + xi * kr], axis=0).astype(bf16)
        b = jnp.dot(fi_ref[...], y, preferred_element_type=f32)
        a_ref[pl.ds(k, n2n, stride=pitch), :] = b[:n2n]
        a_ref[pl.ds(half + k, n2n, stride=pitch), :] = b[n2n:]
        return c

    lax.fori_loop(0, half, stage2, 0, unroll=4)

    def stage3(q, c):
        blk = a_ref[pl.ds(pl.multiple_of(q * pitch, 8), 2 * half), :].astype(bf16)
        y = jnp.dot(l2_ref[q], blk, preferred_element_type=f32)
        rows = pl.ds(pl.multiple_of(q * nh, nh), nh)
        o_ref[0, rows, :] = g_ref[0, rows, :] * (y + z_ref[0, rows, :] * bias_ref[...])
        return c

    lax.fori_loop(0, n2n, stage3, 0, unroll=8)


def hyena_long_conv_gated(zt, gt, spec, bias, tables):
    bsz, n, w = zt.shape
    l1, l2, ff, fi = tables
    nt = w // HY_LANES
    tok = pl.BlockSpec((1, n, HY_LANES), lambda t, b: (b, 0, t))
    return pl.pallas_call(
        _hy_kernel,
        out_shape=jax.ShapeDtypeStruct((bsz, n, w), jnp.float32),
        grid=(nt, bsz),
        in_specs=[tok, tok,
                  pl.BlockSpec((1,) + spec.shape[1:], lambda t, b: (t, 0, 0)),
                  pl.BlockSpec(l1.shape, lambda t, b: (0, 0, 0)),
                  pl.BlockSpec(l2.shape, lambda t, b: (0, 0, 0)),
                  pl.BlockSpec(ff.shape, lambda t, b: (0, 0)),
                  pl.BlockSpec(fi.shape, lambda t, b: (0, 0)),
                  pl.BlockSpec((1, HY_LANES), lambda t, b: (0, t))],
        out_specs=tok,
        scratch_shapes=[pltpu.VMEM((HY_N2 * HY_PITCH, HY_LANES), jnp.float32)],
        compiler_params=pltpu.CompilerParams(dimension_semantics=("parallel", "parallel"),
                                             vmem_limit_bytes=VMEM_LIMIT),
        name="hyena_conv",
    )(zt, gt, spec, l1, l2, ff, fi, bias.reshape(1, w))


def hyena_branch_long(streams, p):
    bsz, n, w3 = streams.shape
    k_spec = hyena_filter_spectra(n, p)
    tables = hyena_dft_tables()
    st = jnp.transpose(streams.reshape(bsz, HY_NH, HY_N2, w3), (0, 2, 1, 3)).reshape(bsz, n, w3)
    z, x1, x2 = jnp.split(st, 3, axis=-1)
    for o, gate in enumerate((x1, x2)):
        z = hyena_long_conv_gated(z, gate, hyena_spectrum_layout(k_spec[:, o]), p['hy_bias'][o], tables)
    w = z.shape[-1]
    return jnp.transpose(z.reshape(bsz, HY_N2, HY_NH, w), (0, 2, 1, 3)).reshape(bsz, n, w)


def merge_branches(y_s5, y_rw, y_hy, gates, p):
    g_s5, g_rw, g_hy = jnp.split(jax.nn.sigmoid(gates), N_BRANCH, axis=-1)
    m = g_s5 * mm(y_s5, p['br_s5']) + g_rw * mm(y_rw, p['br_rw']) + g_hy * mm(y_hy, p['br_hy'])
    return mm(m, p['out_w'])


def token_mixer(h_lat, h_ctx, p, with_ctx_out):
    bsz = h_lat.shape[0]
    cuts = [IN_S5, IN_S5 + IN_RW, IN_S5 + IN_RW + IN_LORA, IN_S5 + IN_RW + IN_LORA + IN_HY]
    edges = [0] + cuts + [IN_COLS]
    w_parts = [p['in_w'][:, edges[i]:edges[i + 1]] for i in range(5)]
    u_l, rkv_l, lora_l, hy_l, gate_l = [mm(h_lat, wp) for wp in w_parts]
    n_ctx_parts = 5 if with_ctx_out else 3
    ctx_parts = [mm(h_ctx, wp) for wp in w_parts[:n_ctx_parts]]
    masks, tri = rwkv_masks()
    zr = jnp.zeros((bsz, 2, RW_WIDTH // RW_LANES, RW_LANES, RW_LANES), jnp.float32)
    s5_l, s5_c = s5_branch(u_l, ctx_parts[0], p)
    rw_c, rw_state = rwkv_branch(short_conv(ctx_parts[1], p['rw_conv_w']), ctx_parts[2], p, zr, masks, tri,
                                 CTX_LEN)
    rw_l, _ = rwkv_branch(latent_short_conv(rkv_l, p['rw_conv_w']), lora_l, p, rw_state, masks, tri,
                          RW_BLOCK_TOKENS)
    hy_lat = hyena_branch_long(latent_short_conv(hy_l, p['hy_conv_w'], p['hy_conv_b']), p)
    y_lat = merge_branches(s5_l, rw_l, hy_lat, gate_l, p)
    if not with_ctx_out:
        return y_lat, None
    hy_ctx = hyena_branch(short_conv(ctx_parts[3], p['hy_conv_w'], p['hy_conv_b']), p)
    y_ctx = merge_branches(s5_c, rw_c, hy_ctx, ctx_parts[4], p)
    return y_lat, y_ctx


def _rms(x, gain):
    return x * lax.rsqrt(jnp.mean(x * x, axis=-1, keepdims=True) + NORM_EPS) * gain


def _ffn_kernel(x_ref, mod_ref, gain_ref, wg_ref, wu_ref, wd_ref, o_ref):
    x = x_ref[...]
    shift, scale, gate = mod_ref[0, 0:1, :], mod_ref[0, 1:2, :], mod_ref[0, 2:3, :]
    h = (_rms(x, gain_ref[0:1, :]) * (1.0 + scale) + shift).astype(jnp.bfloat16)
    g = jnp.dot(h, wg_ref[...], preferred_element_type=jnp.float32)
    u = jnp.dot(h, wu_ref[...], preferred_element_type=jnp.float32)
    a = (g * jax.nn.sigmoid(g) * u).astype(jnp.bfloat16)
    y = jnp.dot(a, wd_ref[...], preferred_element_type=jnp.float32)
    o_ref[...] = x + gate * _rms(y, gain_ref[1:2, :])


def ffn_block(x, mod, gains, wg, wu, wd):
    bsz, n, d = x.shape
    hid = wg.shape[1]
    tm = min(FFN_BLOCK_ROWS, n)
    per_b = n // tm
    xf = x.reshape(bsz * n, d)
    bmap = (lambda i: (i // per_b, 0, 0)) if mod.shape[0] == bsz else (lambda i: (0, 0, 0))
    out = pl.pallas_call(
        _ffn_kernel,
        out_shape=jax.ShapeDtypeStruct((bsz * n, d), jnp.float32),
        grid=(bsz * per_b,),
        in_specs=[pl.BlockSpec((tm, d), lambda i: (i, 0)),
                  pl.BlockSpec((1, 3, d), bmap),
                  pl.BlockSpec((2, d), lambda i: (0, 0)),
                  pl.BlockSpec((d, hid), lambda i: (0, 0)),
                  pl.BlockSpec((d, hid), lambda i: (0, 0)),
                  pl.BlockSpec((hid, d), lambda i: (0, 0))],
        out_specs=pl.BlockSpec((tm, d), lambda i: (i, 0)),
        compiler_params=pltpu.CompilerParams(dimension_semantics=("parallel",), vmem_limit_bytes=VMEM_LIMIT),
        name="ffn_block",
    )(xf, mod, gains, wg.astype(jnp.bfloat16), wu.astype(jnp.bfloat16), wd.astype(jnp.bfloat16))
    return out.reshape(bsz, n, d)


def _moe_kernel(be_ref, x_ref, gate_ref, wg_ref, wu_ref, wd_ref, o_ref):
    j = pl.program_id(1)
    xb = x_ref[...].astype(jnp.bfloat16)
    g = jnp.dot(xb, wg_ref[0], preferred_element_type=jnp.float32)
    u = jnp.dot(xb, wu_ref[0], preferred_element_type=jnp.float32)
    a = (g * jax.nn.sigmoid(g) * u).astype(jnp.bfloat16)
    y = jnp.dot(a, wd_ref[0], preferred_element_type=jnp.float32)

    @pl.when(j == 0)
    def _():
        o_ref[...] = y

    @pl.when(j > 0)
    def _():
        o_ref[...] += y

    @pl.when(j == pl.num_programs(1) - 1)
    def _():
        o_ref[...] *= gate_ref[...]


def moe_expert_blocks(xb, slot_gate, block_e, wg, wu, wd):
    cap, d = xb.shape
    n_blocks = cap // MOE_BLOCK
    hid = wg.shape[-1]
    nh = MOE_HIDDEN_TILES
    th = hid // nh

    def hidx(i, j):
        return j + (i % 2) * (nh - 1 - 2 * j)

    return pl.pallas_call(
        _moe_kernel,
        out_shape=jax.ShapeDtypeStruct((cap, d), jnp.float32),
        grid_spec=pltpu.PrefetchScalarGridSpec(
            num_scalar_prefetch=1,
            grid=(n_blocks, nh),
            in_specs=[pl.BlockSpec((MOE_BLOCK, d), lambda i, j, be: (i, 0)),
                      pl.BlockSpec((MOE_BLOCK, 1), lambda i, j, be: (i, 0)),
                      pl.BlockSpec((1, d, th), lambda i, j, be: (be[i], 0, hidx(i, j))),
                      pl.BlockSpec((1, d, th), lambda i, j, be: (be[i], 0, hidx(i, j))),
                      pl.BlockSpec((1, th, d), lambda i, j, be: (be[i], hidx(i, j), 0))],
            out_specs=pl.BlockSpec((MOE_BLOCK, d), lambda i, j, be: (i, 0))),
        compiler_params=pltpu.CompilerParams(dimension_semantics=("parallel", "arbitrary"),
                                             vmem_limit_bytes=VMEM_LIMIT),
        name="moe_experts",
    )(block_e, xb, slot_gate.reshape(cap, 1), wg.astype(jnp.bfloat16), wu.astype(jnp.bfloat16),
      wd.astype(jnp.bfloat16))


def moe_swiglu(h, router_w, wg, wu, wd):
    d_model = h.shape[-1]
    tok = h.reshape(-1, d_model)
    n = tok.shape[0]
    n_assign = n * TOP_K
    logits = jnp.dot(tok, router_w)
    top_logit, top_e = lax.top_k(logits, TOP_K)
    gate = jax.nn.softmax(top_logit, axis=-1).reshape(-1)
    flat_e = top_e.reshape(-1)
    order = jnp.argsort(flat_e)
    sorted_e = flat_e[order]
    sizes = jnp.zeros((N_EXPERTS,), jnp.int32).at[flat_e].add(1)
    padded = (sizes + MOE_BLOCK - 1) // MOE_BLOCK * MOE_BLOCK
    pad_end = jnp.cumsum(padded)
    pad_start = pad_end - padded
    grp_start = jnp.cumsum(sizes) - sizes
    slot = pad_start[sorted_e] + jnp.arange(n_assign, dtype=jnp.int32) - grp_start[sorted_e]
    n_blocks = -(-n_assign // MOE_BLOCK) + N_EXPERTS
    cap = n_blocks * MOE_BLOCK
    slot_tok = jnp.full((cap,), n, jnp.int32).at[slot].set((order // TOP_K).astype(jnp.int32))
    slot_gate = jnp.zeros((cap,), jnp.float32).at[slot].set(gate[order])
    block_start = jnp.arange(n_blocks, dtype=jnp.int32) * MOE_BLOCK
    block_e = jnp.minimum(jnp.sum(block_start[:, None] >= pad_end[None, :], axis=1), N_EXPERTS - 1)
    tok_pad = jnp.concatenate([tok, jnp.zeros((1, d_model), tok.dtype)], axis=0)
    xb = tok_pad[slot_tok]
    yb = moe_expert_blocks(xb, slot_gate, block_e.astype(jnp.int32), wg, wu, wd)
    pos = jnp.zeros((n_assign,), jnp.int32).at[order].set(slot).reshape(n, TOP_K)
    out = yb[pos[:, 0]]
    for kk in range(1, TOP_K):
        out = out + yb[pos[:, kk]]
    return out.reshape(h.shape)


def kernel(x, c, ctx, c_ctx, mod_w, mod_b, norm_g, in_w, s5_lam_re, s5_lam_im, s5_log_step, s5_b_re, s5_b_im, s5_c_re, s5_c_im, s5_d, s5_glu_w, rw_conv_w, rw_w0, rw_w2, rw_a0, rw_a2, rw_g2, rw_kk, rw_ka, rw_rk, rw_ln_w, rw_ln_b, hy_conv_w, hy_conv_b, hy_f_w1, hy_f_b1, hy_f_freq1, hy_f_w2, hy_f_b2, hy_f_freq2, hy_f_w3, hy_bias, br_s5, br_rw, br_hy, out_w, ffn_wg, ffn_wu, ffn_wd, moe_router, moe_wg, moe_wu, moe_wd):
    silu_c = jax.nn.silu(c)
    silu_cc = jax.nn.silu(c_ctx)
    for i in range(DEPTH):
        last = i == DEPTH - 1
        p = {
            'in_w': in_w[i],
            's5_lam_re': s5_lam_re[i], 's5_lam_im': s5_lam_im[i], 's5_log_step': s5_log_step[i],
            's5_b_re': s5_b_re[i], 's5_b_im': s5_b_im[i], 's5_c_re': s5_c_re[i], 's5_c_im': s5_c_im[i],
            's5_d': s5_d[i], 's5_glu_w': s5_glu_w[i],
            'rw_conv_w': rw_conv_w[i], 'rw_w0': rw_w0[i], 'rw_w2': rw_w2[i], 'rw_a0': rw_a0[i], 'rw_a2': rw_a2[i],
            'rw_g2': rw_g2[i], 'rw_kk': rw_kk[i], 'rw_ka': rw_ka[i], 'rw_rk': rw_rk[i],
            'rw_ln_w': rw_ln_w[i], 'rw_ln_b': rw_ln_b[i],
            'hy_conv_w': hy_conv_w[i], 'hy_conv_b': hy_conv_b[i], 'hy_f_w1': hy_f_w1[i], 'hy_f_b1': hy_f_b1[i],
            'hy_f_freq1': hy_f_freq1[i], 'hy_f_w2': hy_f_w2[i], 'hy_f_b2': hy_f_b2[i], 'hy_f_freq2': hy_f_freq2[i],
            'hy_f_w3': hy_f_w3[i], 'hy_bias': hy_bias[i],
            'br_s5': br_s5[i], 'br_rw': br_rw[i], 'br_hy': br_hy[i], 'out_w': out_w[i],
        }
        ml = jnp.split((silu_c @ mod_w[i] + mod_b[i])[:, None, :], 6, axis=-1)
        mc = jnp.split(silu_cc @ mod_w[i] + mod_b[i], 6, axis=-1)
        g_pre_m, g_post_m, g_pre_f, g_post_f = norm_g[i]
        if i % 2 == 0:
            def channel_mix(t, m, j=i // 2):
                return ffn_block(t, m, norm_g[i, 2:4], ffn_wg[j], ffn_wu[j], ffn_wd[j])
        else:
            def channel_mix(t, m, j=i // 2):
                h = rms_norm(t, g_pre_f) * (1 + m[:, 1:2]) + m[:, 0:1]
                y = moe_swiglu(h, moe_router[j], moe_wg[j], moe_wu[j], moe_wd[j])
                return t + m[:, 2:3] * rms_norm(y, g_post_f)
        h_l = rms_norm(x, g_pre_m) * (1 + ml[1]) + ml[0]
        h_c = rms_norm(ctx, g_pre_m) * (1 + mc[1]) + mc[0]
        y_l, y_c = token_mixer(h_l, h_c, p, not last)
        x = x + ml[2] * rms_norm(y_l, g_post_m)
        x = channel_mix(x, jnp.concatenate(ml[3:6], axis=1))
        if not last:
            ctx = ctx + mc[2] * rms_norm(y_c, g_post_m)
            ctx = channel_mix(ctx, jnp.stack(mc[3:6])[None])
    return x
```

```python
import math
import functools
import numpy as np
import jax
import jax.numpy as jnp
from jax import lax
from jax.experimental import pallas as pl
from jax.experimental.pallas import tpu as pltpu

D_MODEL = 1024
BATCH = 8
SEQ = 4096
DEPTH = 2
GRID_W = 64
CTX_LEN = 256
NORM_EPS = 1e-6
S5_WIDTH = D_MODEL // 4
S5_GROUP = 16
S5_GROUPS = S5_WIDTH // S5_GROUP
S5_STATE = 64
RW_WIDTH = D_MODEL // 2
RW_HEAD = 64
RW_HEADS = RW_WIDTH // RW_HEAD
RW_DECAY_LORA = 64
RW_ICLR_LORA = 64
RW_GATE_LORA = 128
RW_GN_EPS = 64e-5
HY_WIDTH = D_MODEL // 4
HY_ORDER = 2
HY_POS_DIM = 33
HY_FILTER_HIDDEN = 64
HY_DECAY_TARGET = 1e-2
HY_DECAY_PCT_SHORT = 0.3
HY_DECAY_PCT_LONG = 1.5
SHORT_CONV = 3
N_BRANCH = 3
FFN_HIDDEN = 2816
N_EXPERTS = 8
TOP_K = 2
EXPERT_HIDDEN = 3584
MOE_BLOCK = 256
IN_S5 = S5_WIDTH
IN_RW = 3 * RW_WIDTH
IN_LORA = 2 * RW_DECAY_LORA + 2 * RW_ICLR_LORA + RW_GATE_LORA
IN_HY = (HY_ORDER + 1) * HY_WIDTH
IN_GATE = N_BRANCH * D_MODEL
IN_COLS = IN_S5 + IN_RW + IN_LORA + IN_HY + IN_GATE

VMEM_LIMIT = 56 * 1024 * 1024
S5_CHUNK = 64
RW_CHUNK = 64
RW_HG = 4
RW_LANES = RW_HG * RW_HEAD
RW_BLOCK_TOKENS = 512
MOE_HIDDEN_TILES = 2
FFN_BLOCK_ROWS = 256
MIX_BLOCK_ROWS = 256


def _mm_kernel(a_ref, b_ref, o_ref):
    o_ref[...] = jnp.dot(a_ref[...].astype(jnp.bfloat16), b_ref[...],
                         preferred_element_type=jnp.float32)


def _pick(n, cands):
    for c in cands:
        if n % c == 0:
            return c
    return n


def pmatmul(a, b):
    m, k = a.shape
    n = b.shape[1]
    b = b.astype(jnp.bfloat16)
    tm = _pick(m, (512, 256, 128, 8))
    tn = _pick(n, (1024, 768, 512, 384, 256, 128))
    return pl.pallas_call(
        _mm_kernel,
        out_shape=jax.ShapeDtypeStruct((m, n), jnp.float32),
        grid=(n // tn, m // tm),
        in_specs=[pl.BlockSpec((tm, k), lambda j, i: (i, 0)),
                  pl.BlockSpec((k, tn), lambda j, i: (0, j))],
        out_specs=pl.BlockSpec((tm, tn), lambda j, i: (i, j)),
        compiler_params=pltpu.CompilerParams(
            dimension_semantics=("parallel", "parallel"),
            vmem_limit_bytes=VMEM_LIMIT),
        name="mm",
    )(a, b)


def mm(a, b):
    lead = a.shape[:-1]
    return pmatmul(a.reshape(-1, a.shape[-1]), b).reshape(lead + (b.shape[1],))


def rms_norm(x, gain):
    y = x * lax.rsqrt(jnp.mean(x * x, axis=-1, keepdims=True) + NORM_EPS)
    return y * gain


def short_conv(x, w, b=None):
    half = SHORT_CONV // 2
    n = x.shape[1]
    xp = jnp.pad(x, ((0, 0), (half, half), (0, 0)))
    y = sum(xp[:, j:j + n] * w[j] for j in range(SHORT_CONV))
    return y if b is None else y + b


def latent_short_conv(x, w, b=None):
    bsz, n_tok, ch = x.shape
    rows = n_tok // GRID_W
    y = short_conv(x.reshape(bsz * rows, GRID_W, ch), w, b)
    return y.reshape(bsz, n_tok, ch)


def s5_chunk_operators(lam_re, lam_im, log_step, b_re, b_im, c_re, c_im):
    T, G, P, GC = S5_CHUNK, S5_GROUPS, S5_STATE, S5_GROUP
    hp = lax.Precision.HIGHEST
    step = jnp.exp(log_step)[..., None]
    lr, li = lam_re, lam_im
    tau = jnp.arange(T + 1, dtype=jnp.float32)[:, None, None, None]
    mag = jnp.exp(lr * step * tau)
    ang = li * step * tau
    e_re, e_im = mag * jnp.cos(ang), mag * jnp.sin(ang)
    ab_re, ab_im = e_re[1], e_im[1]
    den = lr * lr + li * li
    nr, ni = ab_re - 1.0, ab_im
    q_re = (nr * lr + ni * li) / den
    q_im = (ni * lr - nr * li) / den
    bb_re = q_re[..., None] * b_re - q_im[..., None] * b_im
    bb_im = q_re[..., None] * b_im + q_im[..., None] * b_re
    eb_re = e_re[..., None] * bb_re - e_im[..., None] * bb_im
    eb_im = e_re[..., None] * bb_im + e_im[..., None] * bb_re
    ktau = (jnp.einsum('dgip,tdgpj->tdgij', c_re, eb_re[:T], precision=hp)
            - jnp.einsum('dgip,tdgpj->tdgij', c_im, eb_im[:T], precision=hp))
    t_idx = jnp.arange(T)
    lag = t_idx[:, None] - t_idx[None, :]
    kf = jnp.where((lag >= 0)[:, :, None, None, None], ktau[jnp.clip(lag, 0, T - 1), 0], 0.0)
    kb = jnp.where((lag <= 0)[:, :, None, None, None], ktau[jnp.clip(-lag, 0, T - 1), 1], 0.0)
    kt = jnp.transpose(kf + kb, (2, 1, 4, 0, 3)).reshape(G, T * GC, T * GC)
    wf_re, wf_im = eb_re[T - 1 - t_idx, 0], eb_im[T - 1 - t_idx, 0]
    wb_re, wb_im = eb_re[t_idx, 1], eb_im[t_idx, 1]
    win = jnp.stack([wf_re, wf_im, wb_re, wb_im], axis=0)
    win = jnp.transpose(win, (2, 1, 4, 0, 3)).reshape(G, T * GC, 4 * P)
    ef_re, ef_im = e_re[t_idx + 1, 0], e_im[t_idx + 1, 0]
    eb2_re, eb2_im = e_re[T - t_idx, 1], e_im[T - t_idx, 1]

    def readout(cr, ci, er, ei):
        re = cr[None] * er[:, :, None, :] - ci[None] * ei[:, :, None, :]
        im = -(cr[None] * ei[:, :, None, :] + ci[None] * er[:, :, None, :])
        return re, im

    of_re, of_im = readout(c_re[0], c_im[0], ef_re, ef_im)
    ob_re, ob_im = readout(c_re[1], c_im[1], eb2_re, eb2_im)
    wout = jnp.stack([of_re, of_im, ob_re, ob_im], axis=0)
    wout = jnp.transpose(wout, (2, 0, 4, 1, 3)).reshape(G, 4 * P, T * GC)
    at_re, at_im = e_re[T], e_im[T]
    apow = jnp.concatenate([at_re[0], at_re[0], at_re[1], at_re[1]], axis=-1)[:, None, :]
    aimg = jnp.concatenate([-at_im[0], at_im[0], -at_im[1], at_im[1]], axis=-1)[:, None, :]
    return kt.astype(jnp.bfloat16), win.astype(jnp.bfloat16), wout.astype(jnp.bfloat16), apow, aimg


def _s5_kernel(x_ref, kt_ref, win_ref, wout_ref, apow_ref, aimg_ref, h0_ref, y_ref, hfin_ref, hin_ref, *,
               n_chunks, bsz):
    P2 = 2 * S5_STATE
    xb = x_ref[0].astype(jnp.bfloat16)
    hin_ref[...] = jnp.dot(xb, win_ref[0], preferred_element_type=jnp.float32)
    ap = apow_ref[0]
    ai = aimg_ref[0]
    apf, aif = ap[:, :P2], ai[:, :P2]
    apb, aib = ap[:, P2:], ai[:, P2:]

    def cmul(h, a_p, a_i):
        return h * a_p + pltpu.roll(h, S5_STATE, axis=1) * a_i

    def body(c, carry):
        hf, hb = carry
        rf = pl.ds(pl.multiple_of(c * bsz, bsz), bsz)
        rb = pl.ds(pl.multiple_of((n_chunks - 1 - c) * bsz, bsz), bsz)
        df = hin_ref[rf, :P2]
        db = hin_ref[rb, P2:]
        hin_ref[rf, :P2] = hf
        hin_ref[rb, P2:] = hb
        return cmul(hf, apf, aif) + df, cmul(hb, apb, aib) + db

    h0 = h0_ref[0]
    hf, hb = lax.fori_loop(0, n_chunks, body, (h0[:, :P2], h0[:, P2:]))
    hfin_ref[0, :, :P2] = hf
    hfin_ref[0, :, P2:] = hb
    y = jnp.dot(xb, kt_ref[0], preferred_element_type=jnp.float32)
    y = y + jnp.dot(hin_ref[...].astype(jnp.bfloat16), wout_ref[0], preferred_element_type=jnp.float32)
    y_ref[0] = y


def s5_scan_pallas(u, ops, h0):
    kt, win, wout, apow, aimg = ops
    bsz, n, _ = u.shape
    T, G, P, GC = S5_CHUNK, S5_GROUPS, S5_STATE, S5_GROUP
    nc = n // T
    rows = nc * bsz
    x = jnp.transpose(u.reshape(bsz, nc, T, G, GC), (3, 1, 0, 2, 4)).reshape(G, rows, T * GC)
    y, hfin = pl.pallas_call(
        functools.partial(_s5_kernel, n_chunks=nc, bsz=bsz),
        out_shape=(jax.ShapeDtypeStruct((G, rows, T * GC), jnp.float32),
                   jax.ShapeDtypeStruct((G, bsz, 4 * P), jnp.float32)),
        grid=(G,),
        in_specs=[pl.BlockSpec((1, rows, T * GC), lambda g: (g, 0, 0)),
                  pl.BlockSpec((1, T * GC, T * GC), lambda g: (g, 0, 0)),
                  pl.BlockSpec((1, T * GC, 4 * P), lambda g: (g, 0, 0)),
                  pl.BlockSpec((1, 4 * P, T * GC), lambda g: (g, 0, 0)),
                  pl.BlockSpec((1, 1, 4 * P), lambda g: (g, 0, 0)),
                  pl.BlockSpec((1, 1, 4 * P), lambda g: (g, 0, 0)),
                  pl.BlockSpec((1, bsz, 4 * P), lambda g: (g, 0, 0))],
        out_specs=(pl.BlockSpec((1, rows, T * GC), lambda g: (g, 0, 0)),
                   pl.BlockSpec((1, bsz, 4 * P), lambda g: (g, 0, 0))),
        scratch_shapes=[pltpu.VMEM((rows, 4 * P), jnp.float32)],
        compiler_params=pltpu.CompilerParams(dimension_semantics=("parallel",),
                                             vmem_limit_bytes=VMEM_LIMIT),
        name="s5_scan",
    )(x, kt, win, wout, apow, aimg, h0)
    y = jnp.transpose(y.reshape(G, nc, bsz, T, GC), (2, 1, 3, 0, 4)).reshape(bsz, n, G * GC)
    return y, hfin


def s5_branch(u, u_ctx, p):
    ops = s5_chunk_operators(p['s5_lam_re'], p['s5_lam_im'], p['s5_log_step'], p['s5_b_re'], p['s5_b_im'],
                             p['s5_c_re'], p['s5_c_im'])
    h0 = jnp.zeros((S5_GROUPS, u.shape[0], 4 * S5_STATE), jnp.float32)
    y_ctx, h_ctx = s5_scan_pallas(u_ctx, ops, h0)
    y_lat, _ = s5_scan_pallas(u, ops, h_ctx)
    return y_lat, y_ctx


_NT = (((1,), (1,)), ((), ()))

_M_SAME, _M_EYE, _M_STRICT, _M_INCL, _M_LEVEL0 = 0, 1, 2, 4, 6
_N_LEVELS = 6


def rwkv_masks():
    C, n = RW_CHUNK, RW_LANES
    row = np.arange(n)[:, None]
    col = np.arange(n)[None, :]
    same = (row // C) == (col // C)
    out = [same, row == col]
    per_dir = []
    for reverse in (False, True):
        t, j = (row % C, col % C) if not reverse else (col % C, row % C)
        per_dir.append((same & (j < t), same & (j <= t),
                        [same & ((t // s) % 2 == 1) & ((j // s) == (t // s) - 1) for s in (1, 2, 4, 8, 16, 32)]))
    out += [per_dir[0][0], per_dir[1][0], per_dir[0][1], per_dir[1][1]]
    for l in range(_N_LEVELS):
        out += [per_dir[0][2][l], per_dir[1][2][l]]
    tt, jj = np.arange(C)[:, None], np.arange(C)[None, :]
    tri = np.stack([jj <= tt, jj >= tt])
    return jnp.asarray(np.stack(out), jnp.float32), jnp.asarray(tri, jnp.bfloat16)


def _bdot(a, b, dims=None):
    a = a.astype(jnp.bfloat16)
    b = b.astype(jnp.bfloat16)
    if dims is None:
        return jnp.dot(a, b, preferred_element_type=jnp.float32)
    return lax.dot_general(a, b, dims, preferred_element_type=jnp.float32)


def _rwkv_chunk(S, r, lw, k, v, kk, b, m_ref, tri, d):
    C = RW_CHUNK
    same = m_ref[_M_SAME]
    lw_hi = lw.astype(jnp.bfloat16)
    rem = lw - lw_hi.astype(jnp.float32)
    lw_mid = rem.astype(jnp.bfloat16)
    lw_lo = (rem - lw_mid.astype(jnp.float32)).astype(jnp.bfloat16)
    cl = (jnp.dot(tri, lw_hi, preferred_element_type=jnp.float32)
          + jnp.dot(tri, lw_mid, preferred_element_type=jnp.float32)
          + jnp.dot(tri, lw_lo, preferred_element_type=jnp.float32))
    yield
    tot = jnp.sum(lw, axis=0, keepdims=True)
    e_neg = jnp.exp(-cl)
    e_end = jnp.exp(tot - cl)
    a_t = -kk * jnp.exp(cl - lw)
    r_t = r * jnp.exp(cl)
    b_t = b * e_neg
    k_t = k * e_neg
    b_h = b * e_end
    k_h = k * e_end
    p_c = jnp.exp(tot)

    def tile(x):
        return jnp.concatenate([x] * RW_HG, axis=0)

    def stack(x):
        return tile(x) * same

    def unstack(z):
        acc = z[0:C]
        for h in range(1, RW_HG):
            acc = acc + z[h * C:(h + 1) * C]
        return acc

    n = RW_LANES
    lhs = jnp.concatenate([stack(a_t), stack(r_t)], axis=0)
    rhs = jnp.concatenate([tile(b_t), tile(k_t)], axis=0)
    amat = _bdot(lhs, rhs, _NT)
    yield
    a_ab = amat[:n, :n] * m_ref[_M_STRICT + d]
    a_ak = amat[:n, n:] * m_ref[_M_STRICT + d]
    a_rb = amat[n:, :n] * m_ref[_M_INCL + d]
    a_rk = amat[n:, n:] * m_ref[_M_INCL + d]
    dinv = m_ref[_M_EYE] + a_ab * m_ref[_M_LEVEL0 + d]
    for l in range(1, _N_LEVELS):
        inner = _bdot(a_ab * m_ref[_M_LEVEL0 + 2 * l + d], dinv)
        yield
        dinv = dinv + _bdot(dinv, inner)
        yield
    sv = stack(v)
    av = _bdot(jnp.concatenate([a_ak, a_rk], axis=0), sv)
    yield
    akv = unstack(av[:n])
    arkv = unstack(av[n:])
    mu = _bdot(dinv, jnp.concatenate([stack(a_t), stack(akv)], axis=1))
    yield
    m1 = unstack(mu[:, :n])
    u0 = unstack(mu[:, n:])
    my = _bdot(a_rb, jnp.concatenate([stack(m1), stack(u0)], axis=1))
    yield
    m2 = r_t + unstack(my[:, :n])
    y0 = unstack(my[:, n:]) + arkv
    mut = jnp.concatenate([m1, u0], axis=1).T
    gh = _bdot(mut, b_h)
    yield
    g = m_ref[_M_EYE] * p_c + gh[:n] * same
    hmat = (gh[n:] + _bdot(v.T, k_h)) * same
    y = _bdot(m2, S, _NT) + y0
    s_new = _bdot(S, g) + hmat
    return s_new, y


def _lockstep(gens):
    results = [None] * len(gens)
    active = list(range(len(gens)))
    while active:
        for i in list(active):
            try:
                next(gens[i])
            except StopIteration as stop:
                results[i] = stop.value
                active.remove(i)
    return results


def _rwkv_kernel(m_ref, tri_ref, rf_ref, rb_ref, vf_ref, vb_ref, kkf_ref, kkb_ref, lwf_ref, kf_ref, bf_ref,
                 lwb_ref, kb_ref, bb_ref, s0_ref, yf_ref, yb_ref, sfin_ref, s_scr, *, n_chunks, n_groups):
    i = pl.program_id(1)
    C = RW_CHUNK

    @pl.when(i == 0)
    def _():
        s_scr[...] = s0_ref[0]

    tri_f = tri_ref[0]
    tri_b = tri_ref[1]

    def body(c, carry):
        rf = pl.ds(pl.multiple_of(c * C, C), C)
        rb = pl.ds(pl.multiple_of((n_chunks - 1 - c) * C, C), C)
        gens = []
        for g in range(n_groups):
            ln = slice(g * RW_LANES, (g + 1) * RW_LANES)
            gens.append(_rwkv_chunk(s_scr[0, g], rf_ref[0, rf, ln], lwf_ref[0, rf, ln], kf_ref[0, rf, ln],
                                    vf_ref[0, rf, ln], kkf_ref[0, rf, ln], bf_ref[0, rf, ln], m_ref, tri_f, 0))
            gens.append(_rwkv_chunk(s_scr[1, g], rb_ref[0, rb, ln], lwb_ref[0, rb, ln], kb_ref[0, rb, ln],
                                    vb_ref[0, rb, ln], kkb_ref[0, rb, ln], bb_ref[0, rb, ln], m_ref, tri_b, 1))
        out = _lockstep(gens)
        for g in range(n_groups):
            ln = slice(g * RW_LANES, (g + 1) * RW_LANES)
            (s_f, y_f), (s_b, y_b) = out[2 * g], out[2 * g + 1]
            s_scr[0, g] = s_f
            s_scr[1, g] = s_b
            yf_ref[0, rf, ln] = y_f
            yb_ref[0, rb, ln] = y_b
        return carry

    lax.fori_loop(0, n_chunks, body, 0)

    @pl.when(i == pl.num_programs(1) - 1)
    def _():
        sfin_ref[0] = s_scr[...]


def rwkv_scan_pallas(r, v, kk, lw, kd, bvec, s0, masks, tri, block_tokens):
    bsz, n, width = r.shape
    ng = width // RW_LANES
    tb = block_tokens
    nb = n // tb
    fwd = pl.BlockSpec((1, tb, width), lambda b, i: (b, i, 0))
    bwd = pl.BlockSpec((1, tb, width), lambda b, i: (b, nb - 1 - i, 0))
    state_spec = pl.BlockSpec((1, 2, ng, RW_LANES, RW_LANES), lambda b, i: (b, 0, 0, 0, 0))
    return pl.pallas_call(
        functools.partial(_rwkv_kernel, n_chunks=tb // RW_CHUNK, n_groups=ng),
        out_shape=(jax.ShapeDtypeStruct((bsz, n, width), jnp.float32),
                   jax.ShapeDtypeStruct((bsz, n, width), jnp.float32),
                   jax.ShapeDtypeStruct(s0.shape, jnp.float32)),
        grid=(bsz, nb),
        in_specs=[pl.BlockSpec(masks.shape, lambda b, i: (0, 0, 0)),
                  pl.BlockSpec(tri.shape, lambda b, i: (0, 0, 0)),
                  fwd, bwd, fwd, bwd, fwd, bwd,
                  fwd, fwd, fwd, bwd, bwd, bwd, state_spec],
        out_specs=(fwd, bwd, state_spec),
        scratch_shapes=[pltpu.VMEM((2, ng, RW_LANES, RW_LANES), jnp.float32)],
        compiler_params=pltpu.CompilerParams(dimension_semantics=("parallel", "arbitrary"),
                                             vmem_limit_bytes=VMEM_LIMIT),
        name="rwkv_scan",
    )(masks, tri, r, r, v, v, kk, kk, lw[0], kd[0], bvec[0], lw[1], kd[1], bvec[1], s0)


def rwkv_branch(rkv, lora, p, s0, masks, tri, block_tokens):
    bsz, n, _ = rkv.shape

    def heads(t):
        return t.reshape(bsz, n, RW_HEADS, RW_HEAD)

    def flat(t):
        return t.reshape(bsz, n, RW_WIDTH)

    r, k, v = jnp.split(rkv, 3, axis=-1)
    c1 = RW_DECAY_LORA
    c2 = 2 * RW_DECAY_LORA
    c3 = c2 + RW_ICLR_LORA
    c4 = c3 + RW_ICLR_LORA
    w_lo = (lora[..., :c1], lora[..., c1:c2])
    a_lo = (lora[..., c2:c3], lora[..., c3:c4])
    kk = heads(k * p['rw_kk'])
    kk = flat(kk * lax.rsqrt(jnp.maximum(jnp.sum(kk * kk, axis=-1, keepdims=True), 1e-24)))
    lw, kd, bvec = [], [], []
    bonus = 0.0
    for d in range(2):
        w_log = -jax.nn.softplus(-(p['rw_w0'][d] + mm(jnp.tanh(w_lo[d]), p['rw_w2'][d]))) - 0.5
        lw.append(-jnp.exp(w_log))
        a = jax.nn.sigmoid(p['rw_a0'][d] + mm(a_lo[d], p['rw_a2'][d]))
        kd.append(k * (1.0 + (a - 1.0) * p['rw_ka']))
        bvec.append(kk * a)
        bonus = bonus + jnp.sum(heads(r * p['rw_rk']) * heads(kd[d]), axis=-1, keepdims=True) * heads(v)
    y_f, y_b, s_fin = rwkv_scan_pallas(r, v, kk, lw, kd, bvec, s0, masks, tri, block_tokens)
    return (y_f, y_b, flat(bonus), lora[..., c4:]), s_fin


def hyena_filter_spectra(n_tok, p):
    hp = lax.Precision.HIGHEST
    bands = (HY_POS_DIM - 1) // 2
    t = jnp.linspace(0.0, 1.0, n_tok, dtype=jnp.float32)[:, None]
    w = (2.0 * math.pi / n_tok) * jnp.arange(n_tok, dtype=jnp.float32)[:, None]
    f = jnp.linspace(1e-4, bands - 1, bands, dtype=jnp.float32)[None, :]
    feats = jnp.concatenate([t, jnp.cos(f * w), -jnp.sin(f * w)], axis=-1)
    h = jnp.sin(p['hy_f_freq1'] * (jnp.dot(feats, p['hy_f_w1'], precision=hp) + p['hy_f_b1']))
    h = jnp.sin(p['hy_f_freq2'] * (jnp.dot(h, p['hy_f_w2'], precision=hp) + p['hy_f_b2']))
    h = jnp.dot(h, p['hy_f_w3'], precision=hp).reshape(n_tok, HY_ORDER, 2, HY_WIDTH)
    rates = jnp.abs(jnp.linspace(math.log(HY_DECAY_TARGET) / HY_DECAY_PCT_SHORT,
                                 math.log(HY_DECAY_TARGET) / HY_DECAY_PCT_LONG, HY_WIDTH, dtype=jnp.float32))
    h = h * jnp.exp(-t * rates)[:, None, None, :]
    h_fwd, h_bwd = h[:, :, 0], h[:, :, 1]
    filt = jnp.concatenate([h_fwd, jnp.zeros_like(h_fwd[:1]), h_bwd[:0:-1]], axis=0)
    return jnp.fft.rfft(filt, axis=0)


def long_conv(z, k_spec, bias):
    n = z.shape[1]
    z_spec = jnp.fft.rfft(z, n=2 * n, axis=1)
    y = jnp.fft.irfft(z_spec * k_spec[None], n=2 * n, axis=1)[:, :n]
    return y + z * bias


def hyena_branch(streams, p):
    n = streams.shape[1]
    v, x1, x2 = jnp.split(streams, 3, axis=-1)
    k_spec = hyena_filter_spectra(n, p)
    z = v
    for o, gate in enumerate((x1, x2)):
        z = gate * long_conv(z, k_spec[:, o], p['hy_bias'][o])
    return z


HY_N = 2 * SEQ
HY_N1 = 64
HY_N2 = 128
HY_NH = HY_N1 // 2
HY_PITCH = 136
HY_LANES = 128


def hyena_dft_tables():
    k1 = np.arange(HY_N1)[:, None]
    n1 = np.arange(HY_NH)[None, :]
    n2 = np.arange(HY_N2)
    ph = -2 * np.pi * (k1 * n1 / HY_N1)[None] - 2 * np.pi * (n2[:, None, None] * k1[None] / HY_N)
    lhs1 = np.concatenate([np.cos(ph), np.sin(ph)], axis=1)
    lhs2 = np.concatenate([np.cos(ph).transpose(0, 2, 1), np.sin(ph).transpose(0, 2, 1)], axis=2) / HY_N
    kk = np.arange(HY_N2)
    ang = -2 * np.pi * np.outer(kk, kk) / HY_N2
    cr, ci = np.cos(ang), np.sin(ang)
    f_fwd = np.block([[cr, -ci], [ci, cr]])
    f_inv = np.block([[cr, ci], [-ci, cr]])

    def as_bf(a):
        return jnp.asarray(a, jnp.float32).astype(jnp.bfloat16)

    return as_bf(lhs1), as_bf(lhs2), as_bf(f_fwd), as_bf(f_inv)


def hyena_spectrum_layout(k_spec):
    w = k_spec.shape[1]
    full = jnp.concatenate([k_spec, jnp.conj(k_spec[-2:0:-1])], axis=0)
    parts = jnp.stack([jnp.real(full), jnp.imag(full)], axis=0).reshape(2, HY_N2, HY_N1, w)
    parts = jnp.transpose(parts, (2, 0, 1, 3)).reshape(HY_N1 * 2 * HY_N2, w)
    return jnp.transpose(parts.reshape(-1, w // HY_LANES, HY_LANES), (1, 0, 2)).astype(jnp.float32)


def _hy_kernel(z_ref, g_ref, k_ref, l1_ref, l2_ref, ff_ref, fi_ref, bias_ref, o_ref, a_ref):
    f32, bf16 = jnp.float32, jnp.bfloat16
    n2n, nh, pitch, half = HY_N2, HY_NH, HY_PITCH, HY_N1

    def stage1(q, c):
        slab = z_ref[0, pl.ds(pl.multiple_of(q * nh, nh), nh), :].astype(bf16)
        a_ref[pl.ds(pl.multiple_of(q * pitch, 8), 2 * half), :] = jnp.dot(l1_ref[q], slab, preferred_element_type=f32)
        return c

    lax.fori_loop(0, n2n, stage1, 0, unroll=8)

    def stage2(k, c):
        a = jnp.concatenate([a_ref[pl.ds(k, n2n, stride=pitch), :],
                             a_ref[pl.ds(half + k, n2n, stride=pitch), :]], axis=0).astype(bf16)
        x = jnp.dot(ff_ref[...], a, preferred_element_type=f32)
        xr, xi = x[:n2n], x[n2n:]
        base = pl.multiple_of(k * 2 * n2n, 2 * n2n)
        kr = k_ref[0, pl.ds(base, n2n), :]
        ki = k_ref[0, pl.ds(base + n2n, n2n), :]
        y = jnp.concatenate([xr * kr - xi * ki, xr * ki + xi * kr], axis=0).astype(bf16)
        b = jnp.dot(fi_ref[...], y, preferred_element_type=f32)
        a_ref[pl.ds(k, n2n, stride=pitch), :] = b[:n2n]
        a_ref[pl.ds(half + k, n2n, stride=pitch), :] = b[n2n:]
        return c

    lax.fori_loop(0, half, stage2, 0, unroll=4)

    def stage3(q, c):
        blk = a_ref[pl.ds(pl.multiple_of(q * pitch, 8), 2 * half), :].astype(bf16)
        y = jnp.dot(l2_ref[q], blk, preferred_element_type=f32)
        rows = pl.ds(pl.multiple_of(q * nh, nh), nh)
        o_ref[0, rows, :] = g_ref[0, rows, :] * (y + z_ref[0, rows, :] * bias_ref[...])
        return c

    lax.fori_loop(0, n2n, stage3, 0, unroll=8)


def hyena_long_conv_gated(zt, gt, spec, bias, tables):
    bsz, n, w = zt.shape
    l1, l2, ff, fi = tables
    nt = w // HY_LANES
    tok = pl.BlockSpec((1, n, HY_LANES), lambda t, b: (b, 0, t))
    return pl.pallas_call(
        _hy_kernel,
        out_shape=jax.ShapeDtypeStruct((bsz, n, w), jnp.float32),
        grid=(nt, bsz),
        in_specs=[tok, tok,
                  pl.BlockSpec((1,) + spec.shape[1:], lambda t, b: (t, 0, 0)),
                  pl.BlockSpec(l1.shape, lambda t, b: (0, 0, 0)),
                  pl.BlockSpec(l2.shape, lambda t, b: (0, 0, 0)),
                  pl.BlockSpec(ff.shape, lambda t, b: (0, 0)),
                  pl.BlockSpec(fi.shape, lambda t, b: (0, 0)),
                  pl.BlockSpec((1, HY_LANES), lambda t, b: (0, t))],
        out_specs=tok,
        scratch_shapes=[pltpu.VMEM((HY_N2 * HY_PITCH, HY_LANES), jnp.float32)],
        compiler_params=pltpu.CompilerParams(dimension_semantics=("parallel", "parallel"),
                                             vmem_limit_bytes=VMEM_LIMIT),
        name="hyena_conv",
    )(zt, gt, spec, l1, l2, ff, fi, bias.reshape(1, w))


def hyena_branch_long(streams, p):
    bsz, n, w3 = streams.shape
    k_spec = hyena_filter_spectra(n, p)
    tables = hyena_dft_tables()
    st = jnp.transpose(streams.reshape(bsz, HY_NH, HY_N2, w3), (0, 2, 1, 3)).reshape(bsz, n, w3)
    z, x1, x2 = jnp.split(st, 3, axis=-1)
    for o, gate in enumerate((x1, x2)):
        z = hyena_long_conv_gated(z, gate, hyena_spectrum_layout(k_spec[:, o]), p['hy_bias'][o], tables)
    w = z.shape[-1]
    return jnp.transpose(z.reshape(bsz, HY_N2, HY_NH, w), (0, 2, 1, 3)).reshape(bsz, n, w)


def _rms(x, gain):
    return x * lax.rsqrt(jnp.mean(x * x, axis=-1, keepdims=True) + NORM_EPS) * gain


def _conv3(x, w_ref, period):
    rows = x.shape[0]
    pos = lax.broadcasted_iota(jnp.int32, (rows, 1), 0) % period
    prev = jnp.where(pos == 0, 0.0, pltpu.roll(x, 1, axis=0))
    nxt = jnp.where(pos == period - 1, 0.0, pltpu.roll(x, rows - 1, axis=0))
    return prev * w_ref[0:1, :] + x * w_ref[1:2, :] + nxt * w_ref[2:3, :]


def _inproj_kernel(x_ref, mod_ref, gain_ref, ws5_ref, wrw_ref, wlo_ref, why_ref, wgt_ref, cwr_ref, cwh_ref, cbh_ref,
                   os5_ref, orw_ref, olo_ref, ohy_ref, ogt_ref, *, period):
    f32 = jnp.float32
    shift, scale = mod_ref[0, 0:1, :], mod_ref[0, 1:2, :]
    h = (_rms(x_ref[...], gain_ref[...]) * (1.0 + scale) + shift).astype(jnp.bfloat16)
    os5_ref[...] = jnp.dot(h, ws5_ref[...], preferred_element_type=f32)
    olo_ref[...] = jnp.dot(h, wlo_ref[...], preferred_element_type=f32)
    ogt_ref[...] = jnp.dot(h, wgt_ref[...], preferred_element_type=f32)
    orw_ref[...] = _conv3(jnp.dot(h, wrw_ref[...], preferred_element_type=f32), cwr_ref, period)
    ohy_ref[...] = _conv3(jnp.dot(h, why_ref[...], preferred_element_type=f32), cwh_ref, period) + cbh_ref[...]


def input_projection(x, mod, gain, w_parts, rw_conv_w, hy_conv_w, hy_conv_b, period):
    bsz, n, d = x.shape
    tm = min(MIX_BLOCK_ROWS, n)
    per_b = n // tm
    rows = bsz * n
    bmap = (lambda i: (i // per_b, 0, 0)) if mod.shape[0] == bsz else (lambda i: (0, 0, 0))
    wb = [w.astype(jnp.bfloat16) for w in w_parts]

    def const(a):
        return pl.BlockSpec(a.shape, lambda i: (0,) * a.ndim)

    cbh = hy_conv_b.reshape(1, -1)
    outs = pl.pallas_call(
        functools.partial(_inproj_kernel, period=period),
        out_shape=tuple(jax.ShapeDtypeStruct((rows, w.shape[1]), jnp.float32) for w in wb),
        grid=(rows // tm,),
        in_specs=[pl.BlockSpec((tm, d), lambda i: (i, 0)),
                  pl.BlockSpec((1, 3, d), bmap),
                  pl.BlockSpec((1, d), lambda i: (0, 0))]
                 + [const(w) for w in wb] + [const(rw_conv_w), const(hy_conv_w), const(cbh)],
        out_specs=tuple(pl.BlockSpec((tm, w.shape[1]), lambda i: (i, 0)) for w in wb),
        compiler_params=pltpu.CompilerParams(dimension_semantics=("parallel",), vmem_limit_bytes=VMEM_LIMIT),
        name="input_projection",
    )(x.reshape(rows, d), mod, gain.reshape(1, d), *wb, rw_conv_w, hy_conv_w, cbh)
    return tuple(o.reshape(bsz, n, o.shape[-1]) for o in outs)


def _merge_kernel(x_ref, mod_ref, gain_ref, ys5_ref, us5_ref, wf_ref, wb_ref, bonus_ref, glo_ref, hy_ref, gt_ref,
                  s5d_ref, glu_ref, lnw_ref, lnb_ref, g2_ref, havg_ref, brs_ref, brr_ref, brh_ref, ow_ref, o_ref):
    f32, bf16 = jnp.float32, jnp.bfloat16
    y = ys5_ref[...] + s5d_ref[...] * us5_ref[...]
    y = 0.5 * y * (1.0 + lax.erf(y * (1.0 / math.sqrt(2.0))))
    s5 = y * jax.nn.sigmoid(jnp.dot(y.astype(bf16), glu_ref[...], preferred_element_type=f32))
    wkv = wf_ref[...] + wb_ref[...]

    def head_mean(t):
        hi = t.astype(bf16)
        rem = t - hi.astype(f32)
        mid = rem.astype(bf16)
        lo = (rem - mid.astype(f32)).astype(bf16)
        hm = havg_ref[...]
        tot = (jnp.dot(hi, hm, preferred_element_type=f32) + jnp.dot(mid, hm, preferred_element_type=f32)
               + jnp.dot(lo, hm, preferred_element_type=f32))
        return tot * (1.0 / RW_HEAD)

    mu = head_mean(wkv)
    cen = wkv - mu
    var = head_mean(cen * cen)
    o = cen * lax.rsqrt(var + RW_GN_EPS) * lnw_ref[...] + lnb_ref[...] + bonus_ref[...]
    g = jnp.dot(jax.nn.sigmoid(glo_ref[...]).astype(bf16), g2_ref[...], preferred_element_type=f32)
    rw = o * g
    gt = jax.nn.sigmoid(gt_ref[...])
    d = x_ref.shape[-1]
    m = (gt[:, 0:d] * jnp.dot(s5.astype(bf16), brs_ref[...], preferred_element_type=f32)
         + gt[:, d:2 * d] * jnp.dot(rw.astype(bf16), brr_ref[...], preferred_element_type=f32)
         + gt[:, 2 * d:3 * d] * jnp.dot(hy_ref[...].astype(bf16), brh_ref[...], preferred_element_type=f32))
    yl = jnp.dot(m.astype(bf16), ow_ref[...], preferred_element_type=f32)
    o_ref[...] = x_ref[...] + mod_ref[0, 2:3, :] * _rms(yl, gain_ref[...])


def merge_block(x, mod, gain, ys5, us5, rw_parts, hy, gates, p):
    bsz, n, d = x.shape
    tm = min(MIX_BLOCK_ROWS, n)
    per_b = n // tm
    rows = bsz * n
    bmap = (lambda i: (i // per_b, 0, 0)) if mod.shape[0] == bsz else (lambda i: (0, 0, 0))

    def bf(a):
        return a.astype(jnp.bfloat16)

    def row(a):
        return pl.BlockSpec((tm, a.shape[-1]), lambda i: (i, 0))

    def const(a):
        return pl.BlockSpec(a.shape, lambda i: (0,) * a.ndim)

    hid = np.arange(RW_WIDTH) // RW_HEAD
    havg = jnp.asarray(hid[:, None] == hid[None, :], jnp.bfloat16)
    acts = [a.reshape(rows, a.shape[-1]) for a in (ys5, us5) + tuple(rw_parts) + (hy, gates)]
    consts = [p['s5_d'].reshape(1, -1), bf(p['s5_glu_w']), p['rw_ln_w'].reshape(1, -1), p['rw_ln_b'].reshape(1, -1),
              bf(p['rw_g2']), havg, bf(p['br_s5']), bf(p['br_rw']), bf(p['br_hy']), bf(p['out_w'])]
    out = pl.pallas_call(
        _merge_kernel,
        out_shape=jax.ShapeDtypeStruct((rows, d), jnp.float32),
        grid=(rows // tm,),
        in_specs=[pl.BlockSpec((tm, d), lambda i: (i, 0)), pl.BlockSpec((1, 3, d), bmap),
                  pl.BlockSpec((1, d), lambda i: (0, 0))] + [row(a) for a in acts] + [const(c) for c in consts],
        out_specs=pl.BlockSpec((tm, d), lambda i: (i, 0)),
        compiler_params=pltpu.CompilerParams(dimension_semantics=("parallel",), vmem_limit_bytes=VMEM_LIMIT),
        name="merge_block",
    )(x.reshape(rows, d), mod, gain.reshape(1, d), *acts, *consts)
    return out.reshape(bsz, n, d)


def token_mixer(x, ctx, mod_l, mod_c, g_pre, g_post, p, with_ctx_out):
    bsz = x.shape[0]
    cuts = [IN_S5, IN_S5 + IN_RW, IN_S5 + IN_RW + IN_LORA, IN_S5 + IN_RW + IN_LORA + IN_HY]
    edges = [0] + cuts + [IN_COLS]
    w_parts = [p['in_w'][:, edges[i]:edges[i + 1]] for i in range(5)]
    conv = (p['rw_conv_w'], p['hy_conv_w'], p['hy_conv_b'])
    u_l, rkv_l, lora_l, hy_l, gate_l = input_projection(x, mod_l, g_pre, w_parts, *conv, GRID_W)
    u_c, rkv_c, lora_c, hy_c, gate_c = input_projection(ctx, mod_c, g_pre, w_parts, *conv, CTX_LEN)
    masks, tri = rwkv_masks()
    zr = jnp.zeros((bsz, 2, RW_WIDTH // RW_LANES, RW_LANES, RW_LANES), jnp.float32)
    ys5_l, ys5_c = s5_branch(u_l, u_c, p)
    rw_c, rw_state = rwkv_branch(rkv_c, lora_c, p, zr, masks, tri, CTX_LEN)
    rw_l, _ = rwkv_branch(rkv_l, lora_l, p, rw_state, masks, tri, RW_BLOCK_TOKENS)
    hy_lat = hyena_branch_long(hy_l, p)
    x = merge_block(x, mod_l, g_post, ys5_l, u_l, rw_l, hy_lat, gate_l, p)
    if with_ctx_out:
        ctx = merge_block(ctx, mod_c, g_post, ys5_c, u_c, rw_c, hyena_branch(hy_c, p), gate_c, p)
    return x, ctx


def _ffn_kernel(x_ref, mod_ref, gain_ref, wg_ref, wu_ref, wd_ref, o_ref):
    x = x_ref[...]
    shift, scale, gate = mod_ref[0, 0:1, :], mod_ref[0, 1:2, :], mod_ref[0, 2:3, :]
    h = (_rms(x, gain_ref[0:1, :]) * (1.0 + scale) + shift).astype(jnp.bfloat16)
    g = jnp.dot(h, wg_ref[...], preferred_element_type=jnp.float32)
    u = jnp.dot(h, wu_ref[...], preferred_element_type=jnp.float32)
    a = (g * jax.nn.sigmoid(g) * u).astype(jnp.bfloat16)
    y = jnp.dot(a, wd_ref[...], preferred_element_type=jnp.float32)
    o_ref[...] = x + gate * _rms(y, gain_ref[1:2, :])


def ffn_block(x, mod, gains, wg, wu, wd):
    bsz, n, d = x.shape
    hid = wg.shape[1]
    tm = min(FFN_BLOCK_ROWS, n)
    per_b = n // tm
    xf = x.reshape(bsz * n, d)
    bmap = (lambda i: (i // per_b, 0, 0)) if mod.shape[0] == bsz else (lambda i: (0, 0, 0))
    out = pl.pallas_call(
        _ffn_kernel,
        out_shape=jax.ShapeDtypeStruct((bsz * n, d), jnp.float32),
        grid=(bsz * per_b,),
        in_specs=[pl.BlockSpec((tm, d), lambda i: (i, 0)),
                  pl.BlockSpec((1, 3, d), bmap),
                  pl.BlockSpec((2, d), lambda i: (0, 0)),
                  pl.BlockSpec((d, hid), lambda i: (0, 0)),
                  pl.BlockSpec((d, hid), lambda i: (0, 0)),
                  pl.BlockSpec((hid, d), lambda i: (0, 0))],
        out_specs=pl.BlockSpec((tm, d), lambda i: (i, 0)),
        compiler_params=pltpu.CompilerParams(dimension_semantics=("parallel",), vmem_limit_bytes=VMEM_LIMIT),
        name="ffn_block",
    )(xf, mod, gains, wg.astype(jnp.bfloat16), wu.astype(jnp.bfloat16), wd.astype(jnp.bfloat16))
    return out.reshape(bsz, n, d)


def _moe_kernel(be_ref, x_ref, gate_ref, wg_ref, wu_ref, wd_ref, o_ref):
    j = pl.program_id(1)
    xb = x_ref[...].astype(jnp.bfloat16)
    g = jnp.dot(xb, wg_ref[0], preferred_element_type=jnp.float32)
    u = jnp.dot(xb, wu_ref[0], preferred_element_type=jnp.float32)
    a = (g * jax.nn.sigmoid(g) * u).astype(jnp.bfloat16)
    y = jnp.dot(a, wd_ref[0], preferred_element_type=jnp.float32)

    @pl.when(j == 0)
    def _():
        o_ref[...] = y

    @pl.when(j > 0)
    def _():
        o_ref[...] += y

    @pl.when(j == pl.num_programs(1) - 1)
    def _():
        o_ref[...] *= gate_ref[...]


def moe_expert_blocks(xb, slot_gate, block_e, wg, wu, wd):
    cap, d = xb.shape
    n_blocks = cap // MOE_BLOCK
    hid = wg.shape[-1]
    nh = MOE_HIDDEN_TILES
    th = hid // nh

    def hidx(i, j):
        return j + (i % 2) * (nh - 1 - 2 * j)

    return pl.pallas_call(
        _moe_kernel,
        out_shape=jax.ShapeDtypeStruct((cap, d), jnp.float32),
        grid_spec=pltpu.PrefetchScalarGridSpec(
            num_scalar_prefetch=1,
            grid=(n_blocks, nh),
            in_specs=[pl.BlockSpec((MOE_BLOCK, d), lambda i, j, be: (i, 0)),
                      pl.BlockSpec((MOE_BLOCK, 1), lambda i, j, be: (i, 0)),
                      pl.BlockSpec((1, d, th), lambda i, j, be: (be[i], 0, hidx(i, j))),
                      pl.BlockSpec((1, d, th), lambda i, j, be: (be[i], 0, hidx(i, j))),
                      pl.BlockSpec((1, th, d), lambda i, j, be: (be[i], hidx(i, j), 0))],
            out_specs=pl.BlockSpec((MOE_BLOCK, d), lambda i, j, be: (i, 0))),
        compiler_params=pltpu.CompilerParams(dimension_semantics=("parallel", "arbitrary"),
                                             vmem_limit_bytes=VMEM_LIMIT),
        name="moe_experts",
    )(block_e, xb, slot_gate.reshape(cap, 1), wg.astype(jnp.bfloat16), wu.astype(jnp.bfloat16),
      wd.astype(jnp.bfloat16))


def moe_swiglu(h, router_w, wg, wu, wd):
    d_model = h.shape[-1]
    tok = h.reshape(-1, d_model)
    n = tok.shape[0]
    n_assign = n * TOP_K
    logits = jnp.dot(tok, router_w)
    top_logit, top_e = lax.top_k(logits, TOP_K)
    gate = jax.nn.softmax(top_logit, axis=-1).reshape(-1)
    flat_e = top_e.reshape(-1)
    order = jnp.argsort(flat_e)
    sorted_e = flat_e[order]
    sizes = jnp.zeros((N_EXPERTS,), jnp.int32).at[flat_e].add(1)
    padded = (sizes + MOE_BLOCK - 1) // MOE_BLOCK * MOE_BLOCK
    pad_end = jnp.cumsum(padded)
    pad_start = pad_end - padded
    grp_start = jnp.cumsum(sizes) - sizes
    slot = pad_start[sorted_e] + jnp.arange(n_assign, dtype=jnp.int32) - grp_start[sorted_e]
    n_blocks = -(-n_assign // MOE_BLOCK) + N_EXPERTS
    cap = n_blocks * MOE_BLOCK
    slot_tok = jnp.full((cap,), n, jnp.int32).at[slot].set((order // TOP_K).astype(jnp.int32))
    slot_gate = jnp.zeros((cap,), jnp.float32).at[slot].set(gate[order])
    block_start = jnp.arange(n_blocks, dtype=jnp.int32) * MOE_BLOCK
    block_e = jnp.minimum(jnp.sum(block_start[:, None] >= pad_end[None, :], axis=1), N_EXPERTS - 1)
    tok_pad = jnp.concatenate([tok, jnp.zeros((1, d_model), tok.dtype)], axis=0)
    xb = tok_pad[slot_tok]
    yb = moe_expert_blocks(xb, slot_gate, block_e.astype(jnp.int32), wg, wu, wd)
    pos = jnp.zeros((n_assign,), jnp.int32).at[order].set(slot).reshape(n, TOP_K)
    out = yb[pos[:, 0]]
    for kk in range(1, TOP_K):
        out = out + yb[pos[:, kk]]
    return out.reshape(h.shape)


def kernel(x, c, ctx, c_ctx, mod_w, mod_b, norm_g, in_w, s5_lam_re, s5_lam_im, s5_log_step, s5_b_re, s5_b_im, s5_c_re, s5_c_im, s5_d, s5_glu_w, rw_conv_w, rw_w0, rw_w2, rw_a0, rw_a2, rw_g2, rw_kk, rw_ka, rw_rk, rw_ln_w, rw_ln_b, hy_conv_w, hy_conv_b, hy_f_w1, hy_f_b1, hy_f_freq1, hy_f_w2, hy_f_b2, hy_f_freq2, hy_f_w3, hy_bias, br_s5, br_rw, br_hy, out_w, ffn_wg, ffn_wu, ffn_wd, moe_router, moe_wg, moe_wu, moe_wd):
    silu_c = jax.nn.silu(c)
    silu_cc = jax.nn.silu(c_ctx)
    for i in range(DEPTH):
        last = i == DEPTH - 1
        p = {
            'in_w': in_w[i],
            's5_lam_re': s5_lam_re[i], 's5_lam_im': s5_lam_im[i], 's5_log_step': s5_log_step[i],
            's5_b_re': s5_b_re[i], 's5_b_im': s5_b_im[i], 's5_c_re': s5_c_re[i], 's5_c_im': s5_c_im[i],
            's5_d': s5_d[i], 's5_glu_w': s5_glu_w[i],
            'rw_conv_w': rw_conv_w[i], 'rw_w0': rw_w0[i], 'rw_w2': rw_w2[i], 'rw_a0': rw_a0[i], 'rw_a2': rw_a2[i],
            'rw_g2': rw_g2[i], 'rw_kk': rw_kk[i], 'rw_ka': rw_ka[i], 'rw_rk': rw_rk[i],
            'rw_ln_w': rw_ln_w[i], 'rw_ln_b': rw_ln_b[i],
            'hy_conv_w': hy_conv_w[i], 'hy_conv_b': hy_conv_b[i], 'hy_f_w1': hy_f_w1[i], 'hy_f_b1': hy_f_b1[i],
            'hy_f_freq1': hy_f_freq1[i], 'hy_f_w2': hy_f_w2[i], 'hy_f_b2': hy_f_b2[i], 'hy_f_freq2': hy_f_freq2[i],
            'hy_f_w3': hy_f_w3[i], 'hy_bias': hy_bias[i],
            'br_s5': br_s5[i], 'br_rw': br_rw[i], 'br_hy': br_hy[i], 'out_w': out_w[i],
        }
        ml = jnp.split((silu_c @ mod_w[i] + mod_b[i])[:, None, :], 6, axis=-1)
        mc = jnp.split(silu_cc @ mod_w[i] + mod_b[i], 6, axis=-1)
        g_pre_m, g_post_m, g_pre_f, g_post_f = norm_g[i]
        if i % 2 == 0:
            def channel_mix(t, m, j=i // 2):
                return ffn_block(t, m, norm_g[i, 2:4], ffn_wg[j], ffn_wu[j], ffn_wd[j])
        else:
            def channel_mix(t, m, j=i // 2):
                h = rms_norm(t, g_pre_f) * (1 + m[:, 1:2]) + m[:, 0:1]
                y = moe_swiglu(h, moe_router[j], moe_wg[j], moe_wu[j], moe_wd[j])
                return t + m[:, 2:3] * rms_norm(y, g_post_f)
        x, ctx = token_mixer(x, ctx, jnp.concatenate(ml[0:3], axis=1), jnp.stack(mc[0:3])[None],
                             g_pre_m, g_post_m, p, not last)
        x = channel_mix(x, jnp.concatenate(ml[3:6], axis=1))
        if not last:
            ctx = channel_mix(ctx, jnp.stack(mc[3:6])[None])
    return x
```

```python
import math
import functools
import numpy as np
import jax
import jax.numpy as jnp
from jax import lax
from jax.experimental import pallas as pl
from jax.experimental.pallas import tpu as pltpu

D_MODEL = 1024
BATCH = 8
SEQ = 4096
DEPTH = 2
GRID_W = 64
CTX_LEN = 256
NORM_EPS = 1e-6
S5_WIDTH = D_MODEL // 4
S5_GROUP = 16
S5_GROUPS = S5_WIDTH // S5_GROUP
S5_STATE = 64
RW_WIDTH = D_MODEL // 2
RW_HEAD = 64
RW_HEADS = RW_WIDTH // RW_HEAD
RW_DECAY_LORA = 64
RW_ICLR_LORA = 64
RW_GATE_LORA = 128
RW_GN_EPS = 64e-5
HY_WIDTH = D_MODEL // 4
HY_ORDER = 2
HY_POS_DIM = 33
HY_FILTER_HIDDEN = 64
HY_DECAY_TARGET = 1e-2
HY_DECAY_PCT_SHORT = 0.3
HY_DECAY_PCT_LONG = 1.5
SHORT_CONV = 3
N_BRANCH = 3
FFN_HIDDEN = 2816
N_EXPERTS = 8
TOP_K = 2
EXPERT_HIDDEN = 3584
MOE_BLOCK = 256
IN_S5 = S5_WIDTH
IN_RW = 3 * RW_WIDTH
IN_LORA = 2 * RW_DECAY_LORA + 2 * RW_ICLR_LORA + RW_GATE_LORA
IN_HY = (HY_ORDER + 1) * HY_WIDTH
IN_GATE = N_BRANCH * D_MODEL
IN_COLS = IN_S5 + IN_RW + IN_LORA + IN_HY + IN_GATE

VMEM_LIMIT = 56 * 1024 * 1024
S5_CHUNK = 64
RW_CHUNK = 64
RW_HG = 4
RW_LANES = RW_HG * RW_HEAD
RW_BLOCK_TOKENS = 512
MOE_HIDDEN_TILES = 2
FFN_BLOCK_ROWS = 256
MIX_BLOCK_ROWS = 256


def _mm_kernel(a_ref, b_ref, o_ref):
    o_ref[...] = jnp.dot(a_ref[...].astype(jnp.bfloat16), b_ref[...],
                         preferred_element_type=jnp.float32)


def _pick(n, cands):
    for c in cands:
        if n % c == 0:
            return c
    return n


def pmatmul(a, b):
    m, k = a.shape
    n = b.shape[1]
    b = b.astype(jnp.bfloat16)
    tm = _pick(m, (512, 256, 128, 8))
    tn = _pick(n, (1024, 768, 512, 384, 256, 128))
    return pl.pallas_call(
        _mm_kernel,
        out_shape=jax.ShapeDtypeStruct((m, n), jnp.float32),
        grid=(n // tn, m // tm),
        in_specs=[pl.BlockSpec((tm, k), lambda j, i: (i, 0)),
                  pl.BlockSpec((k, tn), lambda j, i: (0, j))],
        out_specs=pl.BlockSpec((tm, tn), lambda j, i: (i, j)),
        compiler_params=pltpu.CompilerParams(
            dimension_semantics=("parallel", "parallel"),
            vmem_limit_bytes=VMEM_LIMIT),
        name="mm",
    )(a, b)


def mm(a, b):
    lead = a.shape[:-1]
    return pmatmul(a.reshape(-1, a.shape[-1]), b).reshape(lead + (b.shape[1],))


def rms_norm(x, gain):
    y = x * lax.rsqrt(jnp.mean(x * x, axis=-1, keepdims=True) + NORM_EPS)
    return y * gain


def short_conv(x, w, b=None):
    half = SHORT_CONV // 2
    n = x.shape[1]
    xp = jnp.pad(x, ((0, 0), (half, half), (0, 0)))
    y = sum(xp[:, j:j + n] * w[j] for j in range(SHORT_CONV))
    return y if b is None else y + b


def latent_short_conv(x, w, b=None):
    bsz, n_tok, ch = x.shape
    rows = n_tok // GRID_W
    y = short_conv(x.reshape(bsz * rows, GRID_W, ch), w, b)
    return y.reshape(bsz, n_tok, ch)


def s5_chunk_operators(lam_re, lam_im, log_step, b_re, b_im, c_re, c_im):
    T, G, P, GC = S5_CHUNK, S5_GROUPS, S5_STATE, S5_GROUP
    hp = lax.Precision.HIGHEST
    step = jnp.exp(log_step)[..., None]
    lr, li = lam_re, lam_im
    tau = jnp.arange(T + 1, dtype=jnp.float32)[:, None, None, None]
    mag = jnp.exp(lr * step * tau)
    ang = li * step * tau
    e_re, e_im = mag * jnp.cos(ang), mag * jnp.sin(ang)
    ab_re, ab_im = e_re[1], e_im[1]
    den = lr * lr + li * li
    nr, ni = ab_re - 1.0, ab_im
    q_re = (nr * lr + ni * li) / den
    q_im = (ni * lr - nr * li) / den
    bb_re = q_re[..., None] * b_re - q_im[..., None] * b_im
    bb_im = q_re[..., None] * b_im + q_im[..., None] * b_re
    eb_re = e_re[..., None] * bb_re - e_im[..., None] * bb_im
    eb_im = e_re[..., None] * bb_im + e_im[..., None] * bb_re
    ktau = (jnp.einsum('dgip,tdgpj->tdgij', c_re, eb_re[:T], precision=hp)
            - jnp.einsum('dgip,tdgpj->tdgij', c_im, eb_im[:T], precision=hp))
    t_idx = jnp.arange(T)
    lag = t_idx[:, None] - t_idx[None, :]
    kf = jnp.where((lag >= 0)[:, :, None, None, None], ktau[jnp.clip(lag, 0, T - 1), 0], 0.0)
    kb = jnp.where((lag <= 0)[:, :, None, None, None], ktau[jnp.clip(-lag, 0, T - 1), 1], 0.0)
    kt = jnp.transpose(kf + kb, (2, 1, 4, 0, 3)).reshape(G, T * GC, T * GC)
    wf_re, wf_im = eb_re[T - 1 - t_idx, 0], eb_im[T - 1 - t_idx, 0]
    wb_re, wb_im = eb_re[t_idx, 1], eb_im[t_idx, 1]
    win = jnp.stack([wf_re, wf_im, wb_re, wb_im], axis=0)
    win = jnp.transpose(win, (2, 1, 4, 0, 3)).reshape(G, T * GC, 4 * P)
    ef_re, ef_im = e_re[t_idx + 1, 0], e_im[t_idx + 1, 0]
    eb2_re, eb2_im = e_re[T - t_idx, 1], e_im[T - t_idx, 1]

    def readout(cr, ci, er, ei):
        re = cr[None] * er[:, :, None, :] - ci[None] * ei[:, :, None, :]
        im = -(cr[None] * ei[:, :, None, :] + ci[None] * er[:, :, None, :])
        return re, im

    of_re, of_im = readout(c_re[0], c_im[0], ef_re, ef_im)
    ob_re, ob_im = readout(c_re[1], c_im[1], eb2_re, eb2_im)
    wout = jnp.stack([of_re, of_im, ob_re, ob_im], axis=0)
    wout = jnp.transpose(wout, (2, 0, 4, 1, 3)).reshape(G, 4 * P, T * GC)
    at_re, at_im = e_re[T], e_im[T]
    apow = jnp.concatenate([at_re[0], at_re[0], at_re[1], at_re[1]], axis=-1)[:, None, :]
    aimg = jnp.concatenate([-at_im[0], at_im[0], -at_im[1], at_im[1]], axis=-1)[:, None, :]
    return kt.astype(jnp.bfloat16), win.astype(jnp.bfloat16), wout.astype(jnp.bfloat16), apow, aimg


def _s5_kernel(x_ref, kt_ref, win_ref, wout_ref, apow_ref, aimg_ref, h0_ref, y_ref, hfin_ref, hin_ref, *,
               n_chunks, bsz):
    P2 = 2 * S5_STATE
    xb = x_ref[0].astype(jnp.bfloat16)
    hin_ref[...] = jnp.dot(xb, win_ref[0], preferred_element_type=jnp.float32)
    ap = apow_ref[0]
    ai = aimg_ref[0]
    apf, aif = ap[:, :P2], ai[:, :P2]
    apb, aib = ap[:, P2:], ai[:, P2:]

    def cmul(h, a_p, a_i):
        return h * a_p + pltpu.roll(h, S5_STATE, axis=1) * a_i

    def body(c, carry):
        hf, hb = carry
        rf = pl.ds(pl.multiple_of(c * bsz, bsz), bsz)
        rb = pl.ds(pl.multiple_of((n_chunks - 1 - c) * bsz, bsz), bsz)
        df = hin_ref[rf, :P2]
        db = hin_ref[rb, P2:]
        hin_ref[rf, :P2] = hf
        hin_ref[rb, P2:] = hb
        return cmul(hf, apf, aif) + df, cmul(hb, apb, aib) + db

    h0 = h0_ref[0]
    hf, hb = lax.fori_loop(0, n_chunks, body, (h0[:, :P2], h0[:, P2:]))
    hfin_ref[0, :, :P2] = hf
    hfin_ref[0, :, P2:] = hb
    y = jnp.dot(xb, kt_ref[0], preferred_element_type=jnp.float32)
    y = y + jnp.dot(hin_ref[...].astype(jnp.bfloat16), wout_ref[0], preferred_element_type=jnp.float32)
    y_ref[0] = y


def s5_scan_pallas(u, ops, h0):
    kt, win, wout, apow, aimg = ops
    bsz, n, _ = u.shape
    T, G, P, GC = S5_CHUNK, S5_GROUPS, S5_STATE, S5_GROUP
    nc = n // T
    rows = nc * bsz
    x = jnp.transpose(u.reshape(bsz, nc, T, G, GC), (3, 1, 0, 2, 4)).reshape(G, rows, T * GC)
    y, hfin = pl.pallas_call(
        functools.partial(_s5_kernel, n_chunks=nc, bsz=bsz),
        out_shape=(jax.ShapeDtypeStruct((G, rows, T * GC), jnp.float32),
                   jax.ShapeDtypeStruct((G, bsz, 4 * P), jnp.float32)),
        grid=(G,),
        in_specs=[pl.BlockSpec((1, rows, T * GC), lambda g: (g, 0, 0)),
                  pl.BlockSpec((1, T * GC, T * GC), lambda g: (g, 0, 0)),
                  pl.BlockSpec((1, T * GC, 4 * P), lambda g: (g, 0, 0)),
                  pl.BlockSpec((1, 4 * P, T * GC), lambda g: (g, 0, 0)),
                  pl.BlockSpec((1, 1, 4 * P), lambda g: (g, 0, 0)),
                  pl.BlockSpec((1, 1, 4 * P), lambda g: (g, 0, 0)),
                  pl.BlockSpec((1, bsz, 4 * P), lambda g: (g, 0, 0))],
        out_specs=(pl.BlockSpec((1, rows, T * GC), lambda g: (g, 0, 0)),
                   pl.BlockSpec((1, bsz, 4 * P), lambda g: (g, 0, 0))),
        scratch_shapes=[pltpu.VMEM((rows, 4 * P), jnp.float32)],
        compiler_params=pltpu.CompilerParams(dimension_semantics=("parallel",),
                                             vmem_limit_bytes=VMEM_LIMIT),
        name="s5_scan",
    )(x, kt, win, wout, apow, aimg, h0)
    y = jnp.transpose(y.reshape(G, nc, bsz, T, GC), (2, 1, 3, 0, 4)).reshape(bsz, n, G * GC)
    return y, hfin


def s5_branch(u, u_ctx, p):
    ops = s5_chunk_operators(p['s5_lam_re'], p['s5_lam_im'], p['s5_log_step'], p['s5_b_re'], p['s5_b_im'],
                             p['s5_c_re'], p['s5_c_im'])
    h0 = jnp.zeros((S5_GROUPS, u.shape[0], 4 * S5_STATE), jnp.float32)
    y_ctx, h_ctx = s5_scan_pallas(u_ctx, ops, h0)
    y_lat, _ = s5_scan_pallas(u, ops, h_ctx)
    return y_lat, y_ctx


_NT = (((1,), (1,)), ((), ()))

_M_SAME, _M_EYE, _M_STRICT, _M_INCL, _M_LEVEL0 = 0, 1, 2, 4, 6
_N_LEVELS = 6


def rwkv_masks():
    C, n = RW_CHUNK, RW_LANES
    row = np.arange(n)[:, None]
    col = np.arange(n)[None, :]
    same = (row // C) == (col // C)
    out = [same, row == col]
    per_dir = []
    for reverse in (False, True):
        t, j = (row % C, col % C) if not reverse else (col % C, row % C)
        per_dir.append((same & (j < t), same & (j <= t),
                        [same & ((t // s) % 2 == 1) & ((j // s) == (t // s) - 1) for s in (1, 2, 4, 8, 16, 32)]))
    out += [per_dir[0][0], per_dir[1][0], per_dir[0][1], per_dir[1][1]]
    for l in range(_N_LEVELS):
        out += [per_dir[0][2][l], per_dir[1][2][l]]
    tt, jj = np.arange(C)[:, None], np.arange(C)[None, :]
    tri = np.stack([jj <= tt, jj >= tt])
    wide = np.stack([np.tile(m, (1, RW_HG)) for m in (jj < tt, jj > tt, jj <= tt, jj >= tt)])
    return (jnp.asarray(np.stack(out), jnp.float32), jnp.asarray(wide, jnp.float32),
            jnp.asarray(tri, jnp.bfloat16))


def _bdot(a, b, dims=None):
    a = a.astype(jnp.bfloat16)
    b = b.astype(jnp.bfloat16)
    if dims is None:
        return jnp.dot(a, b, preferred_element_type=jnp.float32)
    return lax.dot_general(a, b, dims, preferred_element_type=jnp.float32)


def _rwkv_chunk(S, r, lw, k, v, kk, b, m_ref, w_ref, tri, d):
    C = RW_CHUNK
    same = m_ref[_M_SAME]
    lw_hi = lw.astype(jnp.bfloat16)
    rem = lw - lw_hi.astype(jnp.float32)
    lw_mid = rem.astype(jnp.bfloat16)
    lw_lo = (rem - lw_mid.astype(jnp.float32)).astype(jnp.bfloat16)
    cl = (jnp.dot(tri, lw_hi, preferred_element_type=jnp.float32)
          + jnp.dot(tri, lw_mid, preferred_element_type=jnp.float32)
          + jnp.dot(tri, lw_lo, preferred_element_type=jnp.float32))
    yield
    tot = jnp.sum(lw, axis=0, keepdims=True)
    e_neg = jnp.exp(-cl)
    e_end = jnp.exp(tot - cl)
    a_t = -kk * jnp.exp(cl - lw)
    r_t = r * jnp.exp(cl)
    b_t = b * e_neg
    k_t = k * e_neg
    b_h = b * e_end
    k_h = k * e_end
    p_c = jnp.exp(tot)

    def tile(x):
        return jnp.concatenate([x] * RW_HG, axis=0)

    def stack(x):
        return tile(x) * same

    def unstack(z):
        acc = z[0:C]
        for h in range(1, RW_HG):
            acc = acc + z[h * C:(h + 1) * C]
        return acc

    n = RW_LANES
    sa = stack(a_t)
    a_ab = _bdot(sa, tile(b_t), _NT) * m_ref[_M_STRICT + d]
    wide = _bdot(jnp.concatenate([a_t, r_t], axis=0), jnp.concatenate([stack(b_t), stack(k_t)], axis=0), _NT)
    yield
    a_ak = wide[:C, n:] * w_ref[d]
    a_rb = wide[C:, :n] * w_ref[2 + d]
    a_rk = wide[C:, n:] * w_ref[2 + d]
    dinv = m_ref[_M_EYE] + a_ab * m_ref[_M_LEVEL0 + d]
    for l in range(1, _N_LEVELS):
        inner = _bdot(a_ab * m_ref[_M_LEVEL0 + 2 * l + d], dinv)
        yield
        dinv = dinv + _bdot(dinv, inner)
        yield
    av = _bdot(jnp.concatenate([a_ak, a_rk], axis=0), stack(v))
    yield
    akv = av[:C]
    arkv = av[C:]
    mu = _bdot(unstack(dinv), jnp.concatenate([sa, stack(akv)], axis=1))
    yield
    m1 = mu[:, :n]
    u0 = mu[:, n:]
    my = _bdot(a_rb, jnp.concatenate([stack(m1), stack(u0)], axis=1))
    yield
    m2 = r_t + my[:, :n]
    y0 = my[:, n:] + arkv
    mut = jnp.concatenate([m1, u0], axis=1).T
    gh = _bdot(mut, b_h)
    yield
    g = m_ref[_M_EYE] * p_c + gh[:n] * same
    hmat = (gh[n:] + _bdot(v.T, k_h)) * same
    y = _bdot(m2, S, _NT) + y0
    s_new = _bdot(S, g) + hmat
    return s_new, y


def _lockstep(gens):
    results = [None] * len(gens)
    active = list(range(len(gens)))
    while active:
        for i in list(active):
            try:
                next(gens[i])
            except StopIteration as stop:
                results[i] = stop.value
                active.remove(i)
    return results


def _rwkv_kernel(m_ref, w_ref, tri_ref, rf_ref, rb_ref, vf_ref, vb_ref, kkf_ref, kkb_ref, lwf_ref, kf_ref, bf_ref,
                 lwb_ref, kb_ref, bb_ref, s0_ref, yf_ref, yb_ref, sfin_ref, s_scr, *, n_chunks, n_groups):
    i = pl.program_id(1)
    C = RW_CHUNK

    @pl.when(i == 0)
    def _():
        s_scr[...] = s0_ref[0]

    tri_f = tri_ref[0]
    tri_b = tri_ref[1]

    def body(c, carry):
        rf = pl.ds(pl.multiple_of(c * C, C), C)
        rb = pl.ds(pl.multiple_of((n_chunks - 1 - c) * C, C), C)
        gens = []
        for g in range(n_groups):
            ln = slice(g * RW_LANES, (g + 1) * RW_LANES)
            gens.append(_rwkv_chunk(s_scr[0, g], rf_ref[0, rf, ln], lwf_ref[0, rf, ln], kf_ref[0, rf, ln],
                                    vf_ref[0, rf, ln], kkf_ref[0, rf, ln], bf_ref[0, rf, ln], m_ref, w_ref, tri_f, 0))
            gens.append(_rwkv_chunk(s_scr[1, g], rb_ref[0, rb, ln], lwb_ref[0, rb, ln], kb_ref[0, rb, ln],
                                    vb_ref[0, rb, ln], kkb_ref[0, rb, ln], bb_ref[0, rb, ln], m_ref, w_ref, tri_b, 1))
        out = _lockstep(gens)
        for g in range(n_groups):
            ln = slice(g * RW_LANES, (g + 1) * RW_LANES)
            (s_f, y_f), (s_b, y_b) = out[2 * g], out[2 * g + 1]
            s_scr[0, g] = s_f
            s_scr[1, g] = s_b
            yf_ref[0, rf, ln] = y_f
            yb_ref[0, rb, ln] = y_b
        return carry

    lax.fori_loop(0, n_chunks, body, 0)

    @pl.when(i == pl.num_programs(1) - 1)
    def _():
        sfin_ref[0] = s_scr[...]


def rwkv_scan_pallas(rkv, kk, lw, kd, bvec, s0, mask_set, block_tokens):
    bsz, n, width = kk.shape
    masks, wide, tri = mask_set
    ng = width // RW_LANES
    tb = block_tokens
    nb = n // tb
    fwd = pl.BlockSpec((1, tb, width), lambda b, i: (b, i, 0))
    bwd = pl.BlockSpec((1, tb, width), lambda b, i: (b, nb - 1 - i, 0))
    r_fwd, r_bwd = fwd, bwd
    v_fwd = pl.BlockSpec((1, tb, width), lambda b, i: (b, i, 2))
    v_bwd = pl.BlockSpec((1, tb, width), lambda b, i: (b, nb - 1 - i, 2))
    state_spec = pl.BlockSpec((1, 2, ng, RW_LANES, RW_LANES), lambda b, i: (b, 0, 0, 0, 0))
    return pl.pallas_call(
        functools.partial(_rwkv_kernel, n_chunks=tb // RW_CHUNK, n_groups=ng),
        out_shape=(jax.ShapeDtypeStruct((bsz, n, width), jnp.float32),
                   jax.ShapeDtypeStruct((bsz, n, width), jnp.float32),
                   jax.ShapeDtypeStruct(s0.shape, jnp.float32)),
        grid=(bsz, nb),
        in_specs=[pl.BlockSpec(masks.shape, lambda b, i: (0, 0, 0)),
                  pl.BlockSpec(wide.shape, lambda b, i: (0, 0, 0)),
                  pl.BlockSpec(tri.shape, lambda b, i: (0, 0, 0)),
                  r_fwd, r_bwd, v_fwd, v_bwd, fwd, bwd,
                  fwd, fwd, fwd, bwd, bwd, bwd, state_spec],
        out_specs=(fwd, bwd, state_spec),
        scratch_shapes=[pltpu.VMEM((2, ng, RW_LANES, RW_LANES), jnp.float32)],
        compiler_params=pltpu.CompilerParams(dimension_semantics=("parallel", "arbitrary"),
                                             vmem_limit_bytes=VMEM_LIMIT),
        name="rwkv_scan",
    )(masks, wide, tri, rkv, rkv, rkv, rkv, kk, kk, lw[0], kd[0], bvec[0], lw[1], kd[1], bvec[1], s0)


def _head_sum(t, hm):
    f32, bf16 = jnp.float32, jnp.bfloat16
    hi = t.astype(bf16)
    rem = t - hi.astype(f32)
    mid = rem.astype(bf16)
    lo = (rem - mid.astype(f32)).astype(bf16)
    return (jnp.dot(hi, hm, preferred_element_type=f32) + jnp.dot(mid, hm, preferred_element_type=f32)
            + jnp.dot(lo, hm, preferred_element_type=f32))


def _rwkv_prep_kernel(rkv_ref, lora_ref, vec_ref, w0_ref, a0_ref, w2_ref, a2_ref, hsum_ref,
                      kk_ref, lw0_ref, kd0_ref, b0_ref, lw1_ref, kd1_ref, b1_ref, bonus_ref):
    f32, bf16 = jnp.float32, jnp.bfloat16
    w = RW_WIDTH
    r = rkv_ref[:, 0:w]
    k = rkv_ref[:, w:2 * w]
    v = rkv_ref[:, 2 * w:3 * w]
    hm = hsum_ref[...]
    kk = k * vec_ref[0:1, :]
    kk = kk * lax.rsqrt(jnp.maximum(_head_sum(kk * kk, hm), 1e-24))
    kk_ref[...] = kk
    w_lo = jnp.tanh(lora_ref[:, 0:128]).astype(bf16)
    a_lo = lora_ref[:, 128:256].astype(bf16)
    rrk = r * vec_ref[2:3, :]
    bonus = None
    outs = ((lw0_ref, kd0_ref, b0_ref), (lw1_ref, kd1_ref, b1_ref))
    for d in range(2):
        x = w0_ref[d:d + 1, :] + jnp.dot(w_lo, w2_ref[d], preferred_element_type=f32)
        w_log = -(jnp.maximum(-x, 0.0) + jnp.log(1.0 + jnp.exp(-jnp.abs(x)))) - 0.5
        a = jax.nn.sigmoid(a0_ref[d:d + 1, :] + jnp.dot(a_lo, a2_ref[d], preferred_element_type=f32))
        kd = k * (1.0 + (a - 1.0) * vec_ref[1:2, :])
        outs[d][0][...] = -jnp.exp(w_log)
        outs[d][1][...] = kd
        outs[d][2][...] = kk * a
        term = _head_sum(rrk * kd, hm)
        bonus = term if bonus is None else bonus + term
    bonus_ref[...] = bonus * v


def rwkv_prep(rkv, lora, p):
    bsz, n, _ = rkv.shape
    rows = bsz * n
    tm = min(MIX_BLOCK_ROWS, n)
    w = RW_WIDTH
    hid = np.arange(w) // RW_HEAD
    hsum = jnp.asarray(hid[:, None] == hid[None, :], jnp.bfloat16)
    vec = jnp.stack([p['rw_kk'], p['rw_ka'], p['rw_rk']])
    zero = jnp.zeros((RW_DECAY_LORA, w), jnp.float32)
    w2 = jnp.stack([jnp.concatenate([p['rw_w2'][0], zero]), jnp.concatenate([zero, p['rw_w2'][1]])])
    a2 = jnp.stack([jnp.concatenate([p['rw_a2'][0], zero]), jnp.concatenate([zero, p['rw_a2'][1]])])
    consts = [vec, p['rw_w0'], p['rw_a0'], w2.astype(jnp.bfloat16), a2.astype(jnp.bfloat16), hsum]

    def const(a):
        return pl.BlockSpec(a.shape, lambda i: (0,) * a.ndim)

    outs = pl.pallas_call(
        _rwkv_prep_kernel,
        out_shape=tuple(jax.ShapeDtypeStruct((rows, w), jnp.float32) for _ in range(8)),
        grid=(rows // tm,),
        in_specs=[pl.BlockSpec((tm, 3 * w), lambda i: (i, 0)), pl.BlockSpec((tm, lora.shape[-1]), lambda i: (i, 0))]
                 + [const(c) for c in consts],
        out_specs=tuple(pl.BlockSpec((tm, w), lambda i: (i, 0)) for _ in range(8)),
        compiler_params=pltpu.CompilerParams(dimension_semantics=("parallel",), vmem_limit_bytes=VMEM_LIMIT),
        name="rwkv_prep",
    )(rkv.reshape(rows, 3 * w), lora.reshape(rows, -1), *consts)
    kk, lw0, kd0, b0, lw1, kd1, b1, bonus = [o.reshape(bsz, n, w) for o in outs]
    return kk, (lw0, lw1), (kd0, kd1), (b0, b1), bonus


def rwkv_branch(rkv, lora, p, s0, mask_set, block_tokens):
    kk, lw, kd, bvec, bonus = rwkv_prep(rkv, lora, p)
    y_f, y_b, s_fin = rwkv_scan_pallas(rkv, kk, lw, kd, bvec, s0, mask_set, block_tokens)
    return (y_f, y_b, bonus, lora[..., IN_LORA - RW_GATE_LORA:]), s_fin


def hyena_filter_spectra(n_tok, p):
    hp = lax.Precision.HIGHEST
    bands = (HY_POS_DIM - 1) // 2
    t = jnp.linspace(0.0, 1.0, n_tok, dtype=jnp.float32)[:, None]
    w = (2.0 * math.pi / n_tok) * jnp.arange(n_tok, dtype=jnp.float32)[:, None]
    f = jnp.linspace(1e-4, bands - 1, bands, dtype=jnp.float32)[None, :]
    feats = jnp.concatenate([t, jnp.cos(f * w), -jnp.sin(f * w)], axis=-1)
    h = jnp.sin(p['hy_f_freq1'] * (jnp.dot(feats, p['hy_f_w1'], precision=hp) + p['hy_f_b1']))
    h = jnp.sin(p['hy_f_freq2'] * (jnp.dot(h, p['hy_f_w2'], precision=hp) + p['hy_f_b2']))
    h = jnp.dot(h, p['hy_f_w3'], precision=hp).reshape(n_tok, HY_ORDER, 2, HY_WIDTH)
    rates = jnp.abs(jnp.linspace(math.log(HY_DECAY_TARGET) / HY_DECAY_PCT_SHORT,
                                 math.log(HY_DECAY_TARGET) / HY_DECAY_PCT_LONG, HY_WIDTH, dtype=jnp.float32))
    h = h * jnp.exp(-t * rates)[:, None, None, :]
    h_fwd, h_bwd = h[:, :, 0], h[:, :, 1]
    filt = jnp.concatenate([h_fwd, jnp.zeros_like(h_fwd[:1]), h_bwd[:0:-1]], axis=0)
    return jnp.fft.rfft(filt, axis=0)


def long_conv(z, k_spec, bias):
    n = z.shape[1]
    z_spec = jnp.fft.rfft(z, n=2 * n, axis=1)
    y = jnp.fft.irfft(z_spec * k_spec[None], n=2 * n, axis=1)[:, :n]
    return y + z * bias


def hyena_branch(streams, p):
    n = streams.shape[1]
    v, x1, x2 = jnp.split(streams, 3, axis=-1)
    k_spec = hyena_filter_spectra(n, p)
    z = v
    for o, gate in enumerate((x1, x2)):
        z = gate * long_conv(z, k_spec[:, o], p['hy_bias'][o])
    return z


HY_N = 2 * SEQ
HY_N1 = 64
HY_N2 = 128
HY_NH = HY_N1 // 2
HY_PITCH = 136
HY_LANES = 128


def hyena_dft_tables():
    k1 = np.arange(HY_N1)[:, None]
    n1 = np.arange(HY_NH)[None, :]
    n2 = np.arange(HY_N2)
    ph = -2 * np.pi * (k1 * n1 / HY_N1)[None] - 2 * np.pi * (n2[:, None, None] * k1[None] / HY_N)
    lhs1 = np.concatenate([np.cos(ph), np.sin(ph)], axis=1)
    lhs2 = np.concatenate([np.cos(ph).transpose(0, 2, 1), np.sin(ph).transpose(0, 2, 1)], axis=2) / HY_N
    kk = np.arange(HY_N2)
    ang = -2 * np.pi * np.outer(kk, kk) / HY_N2
    cr, ci = np.cos(ang), np.sin(ang)
    f_fwd = np.block([[cr, -ci], [ci, cr]])
    f_inv = np.block([[cr, ci], [-ci, cr]])

    def as_bf(a):
        return jnp.asarray(a, jnp.float32).astype(jnp.bfloat16)

    return as_bf(lhs1), as_bf(lhs2), as_bf(f_fwd), as_bf(f_inv)


def hyena_spectrum_layout(k_spec):
    w = k_spec.shape[1]
    full = jnp.concatenate([k_spec, jnp.conj(k_spec[-2:0:-1])], axis=0)
    parts = jnp.stack([jnp.real(full), jnp.imag(full)], axis=0).reshape(2, HY_N2, HY_N1, w)
    parts = jnp.transpose(parts, (2, 0, 1, 3)).reshape(HY_N1 * 2 * HY_N2, w)
    return jnp.transpose(parts.reshape(-1, w // HY_LANES, HY_LANES), (1, 0, 2)).astype(jnp.float32)


def _hy_kernel(z_ref, g_ref, k_ref, l1_ref, l2_ref, ff_ref, fi_ref, bias_ref, o_ref, a_ref):
    f32, bf16 = jnp.float32, jnp.bfloat16
    n2n, nh, pitch, half = HY_N2, HY_NH, HY_PITCH, HY_N1

    def stage1(q, c):
        slab = z_ref[0, pl.ds(pl.multiple_of(q * nh, nh), nh), :].astype(bf16)
        a_ref[pl.ds(pl.multiple_of(q * pitch, 8), 2 * half), :] = jnp.dot(l1_ref[q], slab, preferred_element_type=f32)
        return c

    lax.fori_loop(0, n2n, stage1, 0, unroll=8)

    def stage2(k, c):
        a = jnp.concatenate([a_ref[pl.ds(k, n2n, stride=pitch), :],
                             a_ref[pl.ds(half + k, n2n, stride=pitch), :]], axis=0).astype(bf16)
        x = jnp.dot(ff_ref[...], a, preferred_element_type=f32)
        xr, xi = x[:n2n], x[n2n:]
        base = pl.multiple_of(k * 2 * n2n, 2 * n2n)
        kr = k_ref[0, pl.ds(base, n2n), :]
        ki = k_ref[0, pl.ds(base + n2n, n2n), :]
        y = jnp.concatenate([xr * kr - xi * ki, xr * ki + xi * kr], axis=0).astype(bf16)
        b = jnp.dot(fi_ref[...], y, preferred_element_type=f32)
        a_ref[pl.ds(k, n2n, stride=pitch), :] = b[:n2n]
        a_ref[pl.ds(half + k, n2n, stride=pitch), :] = b[n2n:]
        return c

    lax.fori_loop(0, half, stage2, 0, unroll=4)

    def stage3(q, c):
        blk = a_ref[pl.ds(pl.multiple_of(q * pitch, 8), 2 * half), :].astype(bf16)
        y = jnp.dot(l2_ref[q], blk, preferred_element_type=f32)
        rows = pl.ds(pl.multiple_of(q * nh, nh), nh)
        o_ref[0, rows, :] = g_ref[0, rows, :] * (y + z_ref[0, rows, :] * bias_ref[...])
        return c

    lax.fori_loop(0, n2n, stage3, 0, unroll=8)


def hyena_long_conv_gated(zt, gt, spec, bias, tables):
    bsz, n, w = zt.shape
    l1, l2, ff, fi = tables
    nt = w // HY_LANES
    tok = pl.BlockSpec((1, n, HY_LANES), lambda t, b: (b, 0, t))
    return pl.pallas_call(
        _hy_kernel,
        out_shape=jax.ShapeDtypeStruct((bsz, n, w), jnp.float32),
        grid=(nt, bsz),
        in_specs=[tok, tok,
                  pl.BlockSpec((1,) + spec.shape[1:], lambda t, b: (t, 0, 0)),
                  pl.BlockSpec(l1.shape, lambda t, b: (0, 0, 0)),
                  pl.BlockSpec(l2.shape, lambda t, b: (0, 0, 0)),
                  pl.BlockSpec(ff.shape, lambda t, b: (0, 0)),
                  pl.BlockSpec(fi.shape, lambda t, b: (0, 0)),
                  pl.BlockSpec((1, HY_LANES), lambda t, b: (0, t))],
        out_specs=tok,
        scratch_shapes=[pltpu.VMEM((HY_N2 * HY_PITCH, HY_LANES), jnp.float32)],
        compiler_params=pltpu.CompilerParams(dimension_semantics=("parallel", "parallel"),
                                             vmem_limit_bytes=VMEM_LIMIT),
        name="hyena_conv",
    )(zt, gt, spec, l1, l2, ff, fi, bias.reshape(1, w))


def hyena_branch_long(streams, p):
    bsz, n, w3 = streams.shape
    k_spec = hyena_filter_spectra(n, p)
    tables = hyena_dft_tables()
    st = jnp.transpose(streams.reshape(bsz, HY_NH, HY_N2, w3), (0, 2, 1, 3)).reshape(bsz, n, w3)
    z, x1, x2 = jnp.split(st, 3, axis=-1)
    for o, gate in enumerate((x1, x2)):
        z = hyena_long_conv_gated(z, gate, hyena_spectrum_layout(k_spec[:, o]), p['hy_bias'][o], tables)
    w = z.shape[-1]
    return jnp.transpose(z.reshape(bsz, HY_N2, HY_NH, w), (0, 2, 1, 3)).reshape(bsz, n, w)


def _rms(x, gain):
    return x * lax.rsqrt(jnp.mean(x * x, axis=-1, keepdims=True) + NORM_EPS) * gain


def _conv3(x, w_ref, period):
    rows = x.shape[0]
    pos = lax.broadcasted_iota(jnp.int32, (rows, 1), 0) % period
    prev = jnp.where(pos == 0, 0.0, pltpu.roll(x, 1, axis=0))
    nxt = jnp.where(pos == period - 1, 0.0, pltpu.roll(x, rows - 1, axis=0))
    return prev * w_ref[0:1, :] + x * w_ref[1:2, :] + nxt * w_ref[2:3, :]


def _inproj_kernel(x_ref, mod_ref, gain_ref, ws5_ref, wrw_ref, wlo_ref, why_ref, wgt_ref, cwr_ref, cwh_ref, cbh_ref,
                   os5_ref, orw_ref, olo_ref, ohy_ref, ogt_ref, *, period):
    f32 = jnp.float32
    shift, scale = mod_ref[0, 0:1, :], mod_ref[0, 1:2, :]
    h = (_rms(x_ref[...], gain_ref[...]) * (1.0 + scale) + shift).astype(jnp.bfloat16)
    os5_ref[...] = jnp.dot(h, ws5_ref[...], preferred_element_type=f32)
    olo_ref[...] = jnp.dot(h, wlo_ref[...], preferred_element_type=f32)
    ogt_ref[...] = jnp.dot(h, wgt_ref[...], preferred_element_type=f32)
    orw_ref[...] = _conv3(jnp.dot(h, wrw_ref[...], preferred_element_type=f32), cwr_ref, period)
    ohy_ref[...] = _conv3(jnp.dot(h, why_ref[...], preferred_element_type=f32), cwh_ref, period) + cbh_ref[...]


def input_projection(x, mod, gain, w_parts, rw_conv_w, hy_conv_w, hy_conv_b, period):
    bsz, n, d = x.shape
    tm = min(MIX_BLOCK_ROWS, n)
    per_b = n // tm
    rows = bsz * n
    bmap = (lambda i: (i // per_b, 0, 0)) if mod.shape[0] == bsz else (lambda i: (0, 0, 0))
    wb = [w.astype(jnp.bfloat16) for w in w_parts]

    def const(a):
        return pl.BlockSpec(a.shape, lambda i: (0,) * a.ndim)

    cbh = hy_conv_b.reshape(1, -1)
    outs = pl.pallas_call(
        functools.partial(_inproj_kernel, period=period),
        out_shape=tuple(jax.ShapeDtypeStruct((rows, w.shape[1]), jnp.float32) for w in wb),
        grid=(rows // tm,),
        in_specs=[pl.BlockSpec((tm, d), lambda i: (i, 0)),
                  pl.BlockSpec((1, 3, d), bmap),
                  pl.BlockSpec((1, d), lambda i: (0, 0))]
                 + [const(w) for w in wb] + [const(rw_conv_w), const(hy_conv_w), const(cbh)],
        out_specs=tuple(pl.BlockSpec((tm, w.shape[1]), lambda i: (i, 0)) for w in wb),
        compiler_params=pltpu.CompilerParams(dimension_semantics=("parallel",), vmem_limit_bytes=VMEM_LIMIT),
        name="input_projection",
    )(x.reshape(rows, d), mod, gain.reshape(1, d), *wb, rw_conv_w, hy_conv_w, cbh)
    return tuple(o.reshape(bsz, n, o.shape[-1]) for o in outs)


def _merge_kernel(x_ref, mod_ref, gain_ref, ys5_ref, us5_ref, wf_ref, wb_ref, bonus_ref, glo_ref, hy_ref, gt_ref,
                  s5d_ref, glu_ref, lnw_ref, lnb_ref, g2_ref, havg_ref, brs_ref, brr_ref, brh_ref, ow_ref, o_ref):
    f32, bf16 = jnp.float32, jnp.bfloat16
    y = ys5_ref[...] + s5d_ref[...] * us5_ref[...]
    y = 0.5 * y * (1.0 + lax.erf(y * (1.0 / math.sqrt(2.0))))
    s5 = y * jax.nn.sigmoid(jnp.dot(y.astype(bf16), glu_ref[...], preferred_element_type=f32))
    wkv = wf_ref[...] + wb_ref[...]

    def head_mean(t):
        hi = t.astype(bf16)
        rem = t - hi.astype(f32)
        mid = rem.astype(bf16)
        lo = (rem - mid.astype(f32)).astype(bf16)
        hm = havg_ref[...]
        tot = (jnp.dot(hi, hm, preferred_element_type=f32) + jnp.dot(mid, hm, preferred_element_type=f32)
               + jnp.dot(lo, hm, preferred_element_type=f32))
        return tot * (1.0 / RW_HEAD)

    mu = head_mean(wkv)
    cen = wkv - mu
    var = head_mean(cen * cen)
    o = cen * lax.rsqrt(var + RW_GN_EPS) * lnw_ref[...] + lnb_ref[...] + bonus_ref[...]
    g = jnp.dot(jax.nn.sigmoid(glo_ref[...]).astype(bf16), g2_ref[...], preferred_element_type=f32)
    rw = o * g
    gt = jax.nn.sigmoid(gt_ref[...])
    d = x_ref.shape[-1]
    m = (gt[:, 0:d] * jnp.dot(s5.astype(bf16), brs_ref[...], preferred_element_type=f32)
         + gt[:, d:2 * d] * jnp.dot(rw.astype(bf16), brr_ref[...], preferred_element_type=f32)
         + gt[:, 2 * d:3 * d] * jnp.dot(hy_ref[...].astype(bf16), brh_ref[...], preferred_element_type=f32))
    yl = jnp.dot(m.astype(bf16), ow_ref[...], preferred_element_type=f32)
    o_ref[...] = x_ref[...] + mod_ref[0, 2:3, :] * _rms(yl, gain_ref[...])


def merge_block(x, mod, gain, ys5, us5, rw_parts, hy, gates, p):
    bsz, n, d = x.shape
    tm = min(MIX_BLOCK_ROWS, n)
    per_b = n // tm
    rows = bsz * n
    bmap = (lambda i: (i // per_b, 0, 0)) if mod.shape[0] == bsz else (lambda i: (0, 0, 0))

    def bf(a):
        return a.astype(jnp.bfloat16)

    def row(a):
        return pl.BlockSpec((tm, a.shape[-1]), lambda i: (i, 0))

    def const(a):
        return pl.BlockSpec(a.shape, lambda i: (0,) * a.ndim)

    hid = np.arange(RW_WIDTH) // RW_HEAD
    havg = jnp.asarray(hid[:, None] == hid[None, :], jnp.bfloat16)
    acts = [a.reshape(rows, a.shape[-1]) for a in (ys5, us5) + tuple(rw_parts) + (hy, gates)]
    consts = [p['s5_d'].reshape(1, -1), bf(p['s5_glu_w']), p['rw_ln_w'].reshape(1, -1), p['rw_ln_b'].reshape(1, -1),
              bf(p['rw_g2']), havg, bf(p['br_s5']), bf(p['br_rw']), bf(p['br_hy']), bf(p['out_w'])]
    out = pl.pallas_call(
        _merge_kernel,
        out_shape=jax.ShapeDtypeStruct((rows, d), jnp.float32),
        grid=(rows // tm,),
        in_specs=[pl.BlockSpec((tm, d), lambda i: (i, 0)), pl.BlockSpec((1, 3, d), bmap),
                  pl.BlockSpec((1, d), lambda i: (0, 0))] + [row(a) for a in acts] + [const(c) for c in consts],
        out_specs=pl.BlockSpec((tm, d), lambda i: (i, 0)),
        compiler_params=pltpu.CompilerParams(dimension_semantics=("parallel",), vmem_limit_bytes=VMEM_LIMIT),
        name="merge_block",
    )(x.reshape(rows, d), mod, gain.reshape(1, d), *acts, *consts)
    return out.reshape(bsz, n, d)


def token_mixer(x, ctx, mod_l, mod_c, g_pre, g_post, p, with_ctx_out):
    bsz = x.shape[0]
    cuts = [IN_S5, IN_S5 + IN_RW, IN_S5 + IN_RW + IN_LORA, IN_S5 + IN_RW + IN_LORA + IN_HY]
    edges = [0] + cuts + [IN_COLS]
    w_parts = [p['in_w'][:, edges[i]:edges[i + 1]] for i in range(5)]
    conv = (p['rw_conv_w'], p['hy_conv_w'], p['hy_conv_b'])
    u_l, rkv_l, lora_l, hy_l, gate_l = input_projection(x, mod_l, g_pre, w_parts, *conv, GRID_W)
    u_c, rkv_c, lora_c, hy_c, gate_c = input_projection(ctx, mod_c, g_pre, w_parts, *conv, CTX_LEN)
    mask_set = rwkv_masks()
    zr = jnp.zeros((bsz, 2, RW_WIDTH // RW_LANES, RW_LANES, RW_LANES), jnp.float32)
    ys5_l, ys5_c = s5_branch(u_l, u_c, p)
    rw_c, rw_state = rwkv_branch(rkv_c, lora_c, p, zr, mask_set, CTX_LEN)
    rw_l, _ = rwkv_branch(rkv_l, lora_l, p, rw_state, mask_set, RW_BLOCK_TOKENS)
    hy_lat = hyena_branch_long(hy_l, p)
    x = merge_block(x, mod_l, g_post, ys5_l, u_l, rw_l, hy_lat, gate_l, p)
    if with_ctx_out:
        ctx = merge_block(ctx, mod_c, g_post, ys5_c, u_c, rw_c, hyena_branch(hy_c, p), gate_c, p)
    return x, ctx


def _ffn_kernel(x_ref, mod_ref, gain_ref, wg_ref, wu_ref, wd_ref, o_ref):
    x = x_ref[...]
    shift, scale, gate = mod_ref[0, 0:1, :], mod_ref[0, 1:2, :], mod_ref[0, 2:3, :]
    h = (_rms(x, gain_ref[0:1, :]) * (1.0 + scale) + shift).astype(jnp.bfloat16)
    g = jnp.dot(h, wg_ref[...], preferred_element_type=jnp.float32)
    u = jnp.dot(h, wu_ref[...], preferred_element_type=jnp.float32)
    a = (g * jax.nn.sigmoid(g) * u).astype(jnp.bfloat16)
    y = jnp.dot(a, wd_ref[...], preferred_element_type=jnp.float32)
    o_ref[...] = x + gate * _rms(y, gain_ref[1:2, :])


def ffn_block(x, mod, gains, wg, wu, wd):
    bsz, n, d = x.shape
    hid = wg.shape[1]
    tm = min(FFN_BLOCK_ROWS, n)
    per_b = n // tm
    xf = x.reshape(bsz * n, d)
    bmap = (lambda i: (i // per_b, 0, 0)) if mod.shape[0] == bsz else (lambda i: (0, 0, 0))
    out = pl.pallas_call(
        _ffn_kernel,
        out_shape=jax.ShapeDtypeStruct((bsz * n, d), jnp.float32),
        grid=(bsz * per_b,),
        in_specs=[pl.BlockSpec((tm, d), lambda i: (i, 0)),
                  pl.BlockSpec((1, 3, d), bmap),
                  pl.BlockSpec((2, d), lambda i: (0, 0)),
                  pl.BlockSpec((d, hid), lambda i: (0, 0)),
                  pl.BlockSpec((d, hid), lambda i: (0, 0)),
                  pl.BlockSpec((hid, d), lambda i: (0, 0))],
        out_specs=pl.BlockSpec((tm, d), lambda i: (i, 0)),
        compiler_params=pltpu.CompilerParams(dimension_semantics=("parallel",), vmem_limit_bytes=VMEM_LIMIT),
        name="ffn_block",
    )(xf, mod, gains, wg.astype(jnp.bfloat16), wu.astype(jnp.bfloat16), wd.astype(jnp.bfloat16))
    return out.reshape(bsz, n, d)


def _moe_kernel(be_ref, x_ref, gate_ref, wg_ref, wu_ref, wd_ref, o_ref):
    j = pl.program_id(1)
    xb = x_ref[...].astype(jnp.bfloat16)
    g = jnp.dot(xb, wg_ref[0], preferred_element_type=jnp.float32)
    u = jnp.dot(xb, wu_ref[0], preferred_element_type=jnp.float32)
    a = (g * jax.nn.sigmoid(g) * u).astype(jnp.bfloat16)
    y = jnp.dot(a, wd_ref[0], preferred_element_type=jnp.float32)

    @pl.when(j == 0)
    def _():
        o_ref[...] = y

    @pl.when(j > 0)
    def _():
        o_ref[...] += y

    @pl.when(j == pl.num_programs(1) - 1)
    def _():
        o_ref[...] *= gate_ref[...]


def moe_expert_blocks(xb, slot_gate, block_e, wg, wu, wd):
    cap, d = xb.shape
    n_blocks = cap // MOE_BLOCK
    hid = wg.shape[-1]
    nh = MOE_HIDDEN_TILES
    th = hid // nh

    def hidx(i, j):
        return j + (i % 2) * (nh - 1 - 2 * j)

    return pl.pallas_call(
        _moe_kernel,
        out_shape=jax.ShapeDtypeStruct((cap, d), jnp.float32),
        grid_spec=pltpu.PrefetchScalarGridSpec(
            num_scalar_prefetch=1,
            grid=(n_blocks, nh),
            in_specs=[pl.BlockSpec((MOE_BLOCK, d), lambda i, j, be: (i, 0)),
                      pl.BlockSpec((MOE_BLOCK, 1), lambda i, j, be: (i, 0)),
                      pl.BlockSpec((1, d, th), lambda i, j, be: (be[i], 0, hidx(i, j))),
                      pl.BlockSpec((1, d, th), lambda i, j, be: (be[i], 0, hidx(i, j))),
                      pl.BlockSpec((1, th, d), lambda i, j, be: (be[i], hidx(i, j), 0))],
            out_specs=pl.BlockSpec((MOE_BLOCK, d), lambda i, j, be: (i, 0))),
        compiler_params=pltpu.CompilerParams(dimension_semantics=("parallel", "arbitrary"),
                                             vmem_limit_bytes=VMEM_LIMIT),
        name="moe_experts",
    )(block_e, xb, slot_gate.reshape(cap, 1), wg.astype(jnp.bfloat16), wu.astype(jnp.bfloat16),
      wd.astype(jnp.bfloat16))


def moe_swiglu(h, router_w, wg, wu, wd):
    d_model = h.shape[-1]
    tok = h.reshape(-1, d_model)
    n = tok.shape[0]
    n_assign = n * TOP_K
    logits = jnp.dot(tok, router_w)
    top_logit, top_e = lax.top_k(logits, TOP_K)
    gate = jax.nn.softmax(top_logit, axis=-1).reshape(-1)
    flat_e = top_e.reshape(-1)
    order = jnp.argsort(flat_e)
    sorted_e = flat_e[order]
    sizes = jnp.zeros((N_EXPERTS,), jnp.int32).at[flat_e].add(1)
    padded = (sizes + MOE_BLOCK - 1) // MOE_BLOCK * MOE_BLOCK
    pad_end = jnp.cumsum(padded)
    pad_start = pad_end - padded
    grp_start = jnp.cumsum(sizes) - sizes
    slot = pad_start[sorted_e] + jnp.arange(n_assign, dtype=jnp.int32) - grp_start[sorted_e]
    n_blocks = -(-n_assign // MOE_BLOCK) + N_EXPERTS
    cap = n_blocks * MOE_BLOCK
    slot_tok = jnp.full((cap,), n, jnp.int32).at[slot].set((order // TOP_K).astype(jnp.int32))
    slot_gate = jnp.zeros((cap,), jnp.float32).at[slot].set(gate[order])
    block_start = jnp.arange(n_blocks, dtype=jnp.int32) * MOE_BLOCK
    block_e = jnp.minimum(jnp.sum(block_start[:, None] >= pad_end[None, :], axis=1), N_EXPERTS - 1)
    tok_pad = jnp.concatenate([tok, jnp.zeros((1, d_model), tok.dtype)], axis=0)
    xb = tok_pad[slot_tok]
    yb = moe_expert_blocks(xb, slot_gate, block_e.astype(jnp.int32), wg, wu, wd)
    pos = jnp.zeros((n_assign,), jnp.int32).at[order].set(slot).reshape(n, TOP_K)
    out = yb[pos[:, 0]]
    for kk in range(1, TOP_K):
        out = out + yb[pos[:, kk]]
    return out.reshape(h.shape)


def kernel(x, c, ctx, c_ctx, mod_w, mod_b, norm_g, in_w, s5_lam_re, s5_lam_im, s5_log_step, s5_b_re, s5_b_im, s5_c_re, s5_c_im, s5_d, s5_glu_w, rw_conv_w, rw_w0, rw_w2, rw_a0, rw_a2, rw_g2, rw_kk, rw_ka, rw_rk, rw_ln_w, rw_ln_b, hy_conv_w, hy_conv_b, hy_f_w1, hy_f_b1, hy_f_freq1, hy_f_w2, hy_f_b2, hy_f_freq2, hy_f_w3, hy_bias, br_s5, br_rw, br_hy, out_w, ffn_wg, ffn_wu, ffn_wd, moe_router, moe_wg, moe_wu, moe_wd):
    silu_c = jax.nn.silu(c)
    silu_cc = jax.nn.silu(c_ctx)
    for i in range(DEPTH):
        last = i == DEPTH - 1
        p = {
            'in_w': in_w[i],
            's5_lam_re': s5_lam_re[i], 's5_lam_im': s5_lam_im[i], 's5_log_step': s5_log_step[i],
            's5_b_re': s5_b_re[i], 's5_b_im': s5_b_im[i], 's5_c_re': s5_c_re[i], 's5_c_im': s5_c_im[i],
            's5_d': s5_d[i], 's5_glu_w': s5_glu_w[i],
            'rw_conv_w': rw_conv_w[i], 'rw_w0': rw_w0[i], 'rw_w2': rw_w2[i], 'rw_a0': rw_a0[i], 'rw_a2': rw_a2[i],
            'rw_g2': rw_g2[i], 'rw_kk': rw_kk[i], 'rw_ka': rw_ka[i], 'rw_rk': rw_rk[i],
            'rw_ln_w': rw_ln_w[i], 'rw_ln_b': rw_ln_b[i],
            'hy_conv_w': hy_conv_w[i], 'hy_conv_b': hy_conv_b[i], 'hy_f_w1': hy_f_w1[i], 'hy_f_b1': hy_f_b1[i],
            'hy_f_freq1': hy_f_freq1[i], 'hy_f_w2': hy_f_w2[i], 'hy_f_b2': hy_f_b2[i], 'hy_f_freq2': hy_f_freq2[i],
            'hy_f_w3': hy_f_w3[i], 'hy_bias': hy_bias[i],
            'br_s5': br_s5[i], 'br_rw': br_rw[i], 'br_hy': br_hy[i], 'out_w': out_w[i],
        }
        ml = jnp.split((silu_c @ mod_w[i] + mod_b[i])[:, None, :], 6, axis=-1)
        mc = jnp.split(silu_cc @ mod_w[i] + mod_b[i], 6, axis=-1)
        g_pre_m, g_post_m, g_pre_f, g_post_f = norm_g[i]
        if i % 2 == 0:
            def channel_mix(t, m, j=i // 2):
                return ffn_block(t, m, norm_g[i, 2:4], ffn_wg[j], ffn_wu[j], ffn_wd[j])
        else:
            def channel_mix(t, m, j=i // 2):
                h = rms_norm(t, g_pre_f) * (1 + m[:, 1:2]) + m[:, 0:1]
                y = moe_swiglu(h, moe_router[j], moe_wg[j], moe_wu[j], moe_wd[j])
                return t + m[:, 2:3] * rms_norm(y, g_post_f)
        x, ctx = token_mixer(x, ctx, jnp.concatenate(ml[0:3], axis=1), jnp.stack(mc[0:3])[None],
                             g_pre_m, g_post_m, p, not last)
        x = channel_mix(x, jnp.concatenate(ml[3:6], axis=1))
        if not last:
            ctx = channel_mix(ctx, jnp.stack(mc[3:6])[None])
    return x
```

```python
import math
import functools
import numpy as np
import jax
import jax.numpy as jnp
from jax import lax
from jax.experimental import pallas as pl
from jax.experimental.pallas import tpu as pltpu

D_MODEL = 1024
BATCH = 8
SEQ = 4096
DEPTH = 2
GRID_W = 64
CTX_LEN = 256
NORM_EPS = 1e-6
S5_WIDTH = D_MODEL // 4
S5_GROUP = 16
S5_GROUPS = S5_WIDTH // S5_GROUP
S5_STATE = 64
RW_WIDTH = D_MODEL // 2
RW_HEAD = 64
RW_HEADS = RW_WIDTH // RW_HEAD
RW_DECAY_LORA = 64
RW_ICLR_LORA = 64
RW_GATE_LORA = 128
RW_GN_EPS = 64e-5
HY_WIDTH = D_MODEL // 4
HY_ORDER = 2
HY_POS_DIM = 33
HY_FILTER_HIDDEN = 64
HY_DECAY_TARGET = 1e-2
HY_DECAY_PCT_SHORT = 0.3
HY_DECAY_PCT_LONG = 1.5
SHORT_CONV = 3
N_BRANCH = 3
FFN_HIDDEN = 2816
N_EXPERTS = 8
TOP_K = 2
EXPERT_HIDDEN = 3584
MOE_BLOCK = 256
IN_S5 = S5_WIDTH
IN_RW = 3 * RW_WIDTH
IN_LORA = 2 * RW_DECAY_LORA + 2 * RW_ICLR_LORA + RW_GATE_LORA
IN_HY = (HY_ORDER + 1) * HY_WIDTH
IN_GATE = N_BRANCH * D_MODEL
IN_COLS = IN_S5 + IN_RW + IN_LORA + IN_HY + IN_GATE

VMEM_LIMIT = 56 * 1024 * 1024
S5_CHUNK = 64
RW_CHUNK = 64
RW_HG = 4
RW_LANES = RW_HG * RW_HEAD
RW_BLOCK_TOKENS = 512
MOE_HIDDEN_TILES = 2
FFN_BLOCK_ROWS = 256
MIX_BLOCK_ROWS = 256


def _mm_kernel(a_ref, b_ref, o_ref):
    o_ref[...] = jnp.dot(a_ref[...].astype(jnp.bfloat16), b_ref[...],
                         preferred_element_type=jnp.float32)


def _pick(n, cands):
    for c in cands:
        if n % c == 0:
            return c
    return n


def pmatmul(a, b):
    m, k = a.shape
    n = b.shape[1]
    b = b.astype(jnp.bfloat16)
    tm = _pick(m, (512, 256, 128, 8))
    tn = _pick(n, (1024, 768, 512, 384, 256, 128))
    return pl.pallas_call(
        _mm_kernel,
        out_shape=jax.ShapeDtypeStruct((m, n), jnp.float32),
        grid=(n // tn, m // tm),
        in_specs=[pl.BlockSpec((tm, k), lambda j, i: (i, 0)),
                  pl.BlockSpec((k, tn), lambda j, i: (0, j))],
        out_specs=pl.BlockSpec((tm, tn), lambda j, i: (i, j)),
        compiler_params=pltpu.CompilerParams(
            dimension_semantics=("parallel", "parallel"),
            vmem_limit_bytes=VMEM_LIMIT),
        name="mm",
    )(a, b)


def mm(a, b):
    lead = a.shape[:-1]
    return pmatmul(a.reshape(-1, a.shape[-1]), b).reshape(lead + (b.shape[1],))


def rms_norm(x, gain):
    y = x * lax.rsqrt(jnp.mean(x * x, axis=-1, keepdims=True) + NORM_EPS)
    return y * gain


def short_conv(x, w, b=None):
    half = SHORT_CONV // 2
    n = x.shape[1]
    xp = jnp.pad(x, ((0, 0), (half, half), (0, 0)))
    y = sum(xp[:, j:j + n] * w[j] for j in range(SHORT_CONV))
    return y if b is None else y + b


def latent_short_conv(x, w, b=None):
    bsz, n_tok, ch = x.shape
    rows = n_tok // GRID_W
    y = short_conv(x.reshape(bsz * rows, GRID_W, ch), w, b)
    return y.reshape(bsz, n_tok, ch)


def s5_chunk_operators(lam_re, lam_im, log_step, b_re, b_im, c_re, c_im):
    T, G, P, GC = S5_CHUNK, S5_GROUPS, S5_STATE, S5_GROUP
    hp = lax.Precision.HIGHEST
    step = jnp.exp(log_step)[..., None]
    lr, li = lam_re, lam_im
    tau = jnp.arange(T + 1, dtype=jnp.float32)[:, None, None, None]
    mag = jnp.exp(lr * step * tau)
    ang = li * step * tau
    e_re, e_im = mag * jnp.cos(ang), mag * jnp.sin(ang)
    ab_re, ab_im = e_re[1], e_im[1]
    den = lr * lr + li * li
    nr, ni = ab_re - 1.0, ab_im
    q_re = (nr * lr + ni * li) / den
    q_im = (ni * lr - nr * li) / den
    bb_re = q_re[..., None] * b_re - q_im[..., None] * b_im
    bb_im = q_re[..., None] * b_im + q_im[..., None] * b_re
    eb_re = e_re[..., None] * bb_re - e_im[..., None] * bb_im
    eb_im = e_re[..., None] * bb_im + e_im[..., None] * bb_re
    ktau = (jnp.einsum('dgip,tdgpj->tdgij', c_re, eb_re[:T], precision=hp)
            - jnp.einsum('dgip,tdgpj->tdgij', c_im, eb_im[:T], precision=hp))
    t_idx = jnp.arange(T)
    k_f = jnp.transpose(ktau[:, 0], (1, 3, 0, 2))
    k_b = jnp.transpose(ktau[:, 1], (1, 3, 0, 2))
    table = jnp.concatenate([k_b[:, :, :0:-1], k_f[:, :, :1] + k_b[:, :, :1], k_f[:, :, 1:]], axis=2)
    table = table.astype(jnp.bfloat16).reshape(G, GC, (2 * T - 1) * GC)
    kt = jnp.stack([table[:, :, (T - 1 - s) * GC:(2 * T - 1 - s) * GC] for s in range(T)], axis=1)
    kt = kt.reshape(G, T * GC, T * GC)
    wf_re, wf_im = eb_re[T - 1 - t_idx, 0], eb_im[T - 1 - t_idx, 0]
    wb_re, wb_im = eb_re[t_idx, 1], eb_im[t_idx, 1]
    win = jnp.stack([wf_re, wf_im, wb_re, wb_im], axis=0)
    win = jnp.transpose(win, (2, 1, 4, 0, 3)).reshape(G, T * GC, 4 * P)
    ef_re, ef_im = e_re[t_idx + 1, 0], e_im[t_idx + 1, 0]
    eb2_re, eb2_im = e_re[T - t_idx, 1], e_im[T - t_idx, 1]

    def readout(cr, ci, er, ei):
        re = cr[None] * er[:, :, None, :] - ci[None] * ei[:, :, None, :]
        im = -(cr[None] * ei[:, :, None, :] + ci[None] * er[:, :, None, :])
        return re, im

    of_re, of_im = readout(c_re[0], c_im[0], ef_re, ef_im)
    ob_re, ob_im = readout(c_re[1], c_im[1], eb2_re, eb2_im)
    wout = jnp.stack([of_re, of_im, ob_re, ob_im], axis=0)
    wout = jnp.transpose(wout, (2, 0, 4, 1, 3)).reshape(G, 4 * P, T * GC)
    at_re, at_im = e_re[T], e_im[T]
    apow = jnp.concatenate([at_re[0], at_re[0], at_re[1], at_re[1]], axis=-1)[:, None, :]
    aimg = jnp.concatenate([-at_im[0], at_im[0], -at_im[1], at_im[1]], axis=-1)[:, None, :]
    return kt.astype(jnp.bfloat16), win.astype(jnp.bfloat16), wout.astype(jnp.bfloat16), apow, aimg


def _s5_kernel(x_ref, kt_ref, win_ref, wout_ref, apow_ref, aimg_ref, h0_ref, y_ref, hfin_ref, hin_ref, *,
               n_chunks, bsz):
    P2 = 2 * S5_STATE
    xb = x_ref[0].astype(jnp.bfloat16)
    hin_ref[...] = jnp.dot(xb, win_ref[0], preferred_element_type=jnp.float32)
    ap = apow_ref[0]
    ai = aimg_ref[0]
    apf, aif = ap[:, :P2], ai[:, :P2]
    apb, aib = ap[:, P2:], ai[:, P2:]

    def cmul(h, a_p, a_i):
        return h * a_p + pltpu.roll(h, S5_STATE, axis=1) * a_i

    def body(c, carry):
        hf, hb = carry
        rf = pl.ds(pl.multiple_of(c * bsz, bsz), bsz)
        rb = pl.ds(pl.multiple_of((n_chunks - 1 - c) * bsz, bsz), bsz)
        df = hin_ref[rf, :P2]
        db = hin_ref[rb, P2:]
        hin_ref[rf, :P2] = hf
        hin_ref[rb, P2:] = hb
        return cmul(hf, apf, aif) + df, cmul(hb, apb, aib) + db

    h0 = h0_ref[0]
    hf, hb = lax.fori_loop(0, n_chunks, body, (h0[:, :P2], h0[:, P2:]))
    hfin_ref[0, :, :P2] = hf
    hfin_ref[0, :, P2:] = hb
    y = jnp.dot(xb, kt_ref[0], preferred_element_type=jnp.float32)
    y = y + jnp.dot(hin_ref[...].astype(jnp.bfloat16), wout_ref[0], preferred_element_type=jnp.float32)
    y_ref[0] = y


def s5_scan_pallas(u, ops, h0):
    kt, win, wout, apow, aimg = ops
    bsz, n, _ = u.shape
    T, G, P, GC = S5_CHUNK, S5_GROUPS, S5_STATE, S5_GROUP
    nc = n // T
    rows = nc * bsz
    x = jnp.transpose(u.reshape(bsz, nc, T, G, GC), (3, 1, 0, 2, 4)).reshape(G, rows, T * GC)
    y, hfin = pl.pallas_call(
        functools.partial(_s5_kernel, n_chunks=nc, bsz=bsz),
        out_shape=(jax.ShapeDtypeStruct((G, rows, T * GC), jnp.float32),
                   jax.ShapeDtypeStruct((G, bsz, 4 * P), jnp.float32)),
        grid=(G,),
        in_specs=[pl.BlockSpec((1, rows, T * GC), lambda g: (g, 0, 0)),
                  pl.BlockSpec((1, T * GC, T * GC), lambda g: (g, 0, 0)),
                  pl.BlockSpec((1, T * GC, 4 * P), lambda g: (g, 0, 0)),
                  pl.BlockSpec((1, 4 * P, T * GC), lambda g: (g, 0, 0)),
                  pl.BlockSpec((1, 1, 4 * P), lambda g: (g, 0, 0)),
                  pl.BlockSpec((1, 1, 4 * P), lambda g: (g, 0, 0)),
                  pl.BlockSpec((1, bsz, 4 * P), lambda g: (g, 0, 0))],
        out_specs=(pl.BlockSpec((1, rows, T * GC), lambda g: (g, 0, 0)),
                   pl.BlockSpec((1, bsz, 4 * P), lambda g: (g, 0, 0))),
        scratch_shapes=[pltpu.VMEM((rows, 4 * P), jnp.float32)],
        compiler_params=pltpu.CompilerParams(dimension_semantics=("parallel",),
                                             vmem_limit_bytes=VMEM_LIMIT),
        name="s5_scan",
    )(x, kt, win, wout, apow, aimg, h0)
    y = jnp.transpose(y.reshape(G, nc, bsz, T, GC), (2, 1, 3, 0, 4)).reshape(bsz, n, G * GC)
    return y, hfin


def s5_branch(u, u_ctx, p):
    ops = s5_chunk_operators(p['s5_lam_re'], p['s5_lam_im'], p['s5_log_step'], p['s5_b_re'], p['s5_b_im'],
                             p['s5_c_re'], p['s5_c_im'])
    h0 = jnp.zeros((S5_GROUPS, u.shape[0], 4 * S5_STATE), jnp.float32)
    y_ctx, h_ctx = s5_scan_pallas(u_ctx, ops, h0)
    y_lat, _ = s5_scan_pallas(u, ops, h_ctx)
    return y_lat, y_ctx


_NT = (((1,), (1,)), ((), ()))

_M_SAME, _M_EYE, _M_STRICT, _M_INCL, _M_LEVEL0 = 0, 1, 2, 4, 6
_N_LEVELS = 6


def rwkv_masks():
    C, n = RW_CHUNK, RW_LANES
    row = np.arange(n)[:, None]
    col = np.arange(n)[None, :]
    same = (row // C) == (col // C)
    out = [same, row == col]
    per_dir = []
    for reverse in (False, True):
        t, j = (row % C, col % C) if not reverse else (col % C, row % C)
        per_dir.append((same & (j < t), same & (j <= t),
                        [same & ((t // s) % 2 == 1) & ((j // s) == (t // s) - 1) for s in (1, 2, 4, 8, 16, 32)]))
    out += [per_dir[0][0], per_dir[1][0], per_dir[0][1], per_dir[1][1]]
    for l in range(_N_LEVELS):
        out += [per_dir[0][2][l], per_dir[1][2][l]]
    tt, jj = np.arange(C)[:, None], np.arange(C)[None, :]
    tri = np.stack([jj <= tt, jj >= tt])
    wide = np.stack([np.tile(m, (1, RW_HG)) for m in (jj < tt, jj > tt, jj <= tt, jj >= tt)])
    return (jnp.asarray(np.stack(out), jnp.float32), jnp.asarray(wide, jnp.float32),
            jnp.asarray(tri, jnp.bfloat16))


def _bdot(a, b, dims=None):
    a = a.astype(jnp.bfloat16)
    b = b.astype(jnp.bfloat16)
    if dims is None:
        return jnp.dot(a, b, preferred_element_type=jnp.float32)
    return lax.dot_general(a, b, dims, preferred_element_type=jnp.float32)


def _rwkv_chunk(S, r, lw, k, v, kk, b, m_ref, w_ref, tri, d):
    C = RW_CHUNK
    same = m_ref[_M_SAME]
    lw_hi = lw.astype(jnp.bfloat16)
    rem = lw - lw_hi.astype(jnp.float32)
    lw_mid = rem.astype(jnp.bfloat16)
    lw_lo = (rem - lw_mid.astype(jnp.float32)).astype(jnp.bfloat16)
    cl = (jnp.dot(tri, lw_hi, preferred_element_type=jnp.float32)
          + jnp.dot(tri, lw_mid, preferred_element_type=jnp.float32)
          + jnp.dot(tri, lw_lo, preferred_element_type=jnp.float32))
    yield
    tot = jnp.sum(lw, axis=0, keepdims=True)
    e_neg = jnp.exp(-cl)
    e_end = jnp.exp(tot - cl)
    a_t = -kk * jnp.exp(cl - lw)
    r_t = r * jnp.exp(cl)
    b_t = b * e_neg
    k_t = k * e_neg
    b_h = b * e_end
    k_h = k * e_end
    p_c = jnp.exp(tot)

    def tile(x):
        return jnp.concatenate([x] * RW_HG, axis=0)

    def stack(x):
        return tile(x) * same

    def unstack(z):
        acc = z[0:C]
        for h in range(1, RW_HG):
            acc = acc + z[h * C:(h + 1) * C]
        return acc

    n = RW_LANES
    sa = stack(a_t)
    a_ab = _bdot(sa, tile(b_t), _NT) * m_ref[_M_STRICT + d]
    wide = _bdot(jnp.concatenate([a_t, r_t], axis=0), jnp.concatenate([stack(b_t), stack(k_t)], axis=0), _NT)
    yield
    a_ak = wide[:C, n:] * w_ref[d]
    a_rb = wide[C:, :n] * w_ref[2 + d]
    a_rk = wide[C:, n:] * w_ref[2 + d]
    dinv = m_ref[_M_EYE] + a_ab * m_ref[_M_LEVEL0 + d]
    for l in range(1, _N_LEVELS):
        inner = _bdot(a_ab * m_ref[_M_LEVEL0 + 2 * l + d], dinv)
        yield
        dinv = dinv + _bdot(dinv, inner)
        yield
    av = _bdot(jnp.concatenate([a_ak, a_rk], axis=0), stack(v))
    yield
    akv = av[:C]
    arkv = av[C:]
    mu = _bdot(unstack(dinv), jnp.concatenate([sa, stack(akv)], axis=1))
    yield
    m1 = mu[:, :n]
    u0 = mu[:, n:]
    my = _bdot(a_rb, jnp.concatenate([stack(m1), stack(u0)], axis=1))
    yield
    m2 = r_t + my[:, :n]
    y0 = my[:, n:] + arkv
    mut = jnp.concatenate([m1, u0], axis=1).T
    gh = _bdot(mut, b_h)
    yield
    g = m_ref[_M_EYE] * p_c + gh[:n] * same
    hmat = (gh[n:] + _bdot(v.T, k_h)) * same
    y = _bdot(m2, S, _NT) + y0
    s_new = _bdot(S, g) + hmat
    return s_new, y


def _lockstep(gens):
    results = [None] * len(gens)
    active = list(range(len(gens)))
    while active:
        for i in list(active):
            try:
                next(gens[i])
            except StopIteration as stop:
                results[i] = stop.value
                active.remove(i)
    return results


def _rwkv_kernel(m_ref, w_ref, tri_ref, rf_ref, rb_ref, vf_ref, vb_ref, kkf_ref, kkb_ref, lwf_ref, kf_ref, bf_ref,
                 lwb_ref, kb_ref, bb_ref, s0_ref, yf_ref, yb_ref, sfin_ref, s_scr, *, n_chunks, n_groups):
    i = pl.program_id(1)
    C = RW_CHUNK

    @pl.when(i == 0)
    def _():
        s_scr[...] = s0_ref[0]

    tri_f = tri_ref[0]
    tri_b = tri_ref[1]

    def body(c, carry):
        rf = pl.ds(pl.multiple_of(c * C, C), C)
        rb = pl.ds(pl.multiple_of((n_chunks - 1 - c) * C, C), C)
        gens = []
        for g in range(n_groups):
            ln = slice(g * RW_LANES, (g + 1) * RW_LANES)
            gens.append(_rwkv_chunk(s_scr[0, g], rf_ref[0, rf, ln], lwf_ref[0, rf, ln], kf_ref[0, rf, ln],
                                    vf_ref[0, rf, ln], kkf_ref[0, rf, ln], bf_ref[0, rf, ln], m_ref, w_ref, tri_f, 0))
            gens.append(_rwkv_chunk(s_scr[1, g], rb_ref[0, rb, ln], lwb_ref[0, rb, ln], kb_ref[0, rb, ln],
                                    vb_ref[0, rb, ln], kkb_ref[0, rb, ln], bb_ref[0, rb, ln], m_ref, w_ref, tri_b, 1))
        out = _lockstep(gens)
        for g in range(n_groups):
            ln = slice(g * RW_LANES, (g + 1) * RW_LANES)
            (s_f, y_f), (s_b, y_b) = out[2 * g], out[2 * g + 1]
            s_scr[0, g] = s_f
            s_scr[1, g] = s_b
            yf_ref[0, rf, ln] = y_f
            yb_ref[0, rb, ln] = y_b
        return carry

    lax.fori_loop(0, n_chunks, body, 0)

    @pl.when(i == pl.num_programs(1) - 1)
    def _():
        sfin_ref[0] = s_scr[...]


def rwkv_scan_pallas(rkv, kk, lw, kd, bvec, s0, mask_set, block_tokens):
    bsz, n, width = kk.shape
    masks, wide, tri = mask_set
    ng = width // RW_LANES
    tb = block_tokens
    nb = n // tb
    fwd = pl.BlockSpec((1, tb, width), lambda b, i: (b, i, 0))
    bwd = pl.BlockSpec((1, tb, width), lambda b, i: (b, nb - 1 - i, 0))
    r_fwd, r_bwd = fwd, bwd
    v_fwd = pl.BlockSpec((1, tb, width), lambda b, i: (b, i, 2))
    v_bwd = pl.BlockSpec((1, tb, width), lambda b, i: (b, nb - 1 - i, 2))
    state_spec = pl.BlockSpec((1, 2, ng, RW_LANES, RW_LANES), lambda b, i: (b, 0, 0, 0, 0))
    return pl.pallas_call(
        functools.partial(_rwkv_kernel, n_chunks=tb // RW_CHUNK, n_groups=ng),
        out_shape=(jax.ShapeDtypeStruct((bsz, n, width), jnp.float32),
                   jax.ShapeDtypeStruct((bsz, n, width), jnp.float32),
                   jax.ShapeDtypeStruct(s0.shape, jnp.float32)),
        grid=(bsz, nb),
        in_specs=[pl.BlockSpec(masks.shape, lambda b, i: (0, 0, 0)),
                  pl.BlockSpec(wide.shape, lambda b, i: (0, 0, 0)),
                  pl.BlockSpec(tri.shape, lambda b, i: (0, 0, 0)),
                  r_fwd, r_bwd, v_fwd, v_bwd, fwd, bwd,
                  fwd, fwd, fwd, bwd, bwd, bwd, state_spec],
        out_specs=(fwd, bwd, state_spec),
        scratch_shapes=[pltpu.VMEM((2, ng, RW_LANES, RW_LANES), jnp.float32)],
        compiler_params=pltpu.CompilerParams(dimension_semantics=("parallel", "arbitrary"),
                                             vmem_limit_bytes=VMEM_LIMIT),
        name="rwkv_scan",
    )(masks, wide, tri, rkv, rkv, rkv, rkv, kk, kk, lw[0], kd[0], bvec[0], lw[1], kd[1], bvec[1], s0)


def _head_sum(t, hm):
    f32, bf16 = jnp.float32, jnp.bfloat16
    hi = t.astype(bf16)
    rem = t - hi.astype(f32)
    mid = rem.astype(bf16)
    lo = (rem - mid.astype(f32)).astype(bf16)
    return (jnp.dot(hi, hm, preferred_element_type=f32) + jnp.dot(mid, hm, preferred_element_type=f32)
            + jnp.dot(lo, hm, preferred_element_type=f32))


def _rwkv_prep_kernel(rkv_ref, lora_ref, vec_ref, w0_ref, a0_ref, w2_ref, a2_ref, hsum_ref,
                      kk_ref, lw0_ref, kd0_ref, b0_ref, lw1_ref, kd1_ref, b1_ref, bonus_ref):
    f32, bf16 = jnp.float32, jnp.bfloat16
    w = RW_WIDTH
    r = rkv_ref[:, 0:w]
    k = rkv_ref[:, w:2 * w]
    v = rkv_ref[:, 2 * w:3 * w]
    hm = hsum_ref[...]
    kk = k * vec_ref[0:1, :]
    kk = kk * lax.rsqrt(jnp.maximum(_head_sum(kk * kk, hm), 1e-24))
    kk_ref[...] = kk
    w_lo = jnp.tanh(lora_ref[:, 0:128]).astype(bf16)
    a_lo = lora_ref[:, 128:256].astype(bf16)
    rrk = r * vec_ref[2:3, :]
    bonus = None
    outs = ((lw0_ref, kd0_ref, b0_ref), (lw1_ref, kd1_ref, b1_ref))
    for d in range(2):
        x = w0_ref[d:d + 1, :] + jnp.dot(w_lo, w2_ref[d], preferred_element_type=f32)
        w_log = -(jnp.maximum(-x, 0.0) + jnp.log(1.0 + jnp.exp(-jnp.abs(x)))) - 0.5
        a = jax.nn.sigmoid(a0_ref[d:d + 1, :] + jnp.dot(a_lo, a2_ref[d], preferred_element_type=f32))
        kd = k * (1.0 + (a - 1.0) * vec_ref[1:2, :])
        outs[d][0][...] = -jnp.exp(w_log)
        outs[d][1][...] = kd
        outs[d][2][...] = kk * a
        term = _head_sum(rrk * kd, hm)
        bonus = term if bonus is None else bonus + term
    bonus_ref[...] = bonus * v


def rwkv_prep(rkv, lora, p):
    bsz, n, _ = rkv.shape
    rows = bsz * n
    tm = min(MIX_BLOCK_ROWS, n)
    w = RW_WIDTH
    hid = np.arange(w) // RW_HEAD
    hsum = jnp.asarray(hid[:, None] == hid[None, :], jnp.bfloat16)
    vec = jnp.stack([p['rw_kk'], p['rw_ka'], p['rw_rk']])
    zero = jnp.zeros((RW_DECAY_LORA, w), jnp.float32)
    w2 = jnp.stack([jnp.concatenate([p['rw_w2'][0], zero]), jnp.concatenate([zero, p['rw_w2'][1]])])
    a2 = jnp.stack([jnp.concatenate([p['rw_a2'][0], zero]), jnp.concatenate([zero, p['rw_a2'][1]])])
    consts = [vec, p['rw_w0'], p['rw_a0'], w2.astype(jnp.bfloat16), a2.astype(jnp.bfloat16), hsum]

    def const(a):
        return pl.BlockSpec(a.shape, lambda i: (0,) * a.ndim)

    outs = pl.pallas_call(
        _rwkv_prep_kernel,
        out_shape=tuple(jax.ShapeDtypeStruct((rows, w), jnp.float32) for _ in range(8)),
        grid=(rows // tm,),
        in_specs=[pl.BlockSpec((tm, 3 * w), lambda i: (i, 0)), pl.BlockSpec((tm, lora.shape[-1]), lambda i: (i, 0))]
                 + [const(c) for c in consts],
        out_specs=tuple(pl.BlockSpec((tm, w), lambda i: (i, 0)) for _ in range(8)),
        compiler_params=pltpu.CompilerParams(dimension_semantics=("parallel",), vmem_limit_bytes=VMEM_LIMIT),
        name="rwkv_prep",
    )(rkv.reshape(rows, 3 * w), lora.reshape(rows, -1), *consts)
    kk, lw0, kd0, b0, lw1, kd1, b1, bonus = [o.reshape(bsz, n, w) for o in outs]
    return kk, (lw0, lw1), (kd0, kd1), (b0, b1), bonus


def rwkv_branch(rkv, lora, p, s0, mask_set, block_tokens):
    kk, lw, kd, bvec, bonus = rwkv_prep(rkv, lora, p)
    y_f, y_b, s_fin = rwkv_scan_pallas(rkv, kk, lw, kd, bvec, s0, mask_set, block_tokens)
    return (y_f, y_b, bonus, lora[..., IN_LORA - RW_GATE_LORA:]), s_fin


def hyena_filter_spectra(n_tok, p):
    hp = lax.Precision.HIGHEST
    bands = (HY_POS_DIM - 1) // 2
    t = jnp.linspace(0.0, 1.0, n_tok, dtype=jnp.float32)[:, None]
    w = (2.0 * math.pi / n_tok) * jnp.arange(n_tok, dtype=jnp.float32)[:, None]
    f = jnp.linspace(1e-4, bands - 1, bands, dtype=jnp.float32)[None, :]
    feats = jnp.concatenate([t, jnp.cos(f * w), -jnp.sin(f * w)], axis=-1)
    h = jnp.sin(p['hy_f_freq1'] * (jnp.dot(feats, p['hy_f_w1'], precision=hp) + p['hy_f_b1']))
    h = jnp.sin(p['hy_f_freq2'] * (jnp.dot(h, p['hy_f_w2'], precision=hp) + p['hy_f_b2']))
    h = jnp.dot(h, p['hy_f_w3'], precision=hp).reshape(n_tok, HY_ORDER, 2, HY_WIDTH)
    rates = jnp.abs(jnp.linspace(math.log(HY_DECAY_TARGET) / HY_DECAY_PCT_SHORT,
                                 math.log(HY_DECAY_TARGET) / HY_DECAY_PCT_LONG, HY_WIDTH, dtype=jnp.float32))
    h = h * jnp.exp(-t * rates)[:, None, None, :]
    h_fwd, h_bwd = h[:, :, 0], h[:, :, 1]
    filt = jnp.concatenate([h_fwd, jnp.zeros_like(h_fwd[:1]), h_bwd[:0:-1]], axis=0)
    return jnp.fft.rfft(filt, axis=0)


def long_conv(z, k_spec, bias):
    n = z.shape[1]
    z_spec = jnp.fft.rfft(z, n=2 * n, axis=1)
    y = jnp.fft.irfft(z_spec * k_spec[None], n=2 * n, axis=1)[:, :n]
    return y + z * bias


def hyena_branch(streams, p):
    n = streams.shape[1]
    v, x1, x2 = jnp.split(streams, 3, axis=-1)
    k_spec = hyena_filter_spectra(n, p)
    z = v
    for o, gate in enumerate((x1, x2)):
        z = gate * long_conv(z, k_spec[:, o], p['hy_bias'][o])
    return z


HY_N = 2 * SEQ
HY_N1 = 64
HY_N2 = 128
HY_NH = HY_N1 // 2
HY_PITCH = 136
HY_LANES = 128


def hyena_dft_tables():
    k1 = np.arange(HY_N1)[:, None]
    n1 = np.arange(HY_NH)[None, :]
    n2 = np.arange(HY_N2)
    ph = -2 * np.pi * (k1 * n1 / HY_N1)[None] - 2 * np.pi * (n2[:, None, None] * k1[None] / HY_N)
    lhs1 = np.concatenate([np.cos(ph), np.sin(ph)], axis=1)
    lhs2 = np.concatenate([np.cos(ph).transpose(0, 2, 1), np.sin(ph).transpose(0, 2, 1)], axis=2) / HY_N
    kk = np.arange(HY_N2)
    ang = -2 * np.pi * np.outer(kk, kk) / HY_N2
    cr, ci = np.cos(ang), np.sin(ang)
    f_fwd = np.block([[cr, -ci], [ci, cr]])
    f_inv = np.block([[cr, ci], [-ci, cr]])

    def as_bf(a):
        return jnp.asarray(a, jnp.float32).astype(jnp.bfloat16)

    return as_bf(lhs1), as_bf(lhs2), as_bf(f_fwd), as_bf(f_inv)


def hyena_spectrum_layout(k_spec):
    w = k_spec.shape[1]
    full = jnp.concatenate([k_spec, jnp.conj(k_spec[-2:0:-1])], axis=0)
    parts = jnp.stack([jnp.real(full), jnp.imag(full)], axis=0).reshape(2, HY_N2, HY_N1, w)
    parts = jnp.transpose(parts, (2, 0, 1, 3)).reshape(HY_N1 * 2 * HY_N2, w)
    return jnp.transpose(parts.reshape(-1, w // HY_LANES, HY_LANES), (1, 0, 2)).astype(jnp.float32)


def _hy_kernel(z_ref, g_ref, k_ref, l1_ref, l2_ref, ff_ref, fi_ref, bias_ref, o_ref, a_ref):
    f32, bf16 = jnp.float32, jnp.bfloat16
    n2n, nh, pitch, half = HY_N2, HY_NH, HY_PITCH, HY_N1

    def stage1(q, c):
        slab = z_ref[0, pl.ds(pl.multiple_of(q * nh, nh), nh), :].astype(bf16)
        a_ref[pl.ds(pl.multiple_of(q * pitch, 8), 2 * half), :] = jnp.dot(l1_ref[q], slab, preferred_element_type=f32)
        return c

    lax.fori_loop(0, n2n, stage1, 0, unroll=8)

    def stage2(k, c):
        a = jnp.concatenate([a_ref[pl.ds(k, n2n, stride=pitch), :],
                             a_ref[pl.ds(half + k, n2n, stride=pitch), :]], axis=0).astype(bf16)
        x = jnp.dot(ff_ref[...], a, preferred_element_type=f32)
        xr, xi = x[:n2n], x[n2n:]
        base = pl.multiple_of(k * 2 * n2n, 2 * n2n)
        kr = k_ref[0, pl.ds(base, n2n), :]
        ki = k_ref[0, pl.ds(base + n2n, n2n), :]
        y = jnp.concatenate([xr * kr - xi * ki, xr * ki + xi * kr], axis=0).astype(bf16)
        b = jnp.dot(fi_ref[...], y, preferred_element_type=f32)
        a_ref[pl.ds(k, n2n, stride=pitch), :] = b[:n2n]
        a_ref[pl.ds(half + k, n2n, stride=pitch), :] = b[n2n:]
        return c

    lax.fori_loop(0, half, stage2, 0, unroll=4)

    def stage3(q, c):
        blk = a_ref[pl.ds(pl.multiple_of(q * pitch, 8), 2 * half), :].astype(bf16)
        y = jnp.dot(l2_ref[q], blk, preferred_element_type=f32)
        rows = pl.ds(pl.multiple_of(q * nh, nh), nh)
        o_ref[0, rows, :] = g_ref[0, rows, :] * (y + z_ref[0, rows, :] * bias_ref[...])
        return c

    lax.fori_loop(0, n2n, stage3, 0, unroll=8)


def hyena_long_conv_gated(zt, gt, spec, bias, tables):
    bsz, n, w = zt.shape
    l1, l2, ff, fi = tables
    nt = w // HY_LANES
    tok = pl.BlockSpec((1, n, HY_LANES), lambda t, b: (b, 0, t))
    return pl.pallas_call(
        _hy_kernel,
        out_shape=jax.ShapeDtypeStruct((bsz, n, w), jnp.float32),
        grid=(nt, bsz),
        in_specs=[tok, tok,
                  pl.BlockSpec((1,) + spec.shape[1:], lambda t, b: (t, 0, 0)),
                  pl.BlockSpec(l1.shape, lambda t, b: (0, 0, 0)),
                  pl.BlockSpec(l2.shape, lambda t, b: (0, 0, 0)),
                  pl.BlockSpec(ff.shape, lambda t, b: (0, 0)),
                  pl.BlockSpec(fi.shape, lambda t, b: (0, 0)),
                  pl.BlockSpec((1, HY_LANES), lambda t, b: (0, t))],
        out_specs=tok,
        scratch_shapes=[pltpu.VMEM((HY_N2 * HY_PITCH, HY_LANES), jnp.float32)],
        compiler_params=pltpu.CompilerParams(dimension_semantics=("parallel", "parallel"),
                                             vmem_limit_bytes=VMEM_LIMIT),
        name="hyena_conv",
    )(zt, gt, spec, l1, l2, ff, fi, bias.reshape(1, w))


def hyena_branch_long(streams, p):
    bsz, n, w3 = streams.shape
    k_spec = hyena_filter_spectra(n, p)
    tables = hyena_dft_tables()
    st = jnp.transpose(streams.reshape(bsz, HY_NH, HY_N2, w3), (0, 2, 1, 3)).reshape(bsz, n, w3)
    z, x1, x2 = jnp.split(st, 3, axis=-1)
    for o, gate in enumerate((x1, x2)):
        z = hyena_long_conv_gated(z, gate, hyena_spectrum_layout(k_spec[:, o]), p['hy_bias'][o], tables)
    w = z.shape[-1]
    return jnp.transpose(z.reshape(bsz, HY_N2, HY_NH, w), (0, 2, 1, 3)).reshape(bsz, n, w)


def _rms(x, gain):
    return x * lax.rsqrt(jnp.mean(x * x, axis=-1, keepdims=True) + NORM_EPS) * gain


def _conv3(x, w_ref, period):
    rows = x.shape[0]
    pos = lax.broadcasted_iota(jnp.int32, (rows, 1), 0) % period
    prev = jnp.where(pos == 0, 0.0, pltpu.roll(x, 1, axis=0))
    nxt = jnp.where(pos == period - 1, 0.0, pltpu.roll(x, rows - 1, axis=0))
    return prev * w_ref[0:1, :] + x * w_ref[1:2, :] + nxt * w_ref[2:3, :]


def _inproj_kernel(x_ref, mod_ref, gain_ref, ws5_ref, wrw_ref, wlo_ref, why_ref, wgt_ref, cwr_ref, cwh_ref, cbh_ref,
                   os5_ref, orw_ref, olo_ref, ohy_ref, ogt_ref, *, period):
    f32 = jnp.float32
    shift, scale = mod_ref[0, 0:1, :], mod_ref[0, 1:2, :]
    h = (_rms(x_ref[...], gain_ref[...]) * (1.0 + scale) + shift).astype(jnp.bfloat16)
    os5_ref[...] = jnp.dot(h, ws5_ref[...], preferred_element_type=f32)
    olo_ref[...] = jnp.dot(h, wlo_ref[...], preferred_element_type=f32)
    ogt_ref[...] = jnp.dot(h, wgt_ref[...], preferred_element_type=f32)
    orw_ref[...] = _conv3(jnp.dot(h, wrw_ref[...], preferred_element_type=f32), cwr_ref, period)
    ohy_ref[...] = _conv3(jnp.dot(h, why_ref[...], preferred_element_type=f32), cwh_ref, period) + cbh_ref[...]


def input_projection(x, mod, gain, w_parts, rw_conv_w, hy_conv_w, hy_conv_b, period):
    bsz, n, d = x.shape
    tm = min(MIX_BLOCK_ROWS, n)
    per_b = n // tm
    rows = bsz * n
    bmap = (lambda i: (i // per_b, 0, 0)) if mod.shape[0] == bsz else (lambda i: (0, 0, 0))
    wb = [w.astype(jnp.bfloat16) for w in w_parts]

    def const(a):
        return pl.BlockSpec(a.shape, lambda i: (0,) * a.ndim)

    cbh = hy_conv_b.reshape(1, -1)
    outs = pl.pallas_call(
        functools.partial(_inproj_kernel, period=period),
        out_shape=tuple(jax.ShapeDtypeStruct((rows, w.shape[1]), jnp.float32) for w in wb),
        grid=(rows // tm,),
        in_specs=[pl.BlockSpec((tm, d), lambda i: (i, 0)),
                  pl.BlockSpec((1, 3, d), bmap),
                  pl.BlockSpec((1, d), lambda i: (0, 0))]
                 + [const(w) for w in wb] + [const(rw_conv_w), const(hy_conv_w), const(cbh)],
        out_specs=tuple(pl.BlockSpec((tm, w.shape[1]), lambda i: (i, 0)) for w in wb),
        compiler_params=pltpu.CompilerParams(dimension_semantics=("parallel",), vmem_limit_bytes=VMEM_LIMIT),
        name="input_projection",
    )(x.reshape(rows, d), mod, gain.reshape(1, d), *wb, rw_conv_w, hy_conv_w, cbh)
    return tuple(o.reshape(bsz, n, o.shape[-1]) for o in outs)


def _merge_kernel(x_ref, mod_ref, gain_ref, ys5_ref, us5_ref, wf_ref, wb_ref, bonus_ref, glo_ref, hy_ref, gt_ref,
                  s5d_ref, glu_ref, lnw_ref, lnb_ref, g2_ref, havg_ref, brs_ref, brr_ref, brh_ref, ow_ref, o_ref):
    f32, bf16 = jnp.float32, jnp.bfloat16
    y = ys5_ref[...] + s5d_ref[...] * us5_ref[...]
    y = 0.5 * y * (1.0 + lax.erf(y * (1.0 / math.sqrt(2.0))))
    s5 = y * jax.nn.sigmoid(jnp.dot(y.astype(bf16), glu_ref[...], preferred_element_type=f32))
    wkv = wf_ref[...] + wb_ref[...]

    def head_mean(t):
        hi = t.astype(bf16)
        rem = t - hi.astype(f32)
        mid = rem.astype(bf16)
        lo = (rem - mid.astype(f32)).astype(bf16)
        hm = havg_ref[...]
        tot = (jnp.dot(hi, hm, preferred_element_type=f32) + jnp.dot(mid, hm, preferred_element_type=f32)
               + jnp.dot(lo, hm, preferred_element_type=f32))
        return tot * (1.0 / RW_HEAD)

    mu = head_mean(wkv)
    cen = wkv - mu
    var = head_mean(cen * cen)
    o = cen * lax.rsqrt(var + RW_GN_EPS) * lnw_ref[...] + lnb_ref[...] + bonus_ref[...]
    g = jnp.dot(jax.nn.sigmoid(glo_ref[...]).astype(bf16), g2_ref[...], preferred_element_type=f32)
    rw = o * g
    gt = jax.nn.sigmoid(gt_ref[...])
    d = x_ref.shape[-1]
    m = (gt[:, 0:d] * jnp.dot(s5.astype(bf16), brs_ref[...], preferred_element_type=f32)
         + gt[:, d:2 * d] * jnp.dot(rw.astype(bf16), brr_ref[...], preferred_element_type=f32)
         + gt[:, 2 * d:3 * d] * jnp.dot(hy_ref[...].astype(bf16), brh_ref[...], preferred_element_type=f32))
    yl = jnp.dot(m.astype(bf16), ow_ref[...], preferred_element_type=f32)
    o_ref[...] = x_ref[...] + mod_ref[0, 2:3, :] * _rms(yl, gain_ref[...])


def merge_block(x, mod, gain, ys5, us5, rw_parts, hy, gates, p):
    bsz, n, d = x.shape
    tm = min(MIX_BLOCK_ROWS, n)
    per_b = n // tm
    rows = bsz * n
    bmap = (lambda i: (i // per_b, 0, 0)) if mod.shape[0] == bsz else (lambda i: (0, 0, 0))

    def bf(a):
        return a.astype(jnp.bfloat16)

    def row(a):
        return pl.BlockSpec((tm, a.shape[-1]), lambda i: (i, 0))

    def const(a):
        return pl.BlockSpec(a.shape, lambda i: (0,) * a.ndim)

    hid = np.arange(RW_WIDTH) // RW_HEAD
    havg = jnp.asarray(hid[:, None] == hid[None, :], jnp.bfloat16)
    acts = [a.reshape(rows, a.shape[-1]) for a in (ys5, us5) + tuple(rw_parts) + (hy, gates)]
    consts = [p['s5_d'].reshape(1, -1), bf(p['s5_glu_w']), p['rw_ln_w'].reshape(1, -1), p['rw_ln_b'].reshape(1, -1),
              bf(p['rw_g2']), havg, bf(p['br_s5']), bf(p['br_rw']), bf(p['br_hy']), bf(p['out_w'])]
    out = pl.pallas_call(
        _merge_kernel,
        out_shape=jax.ShapeDtypeStruct((rows, d), jnp.float32),
        grid=(rows // tm,),
        in_specs=[pl.BlockSpec((tm, d), lambda i: (i, 0)), pl.BlockSpec((1, 3, d), bmap),
                  pl.BlockSpec((1, d), lambda i: (0, 0))] + [row(a) for a in acts] + [const(c) for c in consts],
        out_specs=pl.BlockSpec((tm, d), lambda i: (i, 0)),
        compiler_params=pltpu.CompilerParams(dimension_semantics=("parallel",), vmem_limit_bytes=VMEM_LIMIT),
        name="merge_block",
    )(x.reshape(rows, d), mod, gain.reshape(1, d), *acts, *consts)
    return out.reshape(bsz, n, d)


def token_mixer(x, ctx, mod_l, mod_c, g_pre, g_post, p, with_ctx_out):
    bsz = x.shape[0]
    cuts = [IN_S5, IN_S5 + IN_RW, IN_S5 + IN_RW + IN_LORA, IN_S5 + IN_RW + IN_LORA + IN_HY]
    edges = [0] + cuts + [IN_COLS]
    w_parts = [p['in_w'][:, edges[i]:edges[i + 1]] for i in range(5)]
    conv = (p['rw_conv_w'], p['hy_conv_w'], p['hy_conv_b'])
    u_l, rkv_l, lora_l, hy_l, gate_l = input_projection(x, mod_l, g_pre, w_parts, *conv, GRID_W)
    u_c, rkv_c, lora_c, hy_c, gate_c = input_projection(ctx, mod_c, g_pre, w_parts, *conv, CTX_LEN)
    mask_set = rwkv_masks()
    zr = jnp.zeros((bsz, 2, RW_WIDTH // RW_LANES, RW_LANES, RW_LANES), jnp.float32)
    ys5_l, ys5_c = s5_branch(u_l, u_c, p)
    rw_c, rw_state = rwkv_branch(rkv_c, lora_c, p, zr, mask_set, CTX_LEN)
    rw_l, _ = rwkv_branch(rkv_l, lora_l, p, rw_state, mask_set, RW_BLOCK_TOKENS)
    hy_lat = hyena_branch_long(hy_l, p)
    x = merge_block(x, mod_l, g_post, ys5_l, u_l, rw_l, hy_lat, gate_l, p)
    if with_ctx_out:
        ctx = merge_block(ctx, mod_c, g_post, ys5_c, u_c, rw_c, hyena_branch(hy_c, p), gate_c, p)
    return x, ctx


def _ffn_kernel(x_ref, mod_ref, gain_ref, wg_ref, wu_ref, wd_ref, o_ref):
    x = x_ref[...]
    shift, scale, gate = mod_ref[0, 0:1, :], mod_ref[0, 1:2, :], mod_ref[0, 2:3, :]
    h = (_rms(x, gain_ref[0:1, :]) * (1.0 + scale) + shift).astype(jnp.bfloat16)
    g = jnp.dot(h, wg_ref[...], preferred_element_type=jnp.float32)
    u = jnp.dot(h, wu_ref[...], preferred_element_type=jnp.float32)
    a = (g * jax.nn.sigmoid(g) * u).astype(jnp.bfloat16)
    y = jnp.dot(a, wd_ref[...], preferred_element_type=jnp.float32)
    o_ref[...] = x + gate * _rms(y, gain_ref[1:2, :])


def ffn_block(x, mod, gains, wg, wu, wd):
    bsz, n, d = x.shape
    hid = wg.shape[1]
    tm = min(FFN_BLOCK_ROWS, n)
    per_b = n // tm
    xf = x.reshape(bsz * n, d)
    bmap = (lambda i: (i // per_b, 0, 0)) if mod.shape[0] == bsz else (lambda i: (0, 0, 0))
    out = pl.pallas_call(
        _ffn_kernel,
        out_shape=jax.ShapeDtypeStruct((bsz * n, d), jnp.float32),
        grid=(bsz * per_b,),
        in_specs=[pl.BlockSpec((tm, d), lambda i: (i, 0)),
                  pl.BlockSpec((1, 3, d), bmap),
                  pl.BlockSpec((2, d), lambda i: (0, 0)),
                  pl.BlockSpec((d, hid), lambda i: (0, 0)),
                  pl.BlockSpec((d, hid), lambda i: (0, 0)),
                  pl.BlockSpec((hid, d), lambda i: (0, 0))],
        out_specs=pl.BlockSpec((tm, d), lambda i: (i, 0)),
        compiler_params=pltpu.CompilerParams(dimension_semantics=("parallel",), vmem_limit_bytes=VMEM_LIMIT),
        name="ffn_block",
    )(xf, mod, gains, wg.astype(jnp.bfloat16), wu.astype(jnp.bfloat16), wd.astype(jnp.bfloat16))
    return out.reshape(bsz, n, d)


def _moe_kernel(be_ref, x_ref, gate_ref, wg_ref, wu_ref, wd_ref, o_ref):
    j = pl.program_id(1)
    xb = x_ref[...].astype(jnp.bfloat16)
    g = jnp.dot(xb, wg_ref[0], preferred_element_type=jnp.float32)
    u = jnp.dot(xb, wu_ref[0], preferred_element_type=jnp.float32)
    a = (g * jax.nn.sigmoid(g) * u).astype(jnp.bfloat16)
    y = jnp.dot(a, wd_ref[0], preferred_element_type=jnp.float32)

    @pl.when(j == 0)
    def _():
        o_ref[...] = y

    @pl.when(j > 0)
    def _():
        o_ref[...] += y

    @pl.when(j == pl.num_programs(1) - 1)
    def _():
        o_ref[...] *= gate_ref[...]


def moe_expert_blocks(xb, slot_gate, block_e, wg, wu, wd):
    cap, d = xb.shape
    n_blocks = cap // MOE_BLOCK
    hid = wg.shape[-1]
    nh = MOE_HIDDEN_TILES
    th = hid // nh

    def hidx(i, j):
        return j + (i % 2) * (nh - 1 - 2 * j)

    return pl.pallas_call(
        _moe_kernel,
        out_shape=jax.ShapeDtypeStruct((cap, d), jnp.float32),
        grid_spec=pltpu.PrefetchScalarGridSpec(
            num_scalar_prefetch=1,
            grid=(n_blocks, nh),
            in_specs=[pl.BlockSpec((MOE_BLOCK, d), lambda i, j, be: (i, 0)),
                      pl.BlockSpec((MOE_BLOCK, 1), lambda i, j, be: (i, 0)),
                      pl.BlockSpec((1, d, th), lambda i, j, be: (be[i], 0, hidx(i, j))),
                      pl.BlockSpec((1, d, th), lambda i, j, be: (be[i], 0, hidx(i, j))),
                      pl.BlockSpec((1, th, d), lambda i, j, be: (be[i], hidx(i, j), 0))],
            out_specs=pl.BlockSpec((MOE_BLOCK, d), lambda i, j, be: (i, 0))),
        compiler_params=pltpu.CompilerParams(dimension_semantics=("parallel", "arbitrary"),
                                             vmem_limit_bytes=VMEM_LIMIT),
        name="moe_experts",
    )(block_e, xb, slot_gate.reshape(cap, 1), wg.astype(jnp.bfloat16), wu.astype(jnp.bfloat16),
      wd.astype(jnp.bfloat16))


def moe_swiglu(h, router_w, wg, wu, wd):
    d_model = h.shape[-1]
    tok = h.reshape(-1, d_model)
    n = tok.shape[0]
    n_assign = n * TOP_K
    logits = jnp.dot(tok, router_w)
    top_logit, top_e = lax.top_k(logits, TOP_K)
    gate = jax.nn.softmax(top_logit, axis=-1).reshape(-1)
    flat_e = top_e.reshape(-1)
    order = jnp.argsort(flat_e)
    onehot = (flat_e[:, None] == jnp.arange(N_EXPERTS, dtype=flat_e.dtype)[None, :]).astype(jnp.int32)
    sizes = jnp.sum(onehot, axis=0)
    padded = (sizes + MOE_BLOCK - 1) // MOE_BLOCK * MOE_BLOCK
    pad_end = jnp.cumsum(padded)
    pad_start = pad_end - padded
    grp_start = jnp.cumsum(sizes) - sizes
    n_blocks = -(-n_assign // MOE_BLOCK) + N_EXPERTS
    cap = n_blocks * MOE_BLOCK
    block_start = jnp.arange(n_blocks, dtype=jnp.int32) * MOE_BLOCK
    block_e = jnp.minimum(jnp.sum(block_start[:, None] >= pad_end[None, :], axis=1), N_EXPERTS - 1)
    slot_e = jnp.repeat(block_e, MOE_BLOCK)
    within = jnp.arange(cap, dtype=jnp.int32) - pad_start[slot_e]
    valid = within < sizes[slot_e]
    assign = order[jnp.clip(grp_start[slot_e] + within, 0, n_assign - 1)]
    slot_tok = jnp.where(valid, (assign // TOP_K).astype(jnp.int32), n)
    slot_gate = jnp.where(valid, gate[assign], 0.0)
    tok_pad = jnp.concatenate([tok, jnp.zeros((1, d_model), tok.dtype)], axis=0)
    xb = tok_pad[slot_tok]
    yb = moe_expert_blocks(xb, slot_gate, block_e.astype(jnp.int32), wg, wu, wd)
    rank = jnp.take_along_axis(jnp.cumsum(onehot, axis=0), flat_e[:, None], axis=1)[:, 0] - 1
    pos = (pad_start[flat_e] + rank).astype(jnp.int32).reshape(n, TOP_K)
    out = yb[pos[:, 0]]
    for kk in range(1, TOP_K):
        out = out + yb[pos[:, kk]]
    return out.reshape(h.shape)


def kernel(x, c, ctx, c_ctx, mod_w, mod_b, norm_g, in_w, s5_lam_re, s5_lam_im, s5_log_step, s5_b_re, s5_b_im, s5_c_re, s5_c_im, s5_d, s5_glu_w, rw_conv_w, rw_w0, rw_w2, rw_a0, rw_a2, rw_g2, rw_kk, rw_ka, rw_rk, rw_ln_w, rw_ln_b, hy_conv_w, hy_conv_b, hy_f_w1, hy_f_b1, hy_f_freq1, hy_f_w2, hy_f_b2, hy_f_freq2, hy_f_w3, hy_bias, br_s5, br_rw, br_hy, out_w, ffn_wg, ffn_wu, ffn_wd, moe_router, moe_wg, moe_wu, moe_wd):
    silu_c = jax.nn.silu(c)
    silu_cc = jax.nn.silu(c_ctx)
    for i in range(DEPTH):
        last = i == DEPTH - 1
        p = {
            'in_w': in_w[i],
            's5_lam_re': s5_lam_re[i], 's5_lam_im': s5_lam_im[i], 's5_log_step': s5_log_step[i],
            's5_b_re': s5_b_re[i], 's5_b_im': s5_b_im[i], 's5_c_re': s5_c_re[i], 's5_c_im': s5_c_im[i],
            's5_d': s5_d[i], 's5_glu_w': s5_glu_w[i],
            'rw_conv_w': rw_conv_w[i], 'rw_w0': rw_w0[i], 'rw_w2': rw_w2[i], 'rw_a0': rw_a0[i], 'rw_a2': rw_a2[i],
            'rw_g2': rw_g2[i], 'rw_kk': rw_kk[i], 'rw_ka': rw_ka[i], 'rw_rk': rw_rk[i],
            'rw_ln_w': rw_ln_w[i], 'rw_ln_b': rw_ln_b[i],
            'hy_conv_w': hy_conv_w[i], 'hy_conv_b': hy_conv_b[i], 'hy_f_w1': hy_f_w1[i], 'hy_f_b1': hy_f_b1[i],
            'hy_f_freq1': hy_f_freq1[i], 'hy_f_w2': hy_f_w2[i], 'hy_f_b2': hy_f_b2[i], 'hy_f_freq2': hy_f_freq2[i],
            'hy_f_w3': hy_f_w3[i], 'hy_bias': hy_bias[i],
            'br_s5': br_s5[i], 'br_rw': br_rw[i], 'br_hy': br_hy[i], 'out_w': out_w[i],
        }
        ml = jnp.split((silu_c @ mod_w[i] + mod_b[i])[:, None, :], 6, axis=-1)
        mc = jnp.split(silu_cc @ mod_w[i] + mod_b[i], 6, axis=-1)
        g_pre_m, g_post_m, g_pre_f, g_post_f = norm_g[i]
        if i % 2 == 0:
            def channel_mix(t, m, j=i // 2):
                return ffn_block(t, m, norm_g[i, 2:4], ffn_wg[j], ffn_wu[j], ffn_wd[j])
        else:
            def channel_mix(t, m, j=i // 2):
                h = rms_norm(t, g_pre_f) * (1 + m[:, 1:2]) + m[:, 0:1]
                y = moe_swiglu(h, moe_router[j], moe_wg[j], moe_wu[j], moe_wd[j])
                return t + m[:, 2:3] * rms_norm(y, g_post_f)
        x, ctx = token_mixer(x, ctx, jnp.concatenate(ml[0:3], axis=1), jnp.stack(mc[0:3])[None],
                             g_pre_m, g_post_m, p, not last)
        x = channel_mix(x, jnp.concatenate(ml[3:6], axis=1))
        if not last:
            ctx = channel_mix(ctx, jnp.stack(mc[3:6])[None])
    return x
```

```python
import math
import functools
import numpy as np
import jax
import jax.numpy as jnp
from jax import lax
from jax.experimental import pallas as pl
from jax.experimental.pallas import tpu as pltpu

D_MODEL = 1024
BATCH = 8
SEQ = 4096
DEPTH = 2
GRID_W = 64
CTX_LEN = 256
NORM_EPS = 1e-6
S5_WIDTH = D_MODEL // 4
S5_GROUP = 16
S5_GROUPS = S5_WIDTH // S5_GROUP
S5_STATE = 64
RW_WIDTH = D_MODEL // 2
RW_HEAD = 64
RW_HEADS = RW_WIDTH // RW_HEAD
RW_DECAY_LORA = 64
RW_ICLR_LORA = 64
RW_GATE_LORA = 128
RW_GN_EPS = 64e-5
HY_WIDTH = D_MODEL // 4
HY_ORDER = 2
HY_POS_DIM = 33
HY_FILTER_HIDDEN = 64
HY_DECAY_TARGET = 1e-2
HY_DECAY_PCT_SHORT = 0.3
HY_DECAY_PCT_LONG = 1.5
SHORT_CONV = 3
N_BRANCH = 3
FFN_HIDDEN = 2816
N_EXPERTS = 8
TOP_K = 2
EXPERT_HIDDEN = 3584
MOE_BLOCK = 512
IN_S5 = S5_WIDTH
IN_RW = 3 * RW_WIDTH
IN_LORA = 2 * RW_DECAY_LORA + 2 * RW_ICLR_LORA + RW_GATE_LORA
IN_HY = (HY_ORDER + 1) * HY_WIDTH
IN_GATE = N_BRANCH * D_MODEL
IN_COLS = IN_S5 + IN_RW + IN_LORA + IN_HY + IN_GATE

VMEM_LIMIT = 56 * 1024 * 1024
S5_CHUNK = 64
RW_CHUNK = 64
RW_HG = 4
RW_LANES = RW_HG * RW_HEAD
RW_BLOCK_TOKENS = 512
MOE_HIDDEN_TILES = 2
FFN_BLOCK_ROWS = 256
MIX_BLOCK_ROWS = 256


def rms_norm(x, gain):
    y = x * lax.rsqrt(jnp.mean(x * x, axis=-1, keepdims=True) + NORM_EPS)
    return y * gain


def s5_chunk_operators(lam_re, lam_im, log_step, b_re, b_im, c_re, c_im):
    T, G, P, GC = S5_CHUNK, S5_GROUPS, S5_STATE, S5_GROUP
    hp = lax.Precision.HIGHEST
    step = jnp.exp(log_step)[..., None]
    lr, li = lam_re, lam_im
    tau = jnp.arange(T + 1, dtype=jnp.float32)[:, None, None, None]
    mag = jnp.exp(lr * step * tau)
    ang = li * step * tau
    e_re, e_im = mag * jnp.cos(ang), mag * jnp.sin(ang)
    ab_re, ab_im = e_re[1], e_im[1]
    den = lr * lr + li * li
    nr, ni = ab_re - 1.0, ab_im
    q_re = (nr * lr + ni * li) / den
    q_im = (ni * lr - nr * li) / den
    bb_re = q_re[..., None] * b_re - q_im[..., None] * b_im
    bb_im = q_re[..., None] * b_im + q_im[..., None] * b_re
    eb_re = e_re[..., None] * bb_re - e_im[..., None] * bb_im
    eb_im = e_re[..., None] * bb_im + e_im[..., None] * bb_re
    ktau = (jnp.einsum('dgip,tdgpj->tdgij', c_re, eb_re[:T], precision=hp)
            - jnp.einsum('dgip,tdgpj->tdgij', c_im, eb_im[:T], precision=hp))
    t_idx = jnp.arange(T)
    k_f = jnp.transpose(ktau[:, 0], (1, 3, 0, 2))
    k_b = jnp.transpose(ktau[:, 1], (1, 3, 0, 2))
    table = jnp.concatenate([k_b[:, :, :0:-1], k_f[:, :, :1] + k_b[:, :, :1], k_f[:, :, 1:]], axis=2)
    table = table.astype(jnp.bfloat16).reshape(G, GC, (2 * T - 1) * GC)
    kt = jnp.stack([table[:, :, (T - 1 - s) * GC:(2 * T - 1 - s) * GC] for s in range(T)], axis=1)
    kt = kt.reshape(G, T * GC, T * GC)
    wf_re, wf_im = eb_re[T - 1 - t_idx, 0], eb_im[T - 1 - t_idx, 0]
    wb_re, wb_im = eb_re[t_idx, 1], eb_im[t_idx, 1]
    win = jnp.stack([wf_re, wf_im, wb_re, wb_im], axis=0)
    win = jnp.transpose(win, (2, 1, 4, 0, 3)).reshape(G, T * GC, 4 * P)
    ef_re, ef_im = e_re[t_idx + 1, 0], e_im[t_idx + 1, 0]
    eb2_re, eb2_im = e_re[T - t_idx, 1], e_im[T - t_idx, 1]

    def readout(cr, ci, er, ei):
        re = cr[None] * er[:, :, None, :] - ci[None] * ei[:, :, None, :]
        im = -(cr[None] * ei[:, :, None, :] + ci[None] * er[:, :, None, :])
        return re, im

    of_re, of_im = readout(c_re[0], c_im[0], ef_re, ef_im)
    ob_re, ob_im = readout(c_re[1], c_im[1], eb2_re, eb2_im)
    wout = jnp.stack([of_re, of_im, ob_re, ob_im], axis=0)
    wout = jnp.transpose(wout, (2, 0, 4, 1, 3)).reshape(G, 4 * P, T * GC)
    at_re, at_im = e_re[T], e_im[T]
    apow = jnp.concatenate([at_re[0], at_re[0], at_re[1], at_re[1]], axis=-1)[:, None, :]
    aimg = jnp.concatenate([-at_im[0], at_im[0], -at_im[1], at_im[1]], axis=-1)[:, None, :]
    return kt.astype(jnp.bfloat16), win.astype(jnp.bfloat16), wout.astype(jnp.bfloat16), apow, aimg


def _s5_kernel(x_ref, kt_ref, win_ref, wout_ref, apow_ref, aimg_ref, h0_ref, y_ref, hfin_ref, hin_ref, *,
               n_chunks, bsz):
    P2 = 2 * S5_STATE
    xb = x_ref[0].astype(jnp.bfloat16)
    hin_ref[...] = jnp.dot(xb, win_ref[0], preferred_element_type=jnp.float32)
    ap = apow_ref[0]
    ai = aimg_ref[0]
    apf, aif = ap[:, :P2], ai[:, :P2]
    apb, aib = ap[:, P2:], ai[:, P2:]

    def cmul(h, a_p, a_i):
        return h * a_p + pltpu.roll(h, S5_STATE, axis=1) * a_i

    def body(c, carry):
        hf, hb = carry
        rf = pl.ds(pl.multiple_of(c * bsz, bsz), bsz)
        rb = pl.ds(pl.multiple_of((n_chunks - 1 - c) * bsz, bsz), bsz)
        df = hin_ref[rf, :P2]
        db = hin_ref[rb, P2:]
        hin_ref[rf, :P2] = hf
        hin_ref[rb, P2:] = hb
        return cmul(hf, apf, aif) + df, cmul(hb, apb, aib) + db

    h0 = h0_ref[0]
    hf, hb = lax.fori_loop(0, n_chunks, body, (h0[:, :P2], h0[:, P2:]))
    hfin_ref[0, :, :P2] = hf
    hfin_ref[0, :, P2:] = hb
    y = jnp.dot(xb, kt_ref[0], preferred_element_type=jnp.float32)
    y = y + jnp.dot(hin_ref[...].astype(jnp.bfloat16), wout_ref[0], preferred_element_type=jnp.float32)
    y_ref[0] = y


def s5_scan_pallas(u, ops, h0):
    kt, win, wout, apow, aimg = ops
    bsz, n, _ = u.shape
    T, G, P, GC = S5_CHUNK, S5_GROUPS, S5_STATE, S5_GROUP
    nc = n // T
    rows = nc * bsz
    x = jnp.transpose(u.reshape(bsz, nc, T, G, GC), (3, 1, 0, 2, 4)).reshape(G, rows, T * GC)
    y, hfin = pl.pallas_call(
        functools.partial(_s5_kernel, n_chunks=nc, bsz=bsz),
        out_shape=(jax.ShapeDtypeStruct((G, rows, T * GC), jnp.float32),
                   jax.ShapeDtypeStruct((G, bsz, 4 * P), jnp.float32)),
        grid=(G,),
        in_specs=[pl.BlockSpec((1, rows, T * GC), lambda g: (g, 0, 0)),
                  pl.BlockSpec((1, T * GC, T * GC), lambda g: (g, 0, 0)),
                  pl.BlockSpec((1, T * GC, 4 * P), lambda g: (g, 0, 0)),
                  pl.BlockSpec((1, 4 * P, T * GC), lambda g: (g, 0, 0)),
                  pl.BlockSpec((1, 1, 4 * P), lambda g: (g, 0, 0)),
                  pl.BlockSpec((1, 1, 4 * P), lambda g: (g, 0, 0)),
                  pl.BlockSpec((1, bsz, 4 * P), lambda g: (g, 0, 0))],
        out_specs=(pl.BlockSpec((1, rows, T * GC), lambda g: (g, 0, 0)),
                   pl.BlockSpec((1, bsz, 4 * P), lambda g: (g, 0, 0))),
        scratch_shapes=[pltpu.VMEM((rows, 4 * P), jnp.float32)],
        compiler_params=pltpu.CompilerParams(dimension_semantics=("parallel",),
                                             vmem_limit_bytes=VMEM_LIMIT),
        name="s5_scan",
    )(x, kt, win, wout, apow, aimg, h0)
    y = jnp.transpose(y.reshape(G, nc, bsz, T, GC), (2, 1, 3, 0, 4)).reshape(bsz, n, G * GC)
    return y, hfin


def s5_branch(u, u_ctx, p):
    ops = s5_chunk_operators(p['s5_lam_re'], p['s5_lam_im'], p['s5_log_step'], p['s5_b_re'], p['s5_b_im'],
                             p['s5_c_re'], p['s5_c_im'])
    h0 = jnp.zeros((S5_GROUPS, u.shape[0], 4 * S5_STATE), jnp.float32)
    y_ctx, h_ctx = s5_scan_pallas(u_ctx, ops, h0)
    y_lat, _ = s5_scan_pallas(u, ops, h_ctx)
    return y_lat, y_ctx


_NT = (((1,), (1,)), ((), ()))

_M_SAME, _M_EYE, _M_STRICT, _M_INCL, _M_LEVEL0 = 0, 1, 2, 4, 6
_N_LEVELS = 6


def rwkv_masks():
    C, n = RW_CHUNK, RW_LANES
    row = np.arange(n)[:, None]
    col = np.arange(n)[None, :]
    same = (row // C) == (col // C)
    out = [same, row == col]
    per_dir = []
    for reverse in (False, True):
        t, j = (row % C, col % C) if not reverse else (col % C, row % C)
        per_dir.append((same & (j < t), same & (j <= t),
                        [same & ((t // s) % 2 == 1) & ((j // s) == (t // s) - 1) for s in (1, 2, 4, 8, 16, 32)]))
    out += [per_dir[0][0], per_dir[1][0], per_dir[0][1], per_dir[1][1]]
    for l in range(_N_LEVELS):
        out += [per_dir[0][2][l], per_dir[1][2][l]]
    tt, jj = np.arange(C)[:, None], np.arange(C)[None, :]
    tri = np.stack([jj <= tt, jj >= tt])
    wide = np.stack([np.tile(m, (1, RW_HG)) for m in (jj < tt, jj > tt, jj <= tt, jj >= tt)])
    return (jnp.asarray(np.stack(out), jnp.float32), jnp.asarray(wide, jnp.float32),
            jnp.asarray(tri, jnp.bfloat16))


def _bdot(a, b, dims=None):
    a = a.astype(jnp.bfloat16)
    b = b.astype(jnp.bfloat16)
    if dims is None:
        return jnp.dot(a, b, preferred_element_type=jnp.float32)
    return lax.dot_general(a, b, dims, preferred_element_type=jnp.float32)


def _rwkv_chunk(S, r, lw, k, v, kk, b, m_ref, w_ref, tri, d):
    C = RW_CHUNK
    same = m_ref[_M_SAME]
    lw_hi = lw.astype(jnp.bfloat16)
    rem = lw - lw_hi.astype(jnp.float32)
    lw_mid = rem.astype(jnp.bfloat16)
    lw_lo = (rem - lw_mid.astype(jnp.float32)).astype(jnp.bfloat16)
    cl = (jnp.dot(tri, lw_hi, preferred_element_type=jnp.float32)
          + jnp.dot(tri, lw_mid, preferred_element_type=jnp.float32)
          + jnp.dot(tri, lw_lo, preferred_element_type=jnp.float32))
    yield
    tot = jnp.sum(lw, axis=0, keepdims=True)
    e_neg = jnp.exp(-cl)
    e_end = jnp.exp(tot - cl)
    a_t = -kk * jnp.exp(cl - lw)
    r_t = r * jnp.exp(cl)
    b_t = b * e_neg
    k_t = k * e_neg
    b_h = b * e_end
    k_h = k * e_end
    p_c = jnp.exp(tot)

    def tile(x):
        return jnp.concatenate([x] * RW_HG, axis=0)

    def stack(x):
        return tile(x) * same

    def unstack(z):
        acc = z[0:C]
        for h in range(1, RW_HG):
            acc = acc + z[h * C:(h + 1) * C]
        return acc

    n = RW_LANES
    sa = stack(a_t)
    a_ab = _bdot(sa, tile(b_t), _NT) * m_ref[_M_STRICT + d]
    wide = _bdot(jnp.concatenate([a_t, r_t], axis=0), jnp.concatenate([stack(b_t), stack(k_t)], axis=0), _NT)
    yield
    a_ak = wide[:C, n:] * w_ref[d]
    a_rb = wide[C:, :n] * w_ref[2 + d]
    a_rk = wide[C:, n:] * w_ref[2 + d]
    dinv = m_ref[_M_EYE] + a_ab * m_ref[_M_LEVEL0 + d]
    for l in range(1, _N_LEVELS):
        inner = _bdot(a_ab * m_ref[_M_LEVEL0 + 2 * l + d], dinv)
        yield
        dinv = dinv + _bdot(dinv, inner)
        yield
    av = _bdot(jnp.concatenate([a_ak, a_rk], axis=0), stack(v))
    yield
    akv = av[:C]
    arkv = av[C:]
    mu = _bdot(unstack(dinv), jnp.concatenate([sa, stack(akv)], axis=1))
    yield
    m1 = mu[:, :n]
    u0 = mu[:, n:]
    my = _bdot(a_rb, jnp.concatenate([stack(m1), stack(u0)], axis=1))
    yield
    m2 = r_t + my[:, :n]
    y0 = my[:, n:] + arkv
    mut = jnp.concatenate([m1, u0], axis=1).T
    gh = _bdot(mut, b_h)
    yield
    g = m_ref[_M_EYE] * p_c + gh[:n] * same
    hmat = (gh[n:] + _bdot(v.T, k_h)) * same
    y = _bdot(m2, S, _NT) + y0
    s_new = _bdot(S, g) + hmat
    return s_new, y


def _lockstep(gens):
    results = [None] * len(gens)
    active = list(range(len(gens)))
    while active:
        for i in list(active):
            try:
                next(gens[i])
            except StopIteration as stop:
                results[i] = stop.value
                active.remove(i)
    return results


def _rwkv_kernel(m_ref, w_ref, tri_ref, rf_ref, rb_ref, vf_ref, vb_ref, kkf_ref, kkb_ref, lwf_ref, kf_ref, bf_ref,
                 lwb_ref, kb_ref, bb_ref, s0_ref, yf_ref, yb_ref, sfin_ref, s_scr, *, n_chunks, n_groups):
    i = pl.program_id(1)
    C = RW_CHUNK

    @pl.when(i == 0)
    def _():
        s_scr[...] = s0_ref[0]

    tri_f = tri_ref[0]
    tri_b = tri_ref[1]

    def body(c, carry):
        rf = pl.ds(pl.multiple_of(c * C, C), C)
        rb = pl.ds(pl.multiple_of((n_chunks - 1 - c) * C, C), C)
        gens = []
        for g in range(n_groups):
            ln = slice(g * RW_LANES, (g + 1) * RW_LANES)
            gens.append(_rwkv_chunk(s_scr[0, g], rf_ref[0, rf, ln], lwf_ref[0, rf, ln], kf_ref[0, rf, ln],
                                    vf_ref[0, rf, ln], kkf_ref[0, rf, ln], bf_ref[0, rf, ln], m_ref, w_ref, tri_f, 0))
            gens.append(_rwkv_chunk(s_scr[1, g], rb_ref[0, rb, ln], lwb_ref[0, rb, ln], kb_ref[0, rb, ln],
                                    vb_ref[0, rb, ln], kkb_ref[0, rb, ln], bb_ref[0, rb, ln], m_ref, w_ref, tri_b, 1))
        out = _lockstep(gens)
        for g in range(n_groups):
            ln = slice(g * RW_LANES, (g + 1) * RW_LANES)
            (s_f, y_f), (s_b, y_b) = out[2 * g], out[2 * g + 1]
            s_scr[0, g] = s_f
            s_scr[1, g] = s_b
            yf_ref[0, rf, ln] = y_f
            yb_ref[0, rb, ln] = y_b
        return carry

    lax.fori_loop(0, n_chunks, body, 0)

    @pl.when(i == pl.num_programs(1) - 1)
    def _():
        sfin_ref[0] = s_scr[...]


def rwkv_scan_pallas(rkv, kk, lw, kd, bvec, s0, mask_set, block_tokens):
    bsz, n, width = kk.shape
    masks, wide, tri = mask_set
    ng = width // RW_LANES
    tb = block_tokens
    nb = n // tb
    fwd = pl.BlockSpec((1, tb, width), lambda b, i: (b, i, 0))
    bwd = pl.BlockSpec((1, tb, width), lambda b, i: (b, nb - 1 - i, 0))
    r_fwd, r_bwd = fwd, bwd
    v_fwd = pl.BlockSpec((1, tb, width), lambda b, i: (b, i, 2))
    v_bwd = pl.BlockSpec((1, tb, width), lambda b, i: (b, nb - 1 - i, 2))
    state_spec = pl.BlockSpec((1, 2, ng, RW_LANES, RW_LANES), lambda b, i: (b, 0, 0, 0, 0))
    return pl.pallas_call(
        functools.partial(_rwkv_kernel, n_chunks=tb // RW_CHUNK, n_groups=ng),
        out_shape=(jax.ShapeDtypeStruct((bsz, n, width), jnp.float32),
                   jax.ShapeDtypeStruct((bsz, n, width), jnp.float32),
                   jax.ShapeDtypeStruct(s0.shape, jnp.float32)),
        grid=(bsz, nb),
        in_specs=[pl.BlockSpec(masks.shape, lambda b, i: (0, 0, 0)),
                  pl.BlockSpec(wide.shape, lambda b, i: (0, 0, 0)),
                  pl.BlockSpec(tri.shape, lambda b, i: (0, 0, 0)),
                  r_fwd, r_bwd, v_fwd, v_bwd, fwd, bwd,
                  fwd, fwd, fwd, bwd, bwd, bwd, state_spec],
        out_specs=(fwd, bwd, state_spec),
        scratch_shapes=[pltpu.VMEM((2, ng, RW_LANES, RW_LANES), jnp.float32)],
        compiler_params=pltpu.CompilerParams(dimension_semantics=("parallel", "arbitrary"),
                                             vmem_limit_bytes=VMEM_LIMIT),
        name="rwkv_scan",
    )(masks, wide, tri, rkv, rkv, rkv, rkv, kk, kk, lw[0], kd[0], bvec[0], lw[1], kd[1], bvec[1], s0)


def _head_sum(t, hm):
    f32, bf16 = jnp.float32, jnp.bfloat16
    hi = t.astype(bf16)
    rem = t - hi.astype(f32)
    mid = rem.astype(bf16)
    lo = (rem - mid.astype(f32)).astype(bf16)
    return (jnp.dot(hi, hm, preferred_element_type=f32) + jnp.dot(mid, hm, preferred_element_type=f32)
            + jnp.dot(lo, hm, preferred_element_type=f32))


def _rwkv_prep_kernel(rkv_ref, lora_ref, vec_ref, w0_ref, a0_ref, w2_ref, a2_ref, hsum_ref,
                      kk_ref, lw0_ref, kd0_ref, b0_ref, lw1_ref, kd1_ref, b1_ref, bonus_ref):
    f32, bf16 = jnp.float32, jnp.bfloat16
    w = RW_WIDTH
    r = rkv_ref[:, 0:w]
    k = rkv_ref[:, w:2 * w]
    v = rkv_ref[:, 2 * w:3 * w]
    hm = hsum_ref[...]
    kk = k * vec_ref[0:1, :]
    kk = kk * lax.rsqrt(jnp.maximum(_head_sum(kk * kk, hm), 1e-24))
    kk_ref[...] = kk
    w_lo = jnp.tanh(lora_ref[:, 0:128]).astype(bf16)
    a_lo = lora_ref[:, 128:256].astype(bf16)
    rrk = r * vec_ref[2:3, :]
    bonus = None
    outs = ((lw0_ref, kd0_ref, b0_ref), (lw1_ref, kd1_ref, b1_ref))
    for d in range(2):
        x = w0_ref[d:d + 1, :] + jnp.dot(w_lo, w2_ref[d], preferred_element_type=f32)
        w_log = -(jnp.maximum(-x, 0.0) + jnp.log(1.0 + jnp.exp(-jnp.abs(x)))) - 0.5
        a = jax.nn.sigmoid(a0_ref[d:d + 1, :] + jnp.dot(a_lo, a2_ref[d], preferred_element_type=f32))
        kd = k * (1.0 + (a - 1.0) * vec_ref[1:2, :])
        outs[d][0][...] = -jnp.exp(w_log)
        outs[d][1][...] = kd
        outs[d][2][...] = kk * a
        term = _head_sum(rrk * kd, hm)
        bonus = term if bonus is None else bonus + term
    bonus_ref[...] = bonus * v


def rwkv_prep(rkv, lora, p):
    bsz, n, _ = rkv.shape
    rows = bsz * n
    tm = min(MIX_BLOCK_ROWS, n)
    w = RW_WIDTH
    hid = np.arange(w) // RW_HEAD
    hsum = jnp.asarray(hid[:, None] == hid[None, :], jnp.bfloat16)
    vec = jnp.stack([p['rw_kk'], p['rw_ka'], p['rw_rk']])
    zero = jnp.zeros((RW_DECAY_LORA, w), jnp.float32)
    w2 = jnp.stack([jnp.concatenate([p['rw_w2'][0], zero]), jnp.concatenate([zero, p['rw_w2'][1]])])
    a2 = jnp.stack([jnp.concatenate([p['rw_a2'][0], zero]), jnp.concatenate([zero, p['rw_a2'][1]])])
    consts = [vec, p['rw_w0'], p['rw_a0'], w2.astype(jnp.bfloat16), a2.astype(jnp.bfloat16), hsum]

    def const(a):
        return pl.BlockSpec(a.shape, lambda i: (0,) * a.ndim)

    outs = pl.pallas_call(
        _rwkv_prep_kernel,
        out_shape=tuple(jax.ShapeDtypeStruct((rows, w), jnp.float32) for _ in range(8)),
        grid=(rows // tm,),
        in_specs=[pl.BlockSpec((tm, 3 * w), lambda i: (i, 0)), pl.BlockSpec((tm, lora.shape[-1]), lambda i: (i, 0))]
                 + [const(c) for c in consts],
        out_specs=tuple(pl.BlockSpec((tm, w), lambda i: (i, 0)) for _ in range(8)),
        compiler_params=pltpu.CompilerParams(dimension_semantics=("parallel",), vmem_limit_bytes=VMEM_LIMIT),
        name="rwkv_prep",
    )(rkv.reshape(rows, 3 * w), lora.reshape(rows, -1), *consts)
    kk, lw0, kd0, b0, lw1, kd1, b1, bonus = [o.reshape(bsz, n, w) for o in outs]
    return kk, (lw0, lw1), (kd0, kd1), (b0, b1), bonus


def rwkv_branch(rkv, lora, p, s0, mask_set, block_tokens):
    kk, lw, kd, bvec, bonus = rwkv_prep(rkv, lora, p)
    y_f, y_b, s_fin = rwkv_scan_pallas(rkv, kk, lw, kd, bvec, s0, mask_set, block_tokens)
    return (y_f, y_b, bonus, lora[..., IN_LORA - RW_GATE_LORA:]), s_fin


def hyena_filter_spectra(n_tok, p):
    hp = lax.Precision.HIGHEST
    bands = (HY_POS_DIM - 1) // 2
    t = jnp.linspace(0.0, 1.0, n_tok, dtype=jnp.float32)[:, None]
    w = (2.0 * math.pi / n_tok) * jnp.arange(n_tok, dtype=jnp.float32)[:, None]
    f = jnp.linspace(1e-4, bands - 1, bands, dtype=jnp.float32)[None, :]
    feats = jnp.concatenate([t, jnp.cos(f * w), -jnp.sin(f * w)], axis=-1)
    h = jnp.sin(p['hy_f_freq1'] * (jnp.dot(feats, p['hy_f_w1'], precision=hp) + p['hy_f_b1']))
    h = jnp.sin(p['hy_f_freq2'] * (jnp.dot(h, p['hy_f_w2'], precision=hp) + p['hy_f_b2']))
    h = jnp.dot(h, p['hy_f_w3'], precision=hp).reshape(n_tok, HY_ORDER, 2, HY_WIDTH)
    rates = jnp.abs(jnp.linspace(math.log(HY_DECAY_TARGET) / HY_DECAY_PCT_SHORT,
                                 math.log(HY_DECAY_TARGET) / HY_DECAY_PCT_LONG, HY_WIDTH, dtype=jnp.float32))
    h = h * jnp.exp(-t * rates)[:, None, None, :]
    h_fwd, h_bwd = h[:, :, 0], h[:, :, 1]
    filt = jnp.concatenate([h_fwd, jnp.zeros_like(h_fwd[:1]), h_bwd[:0:-1]], axis=0)
    return jnp.fft.rfft(filt, axis=0)


HY_N = 2 * SEQ
HY_N1 = 64
HY_N2 = 128
HY_NH = HY_N1 // 2
HY_PITCH = 136
HY_LANES = 128


def hyena_dft_tables():
    k1 = np.arange(HY_N1)[:, None]
    n1 = np.arange(HY_NH)[None, :]
    n2 = np.arange(HY_N2)
    ph = -2 * np.pi * (k1 * n1 / HY_N1)[None] - 2 * np.pi * (n2[:, None, None] * k1[None] / HY_N)
    lhs1 = np.concatenate([np.cos(ph), np.sin(ph)], axis=1)
    lhs2 = np.concatenate([np.cos(ph).transpose(0, 2, 1), np.sin(ph).transpose(0, 2, 1)], axis=2) / HY_N
    kk = np.arange(HY_N2)
    ang = -2 * np.pi * np.outer(kk, kk) / HY_N2
    cr, ci = np.cos(ang), np.sin(ang)
    f_fwd = np.block([[cr, -ci], [ci, cr]])
    f_inv = np.block([[cr, ci], [-ci, cr]])

    def as_bf(a):
        return jnp.asarray(a, jnp.float32).astype(jnp.bfloat16)

    return as_bf(lhs1), as_bf(lhs2), as_bf(f_fwd), as_bf(f_inv)


def hyena_spectrum_layout(k_spec):
    w = k_spec.shape[1]
    full = jnp.concatenate([k_spec, jnp.conj(k_spec[-2:0:-1])], axis=0)
    parts = jnp.stack([jnp.real(full), jnp.imag(full)], axis=0).reshape(2, HY_N2, HY_N1, w)
    parts = jnp.transpose(parts, (2, 0, 1, 3)).reshape(HY_N1 * 2 * HY_N2, w)
    return jnp.transpose(parts.reshape(-1, w // HY_LANES, HY_LANES), (1, 0, 2)).astype(jnp.float32)


def _hy_kernel(z_ref, g_ref, k_ref, l1_ref, l2_ref, ff_ref, fi_ref, bias_ref, o_ref, a_ref):
    f32, bf16 = jnp.float32, jnp.bfloat16
    n2n, nh, pitch, half = HY_N2, HY_NH, HY_PITCH, HY_N1

    def stage1(q, c):
        slab = z_ref[0, pl.ds(pl.multiple_of(q * nh, nh), nh), :].astype(bf16)
        a_ref[pl.ds(pl.multiple_of(q * pitch, 8), 2 * half), :] = jnp.dot(l1_ref[q], slab, preferred_element_type=f32)
        return c

    lax.fori_loop(0, n2n, stage1, 0, unroll=8)

    def stage2(k, c):
        a = jnp.concatenate([a_ref[pl.ds(k, n2n, stride=pitch), :],
                             a_ref[pl.ds(half + k, n2n, stride=pitch), :]], axis=0).astype(bf16)
        x = jnp.dot(ff_ref[...], a, preferred_element_type=f32)
        xr, xi = x[:n2n], x[n2n:]
        base = pl.multiple_of(k * 2 * n2n, 2 * n2n)
        kr = k_ref[0, pl.ds(base, n2n), :]
        ki = k_ref[0, pl.ds(base + n2n, n2n), :]
        y = jnp.concatenate([xr * kr - xi * ki, xr * ki + xi * kr], axis=0).astype(bf16)
        b = jnp.dot(fi_ref[...], y, preferred_element_type=f32)
        a_ref[pl.ds(k, n2n, stride=pitch), :] = b[:n2n]
        a_ref[pl.ds(half + k, n2n, stride=pitch), :] = b[n2n:]
        return c

    lax.fori_loop(0, half, stage2, 0, unroll=4)

    def stage3(q, c):
        blk = a_ref[pl.ds(pl.multiple_of(q * pitch, 8), 2 * half), :].astype(bf16)
        y = jnp.dot(l2_ref[q], blk, preferred_element_type=f32)
        rows = pl.ds(pl.multiple_of(q * nh, nh), nh)
        o_ref[0, rows, :] = g_ref[0, rows, :] * (y + z_ref[0, rows, :] * bias_ref[...])
        return c

    lax.fori_loop(0, n2n, stage3, 0, unroll=8)


def hyena_long_conv_gated(zt, gt, spec, bias, tables):
    bsz, n, w = zt.shape
    l1, l2, ff, fi = tables
    nt = w // HY_LANES
    tok = pl.BlockSpec((1, n, HY_LANES), lambda t, b: (b, 0, t))
    return pl.pallas_call(
        _hy_kernel,
        out_shape=jax.ShapeDtypeStruct((bsz, n, w), jnp.float32),
        grid=(nt, bsz),
        in_specs=[tok, tok,
                  pl.BlockSpec((1,) + spec.shape[1:], lambda t, b: (t, 0, 0)),
                  pl.BlockSpec(l1.shape, lambda t, b: (0, 0, 0)),
                  pl.BlockSpec(l2.shape, lambda t, b: (0, 0, 0)),
                  pl.BlockSpec(ff.shape, lambda t, b: (0, 0)),
                  pl.BlockSpec(fi.shape, lambda t, b: (0, 0)),
                  pl.BlockSpec((1, HY_LANES), lambda t, b: (0, t))],
        out_specs=tok,
        scratch_shapes=[pltpu.VMEM((HY_N2 * HY_PITCH, HY_LANES), jnp.float32)],
        compiler_params=pltpu.CompilerParams(dimension_semantics=("parallel", "parallel"),
                                             vmem_limit_bytes=VMEM_LIMIT),
        name="hyena_conv",
    )(zt, gt, spec, l1, l2, ff, fi, bias.reshape(1, w))


def hyena_branch_long(streams, p):
    bsz, n, w3 = streams.shape
    k_spec = hyena_filter_spectra(n, p)
    tables = hyena_dft_tables()
    st = jnp.transpose(streams.reshape(bsz, HY_NH, HY_N2, w3), (0, 2, 1, 3)).reshape(bsz, n, w3)
    z, x1, x2 = jnp.split(st, 3, axis=-1)
    for o, gate in enumerate((x1, x2)):
        z = hyena_long_conv_gated(z, gate, hyena_spectrum_layout(k_spec[:, o]), p['hy_bias'][o], tables)
    w = z.shape[-1]
    return jnp.transpose(z.reshape(bsz, HY_N2, HY_NH, w), (0, 2, 1, 3)).reshape(bsz, n, w)


def _hy_ctx_kernel(z_ref, g_ref, k_ref, fd_ref, fi_ref, bias_ref, o_ref):
    f32, bf16 = jnp.float32, jnp.bfloat16
    z = z_ref[0]
    n = fd_ref.shape[0] // 2
    x = jnp.dot(fd_ref[...], z.astype(bf16), preferred_element_type=f32)
    xr, xi = x[:n], x[n:]
    kr, ki = k_ref[0, :n, :], k_ref[0, n:, :]
    y = jnp.concatenate([xr * kr - xi * ki, xr * ki + xi * kr], axis=0).astype(bf16)
    conv = jnp.dot(fi_ref[...], y, preferred_element_type=f32)
    o_ref[0] = g_ref[0] * (conv + z * bias_ref[...])


def hyena_branch_ctx(streams, p):
    bsz, n, w3 = streams.shape
    w = w3 // 3
    nn = 2 * n
    k_spec = hyena_filter_spectra(n, p)
    full = jnp.concatenate([k_spec, jnp.conj(k_spec[-2:0:-1])], axis=0)
    spec = jnp.concatenate([jnp.real(full), jnp.imag(full)], axis=0)
    ang = 2 * np.pi * np.outer(np.arange(nn), np.arange(n)) / nn
    fd = jnp.asarray(np.concatenate([np.cos(ang), -np.sin(ang)], axis=0), jnp.float32).astype(jnp.bfloat16)
    fi = jnp.asarray(np.concatenate([np.cos(ang).T, -np.sin(ang).T], axis=1) / nn, jnp.float32).astype(jnp.bfloat16)
    nt = w // HY_LANES
    tok = pl.BlockSpec((1, n, HY_LANES), lambda t, b: (b, 0, t))
    z, x1, x2 = jnp.split(streams, 3, axis=-1)
    for o, gate in enumerate((x1, x2)):
        sp = jnp.transpose(spec[:, o].reshape(2 * nn, nt, HY_LANES), (1, 0, 2)).astype(jnp.float32)
        z = pl.pallas_call(
            _hy_ctx_kernel,
            out_shape=jax.ShapeDtypeStruct((bsz, n, w), jnp.float32),
            grid=(nt, bsz),
            in_specs=[tok, tok,
                      pl.BlockSpec((1, 2 * nn, HY_LANES), lambda t, b: (t, 0, 0)),
                      pl.BlockSpec(fd.shape, lambda t, b: (0, 0)),
                      pl.BlockSpec(fi.shape, lambda t, b: (0, 0)),
                      pl.BlockSpec((1, HY_LANES), lambda t, b: (0, t))],
            out_specs=tok,
            compiler_params=pltpu.CompilerParams(dimension_semantics=("parallel", "parallel"),
                                                 vmem_limit_bytes=VMEM_LIMIT),
            name="hyena_conv_ctx",
        )(z, gate, sp, fd, fi, p['hy_bias'][o].reshape(1, w))
    return z


def _rms(x, gain):
    return x * lax.rsqrt(jnp.mean(x * x, axis=-1, keepdims=True) + NORM_EPS) * gain


def _conv3(x, w_ref, period):
    rows = x.shape[0]
    pos = lax.broadcasted_iota(jnp.int32, (rows, 1), 0) % period
    prev = jnp.where(pos == 0, 0.0, pltpu.roll(x, 1, axis=0))
    nxt = jnp.where(pos == period - 1, 0.0, pltpu.roll(x, rows - 1, axis=0))
    return prev * w_ref[0:1, :] + x * w_ref[1:2, :] + nxt * w_ref[2:3, :]


def _inproj_kernel(x_ref, mod_ref, gain_ref, ws5_ref, wrw_ref, wlo_ref, why_ref, wgt_ref, cwr_ref, cwh_ref, cbh_ref,
                   os5_ref, orw_ref, olo_ref, ohy_ref, ogt_ref, *, period):
    f32 = jnp.float32
    shift, scale = mod_ref[0, 0:1, :], mod_ref[0, 1:2, :]
    h = (_rms(x_ref[...], gain_ref[...]) * (1.0 + scale) + shift).astype(jnp.bfloat16)
    os5_ref[...] = jnp.dot(h, ws5_ref[...], preferred_element_type=f32)
    olo_ref[...] = jnp.dot(h, wlo_ref[...], preferred_element_type=f32)
    ogt_ref[...] = jnp.dot(h, wgt_ref[...], preferred_element_type=f32)
    orw_ref[...] = _conv3(jnp.dot(h, wrw_ref[...], preferred_element_type=f32), cwr_ref, period)
    ohy_ref[...] = _conv3(jnp.dot(h, why_ref[...], preferred_element_type=f32), cwh_ref, period) + cbh_ref[...]


def input_projection(x, mod, gain, w_parts, rw_conv_w, hy_conv_w, hy_conv_b, period):
    bsz, n, d = x.shape
    tm = min(MIX_BLOCK_ROWS, n)
    per_b = n // tm
    rows = bsz * n
    bmap = (lambda i: (i // per_b, 0, 0)) if mod.shape[0] == bsz else (lambda i: (0, 0, 0))
    wb = [w.astype(jnp.bfloat16) for w in w_parts]

    def const(a):
        return pl.BlockSpec(a.shape, lambda i: (0,) * a.ndim)

    cbh = hy_conv_b.reshape(1, -1)
    outs = pl.pallas_call(
        functools.partial(_inproj_kernel, period=period),
        out_shape=tuple(jax.ShapeDtypeStruct((rows, w.shape[1]), jnp.float32) for w in wb),
        grid=(rows // tm,),
        in_specs=[pl.BlockSpec((tm, d), lambda i: (i, 0)),
                  pl.BlockSpec((1, 3, d), bmap),
                  pl.BlockSpec((1, d), lambda i: (0, 0))]
                 + [const(w) for w in wb] + [const(rw_conv_w), const(hy_conv_w), const(cbh)],
        out_specs=tuple(pl.BlockSpec((tm, w.shape[1]), lambda i: (i, 0)) for w in wb),
        compiler_params=pltpu.CompilerParams(dimension_semantics=("parallel",), vmem_limit_bytes=VMEM_LIMIT),
        name="input_projection",
    )(x.reshape(rows, d), mod, gain.reshape(1, d), *wb, rw_conv_w, hy_conv_w, cbh)
    return tuple(o.reshape(bsz, n, o.shape[-1]) for o in outs)


def _merge_kernel(x_ref, mod_ref, gain_ref, ys5_ref, us5_ref, wf_ref, wb_ref, bonus_ref, glo_ref, hy_ref, gt_ref,
                  s5d_ref, glu_ref, lnw_ref, lnb_ref, g2_ref, havg_ref, brs_ref, brr_ref, brh_ref, ow_ref, o_ref):
    f32, bf16 = jnp.float32, jnp.bfloat16
    y = ys5_ref[...] + s5d_ref[...] * us5_ref[...]
    y = 0.5 * y * (1.0 + lax.erf(y * (1.0 / math.sqrt(2.0))))
    s5 = y * jax.nn.sigmoid(jnp.dot(y.astype(bf16), glu_ref[...], preferred_element_type=f32))
    wkv = wf_ref[...] + wb_ref[...]

    def head_mean(t):
        hi = t.astype(bf16)
        rem = t - hi.astype(f32)
        mid = rem.astype(bf16)
        lo = (rem - mid.astype(f32)).astype(bf16)
        hm = havg_ref[...]
        tot = (jnp.dot(hi, hm, preferred_element_type=f32) + jnp.dot(mid, hm, preferred_element_type=f32)
               + jnp.dot(lo, hm, preferred_element_type=f32))
        return tot * (1.0 / RW_HEAD)

    mu = head_mean(wkv)
    cen = wkv - mu
    var = head_mean(cen * cen)
    o = cen * lax.rsqrt(var + RW_GN_EPS) * lnw_ref[...] + lnb_ref[...] + bonus_ref[...]
    g = jnp.dot(jax.nn.sigmoid(glo_ref[...]).astype(bf16), g2_ref[...], preferred_element_type=f32)
    rw = o * g
    gt = jax.nn.sigmoid(gt_ref[...])
    d = x_ref.shape[-1]
    m = (gt[:, 0:d] * jnp.dot(s5.astype(bf16), brs_ref[...], preferred_element_type=f32)
         + gt[:, d:2 * d] * jnp.dot(rw.astype(bf16), brr_ref[...], preferred_element_type=f32)
         + gt[:, 2 * d:3 * d] * jnp.dot(hy_ref[...].astype(bf16), brh_ref[...], preferred_element_type=f32))
    yl = jnp.dot(m.astype(bf16), ow_ref[...], preferred_element_type=f32)
    o_ref[...] = x_ref[...] + mod_ref[0, 2:3, :] * _rms(yl, gain_ref[...])


def merge_block(x, mod, gain, ys5, us5, rw_parts, hy, gates, p):
    bsz, n, d = x.shape
    tm = min(MIX_BLOCK_ROWS, n)
    per_b = n // tm
    rows = bsz * n
    bmap = (lambda i: (i // per_b, 0, 0)) if mod.shape[0] == bsz else (lambda i: (0, 0, 0))

    def bf(a):
        return a.astype(jnp.bfloat16)

    def row(a):
        return pl.BlockSpec((tm, a.shape[-1]), lambda i: (i, 0))

    def const(a):
        return pl.BlockSpec(a.shape, lambda i: (0,) * a.ndim)

    hid = np.arange(RW_WIDTH) // RW_HEAD
    havg = jnp.asarray(hid[:, None] == hid[None, :], jnp.bfloat16)
    acts = [a.reshape(rows, a.shape[-1]) for a in (ys5, us5) + tuple(rw_parts) + (hy, gates)]
    consts = [p['s5_d'].reshape(1, -1), bf(p['s5_glu_w']), p['rw_ln_w'].reshape(1, -1), p['rw_ln_b'].reshape(1, -1),
              bf(p['rw_g2']), havg, bf(p['br_s5']), bf(p['br_rw']), bf(p['br_hy']), bf(p['out_w'])]
    out = pl.pallas_call(
        _merge_kernel,
        out_shape=jax.ShapeDtypeStruct((rows, d), jnp.float32),
        grid=(rows // tm,),
        in_specs=[pl.BlockSpec((tm, d), lambda i: (i, 0)), pl.BlockSpec((1, 3, d), bmap),
                  pl.BlockSpec((1, d), lambda i: (0, 0))] + [row(a) for a in acts] + [const(c) for c in consts],
        out_specs=pl.BlockSpec((tm, d), lambda i: (i, 0)),
        compiler_params=pltpu.CompilerParams(dimension_semantics=("parallel",), vmem_limit_bytes=VMEM_LIMIT),
        name="merge_block",
    )(x.reshape(rows, d), mod, gain.reshape(1, d), *acts, *consts)
    return out.reshape(bsz, n, d)


def token_mixer(x, ctx, mod_l, mod_c, g_pre, g_post, p, with_ctx_out):
    bsz = x.shape[0]
    cuts = [IN_S5, IN_S5 + IN_RW, IN_S5 + IN_RW + IN_LORA, IN_S5 + IN_RW + IN_LORA + IN_HY]
    edges = [0] + cuts + [IN_COLS]
    w_parts = [p['in_w'][:, edges[i]:edges[i + 1]] for i in range(5)]
    conv = (p['rw_conv_w'], p['hy_conv_w'], p['hy_conv_b'])
    u_l, rkv_l, lora_l, hy_l, gate_l = input_projection(x, mod_l, g_pre, w_parts, *conv, GRID_W)
    u_c, rkv_c, lora_c, hy_c, gate_c = input_projection(ctx, mod_c, g_pre, w_parts, *conv, CTX_LEN)
    mask_set = rwkv_masks()
    zr = jnp.zeros((bsz, 2, RW_WIDTH // RW_LANES, RW_LANES, RW_LANES), jnp.float32)
    ys5_l, ys5_c = s5_branch(u_l, u_c, p)
    rw_c, rw_state = rwkv_branch(rkv_c, lora_c, p, zr, mask_set, CTX_LEN)
    rw_l, _ = rwkv_branch(rkv_l, lora_l, p, rw_state, mask_set, RW_BLOCK_TOKENS)
    hy_lat = hyena_branch_long(hy_l, p)
    x = merge_block(x, mod_l, g_post, ys5_l, u_l, rw_l, hy_lat, gate_l, p)
    if with_ctx_out:
        ctx = merge_block(ctx, mod_c, g_post, ys5_c, u_c, rw_c, hyena_branch_ctx(hy_c, p), gate_c, p)
    return x, ctx


def _ffn_kernel(x_ref, mod_ref, gain_ref, wg_ref, wu_ref, wd_ref, o_ref):
    x = x_ref[...]
    shift, scale, gate = mod_ref[0, 0:1, :], mod_ref[0, 1:2, :], mod_ref[0, 2:3, :]
    h = (_rms(x, gain_ref[0:1, :]) * (1.0 + scale) + shift).astype(jnp.bfloat16)
    g = jnp.dot(h, wg_ref[...], preferred_element_type=jnp.float32)
    u = jnp.dot(h, wu_ref[...], preferred_element_type=jnp.float32)
    a = (g * jax.nn.sigmoid(g) * u).astype(jnp.bfloat16)
    y = jnp.dot(a, wd_ref[...], preferred_element_type=jnp.float32)
    o_ref[...] = x + gate * _rms(y, gain_ref[1:2, :])


def ffn_block(x, mod, gains, wg, wu, wd):
    bsz, n, d = x.shape
    hid = wg.shape[1]
    tm = min(FFN_BLOCK_ROWS, n)
    per_b = n // tm
    xf = x.reshape(bsz * n, d)
    bmap = (lambda i: (i // per_b, 0, 0)) if mod.shape[0] == bsz else (lambda i: (0, 0, 0))
    out = pl.pallas_call(
        _ffn_kernel,
        out_shape=jax.ShapeDtypeStruct((bsz * n, d), jnp.float32),
        grid=(bsz * per_b,),
        in_specs=[pl.BlockSpec((tm, d), lambda i: (i, 0)),
                  pl.BlockSpec((1, 3, d), bmap),
                  pl.BlockSpec((2, d), lambda i: (0, 0)),
                  pl.BlockSpec((d, hid), lambda i: (0, 0)),
                  pl.BlockSpec((d, hid), lambda i: (0, 0)),
                  pl.BlockSpec((hid, d), lambda i: (0, 0))],
        out_specs=pl.BlockSpec((tm, d), lambda i: (i, 0)),
        compiler_params=pltpu.CompilerParams(dimension_semantics=("parallel",), vmem_limit_bytes=VMEM_LIMIT),
        name="ffn_block",
    )(xf, mod, gains, wg.astype(jnp.bfloat16), wu.astype(jnp.bfloat16), wd.astype(jnp.bfloat16))
    return out.reshape(bsz, n, d)


def _moe_kernel(be_ref, x_ref, gate_ref, wg_ref, wu_ref, wd_ref, o_ref):
    j = pl.program_id(1)
    xb = x_ref[...].astype(jnp.bfloat16)
    g = jnp.dot(xb, wg_ref[0], preferred_element_type=jnp.float32)
    u = jnp.dot(xb, wu_ref[0], preferred_element_type=jnp.float32)
    a = (g * jax.nn.sigmoid(g) * u).astype(jnp.bfloat16)
    y = jnp.dot(a, wd_ref[0], preferred_element_type=jnp.float32)

    @pl.when(j == 0)
    def _():
        o_ref[...] = y

    @pl.when(j > 0)
    def _():
        o_ref[...] += y

    @pl.when(j == pl.num_programs(1) - 1)
    def _():
        o_ref[...] *= gate_ref[...]


def moe_expert_blocks(xb, slot_gate, block_e, wg, wu, wd):
    cap, d = xb.shape
    n_blocks = cap // MOE_BLOCK
    hid = wg.shape[-1]
    nh = MOE_HIDDEN_TILES
    th = hid // nh

    def hidx(i, j):
        return j + (i % 2) * (nh - 1 - 2 * j)

    return pl.pallas_call(
        _moe_kernel,
        out_shape=jax.ShapeDtypeStruct((cap, d), jnp.float32),
        grid_spec=pltpu.PrefetchScalarGridSpec(
            num_scalar_prefetch=1,
            grid=(n_blocks, nh),
            in_specs=[pl.BlockSpec((MOE_BLOCK, d), lambda i, j, be: (i, 0)),
                      pl.BlockSpec((MOE_BLOCK, 1), lambda i, j, be: (i, 0)),
                      pl.BlockSpec((1, d, th), lambda i, j, be: (be[i], 0, hidx(i, j))),
                      pl.BlockSpec((1, d, th), lambda i, j, be: (be[i], 0, hidx(i, j))),
                      pl.BlockSpec((1, th, d), lambda i, j, be: (be[i], hidx(i, j), 0))],
            out_specs=pl.BlockSpec((MOE_BLOCK, d), lambda i, j, be: (i, 0))),
        compiler_params=pltpu.CompilerParams(dimension_semantics=("parallel", "arbitrary"),
                                             vmem_limit_bytes=VMEM_LIMIT),
        name="moe_experts",
    )(block_e, xb, slot_gate.reshape(cap, 1), wg.astype(jnp.bfloat16), wu.astype(jnp.bfloat16),
      wd.astype(jnp.bfloat16))


def moe_swiglu(h, router_w, wg, wu, wd):
    d_model = h.shape[-1]
    tok = h.reshape(-1, d_model)
    n = tok.shape[0]
    n_assign = n * TOP_K
    logits = jnp.dot(tok, router_w)
    top_logit, top_e = lax.top_k(logits, TOP_K)
    gate = jax.nn.softmax(top_logit, axis=-1).reshape(-1)
    flat_e = top_e.reshape(-1)
    order = jnp.argsort(flat_e)
    onehot = (flat_e[:, None] == jnp.arange(N_EXPERTS, dtype=flat_e.dtype)[None, :]).astype(jnp.int32)
    sizes = jnp.sum(onehot, axis=0)
    padded = (sizes + MOE_BLOCK - 1) // MOE_BLOCK * MOE_BLOCK
    pad_end = jnp.cumsum(padded)
    pad_start = pad_end - padded
    grp_start = jnp.cumsum(sizes) - sizes
    n_blocks = -(-n_assign // MOE_BLOCK) + N_EXPERTS
    cap = n_blocks * MOE_BLOCK
    block_start = jnp.arange(n_blocks, dtype=jnp.int32) * MOE_BLOCK
    block_e = jnp.minimum(jnp.sum(block_start[:, None] >= pad_end[None, :], axis=1), N_EXPERTS - 1)
    slot_e = jnp.repeat(block_e, MOE_BLOCK)
    within = jnp.arange(cap, dtype=jnp.int32) - pad_start[slot_e]
    valid = within < sizes[slot_e]
    assign = order[jnp.clip(grp_start[slot_e] + within, 0, n_assign - 1)]
    slot_tok = jnp.where(valid, (assign // TOP_K).astype(jnp.int32), n)
    slot_gate = jnp.where(valid, gate[assign], 0.0)
    tok_pad = jnp.concatenate([tok, jnp.zeros((1, d_model), tok.dtype)], axis=0)
    xb = tok_pad[slot_tok]
    yb = moe_expert_blocks(xb, slot_gate, block_e.astype(jnp.int32), wg, wu, wd)
    rank = jnp.take_along_axis(jnp.cumsum(onehot, axis=0), flat_e[:, None], axis=1)[:, 0] - 1
    pos = (pad_start[flat_e] + rank).astype(jnp.int32).reshape(n, TOP_K)
    out = yb[pos[:, 0]]
    for kk in range(1, TOP_K):
        out = out + yb[pos[:, kk]]
    return out.reshape(h.shape)


def kernel(x, c, ctx, c_ctx, mod_w, mod_b, norm_g, in_w, s5_lam_re, s5_lam_im, s5_log_step, s5_b_re, s5_b_im, s5_c_re, s5_c_im, s5_d, s5_glu_w, rw_conv_w, rw_w0, rw_w2, rw_a0, rw_a2, rw_g2, rw_kk, rw_ka, rw_rk, rw_ln_w, rw_ln_b, hy_conv_w, hy_conv_b, hy_f_w1, hy_f_b1, hy_f_freq1, hy_f_w2, hy_f_b2, hy_f_freq2, hy_f_w3, hy_bias, br_s5, br_rw, br_hy, out_w, ffn_wg, ffn_wu, ffn_wd, moe_router, moe_wg, moe_wu, moe_wd):
    silu_c = jax.nn.silu(c)
    silu_cc = jax.nn.silu(c_ctx)
    for i in range(DEPTH):
        last = i == DEPTH - 1
        p = {
            'in_w': in_w[i],
            's5_lam_re': s5_lam_re[i], 's5_lam_im': s5_lam_im[i], 's5_log_step': s5_log_step[i],
            's5_b_re': s5_b_re[i], 's5_b_im': s5_b_im[i], 's5_c_re': s5_c_re[i], 's5_c_im': s5_c_im[i],
            's5_d': s5_d[i], 's5_glu_w': s5_glu_w[i],
            'rw_conv_w': rw_conv_w[i], 'rw_w0': rw_w0[i], 'rw_w2': rw_w2[i], 'rw_a0': rw_a0[i], 'rw_a2': rw_a2[i],
            'rw_g2': rw_g2[i], 'rw_kk': rw_kk[i], 'rw_ka': rw_ka[i], 'rw_rk': rw_rk[i],
            'rw_ln_w': rw_ln_w[i], 'rw_ln_b': rw_ln_b[i],
            'hy_conv_w': hy_conv_w[i], 'hy_conv_b': hy_conv_b[i], 'hy_f_w1': hy_f_w1[i], 'hy_f_b1': hy_f_b1[i],
            'hy_f_freq1': hy_f_freq1[i], 'hy_f_w2': hy_f_w2[i], 'hy_f_b2': hy_f_b2[i], 'hy_f_freq2': hy_f_freq2[i],
            'hy_f_w3': hy_f_w3[i], 'hy_bias': hy_bias[i],
            'br_s5': br_s5[i], 'br_rw': br_rw[i], 'br_hy': br_hy[i], 'out_w': out_w[i],
        }
        ml = jnp.split((silu_c @ mod_w[i] + mod_b[i])[:, None, :], 6, axis=-1)
        mc = jnp.split(silu_cc @ mod_w[i] + mod_b[i], 6, axis=-1)
        g_pre_m, g_post_m, g_pre_f, g_post_f = norm_g[i]
        if i % 2 == 0:
            def channel_mix(t, m, j=i // 2):
                return ffn_block(t, m, norm_g[i, 2:4], ffn_wg[j], ffn_wu[j], ffn_wd[j])
        else:
            def channel_mix(t, m, j=i // 2):
                h = rms_norm(t, g_pre_f) * (1 + m[:, 1:2]) + m[:, 0:1]
                y = moe_swiglu(h, moe_router[j], moe_wg[j], moe_wu[j], moe_wd[j])
                return t + m[:, 2:3] * rms_norm(y, g_post_f)
        x, ctx = token_mixer(x, ctx, jnp.concatenate(ml[0:3], axis=1), jnp.stack(mc[0:3])[None],
                             g_pre_m, g_post_m, p, not last)
        x = channel_mix(x, jnp.concatenate(ml[3:6], axis=1))
        if not last:
            ctx = channel_mix(ctx, jnp.stack(mc[3:6])[None])
    return x
```

```python
import math
import functools
import numpy as np
import jax
import jax.numpy as jnp
from jax import lax
from jax.experimental import pallas as pl
from jax.experimental.pallas import tpu as pltpu

D_MODEL = 1024
BATCH = 8
SEQ = 4096
DEPTH = 2
GRID_W = 64
CTX_LEN = 256
NORM_EPS = 1e-6
S5_WIDTH = D_MODEL // 4
S5_GROUP = 16
S5_GROUPS = S5_WIDTH // S5_GROUP
S5_STATE = 64
RW_WIDTH = D_MODEL // 2
RW_HEAD = 64
RW_HEADS = RW_WIDTH // RW_HEAD
RW_DECAY_LORA = 64
RW_ICLR_LORA = 64
RW_GATE_LORA = 128
RW_GN_EPS = 64e-5
HY_WIDTH = D_MODEL // 4
HY_ORDER = 2
HY_POS_DIM = 33
HY_FILTER_HIDDEN = 64
HY_DECAY_TARGET = 1e-2
HY_DECAY_PCT_SHORT = 0.3
HY_DECAY_PCT_LONG = 1.5
SHORT_CONV = 3
N_BRANCH = 3
FFN_HIDDEN = 2816
N_EXPERTS = 8
TOP_K = 2
EXPERT_HIDDEN = 3584
MOE_BLOCK = 512
IN_S5 = S5_WIDTH
IN_RW = 3 * RW_WIDTH
IN_LORA = 2 * RW_DECAY_LORA + 2 * RW_ICLR_LORA + RW_GATE_LORA
IN_HY = (HY_ORDER + 1) * HY_WIDTH
IN_GATE = N_BRANCH * D_MODEL
IN_COLS = IN_S5 + IN_RW + IN_LORA + IN_HY + IN_GATE

VMEM_LIMIT = 56 * 1024 * 1024
S5_CHUNK = 64
RW_CHUNK = 64
RW_HG = 4
RW_LANES = RW_HG * RW_HEAD
RW_BLOCK_TOKENS = 256
RW_BATCH_PER_STEP = 2
MOE_HIDDEN_TILES = 2
FFN_BLOCK_ROWS = 256
MIX_BLOCK_ROWS = 256


def rms_norm(x, gain):
    y = x * lax.rsqrt(jnp.mean(x * x, axis=-1, keepdims=True) + NORM_EPS)
    return y * gain


def s5_chunk_operators(lam_re, lam_im, log_step, b_re, b_im, c_re, c_im):
    T, G, P, GC = S5_CHUNK, S5_GROUPS, S5_STATE, S5_GROUP
    hp = lax.Precision.HIGHEST
    step = jnp.exp(log_step)[..., None]
    lr, li = lam_re, lam_im
    tau = jnp.arange(T + 1, dtype=jnp.float32)[:, None, None, None]
    mag = jnp.exp(lr * step * tau)
    ang = li * step * tau
    e_re, e_im = mag * jnp.cos(ang), mag * jnp.sin(ang)
    ab_re, ab_im = e_re[1], e_im[1]
    den = lr * lr + li * li
    nr, ni = ab_re - 1.0, ab_im
    q_re = (nr * lr + ni * li) / den
    q_im = (ni * lr - nr * li) / den
    bb_re = q_re[..., None] * b_re - q_im[..., None] * b_im
    bb_im = q_re[..., None] * b_im + q_im[..., None] * b_re
    eb_re = e_re[..., None] * bb_re - e_im[..., None] * bb_im
    eb_im = e_re[..., None] * bb_im + e_im[..., None] * bb_re
    ktau = (jnp.einsum('dgip,tdgpj->tdgij', c_re, eb_re[:T], precision=hp)
            - jnp.einsum('dgip,tdgpj->tdgij', c_im, eb_im[:T], precision=hp))
    t_idx = jnp.arange(T)
    k_f = jnp.transpose(ktau[:, 0], (1, 3, 0, 2))
    k_b = jnp.transpose(ktau[:, 1], (1, 3, 0, 2))
    table = jnp.concatenate([k_b[:, :, :0:-1], k_f[:, :, :1] + k_b[:, :, :1], k_f[:, :, 1:]], axis=2)
    table = table.astype(jnp.bfloat16).reshape(G, GC, (2 * T - 1) * GC)
    kt = jnp.stack([table[:, :, (T - 1 - s) * GC:(2 * T - 1 - s) * GC] for s in range(T)], axis=1)
    kt = kt.reshape(G, T * GC, T * GC)
    wf_re, wf_im = eb_re[T - 1 - t_idx, 0], eb_im[T - 1 - t_idx, 0]
    wb_re, wb_im = eb_re[t_idx, 1], eb_im[t_idx, 1]
    win = jnp.stack([wf_re, wf_im, wb_re, wb_im], axis=0)
    win = jnp.transpose(win, (2, 1, 4, 0, 3)).reshape(G, T * GC, 4 * P)
    ef_re, ef_im = e_re[t_idx + 1, 0], e_im[t_idx + 1, 0]
    eb2_re, eb2_im = e_re[T - t_idx, 1], e_im[T - t_idx, 1]

    def readout(cr, ci, er, ei):
        re = cr[None] * er[:, :, None, :] - ci[None] * ei[:, :, None, :]
        im = -(cr[None] * ei[:, :, None, :] + ci[None] * er[:, :, None, :])
        return re, im

    of_re, of_im = readout(c_re[0], c_im[0], ef_re, ef_im)
    ob_re, ob_im = readout(c_re[1], c_im[1], eb2_re, eb2_im)
    wout = jnp.stack([of_re, of_im, ob_re, ob_im], axis=0)
    wout = jnp.transpose(wout, (2, 0, 4, 1, 3)).reshape(G, 4 * P, T * GC)
    at_re, at_im = e_re[T], e_im[T]
    apow = jnp.concatenate([at_re[0], at_re[0], at_re[1], at_re[1]], axis=-1)[:, None, :]
    aimg = jnp.concatenate([-at_im[0], at_im[0], -at_im[1], at_im[1]], axis=-1)[:, None, :]
    return kt.astype(jnp.bfloat16), win.astype(jnp.bfloat16), wout.astype(jnp.bfloat16), apow, aimg


def _s5_kernel(x_ref, kt_ref, win_ref, wout_ref, apow_ref, aimg_ref, h0_ref, y_ref, hfin_ref, hin_ref, *,
               n_chunks, bsz):
    P2 = 2 * S5_STATE
    xb = x_ref[0].astype(jnp.bfloat16)
    hin_ref[...] = jnp.dot(xb, win_ref[0], preferred_element_type=jnp.float32)
    ap = apow_ref[0]
    ai = aimg_ref[0]
    apf, aif = ap[:, :P2], ai[:, :P2]
    apb, aib = ap[:, P2:], ai[:, P2:]

    def cmul(h, a_p, a_i):
        return h * a_p + pltpu.roll(h, S5_STATE, axis=1) * a_i

    def body(c, carry):
        hf, hb = carry
        rf = pl.ds(pl.multiple_of(c * bsz, bsz), bsz)
        rb = pl.ds(pl.multiple_of((n_chunks - 1 - c) * bsz, bsz), bsz)
        df = hin_ref[rf, :P2]
        db = hin_ref[rb, P2:]
        hin_ref[rf, :P2] = hf
        hin_ref[rb, P2:] = hb
        return cmul(hf, apf, aif) + df, cmul(hb, apb, aib) + db

    h0 = h0_ref[0]
    hf, hb = lax.fori_loop(0, n_chunks, body, (h0[:, :P2], h0[:, P2:]))
    hfin_ref[0, :, :P2] = hf
    hfin_ref[0, :, P2:] = hb
    y = jnp.dot(xb, kt_ref[0], preferred_element_type=jnp.float32)
    y = y + jnp.dot(hin_ref[...].astype(jnp.bfloat16), wout_ref[0], preferred_element_type=jnp.float32)
    y_ref[0] = y


def s5_scan_pallas(u, ops, h0):
    kt, win, wout, apow, aimg = ops
    bsz, n, _ = u.shape
    T, G, P, GC = S5_CHUNK, S5_GROUPS, S5_STATE, S5_GROUP
    nc = n // T
    rows = nc * bsz
    x = jnp.transpose(u.reshape(bsz, nc, T, G, GC), (3, 1, 0, 2, 4)).reshape(G, rows, T * GC)
    y, hfin = pl.pallas_call(
        functools.partial(_s5_kernel, n_chunks=nc, bsz=bsz),
        out_shape=(jax.ShapeDtypeStruct((G, rows, T * GC), jnp.float32),
                   jax.ShapeDtypeStruct((G, bsz, 4 * P), jnp.float32)),
        grid=(G,),
        in_specs=[pl.BlockSpec((1, rows, T * GC), lambda g: (g, 0, 0)),
                  pl.BlockSpec((1, T * GC, T * GC), lambda g: (g, 0, 0)),
                  pl.BlockSpec((1, T * GC, 4 * P), lambda g: (g, 0, 0)),
                  pl.BlockSpec((1, 4 * P, T * GC), lambda g: (g, 0, 0)),
                  pl.BlockSpec((1, 1, 4 * P), lambda g: (g, 0, 0)),
                  pl.BlockSpec((1, 1, 4 * P), lambda g: (g, 0, 0)),
                  pl.BlockSpec((1, bsz, 4 * P), lambda g: (g, 0, 0))],
        out_specs=(pl.BlockSpec((1, rows, T * GC), lambda g: (g, 0, 0)),
                   pl.BlockSpec((1, bsz, 4 * P), lambda g: (g, 0, 0))),
        scratch_shapes=[pltpu.VMEM((rows, 4 * P), jnp.float32)],
        compiler_params=pltpu.CompilerParams(dimension_semantics=("parallel",),
                                             vmem_limit_bytes=VMEM_LIMIT),
        name="s5_scan",
    )(x, kt, win, wout, apow, aimg, h0)
    y = jnp.transpose(y.reshape(G, nc, bsz, T, GC), (2, 1, 3, 0, 4)).reshape(bsz, n, G * GC)
    return y, hfin


def s5_branch(u, u_ctx, p):
    ops = s5_chunk_operators(p['s5_lam_re'], p['s5_lam_im'], p['s5_log_step'], p['s5_b_re'], p['s5_b_im'],
                             p['s5_c_re'], p['s5_c_im'])
    h0 = jnp.zeros((S5_GROUPS, u.shape[0], 4 * S5_STATE), jnp.float32)
    y_ctx, h_ctx = s5_scan_pallas(u_ctx, ops, h0)
    y_lat, _ = s5_scan_pallas(u, ops, h_ctx)
    return y_lat, y_ctx


_NT = (((1,), (1,)), ((), ()))

_M_SAME, _M_EYE = 0, 1
_W_STRICT, _W_INCL, _W_EYE, _W_LEVEL0 = 0, 2, 4, 5
_N_LEVELS = 6


def rwkv_masks():
    C, n = RW_CHUNK, RW_LANES
    row = np.arange(n)[:, None]
    col = np.arange(n)[None, :]
    square = np.stack([(row // C) == (col // C), row == col])
    tt, jj = np.arange(C)[:, None], np.arange(C)[None, :]
    wide = [jj < tt, jj > tt, jj <= tt, jj >= tt, jj == tt]
    for s in (1, 2, 4, 8, 16, 32):
        fwd = ((tt // s) % 2 == 1) & ((jj // s) == (tt // s) - 1)
        wide += [fwd, fwd.T]
    wide = np.stack([np.tile(m, (1, RW_HG)) for m in wide])
    tri = np.stack([jj <= tt, jj >= tt])
    return (jnp.asarray(square, jnp.float32), jnp.asarray(wide, jnp.float32), jnp.asarray(tri, jnp.bfloat16))


def _bdot(a, b, dims=None):
    a = a.astype(jnp.bfloat16)
    b = b.astype(jnp.bfloat16)
    if dims is None:
        return jnp.dot(a, b, preferred_element_type=jnp.float32)
    return lax.dot_general(a, b, dims, preferred_element_type=jnp.float32)


def _rwkv_chunk(S, r, lw, k, v, kk, b, m_ref, w_ref, same_b, tri, d):
    C = RW_CHUNK
    same = m_ref[_M_SAME]
    lw_hi = lw.astype(jnp.bfloat16)
    rem = lw - lw_hi.astype(jnp.float32)
    lw_mid = rem.astype(jnp.bfloat16)
    lw_lo = (rem - lw_mid.astype(jnp.float32)).astype(jnp.bfloat16)
    cl = (jnp.dot(tri, lw_hi, preferred_element_type=jnp.float32)
          + jnp.dot(tri, lw_mid, preferred_element_type=jnp.float32)
          + jnp.dot(tri, lw_lo, preferred_element_type=jnp.float32))
    yield
    tot = jnp.sum(lw, axis=0, keepdims=True)
    e_neg = jnp.exp(-cl)
    e_end = jnp.exp(tot - cl)
    a_t = -kk * jnp.exp(cl - lw)
    r_t = r * jnp.exp(cl)
    b_t = b * e_neg
    k_t = k * e_neg
    b_h = b * e_end
    k_h = k * e_end
    p_c = jnp.exp(tot)

    def tile(x):
        return jnp.concatenate([x] * RW_HG, axis=0)

    def stack(x):
        return tile(x.astype(jnp.bfloat16)) * same_b

    n = RW_LANES
    sa = stack(a_t)
    wide = _bdot(jnp.concatenate([a_t, r_t], axis=0), jnp.concatenate([stack(b_t), stack(k_t)], axis=0), _NT)
    yield
    a_ab = wide[:C, :n] * w_ref[_W_STRICT + d]
    a_ak = wide[:C, n:] * w_ref[_W_STRICT + d]
    a_rb = wide[C:, :n] * w_ref[_W_INCL + d]
    a_rk = wide[C:, n:] * w_ref[_W_INCL + d]
    dinv = w_ref[_W_EYE] + a_ab * w_ref[_W_LEVEL0 + d]
    for l in range(1, _N_LEVELS):
        inner = _bdot(a_ab * w_ref[_W_LEVEL0 + 2 * l + d], stack(dinv))
        yield
        dinv = dinv + _bdot(dinv, stack(inner))
        yield
    av = _bdot(jnp.concatenate([a_ak, a_rk], axis=0), stack(v))
    yield
    akv = av[:C]
    arkv = av[C:]
    mu = _bdot(dinv, jnp.concatenate([sa, stack(akv)], axis=1))
    yield
    m1 = mu[:, :n]
    u0 = mu[:, n:]
    my = _bdot(a_rb, jnp.concatenate([stack(m1), stack(u0)], axis=1))
    yield
    m2 = r_t + my[:, :n]
    y0 = my[:, n:] + arkv
    mut = jnp.concatenate([m1, u0], axis=1).T
    gh = _bdot(mut, b_h)
    yield
    g = m_ref[_M_EYE] * p_c + gh[:n] * same
    hmat = (gh[n:] + _bdot(v.T, k_h)) * same
    y = _bdot(m2, S, _NT) + y0
    s_new = _bdot(S, g) + hmat
    return s_new, y


def _lockstep(gens):
    results = [None] * len(gens)
    active = list(range(len(gens)))
    while active:
        for i in list(active):
            try:
                next(gens[i])
            except StopIteration as stop:
                results[i] = stop.value
                active.remove(i)
    return results


def _rwkv_kernel(m_ref, w_ref, tri_ref, rf_ref, rb_ref, vf_ref, vb_ref, kkf_ref, kkb_ref, lwf_ref, kf_ref, bf_ref,
                 lwb_ref, kb_ref, bb_ref, s0_ref, yf_ref, yb_ref, sfin_ref, s_scr, *, n_chunks, n_groups, n_batch):
    i = pl.program_id(1)
    C = RW_CHUNK

    @pl.when(i == 0)
    def _():
        s_scr[...] = s0_ref[...]

    tri_f = tri_ref[0]
    tri_b = tri_ref[1]
    same_b = m_ref[_M_SAME].astype(jnp.bfloat16)

    def body(c, carry):
        rf = pl.ds(pl.multiple_of(c * C, C), C)
        rb = pl.ds(pl.multiple_of((n_chunks - 1 - c) * C, C), C)
        gens = []
        probs = [(e, g) for e in range(n_batch) for g in range(n_groups)]
        for e, g in probs:
            ln = slice(g * RW_LANES, (g + 1) * RW_LANES)
            gens.append(_rwkv_chunk(s_scr[e, 0, g], rf_ref[e, rf, ln], lwf_ref[e, rf, ln], kf_ref[e, rf, ln],
                                    vf_ref[e, rf, ln], kkf_ref[e, rf, ln], bf_ref[e, rf, ln],
                                    m_ref, w_ref, same_b, tri_f, 0))
            gens.append(_rwkv_chunk(s_scr[e, 1, g], rb_ref[e, rb, ln], lwb_ref[e, rb, ln], kb_ref[e, rb, ln],
                                    vb_ref[e, rb, ln], kkb_ref[e, rb, ln], bb_ref[e, rb, ln],
                                    m_ref, w_ref, same_b, tri_b, 1))
        out = _lockstep(gens)
        for q, (e, g) in enumerate(probs):
            ln = slice(g * RW_LANES, (g + 1) * RW_LANES)
            (s_f, y_f), (s_b, y_b) = out[2 * q], out[2 * q + 1]
            s_scr[e, 0, g] = s_f
            s_scr[e, 1, g] = s_b
            yf_ref[e, rf, ln] = y_f
            yb_ref[e, rb, ln] = y_b
        return carry

    lax.fori_loop(0, n_chunks, body, 0)

    @pl.when(i == pl.num_programs(1) - 1)
    def _():
        sfin_ref[...] = s_scr[...]


def rwkv_scan_pallas(rkv, kk, lw, kd, bvec, s0, mask_set, block_tokens):
    bsz, n, width = kk.shape
    masks, wide, tri = mask_set
    ng = width // RW_LANES
    tb = block_tokens
    nb = n // tb
    eb = RW_BATCH_PER_STEP
    assert bsz % eb == 0
    fwd = pl.BlockSpec((eb, tb, width), lambda b, i: (b, i, 0))
    bwd = pl.BlockSpec((eb, tb, width), lambda b, i: (b, nb - 1 - i, 0))
    r_fwd, r_bwd = fwd, bwd
    v_fwd = pl.BlockSpec((eb, tb, width), lambda b, i: (b, i, 2))
    v_bwd = pl.BlockSpec((eb, tb, width), lambda b, i: (b, nb - 1 - i, 2))
    state_spec = pl.BlockSpec((eb, 2, ng, RW_LANES, RW_LANES), lambda b, i: (b, 0, 0, 0, 0))
    return pl.pallas_call(
        functools.partial(_rwkv_kernel, n_chunks=tb // RW_CHUNK, n_groups=ng, n_batch=eb),
        out_shape=(jax.ShapeDtypeStruct((bsz, n, width), jnp.float32),
                   jax.ShapeDtypeStruct((bsz, n, width), jnp.float32),
                   jax.ShapeDtypeStruct(s0.shape, jnp.float32)),
        grid=(bsz // eb, nb),
        in_specs=[pl.BlockSpec(masks.shape, lambda b, i: (0, 0, 0)),
                  pl.BlockSpec(wide.shape, lambda b, i: (0, 0, 0)),
                  pl.BlockSpec(tri.shape, lambda b, i: (0, 0, 0)),
                  r_fwd, r_bwd, v_fwd, v_bwd, fwd, bwd,
                  fwd, fwd, fwd, bwd, bwd, bwd, state_spec],
        out_specs=(fwd, bwd, state_spec),
        scratch_shapes=[pltpu.VMEM((eb, 2, ng, RW_LANES, RW_LANES), jnp.float32)],
        compiler_params=pltpu.CompilerParams(dimension_semantics=("parallel", "arbitrary"),
                                             vmem_limit_bytes=VMEM_LIMIT),
        name="rwkv_scan",
    )(masks, wide, tri, rkv, rkv, rkv, rkv, kk, kk, lw[0], kd[0], bvec[0], lw[1], kd[1], bvec[1], s0)


def _head_sum(t, hm):
    f32, bf16 = jnp.float32, jnp.bfloat16
    hi = t.astype(bf16)
    rem = t - hi.astype(f32)
    mid = rem.astype(bf16)
    lo = (rem - mid.astype(f32)).astype(bf16)
    return (jnp.dot(hi, hm, preferred_element_type=f32) + jnp.dot(mid, hm, preferred_element_type=f32)
            + jnp.dot(lo, hm, preferred_element_type=f32))


def _rwkv_prep_kernel(rkv_ref, lora_ref, vec_ref, w0_ref, a0_ref, w2_ref, a2_ref, hsum_ref,
                      kk_ref, lw0_ref, kd0_ref, b0_ref, lw1_ref, kd1_ref, b1_ref, bonus_ref):
    f32, bf16 = jnp.float32, jnp.bfloat16
    w = RW_WIDTH
    r = rkv_ref[:, 0:w]
    k = rkv_ref[:, w:2 * w]
    v = rkv_ref[:, 2 * w:3 * w]
    hm = hsum_ref[...]
    kk = k * vec_ref[0:1, :]
    kk = kk * lax.rsqrt(jnp.maximum(_head_sum(kk * kk, hm), 1e-24))
    kk_ref[...] = kk
    w_lo = jnp.tanh(lora_ref[:, 0:128]).astype(bf16)
    a_lo = lora_ref[:, 128:256].astype(bf16)
    rrk = r * vec_ref[2:3, :]
    bonus = None
    outs = ((lw0_ref, kd0_ref, b0_ref), (lw1_ref, kd1_ref, b1_ref))
    for d in range(2):
        x = w0_ref[d:d + 1, :] + jnp.dot(w_lo, w2_ref[d], preferred_element_type=f32)
        w_log = -(jnp.maximum(-x, 0.0) + jnp.log(1.0 + jnp.exp(-jnp.abs(x)))) - 0.5
        a = jax.nn.sigmoid(a0_ref[d:d + 1, :] + jnp.dot(a_lo, a2_ref[d], preferred_element_type=f32))
        kd = k * (1.0 + (a - 1.0) * vec_ref[1:2, :])
        outs[d][0][...] = -jnp.exp(w_log)
        outs[d][1][...] = kd
        outs[d][2][...] = kk * a
        term = _head_sum(rrk * kd, hm)
        bonus = term if bonus is None else bonus + term
    bonus_ref[...] = bonus * v


def rwkv_prep(rkv, lora, p):
    bsz, n, _ = rkv.shape
    rows = bsz * n
    tm = min(MIX_BLOCK_ROWS, n)
    w = RW_WIDTH
    hid = np.arange(w) // RW_HEAD
    hsum = jnp.asarray(hid[:, None] == hid[None, :], jnp.bfloat16)
    vec = jnp.stack([p['rw_kk'], p['rw_ka'], p['rw_rk']])
    zero = jnp.zeros((RW_DECAY_LORA, w), jnp.float32)
    w2 = jnp.stack([jnp.concatenate([p['rw_w2'][0], zero]), jnp.concatenate([zero, p['rw_w2'][1]])])
    a2 = jnp.stack([jnp.concatenate([p['rw_a2'][0], zero]), jnp.concatenate([zero, p['rw_a2'][1]])])
    consts = [vec, p['rw_w0'], p['rw_a0'], w2.astype(jnp.bfloat16), a2.astype(jnp.bfloat16), hsum]

    def const(a):
        return pl.BlockSpec(a.shape, lambda i: (0,) * a.ndim)

    outs = pl.pallas_call(
        _rwkv_prep_kernel,
        out_shape=tuple(jax.ShapeDtypeStruct((rows, w), jnp.float32) for _ in range(8)),
        grid=(rows // tm,),
        in_specs=[pl.BlockSpec((tm, 3 * w), lambda i: (i, 0)), pl.BlockSpec((tm, lora.shape[-1]), lambda i: (i, 0))]
                 + [const(c) for c in consts],
        out_specs=tuple(pl.BlockSpec((tm, w), lambda i: (i, 0)) for _ in range(8)),
        compiler_params=pltpu.CompilerParams(dimension_semantics=("parallel",), vmem_limit_bytes=VMEM_LIMIT),
        name="rwkv_prep",
    )(rkv.reshape(rows, 3 * w), lora.reshape(rows, -1), *consts)
    kk, lw0, kd0, b0, lw1, kd1, b1, bonus = [o.reshape(bsz, n, w) for o in outs]
    return kk, (lw0, lw1), (kd0, kd1), (b0, b1), bonus


def rwkv_branch(rkv, lora, p, s0, mask_set, block_tokens):
    kk, lw, kd, bvec, bonus = rwkv_prep(rkv, lora, p)
    y_f, y_b, s_fin = rwkv_scan_pallas(rkv, kk, lw, kd, bvec, s0, mask_set, block_tokens)
    return (y_f, y_b, bonus, lora[..., IN_LORA - RW_GATE_LORA:]), s_fin


def hyena_filter_spectra(n_tok, p):
    hp = lax.Precision.HIGHEST
    bands = (HY_POS_DIM - 1) // 2
    t = jnp.linspace(0.0, 1.0, n_tok, dtype=jnp.float32)[:, None]
    w = (2.0 * math.pi / n_tok) * jnp.arange(n_tok, dtype=jnp.float32)[:, None]
    f = jnp.linspace(1e-4, bands - 1, bands, dtype=jnp.float32)[None, :]
    feats = jnp.concatenate([t, jnp.cos(f * w), -jnp.sin(f * w)], axis=-1)
    h = jnp.sin(p['hy_f_freq1'] * (jnp.dot(feats, p['hy_f_w1'], precision=hp) + p['hy_f_b1']))
    h = jnp.sin(p['hy_f_freq2'] * (jnp.dot(h, p['hy_f_w2'], precision=hp) + p['hy_f_b2']))
    h = jnp.dot(h, p['hy_f_w3'], precision=hp).reshape(n_tok, HY_ORDER, 2, HY_WIDTH)
    rates = jnp.abs(jnp.linspace(math.log(HY_DECAY_TARGET) / HY_DECAY_PCT_SHORT,
                                 math.log(HY_DECAY_TARGET) / HY_DECAY_PCT_LONG, HY_WIDTH, dtype=jnp.float32))
    h = h * jnp.exp(-t * rates)[:, None, None, :]
    h_fwd, h_bwd = h[:, :, 0], h[:, :, 1]
    filt = jnp.concatenate([h_fwd, jnp.zeros_like(h_fwd[:1]), h_bwd[:0:-1]], axis=0)
    return jnp.fft.rfft(filt, axis=0)


HY_N = 2 * SEQ
HY_N1 = 64
HY_N2 = 128
HY_NH = HY_N1 // 2
HY_PITCH = 136
HY_LANES = 128


def hyena_dft_tables():
    k1 = np.arange(HY_N1)[:, None]
    n1 = np.arange(HY_NH)[None, :]
    n2 = np.arange(HY_N2)
    ph = -2 * np.pi * (k1 * n1 / HY_N1)[None] - 2 * np.pi * (n2[:, None, None] * k1[None] / HY_N)
    lhs1 = np.concatenate([np.cos(ph), np.sin(ph)], axis=1)
    lhs2 = np.concatenate([np.cos(ph).transpose(0, 2, 1), np.sin(ph).transpose(0, 2, 1)], axis=2) / HY_N
    kk = np.arange(HY_N2)
    ang = -2 * np.pi * np.outer(kk, kk) / HY_N2
    cr, ci = np.cos(ang), np.sin(ang)
    f_fwd = np.block([[cr, -ci], [ci, cr]])
    f_inv = np.block([[cr, ci], [-ci, cr]])

    def as_bf(a):
        return jnp.asarray(a, jnp.float32).astype(jnp.bfloat16)

    return as_bf(lhs1), as_bf(lhs2), as_bf(f_fwd), as_bf(f_inv)


def hyena_spectrum_layout(k_spec):
    w = k_spec.shape[1]
    full = jnp.concatenate([k_spec, jnp.conj(k_spec[-2:0:-1])], axis=0)
    parts = jnp.stack([jnp.real(full), jnp.imag(full)], axis=0).reshape(2, HY_N2, HY_N1, w)
    parts = jnp.transpose(parts, (2, 0, 1, 3)).reshape(HY_N1 * 2 * HY_N2, w)
    return jnp.transpose(parts.reshape(-1, w // HY_LANES, HY_LANES), (1, 0, 2)).astype(jnp.float32)


def _hy_kernel(z_ref, g_ref, k_ref, l1_ref, l2_ref, ff_ref, fi_ref, bias_ref, o_ref, a_ref):
    f32, bf16 = jnp.float32, jnp.bfloat16
    n2n, nh, pitch, half = HY_N2, HY_NH, HY_PITCH, HY_N1

    def stage1(q, c):
        slab = z_ref[0, pl.ds(pl.multiple_of(q * nh, nh), nh), :].astype(bf16)
        a_ref[pl.ds(pl.multiple_of(q * pitch, 8), 2 * half), :] = jnp.dot(l1_ref[q], slab, preferred_element_type=f32)
        return c

    lax.fori_loop(0, n2n, stage1, 0, unroll=8)

    def stage2(k, c):
        a = jnp.concatenate([a_ref[pl.ds(k, n2n, stride=pitch), :],
                             a_ref[pl.ds(half + k, n2n, stride=pitch), :]], axis=0).astype(bf16)
        x = jnp.dot(ff_ref[...], a, preferred_element_type=f32)
        xr, xi = x[:n2n], x[n2n:]
        base = pl.multiple_of(k * 2 * n2n, 2 * n2n)
        kr = k_ref[0, pl.ds(base, n2n), :]
        ki = k_ref[0, pl.ds(base + n2n, n2n), :]
        y = jnp.concatenate([xr * kr - xi * ki, xr * ki + xi * kr], axis=0).astype(bf16)
        b = jnp.dot(fi_ref[...], y, preferred_element_type=f32)
        a_ref[pl.ds(k, n2n, stride=pitch), :] = b[:n2n]
        a_ref[pl.ds(half + k, n2n, stride=pitch), :] = b[n2n:]
        return c

    lax.fori_loop(0, half, stage2, 0, unroll=4)

    def stage3(q, c):
        blk = a_ref[pl.ds(pl.multiple_of(q * pitch, 8), 2 * half), :].astype(bf16)
        y = jnp.dot(l2_ref[q], blk, preferred_element_type=f32)
        rows = pl.ds(pl.multiple_of(q * nh, nh), nh)
        o_ref[0, rows, :] = g_ref[0, rows, :] * (y + z_ref[0, rows, :] * bias_ref[...])
        return c

    lax.fori_loop(0, n2n, stage3, 0, unroll=8)


def hyena_long_conv_gated(zt, gt, spec, bias, tables):
    bsz, n, w = zt.shape
    l1, l2, ff, fi = tables
    nt = w // HY_LANES
    tok = pl.BlockSpec((1, n, HY_LANES), lambda t, b: (b, 0, t))
    return pl.pallas_call(
        _hy_kernel,
        out_shape=jax.ShapeDtypeStruct((bsz, n, w), jnp.float32),
        grid=(nt, bsz),
        in_specs=[tok, tok,
                  pl.BlockSpec((1,) + spec.shape[1:], lambda t, b: (t, 0, 0)),
                  pl.BlockSpec(l1.shape, lambda t, b: (0, 0, 0)),
                  pl.BlockSpec(l2.shape, lambda t, b: (0, 0, 0)),
                  pl.BlockSpec(ff.shape, lambda t, b: (0, 0)),
                  pl.BlockSpec(fi.shape, lambda t, b: (0, 0)),
                  pl.BlockSpec((1, HY_LANES), lambda t, b: (0, t))],
        out_specs=tok,
        scratch_shapes=[pltpu.VMEM((HY_N2 * HY_PITCH, HY_LANES), jnp.float32)],
        compiler_params=pltpu.CompilerParams(dimension_semantics=("parallel", "parallel"),
                                             vmem_limit_bytes=VMEM_LIMIT),
        name="hyena_conv",
    )(zt, gt, spec, l1, l2, ff, fi, bias.reshape(1, w))


def hyena_branch_long(streams, p):
    bsz, n, w3 = streams.shape
    k_spec = hyena_filter_spectra(n, p)
    tables = hyena_dft_tables()
    st = jnp.transpose(streams.reshape(bsz, HY_NH, HY_N2, w3), (0, 2, 1, 3)).reshape(bsz, n, w3)
    z, x1, x2 = jnp.split(st, 3, axis=-1)
    for o, gate in enumerate((x1, x2)):
        z = hyena_long_conv_gated(z, gate, hyena_spectrum_layout(k_spec[:, o]), p['hy_bias'][o], tables)
    w = z.shape[-1]
    return jnp.transpose(z.reshape(bsz, HY_N2, HY_NH, w), (0, 2, 1, 3)).reshape(bsz, n, w)


def _hy_ctx_kernel(z_ref, g_ref, k_ref, fd_ref, fi_ref, bias_ref, o_ref):
    f32, bf16 = jnp.float32, jnp.bfloat16
    z = z_ref[0]
    n = fd_ref.shape[0] // 2
    x = jnp.dot(fd_ref[...], z.astype(bf16), preferred_element_type=f32)
    xr, xi = x[:n], x[n:]
    kr, ki = k_ref[0, :n, :], k_ref[0, n:, :]
    y = jnp.concatenate([xr * kr - xi * ki, xr * ki + xi * kr], axis=0).astype(bf16)
    conv = jnp.dot(fi_ref[...], y, preferred_element_type=f32)
    o_ref[0] = g_ref[0] * (conv + z * bias_ref[...])


def hyena_branch_ctx(streams, p):
    bsz, n, w3 = streams.shape
    w = w3 // 3
    nn = 2 * n
    k_spec = hyena_filter_spectra(n, p)
    full = jnp.concatenate([k_spec, jnp.conj(k_spec[-2:0:-1])], axis=0)
    spec = jnp.concatenate([jnp.real(full), jnp.imag(full)], axis=0)
    ang = 2 * np.pi * np.outer(np.arange(nn), np.arange(n)) / nn
    fd = jnp.asarray(np.concatenate([np.cos(ang), -np.sin(ang)], axis=0), jnp.float32).astype(jnp.bfloat16)
    fi = jnp.asarray(np.concatenate([np.cos(ang).T, -np.sin(ang).T], axis=1) / nn, jnp.float32).astype(jnp.bfloat16)
    nt = w // HY_LANES
    tok = pl.BlockSpec((1, n, HY_LANES), lambda t, b: (b, 0, t))
    z, x1, x2 = jnp.split(streams, 3, axis=-1)
    for o, gate in enumerate((x1, x2)):
        sp = jnp.transpose(spec[:, o].reshape(2 * nn, nt, HY_LANES), (1, 0, 2)).astype(jnp.float32)
        z = pl.pallas_call(
            _hy_ctx_kernel,
            out_shape=jax.ShapeDtypeStruct((bsz, n, w), jnp.float32),
            grid=(nt, bsz),
            in_specs=[tok, tok,
                      pl.BlockSpec((1, 2 * nn, HY_LANES), lambda t, b: (t, 0, 0)),
                      pl.BlockSpec(fd.shape, lambda t, b: (0, 0)),
                      pl.BlockSpec(fi.shape, lambda t, b: (0, 0)),
                      pl.BlockSpec((1, HY_LANES), lambda t, b: (0, t))],
            out_specs=tok,
            compiler_params=pltpu.CompilerParams(dimension_semantics=("parallel", "parallel"),
                                                 vmem_limit_bytes=VMEM_LIMIT),
            name="hyena_conv_ctx",
        )(z, gate, sp, fd, fi, p['hy_bias'][o].reshape(1, w))
    return z


def _rms(x, gain):
    return x * lax.rsqrt(jnp.mean(x * x, axis=-1, keepdims=True) + NORM_EPS) * gain


def _conv3(x, w_ref, period):
    rows = x.shape[0]
    pos = lax.broadcasted_iota(jnp.int32, (rows, 1), 0) % period
    prev = jnp.where(pos == 0, 0.0, pltpu.roll(x, 1, axis=0))
    nxt = jnp.where(pos == period - 1, 0.0, pltpu.roll(x, rows - 1, axis=0))
    return prev * w_ref[0:1, :] + x * w_ref[1:2, :] + nxt * w_ref[2:3, :]


def _inproj_kernel(x_ref, mod_ref, gain_ref, ws5_ref, wrw_ref, wlo_ref, why_ref, wgt_ref, cwr_ref, cwh_ref, cbh_ref,
                   os5_ref, orw_ref, olo_ref, ohy_ref, ogt_ref, *, period):
    f32 = jnp.float32
    shift, scale = mod_ref[0, 0:1, :], mod_ref[0, 1:2, :]
    h = (_rms(x_ref[...], gain_ref[...]) * (1.0 + scale) + shift).astype(jnp.bfloat16)
    os5_ref[...] = jnp.dot(h, ws5_ref[...], preferred_element_type=f32)
    olo_ref[...] = jnp.dot(h, wlo_ref[...], preferred_element_type=f32)
    ogt_ref[...] = jnp.dot(h, wgt_ref[...], preferred_element_type=f32)
    orw_ref[...] = _conv3(jnp.dot(h, wrw_ref[...], preferred_element_type=f32), cwr_ref, period)
    ohy_ref[...] = _conv3(jnp.dot(h, why_ref[...], preferred_element_type=f32), cwh_ref, period) + cbh_ref[...]


def input_projection(x, mod, gain, w_parts, rw_conv_w, hy_conv_w, hy_conv_b, period):
    bsz, n, d = x.shape
    tm = min(MIX_BLOCK_ROWS, n)
    per_b = n // tm
    rows = bsz * n
    bmap = (lambda i: (i // per_b, 0, 0)) if mod.shape[0] == bsz else (lambda i: (0, 0, 0))
    wb = [w.astype(jnp.bfloat16) for w in w_parts]

    def const(a):
        return pl.BlockSpec(a.shape, lambda i: (0,) * a.ndim)

    cbh = hy_conv_b.reshape(1, -1)
    outs = pl.pallas_call(
        functools.partial(_inproj_kernel, period=period),
        out_shape=tuple(jax.ShapeDtypeStruct((rows, w.shape[1]), jnp.float32) for w in wb),
        grid=(rows // tm,),
        in_specs=[pl.BlockSpec((tm, d), lambda i: (i, 0)),
                  pl.BlockSpec((1, 3, d), bmap),
                  pl.BlockSpec((1, d), lambda i: (0, 0))]
                 + [const(w) for w in wb] + [const(rw_conv_w), const(hy_conv_w), const(cbh)],
        out_specs=tuple(pl.BlockSpec((tm, w.shape[1]), lambda i: (i, 0)) for w in wb),
        compiler_params=pltpu.CompilerParams(dimension_semantics=("parallel",), vmem_limit_bytes=VMEM_LIMIT),
        name="input_projection",
    )(x.reshape(rows, d), mod, gain.reshape(1, d), *wb, rw_conv_w, hy_conv_w, cbh)
    return tuple(o.reshape(bsz, n, o.shape[-1]) for o in outs)


def _merge_kernel(x_ref, mod_ref, gain_ref, ys5_ref, us5_ref, wf_ref, wb_ref, bonus_ref, glo_ref, hy_ref, gt_ref,
                  s5d_ref, glu_ref, lnw_ref, lnb_ref, g2_ref, havg_ref, brs_ref, brr_ref, brh_ref, ow_ref, o_ref):
    f32, bf16 = jnp.float32, jnp.bfloat16
    y = ys5_ref[...] + s5d_ref[...] * us5_ref[...]
    y = 0.5 * y * (1.0 + lax.erf(y * (1.0 / math.sqrt(2.0))))
    s5 = y * jax.nn.sigmoid(jnp.dot(y.astype(bf16), glu_ref[...], preferred_element_type=f32))
    wkv = wf_ref[...] + wb_ref[...]

    def head_mean(t):
        hi = t.astype(bf16)
        rem = t - hi.astype(f32)
        mid = rem.astype(bf16)
        lo = (rem - mid.astype(f32)).astype(bf16)
        hm = havg_ref[...]
        tot = (jnp.dot(hi, hm, preferred_element_type=f32) + jnp.dot(mid, hm, preferred_element_type=f32)
               + jnp.dot(lo, hm, preferred_element_type=f32))
        return tot * (1.0 / RW_HEAD)

    mu = head_mean(wkv)
    cen = wkv - mu
    var = head_mean(cen * cen)
    o = cen * lax.rsqrt(var + RW_GN_EPS) * lnw_ref[...] + lnb_ref[...] + bonus_ref[...]
    g = jnp.dot(jax.nn.sigmoid(glo_ref[...]).astype(bf16), g2_ref[...], preferred_element_type=f32)
    rw = o * g
    gt = jax.nn.sigmoid(gt_ref[...])
    d = x_ref.shape[-1]
    m = (gt[:, 0:d] * jnp.dot(s5.astype(bf16), brs_ref[...], preferred_element_type=f32)
         + gt[:, d:2 * d] * jnp.dot(rw.astype(bf16), brr_ref[...], preferred_element_type=f32)
         + gt[:, 2 * d:3 * d] * jnp.dot(hy_ref[...].astype(bf16), brh_ref[...], preferred_element_type=f32))
    yl = jnp.dot(m.astype(bf16), ow_ref[...], preferred_element_type=f32)
    o_ref[...] = x_ref[...] + mod_ref[0, 2:3, :] * _rms(yl, gain_ref[...])


def merge_block(x, mod, gain, ys5, us5, rw_parts, hy, gates, p):
    bsz, n, d = x.shape
    tm = min(MIX_BLOCK_ROWS, n)
    per_b = n // tm
    rows = bsz * n
    bmap = (lambda i: (i // per_b, 0, 0)) if mod.shape[0] == bsz else (lambda i: (0, 0, 0))

    def bf(a):
        return a.astype(jnp.bfloat16)

    def row(a):
        return pl.BlockSpec((tm, a.shape[-1]), lambda i: (i, 0))

    def const(a):
        return pl.BlockSpec(a.shape, lambda i: (0,) * a.ndim)

    hid = np.arange(RW_WIDTH) // RW_HEAD
    havg = jnp.asarray(hid[:, None] == hid[None, :], jnp.bfloat16)
    acts = [a.reshape(rows, a.shape[-1]) for a in (ys5, us5) + tuple(rw_parts) + (hy, gates)]
    consts = [p['s5_d'].reshape(1, -1), bf(p['s5_glu_w']), p['rw_ln_w'].reshape(1, -1), p['rw_ln_b'].reshape(1, -1),
              bf(p['rw_g2']), havg, bf(p['br_s5']), bf(p['br_rw']), bf(p['br_hy']), bf(p['out_w'])]
    out = pl.pallas_call(
        _merge_kernel,
        out_shape=jax.ShapeDtypeStruct((rows, d), jnp.float32),
        grid=(rows // tm,),
        in_specs=[pl.BlockSpec((tm, d), lambda i: (i, 0)), pl.BlockSpec((1, 3, d), bmap),
                  pl.BlockSpec((1, d), lambda i: (0, 0))] + [row(a) for a in acts] + [const(c) for c in consts],
        out_specs=pl.BlockSpec((tm, d), lambda i: (i, 0)),
        compiler_params=pltpu.CompilerParams(dimension_semantics=("parallel",), vmem_limit_bytes=VMEM_LIMIT),
        name="merge_block",
    )(x.reshape(rows, d), mod, gain.reshape(1, d), *acts, *consts)
    return out.reshape(bsz, n, d)


def token_mixer(x, ctx, mod_l, mod_c, g_pre, g_post, p, with_ctx_out):
    bsz = x.shape[0]
    cuts = [IN_S5, IN_S5 + IN_RW, IN_S5 + IN_RW + IN_LORA, IN_S5 + IN_RW + IN_LORA + IN_HY]
    edges = [0] + cuts + [IN_COLS]
    w_parts = [p['in_w'][:, edges[i]:edges[i + 1]] for i in range(5)]
    conv = (p['rw_conv_w'], p['hy_conv_w'], p['hy_conv_b'])
    u_l, rkv_l, lora_l, hy_l, gate_l = input_projection(x, mod_l, g_pre, w_parts, *conv, GRID_W)
    u_c, rkv_c, lora_c, hy_c, gate_c = input_projection(ctx, mod_c, g_pre, w_parts, *conv, CTX_LEN)
    mask_set = rwkv_masks()
    zr = jnp.zeros((bsz, 2, RW_WIDTH // RW_LANES, RW_LANES, RW_LANES), jnp.float32)
    ys5_l, ys5_c = s5_branch(u_l, u_c, p)
    rw_c, rw_state = rwkv_branch(rkv_c, lora_c, p, zr, mask_set, CTX_LEN)
    rw_l, _ = rwkv_branch(rkv_l, lora_l, p, rw_state, mask_set, RW_BLOCK_TOKENS)
    hy_lat = hyena_branch_long(hy_l, p)
    x = merge_block(x, mod_l, g_post, ys5_l, u_l, rw_l, hy_lat, gate_l, p)
    if with_ctx_out:
        ctx = merge_block(ctx, mod_c, g_post, ys5_c, u_c, rw_c, hyena_branch_ctx(hy_c, p), gate_c, p)
    return x, ctx


def _ffn_kernel(x_ref, mod_ref, gain_ref, wg_ref, wu_ref, wd_ref, o_ref):
    x = x_ref[...]
    shift, scale, gate = mod_ref[0, 0:1, :], mod_ref[0, 1:2, :], mod_ref[0, 2:3, :]
    h = (_rms(x, gain_ref[0:1, :]) * (1.0 + scale) + shift).astype(jnp.bfloat16)
    g = jnp.dot(h, wg_ref[...], preferred_element_type=jnp.float32)
    u = jnp.dot(h, wu_ref[...], preferred_element_type=jnp.float32)
    a = (g * jax.nn.sigmoid(g) * u).astype(jnp.bfloat16)
    y = jnp.dot(a, wd_ref[...], preferred_element_type=jnp.float32)
    o_ref[...] = x + gate * _rms(y, gain_ref[1:2, :])


def ffn_block(x, mod, gains, wg, wu, wd):
    bsz, n, d = x.shape
    hid = wg.shape[1]
    tm = min(FFN_BLOCK_ROWS, n)
    per_b = n // tm
    xf = x.reshape(bsz * n, d)
    bmap = (lambda i: (i // per_b, 0, 0)) if mod.shape[0] == bsz else (lambda i: (0, 0, 0))
    out = pl.pallas_call(
        _ffn_kernel,
        out_shape=jax.ShapeDtypeStruct((bsz * n, d), jnp.float32),
        grid=(bsz * per_b,),
        in_specs=[pl.BlockSpec((tm, d), lambda i: (i, 0)),
                  pl.BlockSpec((1, 3, d), bmap),
                  pl.BlockSpec((2, d), lambda i: (0, 0)),
                  pl.BlockSpec((d, hid), lambda i: (0, 0)),
                  pl.BlockSpec((d, hid), lambda i: (0, 0)),
                  pl.BlockSpec((hid, d), lambda i: (0, 0))],
        out_specs=pl.BlockSpec((tm, d), lambda i: (i, 0)),
        compiler_params=pltpu.CompilerParams(dimension_semantics=("parallel",), vmem_limit_bytes=VMEM_LIMIT),
        name="ffn_block",
    )(xf, mod, gains, wg.astype(jnp.bfloat16), wu.astype(jnp.bfloat16), wd.astype(jnp.bfloat16))
    return out.reshape(bsz, n, d)


def _moe_kernel(be_ref, x_ref, gate_ref, wg_ref, wu_ref, wd_ref, o_ref):
    j = pl.program_id(1)
    xb = x_ref[...].astype(jnp.bfloat16)
    g = jnp.dot(xb, wg_ref[0], preferred_element_type=jnp.float32)
    u = jnp.dot(xb, wu_ref[0], preferred_element_type=jnp.float32)
    a = (g * jax.nn.sigmoid(g) * u).astype(jnp.bfloat16)
    y = jnp.dot(a, wd_ref[0], preferred_element_type=jnp.float32)

    @pl.when(j == 0)
    def _():
        o_ref[...] = y

    @pl.when(j > 0)
    def _():
        o_ref[...] += y

    @pl.when(j == pl.num_programs(1) - 1)
    def _():
        o_ref[...] *= gate_ref[...]


def moe_expert_blocks(xb, slot_gate, block_e, wg, wu, wd):
    cap, d = xb.shape
    n_blocks = cap // MOE_BLOCK
    hid = wg.shape[-1]
    nh = MOE_HIDDEN_TILES
    th = hid // nh

    def hidx(i, j):
        return j + (i % 2) * (nh - 1 - 2 * j)

    return pl.pallas_call(
        _moe_kernel,
        out_shape=jax.ShapeDtypeStruct((cap, d), jnp.float32),
        grid_spec=pltpu.PrefetchScalarGridSpec(
            num_scalar_prefetch=1,
            grid=(n_blocks, nh),
            in_specs=[pl.BlockSpec((MOE_BLOCK, d), lambda i, j, be: (i, 0)),
                      pl.BlockSpec((MOE_BLOCK, 1), lambda i, j, be: (i, 0)),
                      pl.BlockSpec((1, d, th), lambda i, j, be: (be[i], 0, hidx(i, j))),
                      pl.BlockSpec((1, d, th), lambda i, j, be: (be[i], 0, hidx(i, j))),
                      pl.BlockSpec((1, th, d), lambda i, j, be: (be[i], hidx(i, j), 0))],
            out_specs=pl.BlockSpec((MOE_BLOCK, d), lambda i, j, be: (i, 0))),
        compiler_params=pltpu.CompilerParams(dimension_semantics=("parallel", "arbitrary"),
                                             vmem_limit_bytes=VMEM_LIMIT),
        name="moe_experts",
    )(block_e, xb, slot_gate.reshape(cap, 1), wg.astype(jnp.bfloat16), wu.astype(jnp.bfloat16),
      wd.astype(jnp.bfloat16))


def moe_swiglu(h, router_w, wg, wu, wd):
    d_model = h.shape[-1]
    tok = h.reshape(-1, d_model)
    n = tok.shape[0]
    n_assign = n * TOP_K
    logits = jnp.dot(tok, router_w)
    top_logit, top_e = lax.top_k(logits, TOP_K)
    gate = jax.nn.softmax(top_logit, axis=-1).reshape(-1)
    flat_e = top_e.reshape(-1)
    order = jnp.argsort(flat_e)
    onehot = (flat_e[:, None] == jnp.arange(N_EXPERTS, dtype=flat_e.dtype)[None, :]).astype(jnp.int32)
    sizes = jnp.sum(onehot, axis=0)
    padded = (sizes + MOE_BLOCK - 1) // MOE_BLOCK * MOE_BLOCK
    pad_end = jnp.cumsum(padded)
    pad_start = pad_end - padded
    grp_start = jnp.cumsum(sizes) - sizes
    n_blocks = -(-n_assign // MOE_BLOCK) + N_EXPERTS
    cap = n_blocks * MOE_BLOCK
    block_start = jnp.arange(n_blocks, dtype=jnp.int32) * MOE_BLOCK
    block_e = jnp.minimum(jnp.sum(block_start[:, None] >= pad_end[None, :], axis=1), N_EXPERTS - 1)
    slot_e = jnp.repeat(block_e, MOE_BLOCK)
    within = jnp.arange(cap, dtype=jnp.int32) - pad_start[slot_e]
    valid = within < sizes[slot_e]
    assign = order[jnp.clip(grp_start[slot_e] + within, 0, n_assign - 1)]
    slot_tok = jnp.where(valid, (assign // TOP_K).astype(jnp.int32), n)
    slot_gate = jnp.where(valid, gate[assign], 0.0)
    tok_pad = jnp.concatenate([tok, jnp.zeros((1, d_model), tok.dtype)], axis=0)
    xb = tok_pad[slot_tok]
    yb = moe_expert_blocks(xb, slot_gate, block_e.astype(jnp.int32), wg, wu, wd)
    rank = jnp.take_along_axis(jnp.cumsum(onehot, axis=0), flat_e[:, None], axis=1)[:, 0] - 1
    pos = (pad_start[flat_e] + rank).astype(jnp.int32).reshape(n, TOP_K)
    out = yb[pos[:, 0]]
    for kk in range(1, TOP_K):
        out = out + yb[pos[:, kk]]
    return out.reshape(h.shape)


def kernel(x, c, ctx, c_ctx, mod_w, mod_b, norm_g, in_w, s5_lam_re, s5_lam_im, s5_log_step, s5_b_re, s5_b_im, s5_c_re, s5_c_im, s5_d, s5_glu_w, rw_conv_w, rw_w0, rw_w2, rw_a0, rw_a2, rw_g2, rw_kk, rw_ka, rw_rk, rw_ln_w, rw_ln_b, hy_conv_w, hy_conv_b, hy_f_w1, hy_f_b1, hy_f_freq1, hy_f_w2, hy_f_b2, hy_f_freq2, hy_f_w3, hy_bias, br_s5, br_rw, br_hy, out_w, ffn_wg, ffn_wu, ffn_wd, moe_router, moe_wg, moe_wu, moe_wd):
    silu_c = jax.nn.silu(c)
    silu_cc = jax.nn.silu(c_ctx)
    for i in range(DEPTH):
        last = i == DEPTH - 1
        p = {
            'in_w': in_w[i],
            's5_lam_re': s5_lam_re[i], 's5_lam_im': s5_lam_im[i], 's5_log_step': s5_log_step[i],
            's5_b_re': s5_b_re[i], 's5_b_im': s5_b_im[i], 's5_c_re': s5_c_re[i], 's5_c_im': s5_c_im[i],
            's5_d': s5_d[i], 's5_glu_w': s5_glu_w[i],
            'rw_conv_w': rw_conv_w[i], 'rw_w0': rw_w0[i], 'rw_w2': rw_w2[i], 'rw_a0': rw_a0[i], 'rw_a2': rw_a2[i],
            'rw_g2': rw_g2[i], 'rw_kk': rw_kk[i], 'rw_ka': rw_ka[i], 'rw_rk': rw_rk[i],
            'rw_ln_w': rw_ln_w[i], 'rw_ln_b': rw_ln_b[i],
            'hy_conv_w': hy_conv_w[i], 'hy_conv_b': hy_conv_b[i], 'hy_f_w1': hy_f_w1[i], 'hy_f_b1': hy_f_b1[i],
            'hy_f_freq1': hy_f_freq1[i], 'hy_f_w2': hy_f_w2[i], 'hy_f_b2': hy_f_b2[i], 'hy_f_freq2': hy_f_freq2[i],
            'hy_f_w3': hy_f_w3[i], 'hy_bias': hy_bias[i],
            'br_s5': br_s5[i], 'br_rw': br_rw[i], 'br_hy': br_hy[i], 'out_w': out_w[i],
        }
        ml = jnp.split((silu_c @ mod_w[i] + mod_b[i])[:, None, :], 6, axis=-1)
        mc = jnp.split(silu_cc @ mod_w[i] + mod_b[i], 6, axis=-1)
        g_pre_m, g_post_m, g_pre_f, g_post_f = norm_g[i]
        if i % 2 == 0:
            def channel_mix(t, m, j=i // 2):
                return ffn_block(t, m, norm_g[i, 2:4], ffn_wg[j], ffn_wu[j], ffn_wd[j])
        else:
            def channel_mix(t, m, j=i // 2):
                h = rms_norm(t, g_pre_f) * (1 + m[:, 1:2]) + m[:, 0:1]
                y = moe_swiglu(h, moe_router[j], moe_wg[j], moe_wu[j], moe_wd[j])
                return t + m[:, 2:3] * rms_norm(y, g_post_f)
        x, ctx = token_mixer(x, ctx, jnp.concatenate(ml[0:3], axis=1), jnp.stack(mc[0:3])[None],
                             g_pre_m, g_post_m, p, not last)
        x = channel_mix(x, jnp.concatenate(ml[3:6], axis=1))
        if not last:
            ctx = channel_mix(ctx, jnp.stack(mc[3:6])[None])
    return x
```

```python
import math
import functools
import numpy as np
import jax
import jax.numpy as jnp
from jax import lax
from jax.experimental import pallas as pl
from jax.experimental.pallas import tpu as pltpu

D_MODEL = 1024
BATCH = 8
SEQ = 4096
DEPTH = 2
GRID_W = 64
CTX_LEN = 256
NORM_EPS = 1e-6
S5_WIDTH = D_MODEL // 4
S5_GROUP = 16
S5_GROUPS = S5_WIDTH // S5_GROUP
S5_STATE = 64
RW_WIDTH = D_MODEL // 2
RW_HEAD = 64
RW_HEADS = RW_WIDTH // RW_HEAD
RW_DECAY_LORA = 64
RW_ICLR_LORA = 64
RW_GATE_LORA = 128
RW_GN_EPS = 64e-5
HY_WIDTH = D_MODEL // 4
HY_ORDER = 2
HY_POS_DIM = 33
HY_FILTER_HIDDEN = 64
HY_DECAY_TARGET = 1e-2
HY_DECAY_PCT_SHORT = 0.3
HY_DECAY_PCT_LONG = 1.5
SHORT_CONV = 3
N_BRANCH = 3
FFN_HIDDEN = 2816
N_EXPERTS = 8
TOP_K = 2
EXPERT_HIDDEN = 3584
MOE_BLOCK = 512
IN_S5 = S5_WIDTH
IN_RW = 3 * RW_WIDTH
IN_LORA = 2 * RW_DECAY_LORA + 2 * RW_ICLR_LORA + RW_GATE_LORA
IN_HY = (HY_ORDER + 1) * HY_WIDTH
IN_GATE = N_BRANCH * D_MODEL
IN_COLS = IN_S5 + IN_RW + IN_LORA + IN_HY + IN_GATE

VMEM_LIMIT = 56 * 1024 * 1024
S5_CHUNK = 64
RW_CHUNK = 64
RW_HG = 4
RW_LANES = RW_HG * RW_HEAD
RW_BLOCK_TOKENS = 256
RW_BATCH_PER_STEP = 2
MOE_HIDDEN_TILES = 2
FFN_BLOCK_ROWS = 256
MIX_BLOCK_ROWS = 256


def rms_norm(x, gain):
    y = x * lax.rsqrt(jnp.mean(x * x, axis=-1, keepdims=True) + NORM_EPS)
    return y * gain


def s5_chunk_operators(lam_re, lam_im, log_step, b_re, b_im, c_re, c_im):
    T, G, P, GC = S5_CHUNK, S5_GROUPS, S5_STATE, S5_GROUP
    hp = lax.Precision.HIGHEST
    step = jnp.exp(log_step)[..., None]
    lr, li = lam_re, lam_im
    tau = jnp.arange(T + 1, dtype=jnp.float32)[:, None, None, None]
    mag = jnp.exp(lr * step * tau)
    ang = li * step * tau
    e_re, e_im = mag * jnp.cos(ang), mag * jnp.sin(ang)
    ab_re, ab_im = e_re[1], e_im[1]
    den = lr * lr + li * li
    nr, ni = ab_re - 1.0, ab_im
    q_re = (nr * lr + ni * li) / den
    q_im = (ni * lr - nr * li) / den
    bb_re = q_re[..., None] * b_re - q_im[..., None] * b_im
    bb_im = q_re[..., None] * b_im + q_im[..., None] * b_re
    eb_re = e_re[..., None] * bb_re - e_im[..., None] * bb_im
    eb_im = e_re[..., None] * bb_im + e_im[..., None] * bb_re
    ktau = (jnp.einsum('dgip,tdgpj->tdgij', c_re, eb_re[:T], precision=hp)
            - jnp.einsum('dgip,tdgpj->tdgij', c_im, eb_im[:T], precision=hp))
    t_idx = jnp.arange(T)
    k_f = jnp.transpose(ktau[:, 0], (1, 3, 0, 2))
    k_b = jnp.transpose(ktau[:, 1], (1, 3, 0, 2))
    table = jnp.concatenate([k_b[:, :, :0:-1], k_f[:, :, :1] + k_b[:, :, :1], k_f[:, :, 1:]], axis=2)
    table = table.astype(jnp.bfloat16).reshape(G, GC, (2 * T - 1) * GC)
    kt = jnp.stack([table[:, :, (T - 1 - s) * GC:(2 * T - 1 - s) * GC] for s in range(T)], axis=1)
    kt = kt.reshape(G, T * GC, T * GC)
    wf_re, wf_im = eb_re[T - 1 - t_idx, 0], eb_im[T - 1 - t_idx, 0]
    wb_re, wb_im = eb_re[t_idx, 1], eb_im[t_idx, 1]
    win = jnp.stack([wf_re, wf_im, wb_re, wb_im], axis=0)
    win = jnp.transpose(win, (2, 1, 4, 0, 3)).reshape(G, T * GC, 4 * P)
    ef_re, ef_im = e_re[t_idx + 1, 0], e_im[t_idx + 1, 0]
    eb2_re, eb2_im = e_re[T - t_idx, 1], e_im[T - t_idx, 1]

    def readout(cr, ci, er, ei):
        re = cr[None] * er[:, :, None, :] - ci[None] * ei[:, :, None, :]
        im = -(cr[None] * ei[:, :, None, :] + ci[None] * er[:, :, None, :])
        return re, im

    of_re, of_im = readout(c_re[0], c_im[0], ef_re, ef_im)
    ob_re, ob_im = readout(c_re[1], c_im[1], eb2_re, eb2_im)
    wout = jnp.stack([of_re, of_im, ob_re, ob_im], axis=0)
    wout = jnp.transpose(wout, (2, 0, 4, 1, 3)).reshape(G, 4 * P, T * GC)
    at_re, at_im = e_re[T], e_im[T]
    apow = jnp.concatenate([at_re[0], at_re[0], at_re[1], at_re[1]], axis=-1)[:, None, :]
    aimg = jnp.concatenate([-at_im[0], at_im[0], -at_im[1], at_im[1]], axis=-1)[:, None, :]
    return kt.astype(jnp.bfloat16), win.astype(jnp.bfloat16), wout.astype(jnp.bfloat16), apow, aimg


def _s5_kernel(x_ref, kt_ref, win_ref, wout_ref, apow_ref, aimg_ref, h0_ref, y_ref, hfin_ref, hin_ref, *,
               n_chunks, bsz):
    P2 = 2 * S5_STATE
    xb = x_ref[0].astype(jnp.bfloat16)
    hin_ref[...] = jnp.dot(xb, win_ref[0], preferred_element_type=jnp.float32)
    ap = apow_ref[0]
    ai = aimg_ref[0]
    apf, aif = ap[:, :P2], ai[:, :P2]
    apb, aib = ap[:, P2:], ai[:, P2:]

    def cmul(h, a_p, a_i):
        return h * a_p + pltpu.roll(h, S5_STATE, axis=1) * a_i

    def body(c, carry):
        hf, hb = carry
        rf = pl.ds(pl.multiple_of(c * bsz, bsz), bsz)
        rb = pl.ds(pl.multiple_of((n_chunks - 1 - c) * bsz, bsz), bsz)
        df = hin_ref[rf, :P2]
        db = hin_ref[rb, P2:]
        hin_ref[rf, :P2] = hf
        hin_ref[rb, P2:] = hb
        return cmul(hf, apf, aif) + df, cmul(hb, apb, aib) + db

    h0 = h0_ref[0]
    hf, hb = lax.fori_loop(0, n_chunks, body, (h0[:, :P2], h0[:, P2:]))
    hfin_ref[0, :, :P2] = hf
    hfin_ref[0, :, P2:] = hb
    y = jnp.dot(xb, kt_ref[0], preferred_element_type=jnp.float32)
    y = y + jnp.dot(hin_ref[...].astype(jnp.bfloat16), wout_ref[0], preferred_element_type=jnp.float32)
    y_ref[0] = y


def s5_scan_pallas(u, ops, h0):
    kt, win, wout, apow, aimg = ops
    bsz, n, _ = u.shape
    T, G, P, GC = S5_CHUNK, S5_GROUPS, S5_STATE, S5_GROUP
    nc = n // T
    rows = nc * bsz
    x = jnp.transpose(u.reshape(bsz, nc, T, G, GC), (3, 1, 0, 2, 4)).reshape(G, rows, T * GC)
    y, hfin = pl.pallas_call(
        functools.partial(_s5_kernel, n_chunks=nc, bsz=bsz),
        out_shape=(jax.ShapeDtypeStruct((G, rows, T * GC), jnp.float32),
                   jax.ShapeDtypeStruct((G, bsz, 4 * P), jnp.float32)),
        grid=(G,),
        in_specs=[pl.BlockSpec((1, rows, T * GC), lambda g: (g, 0, 0)),
                  pl.BlockSpec((1, T * GC, T * GC), lambda g: (g, 0, 0)),
                  pl.BlockSpec((1, T * GC, 4 * P), lambda g: (g, 0, 0)),
                  pl.BlockSpec((1, 4 * P, T * GC), lambda g: (g, 0, 0)),
                  pl.BlockSpec((1, 1, 4 * P), lambda g: (g, 0, 0)),
                  pl.BlockSpec((1, 1, 4 * P), lambda g: (g, 0, 0)),
                  pl.BlockSpec((1, bsz, 4 * P), lambda g: (g, 0, 0))],
        out_specs=(pl.BlockSpec((1, rows, T * GC), lambda g: (g, 0, 0)),
                   pl.BlockSpec((1, bsz, 4 * P), lambda g: (g, 0, 0))),
        scratch_shapes=[pltpu.VMEM((rows, 4 * P), jnp.float32)],
        compiler_params=pltpu.CompilerParams(dimension_semantics=("parallel",),
                                             vmem_limit_bytes=VMEM_LIMIT),
        name="s5_scan",
    )(x, kt, win, wout, apow, aimg, h0)
    y = jnp.transpose(y.reshape(G, nc, bsz, T, GC), (2, 1, 3, 0, 4)).reshape(bsz, n, G * GC)
    return y, hfin


def s5_branch(u, u_ctx, p):
    ops = s5_chunk_operators(p['s5_lam_re'], p['s5_lam_im'], p['s5_log_step'], p['s5_b_re'], p['s5_b_im'],
                             p['s5_c_re'], p['s5_c_im'])
    h0 = jnp.zeros((S5_GROUPS, u.shape[0], 4 * S5_STATE), jnp.float32)
    y_ctx, h_ctx = s5_scan_pallas(u_ctx, ops, h0)
    y_lat, _ = s5_scan_pallas(u, ops, h_ctx)
    return y_lat, y_ctx


_NT = (((1,), (1,)), ((), ()))

_M_SAME, _M_EYE = 0, 1
_W_STRICT, _W_INCL, _W_EYE, _W_LEVEL0 = 0, 2, 4, 5
_N_LEVELS = 6


def rwkv_masks():
    C, n = RW_CHUNK, RW_LANES
    row = np.arange(n)[:, None]
    col = np.arange(n)[None, :]
    square = np.stack([(row // C) == (col // C), row == col])
    tt, jj = np.arange(C)[:, None], np.arange(C)[None, :]
    wide = [jj < tt, jj > tt, jj <= tt, jj >= tt, jj == tt]
    for s in (1, 2, 4, 8, 16, 32):
        fwd = ((tt // s) % 2 == 1) & ((jj // s) == (tt // s) - 1)
        wide += [fwd, fwd.T]
    wide = np.stack([np.tile(m, (1, RW_HG)) for m in wide])
    tri = np.stack([jj <= tt, jj >= tt])
    return (jnp.asarray(square, jnp.float32), jnp.asarray(wide, jnp.float32), jnp.asarray(tri, jnp.bfloat16))


def _bdot(a, b, dims=None):
    a = a.astype(jnp.bfloat16)
    b = b.astype(jnp.bfloat16)
    if dims is None:
        return jnp.dot(a, b, preferred_element_type=jnp.float32)
    return lax.dot_general(a, b, dims, preferred_element_type=jnp.float32)


def _rwkv_chunk(S, r, lw, k, v, kk, b, m_ref, w_ref, same_b, tri, d):
    C = RW_CHUNK
    same = m_ref[_M_SAME]
    lw_hi = lw.astype(jnp.bfloat16)
    rem = lw - lw_hi.astype(jnp.float32)
    lw_mid = rem.astype(jnp.bfloat16)
    lw_lo = (rem - lw_mid.astype(jnp.float32)).astype(jnp.bfloat16)
    cl = (jnp.dot(tri, lw_hi, preferred_element_type=jnp.float32)
          + jnp.dot(tri, lw_mid, preferred_element_type=jnp.float32)
          + jnp.dot(tri, lw_lo, preferred_element_type=jnp.float32))
    yield
    tot = jnp.sum(lw, axis=0, keepdims=True)
    e_neg = jnp.exp(-cl)
    e_end = jnp.exp(tot - cl)
    a_t = -kk * jnp.exp(cl - lw)
    r_t = r * jnp.exp(cl)
    b_t = b * e_neg
    k_t = k * e_neg
    b_h = b * e_end
    k_h = k * e_end
    p_c = jnp.exp(tot)

    def tile(x):
        return jnp.concatenate([x] * RW_HG, axis=0)

    def stack(x):
        return tile(x.astype(jnp.bfloat16)) * same_b

    n = RW_LANES
    sa = stack(a_t)
    wide = _bdot(jnp.concatenate([a_t, r_t], axis=0), jnp.concatenate([stack(b_t), stack(k_t)], axis=0), _NT)
    yield
    a_ab = wide[:C, :n] * w_ref[_W_STRICT + d]
    a_ak = wide[:C, n:] * w_ref[_W_STRICT + d]
    a_rb = wide[C:, :n] * w_ref[_W_INCL + d]
    a_rk = wide[C:, n:] * w_ref[_W_INCL + d]
    dinv = w_ref[_W_EYE] + a_ab * w_ref[_W_LEVEL0 + d]
    for l in range(1, _N_LEVELS):
        inner = _bdot(a_ab * w_ref[_W_LEVEL0 + 2 * l + d], stack(dinv))
        yield
        dinv = dinv + _bdot(dinv, stack(inner))
        yield
    av = _bdot(jnp.concatenate([a_ak, a_rk], axis=0), stack(v))
    yield
    akv = av[:C]
    arkv = av[C:]
    mu = _bdot(dinv, jnp.concatenate([sa, stack(akv)], axis=1))
    yield
    m1 = mu[:, :n]
    u0 = mu[:, n:]
    my = _bdot(a_rb, jnp.concatenate([stack(m1), stack(u0)], axis=1))
    yield
    m2 = r_t + my[:, :n]
    y0 = my[:, n:] + arkv
    mut = jnp.concatenate([m1, u0], axis=1).T
    gh = _bdot(mut, b_h)
    yield
    g = m_ref[_M_EYE] * p_c + gh[:n] * same
    hmat = (gh[n:] + _bdot(v.T, k_h)) * same
    y = _bdot(m2, S, _NT) + y0
    s_new = _bdot(S, g) + hmat
    return s_new, y


def _lockstep(gens):
    results = [None] * len(gens)
    active = list(range(len(gens)))
    while active:
        for i in list(active):
            try:
                next(gens[i])
            except StopIteration as stop:
                results[i] = stop.value
                active.remove(i)
    return results


def _rwkv_kernel(m_ref, w_ref, tri_ref, rf_ref, rb_ref, vf_ref, vb_ref, kkf_ref, kkb_ref, lwf_ref, kf_ref, bf_ref,
                 lwb_ref, kb_ref, bb_ref, s0_ref, yf_ref, yb_ref, sfin_ref, s_scr, *, n_chunks, n_groups, n_batch):
    i = pl.program_id(1)
    C = RW_CHUNK

    @pl.when(i == 0)
    def _():
        s_scr[...] = s0_ref[...]

    tri_f = tri_ref[0]
    tri_b = tri_ref[1]
    same_b = m_ref[_M_SAME].astype(jnp.bfloat16)

    def body(c, carry):
        rf = pl.ds(pl.multiple_of(c * C, C), C)
        rb = pl.ds(pl.multiple_of((n_chunks - 1 - c) * C, C), C)
        gens = []
        probs = [(e, g) for e in range(n_batch) for g in range(n_groups)]
        for e, g in probs:
            ln = slice(g * RW_LANES, (g + 1) * RW_LANES)
            gens.append(_rwkv_chunk(s_scr[e, 0, g], rf_ref[e, rf, ln], lwf_ref[e, rf, ln], kf_ref[e, rf, ln],
                                    vf_ref[e, rf, ln], kkf_ref[e, rf, ln], bf_ref[e, rf, ln],
                                    m_ref, w_ref, same_b, tri_f, 0))
            gens.append(_rwkv_chunk(s_scr[e, 1, g], rb_ref[e, rb, ln], lwb_ref[e, rb, ln], kb_ref[e, rb, ln],
                                    vb_ref[e, rb, ln], kkb_ref[e, rb, ln], bb_ref[e, rb, ln],
                                    m_ref, w_ref, same_b, tri_b, 1))
        out = _lockstep(gens)
        for q, (e, g) in enumerate(probs):
            ln = slice(g * RW_LANES, (g + 1) * RW_LANES)
            (s_f, y_f), (s_b, y_b) = out[2 * q], out[2 * q + 1]
            s_scr[e, 0, g] = s_f
            s_scr[e, 1, g] = s_b
            yf_ref[e, rf, ln] = y_f
            yb_ref[e, rb, ln] = y_b
        return carry

    lax.fori_loop(0, n_chunks, body, 0)

    @pl.when(i == pl.num_programs(1) - 1)
    def _():
        sfin_ref[...] = s_scr[...]


def rwkv_scan_pallas(rkv, kk, lw, kd, bvec, s0, mask_set, block_tokens):
    bsz, n, width = kk.shape
    masks, wide, tri = mask_set
    ng = width // RW_LANES
    tb = block_tokens
    nb = n // tb
    eb = RW_BATCH_PER_STEP
    assert bsz % eb == 0
    fwd = pl.BlockSpec((eb, tb, width), lambda b, i: (b, i, 0))
    bwd = pl.BlockSpec((eb, tb, width), lambda b, i: (b, nb - 1 - i, 0))
    r_fwd, r_bwd = fwd, bwd
    v_fwd = pl.BlockSpec((eb, tb, width), lambda b, i: (b, i, 2))
    v_bwd = pl.BlockSpec((eb, tb, width), lambda b, i: (b, nb - 1 - i, 2))
    state_spec = pl.BlockSpec((eb, 2, ng, RW_LANES, RW_LANES), lambda b, i: (b, 0, 0, 0, 0))
    return pl.pallas_call(
        functools.partial(_rwkv_kernel, n_chunks=tb // RW_CHUNK, n_groups=ng, n_batch=eb),
        out_shape=(jax.ShapeDtypeStruct((bsz, n, width), jnp.float32),
                   jax.ShapeDtypeStruct((bsz, n, width), jnp.float32),
                   jax.ShapeDtypeStruct(s0.shape, jnp.float32)),
        grid=(bsz // eb, nb),
        in_specs=[pl.BlockSpec(masks.shape, lambda b, i: (0, 0, 0)),
                  pl.BlockSpec(wide.shape, lambda b, i: (0, 0, 0)),
                  pl.BlockSpec(tri.shape, lambda b, i: (0, 0, 0)),
                  r_fwd, r_bwd, v_fwd, v_bwd, fwd, bwd,
                  fwd, fwd, fwd, bwd, bwd, bwd, state_spec],
        out_specs=(fwd, bwd, state_spec),
        scratch_shapes=[pltpu.VMEM((eb, 2, ng, RW_LANES, RW_LANES), jnp.float32)],
        compiler_params=pltpu.CompilerParams(dimension_semantics=("parallel", "arbitrary"),
                                             vmem_limit_bytes=VMEM_LIMIT),
        name="rwkv_scan",
    )(masks, wide, tri, rkv, rkv, rkv, rkv, kk, kk, lw[0], kd[0], bvec[0], lw[1], kd[1], bvec[1], s0)


def _head_sum(t, hm):
    f32, bf16 = jnp.float32, jnp.bfloat16
    hi = t.astype(bf16)
    rem = t - hi.astype(f32)
    mid = rem.astype(bf16)
    lo = (rem - mid.astype(f32)).astype(bf16)
    return (jnp.dot(hi, hm, preferred_element_type=f32) + jnp.dot(mid, hm, preferred_element_type=f32)
            + jnp.dot(lo, hm, preferred_element_type=f32))


def _rwkv_prep_kernel(rkv_ref, lora_ref, vec_ref, w0_ref, a0_ref, w2_ref, a2_ref, hsum_ref,
                      kk_ref, lw0_ref, kd0_ref, b0_ref, lw1_ref, kd1_ref, b1_ref, bonus_ref):
    f32, bf16 = jnp.float32, jnp.bfloat16
    w = RW_WIDTH
    r = rkv_ref[:, 0:w]
    k = rkv_ref[:, w:2 * w]
    v = rkv_ref[:, 2 * w:3 * w]
    hm = hsum_ref[...]
    kk = k * vec_ref[0:1, :]
    kk = kk * lax.rsqrt(jnp.maximum(_head_sum(kk * kk, hm), 1e-24))
    kk_ref[...] = kk
    w_lo = jnp.tanh(lora_ref[:, 0:128]).astype(bf16)
    a_lo = lora_ref[:, 128:256].astype(bf16)
    rrk = r * vec_ref[2:3, :]
    bonus = None
    outs = ((lw0_ref, kd0_ref, b0_ref), (lw1_ref, kd1_ref, b1_ref))
    for d in range(2):
        x = w0_ref[d:d + 1, :] + jnp.dot(w_lo, w2_ref[d], preferred_element_type=f32)
        w_log = -(jnp.maximum(-x, 0.0) + jnp.log(1.0 + jnp.exp(-jnp.abs(x)))) - 0.5
        a = jax.nn.sigmoid(a0_ref[d:d + 1, :] + jnp.dot(a_lo, a2_ref[d], preferred_element_type=f32))
        kd = k * (1.0 + (a - 1.0) * vec_ref[1:2, :])
        outs[d][0][...] = -jnp.exp(w_log)
        outs[d][1][...] = kd
        outs[d][2][...] = kk * a
        term = _head_sum(rrk * kd, hm)
        bonus = term if bonus is None else bonus + term
    bonus_ref[...] = bonus * v


def rwkv_prep(rkv, lora, p):
    bsz, n, _ = rkv.shape
    rows = bsz * n
    tm = min(MIX_BLOCK_ROWS, n)
    w = RW_WIDTH
    hid = np.arange(w) // RW_HEAD
    hsum = jnp.asarray(hid[:, None] == hid[None, :], jnp.bfloat16)
    vec = jnp.stack([p['rw_kk'], p['rw_ka'], p['rw_rk']])
    zero = jnp.zeros((RW_DECAY_LORA, w), jnp.float32)
    w2 = jnp.stack([jnp.concatenate([p['rw_w2'][0], zero]), jnp.concatenate([zero, p['rw_w2'][1]])])
    a2 = jnp.stack([jnp.concatenate([p['rw_a2'][0], zero]), jnp.concatenate([zero, p['rw_a2'][1]])])
    consts = [vec, p['rw_w0'], p['rw_a0'], w2.astype(jnp.bfloat16), a2.astype(jnp.bfloat16), hsum]

    def const(a):
        return pl.BlockSpec(a.shape, lambda i: (0,) * a.ndim)

    outs = pl.pallas_call(
        _rwkv_prep_kernel,
        out_shape=tuple(jax.ShapeDtypeStruct((rows, w), jnp.float32) for _ in range(8)),
        grid=(rows // tm,),
        in_specs=[pl.BlockSpec((tm, 3 * w), lambda i: (i, 0)), pl.BlockSpec((tm, lora.shape[-1]), lambda i: (i, 0))]
                 + [const(c) for c in consts],
        out_specs=tuple(pl.BlockSpec((tm, w), lambda i: (i, 0)) for _ in range(8)),
        compiler_params=pltpu.CompilerParams(dimension_semantics=("parallel",), vmem_limit_bytes=VMEM_LIMIT),
        name="rwkv_prep",
    )(rkv.reshape(rows, 3 * w), lora.reshape(rows, -1), *consts)
    kk, lw0, kd0, b0, lw1, kd1, b1, bonus = [o.reshape(bsz, n, w) for o in outs]
    return kk, (lw0, lw1), (kd0, kd1), (b0, b1), bonus


def rwkv_branch(rkv, lora, p, s0, mask_set, block_tokens):
    kk, lw, kd, bvec, bonus = rwkv_prep(rkv, lora, p)
    y_f, y_b, s_fin = rwkv_scan_pallas(rkv, kk, lw, kd, bvec, s0, mask_set, block_tokens)
    return (y_f, y_b, bonus, lora[..., IN_LORA - RW_GATE_LORA:]), s_fin


def hyena_filter_spectra(n_tok, p):
    hp = lax.Precision.HIGHEST
    bands = (HY_POS_DIM - 1) // 2
    t = jnp.linspace(0.0, 1.0, n_tok, dtype=jnp.float32)[:, None]
    w = (2.0 * math.pi / n_tok) * jnp.arange(n_tok, dtype=jnp.float32)[:, None]
    f = jnp.linspace(1e-4, bands - 1, bands, dtype=jnp.float32)[None, :]
    feats = jnp.concatenate([t, jnp.cos(f * w), -jnp.sin(f * w)], axis=-1)
    h = jnp.sin(p['hy_f_freq1'] * (jnp.dot(feats, p['hy_f_w1'], precision=hp) + p['hy_f_b1']))
    h = jnp.sin(p['hy_f_freq2'] * (jnp.dot(h, p['hy_f_w2'], precision=hp) + p['hy_f_b2']))
    h = jnp.dot(h, p['hy_f_w3'], precision=hp).reshape(n_tok, HY_ORDER, 2, HY_WIDTH)
    rates = jnp.abs(jnp.linspace(math.log(HY_DECAY_TARGET) / HY_DECAY_PCT_SHORT,
                                 math.log(HY_DECAY_TARGET) / HY_DECAY_PCT_LONG, HY_WIDTH, dtype=jnp.float32))
    h = h * jnp.exp(-t * rates)[:, None, None, :]
    h_fwd, h_bwd = h[:, :, 0], h[:, :, 1]
    filt = jnp.concatenate([h_fwd, jnp.zeros_like(h_fwd[:1]), h_bwd[:0:-1]], axis=0)
    return jnp.fft.rfft(filt, axis=0)


HY_N = 2 * SEQ
HY_N1 = 64
HY_N2 = 128
HY_NH = HY_N1 // 2
HY_KH = HY_N1 // 2 + 1
HY_KP = 40
HY_PITCH = 88
HY_LANES = 128


def hyena_dft_tables():
    k1 = np.arange(HY_KH)[:, None]
    n1 = np.arange(HY_NH)[None, :]
    n2 = np.arange(HY_N2)
    ph = -2 * np.pi * (k1 * n1 / HY_N1)[None] - 2 * np.pi * (n2[:, None, None] * k1[None] / HY_N)
    lhs1 = np.zeros((HY_N2, 2 * HY_KP, HY_NH))
    lhs1[:, :HY_KH] = np.cos(ph)
    lhs1[:, HY_KP:HY_KP + HY_KH] = np.sin(ph)
    wgt = np.where((k1 == 0) | (k1 == HY_N1 // 2), 1.0, 2.0)[None] / HY_N
    lhs2 = np.zeros((HY_N2, HY_NH, 2 * HY_KP))
    lhs2[:, :, :HY_KH] = (wgt * np.cos(ph)).transpose(0, 2, 1)
    lhs2[:, :, HY_KP:HY_KP + HY_KH] = (wgt * np.sin(ph)).transpose(0, 2, 1)
    kk = np.arange(HY_N2)
    ang = -2 * np.pi * np.outer(kk, kk) / HY_N2
    cr, ci = np.cos(ang), np.sin(ang)
    f_fwd = np.block([[cr, -ci], [ci, cr]])
    f_inv = np.block([[cr, ci], [-ci, cr]])

    def as_bf(a):
        return jnp.asarray(a, jnp.float32).astype(jnp.bfloat16)

    return as_bf(lhs1), as_bf(lhs2), as_bf(f_fwd), as_bf(f_inv)


def hyena_spectrum_layout(k_spec):
    w = k_spec.shape[1]
    full = jnp.concatenate([k_spec, jnp.conj(k_spec[-2:0:-1])], axis=0)
    parts = jnp.stack([jnp.real(full), jnp.imag(full)], axis=0).reshape(2, HY_N2, HY_N1, w)
    parts = jnp.transpose(parts[:, :, :HY_KH], (2, 0, 1, 3)).reshape(HY_KH * 2 * HY_N2, w)
    return jnp.transpose(parts.reshape(-1, w // HY_LANES, HY_LANES), (1, 0, 2)).astype(jnp.float32)


def _hy_kernel(z_ref, g_ref, k_ref, l1_ref, l2_ref, ff_ref, fi_ref, bias_ref, o_ref, a_ref):
    f32, bf16 = jnp.float32, jnp.bfloat16
    n2n, nh, pitch, kp = HY_N2, HY_NH, HY_PITCH, HY_KP

    def stage1(q, c):
        slab = z_ref[0, pl.ds(pl.multiple_of(q * nh, nh), nh), :].astype(bf16)
        a_ref[pl.ds(pl.multiple_of(q * pitch, 8), 2 * kp), :] = jnp.dot(l1_ref[q], slab, preferred_element_type=f32)
        return c

    lax.fori_loop(0, n2n, stage1, 0, unroll=8)

    def stage2(k, c):
        a = jnp.concatenate([a_ref[pl.ds(k, n2n, stride=pitch), :],
                             a_ref[pl.ds(kp + k, n2n, stride=pitch), :]], axis=0).astype(bf16)
        x = jnp.dot(ff_ref[...], a, preferred_element_type=f32)
        xr, xi = x[:n2n], x[n2n:]
        base = pl.multiple_of(k * 2 * n2n, 2 * n2n)
        kr = k_ref[0, pl.ds(base, n2n), :]
        ki = k_ref[0, pl.ds(base + n2n, n2n), :]
        y = jnp.concatenate([xr * kr - xi * ki, xr * ki + xi * kr], axis=0).astype(bf16)
        b = jnp.dot(fi_ref[...], y, preferred_element_type=f32)
        a_ref[pl.ds(k, n2n, stride=pitch), :] = b[:n2n]
        a_ref[pl.ds(kp + k, n2n, stride=pitch), :] = b[n2n:]
        return c

    lax.fori_loop(0, HY_KH, stage2, 0, unroll=3)

    def stage3(q, c):
        blk = a_ref[pl.ds(pl.multiple_of(q * pitch, 8), 2 * kp), :].astype(bf16)
        y = jnp.dot(l2_ref[q], blk, preferred_element_type=f32)
        rows = pl.ds(pl.multiple_of(q * nh, nh), nh)
        o_ref[0, rows, :] = g_ref[0, rows, :] * (y + z_ref[0, rows, :] * bias_ref[...])
        return c

    lax.fori_loop(0, n2n, stage3, 0, unroll=8)


def hyena_long_conv_gated(zt, gt, spec, bias, tables):
    bsz, n, w = zt.shape
    l1, l2, ff, fi = tables
    nt = w // HY_LANES
    tok = pl.BlockSpec((1, n, HY_LANES), lambda t, b: (b, 0, t))
    return pl.pallas_call(
        _hy_kernel,
        out_shape=jax.ShapeDtypeStruct((bsz, n, w), jnp.float32),
        grid=(nt, bsz),
        in_specs=[tok, tok,
                  pl.BlockSpec((1,) + spec.shape[1:], lambda t, b: (t, 0, 0)),
                  pl.BlockSpec(l1.shape, lambda t, b: (0, 0, 0)),
                  pl.BlockSpec(l2.shape, lambda t, b: (0, 0, 0)),
                  pl.BlockSpec(ff.shape, lambda t, b: (0, 0)),
                  pl.BlockSpec(fi.shape, lambda t, b: (0, 0)),
                  pl.BlockSpec((1, HY_LANES), lambda t, b: (0, t))],
        out_specs=tok,
        scratch_shapes=[pltpu.VMEM((HY_N2 * HY_PITCH, HY_LANES), jnp.float32)],
        compiler_params=pltpu.CompilerParams(dimension_semantics=("parallel", "parallel"),
                                             vmem_limit_bytes=VMEM_LIMIT),
        name="hyena_conv",
    )(zt, gt, spec, l1, l2, ff, fi, bias.reshape(1, w))


def hyena_branch_long(streams, p):
    bsz, n, w3 = streams.shape
    k_spec = hyena_filter_spectra(n, p)
    tables = hyena_dft_tables()
    st = jnp.transpose(streams.reshape(bsz, HY_NH, HY_N2, w3), (0, 2, 1, 3)).reshape(bsz, n, w3)
    z, x1, x2 = jnp.split(st, 3, axis=-1)
    for o, gate in enumerate((x1, x2)):
        z = hyena_long_conv_gated(z, gate, hyena_spectrum_layout(k_spec[:, o]), p['hy_bias'][o], tables)
    w = z.shape[-1]
    return jnp.transpose(z.reshape(bsz, HY_N2, HY_NH, w), (0, 2, 1, 3)).reshape(bsz, n, w)


def _hy_ctx_kernel(z_ref, g_ref, k_ref, fd_ref, fi_ref, bias_ref, o_ref):
    f32, bf16 = jnp.float32, jnp.bfloat16
    z = z_ref[0]
    n = fd_ref.shape[0] // 2
    x = jnp.dot(fd_ref[...], z.astype(bf16), preferred_element_type=f32)
    xr, xi = x[:n], x[n:]
    kr, ki = k_ref[0, :n, :], k_ref[0, n:, :]
    y = jnp.concatenate([xr * kr - xi * ki, xr * ki + xi * kr], axis=0).astype(bf16)
    conv = jnp.dot(fi_ref[...], y, preferred_element_type=f32)
    o_ref[0] = g_ref[0] * (conv + z * bias_ref[...])


def hyena_branch_ctx(streams, p):
    bsz, n, w3 = streams.shape
    w = w3 // 3
    nn = 2 * n
    k_spec = hyena_filter_spectra(n, p)
    full = jnp.concatenate([k_spec, jnp.conj(k_spec[-2:0:-1])], axis=0)
    spec = jnp.concatenate([jnp.real(full), jnp.imag(full)], axis=0)
    ang = 2 * np.pi * np.outer(np.arange(nn), np.arange(n)) / nn
    fd = jnp.asarray(np.concatenate([np.cos(ang), -np.sin(ang)], axis=0), jnp.float32).astype(jnp.bfloat16)
    fi = jnp.asarray(np.concatenate([np.cos(ang).T, -np.sin(ang).T], axis=1) / nn, jnp.float32).astype(jnp.bfloat16)
    nt = w // HY_LANES
    tok = pl.BlockSpec((1, n, HY_LANES), lambda t, b: (b, 0, t))
    z, x1, x2 = jnp.split(streams, 3, axis=-1)
    for o, gate in enumerate((x1, x2)):
        sp = jnp.transpose(spec[:, o].reshape(2 * nn, nt, HY_LANES), (1, 0, 2)).astype(jnp.float32)
        z = pl.pallas_call(
            _hy_ctx_kernel,
            out_shape=jax.ShapeDtypeStruct((bsz, n, w), jnp.float32),
            grid=(nt, bsz),
            in_specs=[tok, tok,
                      pl.BlockSpec((1, 2 * nn, HY_LANES), lambda t, b: (t, 0, 0)),
                      pl.BlockSpec(fd.shape, lambda t, b: (0, 0)),
                      pl.BlockSpec(fi.shape, lambda t, b: (0, 0)),
                      pl.BlockSpec((1, HY_LANES), lambda t, b: (0, t))],
            out_specs=tok,
            compiler_params=pltpu.CompilerParams(dimension_semantics=("parallel", "parallel"),
                                                 vmem_limit_bytes=VMEM_LIMIT),
            name="hyena_conv_ctx",
        )(z, gate, sp, fd, fi, p['hy_bias'][o].reshape(1, w))
    return z


def _rms(x, gain):
    return x * lax.rsqrt(jnp.mean(x * x, axis=-1, keepdims=True) + NORM_EPS) * gain


def _conv3(x, w_ref, period):
    rows = x.shape[0]
    pos = lax.broadcasted_iota(jnp.int32, (rows, 1), 0) % period
    prev = jnp.where(pos == 0, 0.0, pltpu.roll(x, 1, axis=0))
    nxt = jnp.where(pos == period - 1, 0.0, pltpu.roll(x, rows - 1, axis=0))
    return prev * w_ref[0:1, :] + x * w_ref[1:2, :] + nxt * w_ref[2:3, :]


def _inproj_kernel(x_ref, mod_ref, gain_ref, ws5_ref, wrw_ref, wlo_ref, why_ref, wgt_ref, cwr_ref, cwh_ref, cbh_ref,
                   os5_ref, orw_ref, olo_ref, ohy_ref, ogt_ref, *, period):
    f32 = jnp.float32
    shift, scale = mod_ref[0, 0:1, :], mod_ref[0, 1:2, :]
    h = (_rms(x_ref[...], gain_ref[...]) * (1.0 + scale) + shift).astype(jnp.bfloat16)
    os5_ref[...] = jnp.dot(h, ws5_ref[...], preferred_element_type=f32)
    olo_ref[...] = jnp.dot(h, wlo_ref[...], preferred_element_type=f32)
    ogt_ref[...] = jnp.dot(h, wgt_ref[...], preferred_element_type=f32)
    orw_ref[...] = _conv3(jnp.dot(h, wrw_ref[...], preferred_element_type=f32), cwr_ref, period)
    ohy_ref[...] = _conv3(jnp.dot(h, why_ref[...], preferred_element_type=f32), cwh_ref, period) + cbh_ref[...]


def input_projection(x, mod, gain, w_parts, rw_conv_w, hy_conv_w, hy_conv_b, period):
    bsz, n, d = x.shape
    tm = min(MIX_BLOCK_ROWS, n)
    per_b = n // tm
    rows = bsz * n
    bmap = (lambda i: (i // per_b, 0, 0)) if mod.shape[0] == bsz else (lambda i: (0, 0, 0))
    wb = [w.astype(jnp.bfloat16) for w in w_parts]

    def const(a):
        return pl.BlockSpec(a.shape, lambda i: (0,) * a.ndim)

    cbh = hy_conv_b.reshape(1, -1)
    outs = pl.pallas_call(
        functools.partial(_inproj_kernel, period=period),
        out_shape=tuple(jax.ShapeDtypeStruct((rows, w.shape[1]), jnp.float32) for w in wb),
        grid=(rows // tm,),
        in_specs=[pl.BlockSpec((tm, d), lambda i: (i, 0)),
                  pl.BlockSpec((1, 3, d), bmap),
                  pl.BlockSpec((1, d), lambda i: (0, 0))]
                 + [const(w) for w in wb] + [const(rw_conv_w), const(hy_conv_w), const(cbh)],
        out_specs=tuple(pl.BlockSpec((tm, w.shape[1]), lambda i: (i, 0)) for w in wb),
        compiler_params=pltpu.CompilerParams(dimension_semantics=("parallel",), vmem_limit_bytes=VMEM_LIMIT),
        name="input_projection",
    )(x.reshape(rows, d), mod, gain.reshape(1, d), *wb, rw_conv_w, hy_conv_w, cbh)
    return tuple(o.reshape(bsz, n, o.shape[-1]) for o in outs)


def _merge_kernel(x_ref, mod_ref, gain_ref, ys5_ref, us5_ref, wf_ref, wb_ref, bonus_ref, glo_ref, hy_ref, gt_ref,
                  s5d_ref, glu_ref, lnw_ref, lnb_ref, g2_ref, havg_ref, brs_ref, brr_ref, brh_ref, ow_ref, o_ref):
    f32, bf16 = jnp.float32, jnp.bfloat16
    y = ys5_ref[...] + s5d_ref[...] * us5_ref[...]
    y = 0.5 * y * (1.0 + lax.erf(y * (1.0 / math.sqrt(2.0))))
    s5 = y * jax.nn.sigmoid(jnp.dot(y.astype(bf16), glu_ref[...], preferred_element_type=f32))
    wkv = wf_ref[...] + wb_ref[...]

    def head_mean(t):
        hi = t.astype(bf16)
        rem = t - hi.astype(f32)
        mid = rem.astype(bf16)
        lo = (rem - mid.astype(f32)).astype(bf16)
        hm = havg_ref[...]
        tot = (jnp.dot(hi, hm, preferred_element_type=f32) + jnp.dot(mid, hm, preferred_element_type=f32)
               + jnp.dot(lo, hm, preferred_element_type=f32))
        return tot * (1.0 / RW_HEAD)

    mu = head_mean(wkv)
    cen = wkv - mu
    var = head_mean(cen * cen)
    o = cen * lax.rsqrt(var + RW_GN_EPS) * lnw_ref[...] + lnb_ref[...] + bonus_ref[...]
    g = jnp.dot(jax.nn.sigmoid(glo_ref[...]).astype(bf16), g2_ref[...], preferred_element_type=f32)
    rw = o * g
    gt = jax.nn.sigmoid(gt_ref[...])
    d = x_ref.shape[-1]
    m = (gt[:, 0:d] * jnp.dot(s5.astype(bf16), brs_ref[...], preferred_element_type=f32)
         + gt[:, d:2 * d] * jnp.dot(rw.astype(bf16), brr_ref[...], preferred_element_type=f32)
         + gt[:, 2 * d:3 * d] * jnp.dot(hy_ref[...].astype(bf16), brh_ref[...], preferred_element_type=f32))
    yl = jnp.dot(m.astype(bf16), ow_ref[...], preferred_element_type=f32)
    o_ref[...] = x_ref[...] + mod_ref[0, 2:3, :] * _rms(yl, gain_ref[...])


def merge_block(x, mod, gain, ys5, us5, rw_parts, hy, gates, p):
    bsz, n, d = x.shape
    tm = min(MIX_BLOCK_ROWS, n)
    per_b = n // tm
    rows = bsz * n
    bmap = (lambda i: (i // per_b, 0, 0)) if mod.shape[0] == bsz else (lambda i: (0, 0, 0))

    def bf(a):
        return a.astype(jnp.bfloat16)

    def row(a):
        return pl.BlockSpec((tm, a.shape[-1]), lambda i: (i, 0))

    def const(a):
        return pl.BlockSpec(a.shape, lambda i: (0,) * a.ndim)

    hid = np.arange(RW_WIDTH) // RW_HEAD
    havg = jnp.asarray(hid[:, None] == hid[None, :], jnp.bfloat16)
    acts = [a.reshape(rows, a.shape[-1]) for a in (ys5, us5) + tuple(rw_parts) + (hy, gates)]
    consts = [p['s5_d'].reshape(1, -1), bf(p['s5_glu_w']), p['rw_ln_w'].reshape(1, -1), p['rw_ln_b'].reshape(1, -1),
              bf(p['rw_g2']), havg, bf(p['br_s5']), bf(p['br_rw']), bf(p['br_hy']), bf(p['out_w'])]
    out = pl.pallas_call(
        _merge_kernel,
        out_shape=jax.ShapeDtypeStruct((rows, d), jnp.float32),
        grid=(rows // tm,),
        in_specs=[pl.BlockSpec((tm, d), lambda i: (i, 0)), pl.BlockSpec((1, 3, d), bmap),
                  pl.BlockSpec((1, d), lambda i: (0, 0))] + [row(a) for a in acts] + [const(c) for c in consts],
        out_specs=pl.BlockSpec((tm, d), lambda i: (i, 0)),
        compiler_params=pltpu.CompilerParams(dimension_semantics=("parallel",), vmem_limit_bytes=VMEM_LIMIT),
        name="merge_block",
    )(x.reshape(rows, d), mod, gain.reshape(1, d), *acts, *consts)
    return out.reshape(bsz, n, d)


def token_mixer(x, ctx, mod_l, mod_c, g_pre, g_post, p, with_ctx_out):
    bsz = x.shape[0]
    cuts = [IN_S5, IN_S5 + IN_RW, IN_S5 + IN_RW + IN_LORA, IN_S5 + IN_RW + IN_LORA + IN_HY]
    edges = [0] + cuts + [IN_COLS]
    w_parts = [p['in_w'][:, edges[i]:edges[i + 1]] for i in range(5)]
    conv = (p['rw_conv_w'], p['hy_conv_w'], p['hy_conv_b'])
    u_l, rkv_l, lora_l, hy_l, gate_l = input_projection(x, mod_l, g_pre, w_parts, *conv, GRID_W)
    u_c, rkv_c, lora_c, hy_c, gate_c = input_projection(ctx, mod_c, g_pre, w_parts, *conv, CTX_LEN)
    mask_set = rwkv_masks()
    zr = jnp.zeros((bsz, 2, RW_WIDTH // RW_LANES, RW_LANES, RW_LANES), jnp.float32)
    ys5_l, ys5_c = s5_branch(u_l, u_c, p)
    rw_c, rw_state = rwkv_branch(rkv_c, lora_c, p, zr, mask_set, CTX_LEN)
    rw_l, _ = rwkv_branch(rkv_l, lora_l, p, rw_state, mask_set, RW_BLOCK_TOKENS)
    hy_lat = hyena_branch_long(hy_l, p)
    x = merge_block(x, mod_l, g_post, ys5_l, u_l, rw_l, hy_lat, gate_l, p)
    if with_ctx_out:
        ctx = merge_block(ctx, mod_c, g_post, ys5_c, u_c, rw_c, hyena_branch_ctx(hy_c, p), gate_c, p)
    return x, ctx


def _ffn_kernel(x_ref, mod_ref, gain_ref, wg_ref, wu_ref, wd_ref, o_ref):
    x = x_ref[...]
    shift, scale, gate = mod_ref[0, 0:1, :], mod_ref[0, 1:2, :], mod_ref[0, 2:3, :]
    h = (_rms(x, gain_ref[0:1, :]) * (1.0 + scale) + shift).astype(jnp.bfloat16)
    g = jnp.dot(h, wg_ref[...], preferred_element_type=jnp.float32)
    u = jnp.dot(h, wu_ref[...], preferred_element_type=jnp.float32)
    a = (g * jax.nn.sigmoid(g) * u).astype(jnp.bfloat16)
    y = jnp.dot(a, wd_ref[...], preferred_element_type=jnp.float32)
    o_ref[...] = x + gate * _rms(y, gain_ref[1:2, :])


def ffn_block(x, mod, gains, wg, wu, wd):
    bsz, n, d = x.shape
    hid = wg.shape[1]
    tm = min(FFN_BLOCK_ROWS, n)
    per_b = n // tm
    xf = x.reshape(bsz * n, d)
    bmap = (lambda i: (i // per_b, 0, 0)) if mod.shape[0] == bsz else (lambda i: (0, 0, 0))
    out = pl.pallas_call(
        _ffn_kernel,
        out_shape=jax.ShapeDtypeStruct((bsz * n, d), jnp.float32),
        grid=(bsz * per_b,),
        in_specs=[pl.BlockSpec((tm, d), lambda i: (i, 0)),
                  pl.BlockSpec((1, 3, d), bmap),
                  pl.BlockSpec((2, d), lambda i: (0, 0)),
                  pl.BlockSpec((d, hid), lambda i: (0, 0)),
                  pl.BlockSpec((d, hid), lambda i: (0, 0)),
                  pl.BlockSpec((hid, d), lambda i: (0, 0))],
        out_specs=pl.BlockSpec((tm, d), lambda i: (i, 0)),
        compiler_params=pltpu.CompilerParams(dimension_semantics=("parallel",), vmem_limit_bytes=VMEM_LIMIT),
        name="ffn_block",
    )(xf, mod, gains, wg.astype(jnp.bfloat16), wu.astype(jnp.bfloat16), wd.astype(jnp.bfloat16))
    return out.reshape(bsz, n, d)


def _moe_kernel(be_ref, x_ref, gate_ref, wg_ref, wu_ref, wd_ref, o_ref):
    j = pl.program_id(1)
    xb = x_ref[...].astype(jnp.bfloat16)
    g = jnp.dot(xb, wg_ref[0], preferred_element_type=jnp.float32)
    u = jnp.dot(xb, wu_ref[0], preferred_element_type=jnp.float32)
    a = (g * jax.nn.sigmoid(g) * u).astype(jnp.bfloat16)
    y = jnp.dot(a, wd_ref[0], preferred_element_type=jnp.float32)

    @pl.when(j == 0)
    def _():
        o_ref[...] = y

    @pl.when(j > 0)
    def _():
        o_ref[...] += y

    @pl.when(j == pl.num_programs(1) - 1)
    def _():
        o_ref[...] *= gate_ref[...]


def moe_expert_blocks(xb, slot_gate, block_e, wg, wu, wd):
    cap, d = xb.shape
    n_blocks = cap // MOE_BLOCK
    hid = wg.shape[-1]
    nh = MOE_HIDDEN_TILES
    th = hid // nh

    def hidx(i, j):
        return j + (i % 2) * (nh - 1 - 2 * j)

    return pl.pallas_call(
        _moe_kernel,
        out_shape=jax.ShapeDtypeStruct((cap, d), jnp.float32),
        grid_spec=pltpu.PrefetchScalarGridSpec(
            num_scalar_prefetch=1,
            grid=(n_blocks, nh),
            in_specs=[pl.BlockSpec((MOE_BLOCK, d), lambda i, j, be: (i, 0)),
                      pl.BlockSpec((MOE_BLOCK, 1), lambda i, j, be: (i, 0)),
                      pl.BlockSpec((1, d, th), lambda i, j, be: (be[i], 0, hidx(i, j))),
                      pl.BlockSpec((1, d, th), lambda i, j, be: (be[i], 0, hidx(i, j))),
                      pl.BlockSpec((1, th, d), lambda i, j, be: (be[i], hidx(i, j), 0))],
            out_specs=pl.BlockSpec((MOE_BLOCK, d), lambda i, j, be: (i, 0))),
        compiler_params=pltpu.CompilerParams(dimension_semantics=("parallel", "arbitrary"),
                                             vmem_limit_bytes=VMEM_LIMIT),
        name="moe_experts",
    )(block_e, xb, slot_gate.reshape(cap, 1), wg.astype(jnp.bfloat16), wu.astype(jnp.bfloat16),
      wd.astype(jnp.bfloat16))


def moe_swiglu(h, router_w, wg, wu, wd):
    d_model = h.shape[-1]
    tok = h.reshape(-1, d_model)
    n = tok.shape[0]
    n_assign = n * TOP_K
    logits = jnp.dot(tok, router_w)
    top_logit, top_e = lax.top_k(logits, TOP_K)
    gate = jax.nn.softmax(top_logit, axis=-1).reshape(-1)
    flat_e = top_e.reshape(-1)
    order = jnp.argsort(flat_e)
    onehot = (flat_e[:, None] == jnp.arange(N_EXPERTS, dtype=flat_e.dtype)[None, :]).astype(jnp.int32)
    sizes = jnp.sum(onehot, axis=0)
    padded = (sizes + MOE_BLOCK - 1) // MOE_BLOCK * MOE_BLOCK
    pad_end = jnp.cumsum(padded)
    pad_start = pad_end - padded
    grp_start = jnp.cumsum(sizes) - sizes
    n_blocks = -(-n_assign // MOE_BLOCK) + N_EXPERTS
    cap = n_blocks * MOE_BLOCK
    block_start = jnp.arange(n_blocks, dtype=jnp.int32) * MOE_BLOCK
    block_e = jnp.minimum(jnp.sum(block_start[:, None] >= pad_end[None, :], axis=1), N_EXPERTS - 1)
    slot_e = jnp.repeat(block_e, MOE_BLOCK)
    within = jnp.arange(cap, dtype=jnp.int32) - pad_start[slot_e]
    valid = within < sizes[slot_e]
    assign = order[jnp.clip(grp_start[slot_e] + within, 0, n_assign - 1)]
    slot_tok = jnp.where(valid, (assign // TOP_K).astype(jnp.int32), 0)
    slot_gate = jnp.where(valid, gate[assign], 0.0)
    xb = tok[slot_tok]
    yb = moe_expert_blocks(xb, slot_gate, block_e.astype(jnp.int32), wg, wu, wd)
    rank = jnp.take_along_axis(jnp.cumsum(onehot, axis=0), flat_e[:, None], axis=1)[:, 0] - 1
    pos = (pad_start[flat_e] + rank).astype(jnp.int32).reshape(n, TOP_K)
    out = yb[pos[:, 0]]
    for kk in range(1, TOP_K):
        out = out + yb[pos[:, kk]]
    return out.reshape(h.shape)


def kernel(x, c, ctx, c_ctx, mod_w, mod_b, norm_g, in_w, s5_lam_re, s5_lam_im, s5_log_step, s5_b_re, s5_b_im, s5_c_re, s5_c_im, s5_d, s5_glu_w, rw_conv_w, rw_w0, rw_w2, rw_a0, rw_a2, rw_g2, rw_kk, rw_ka, rw_rk, rw_ln_w, rw_ln_b, hy_conv_w, hy_conv_b, hy_f_w1, hy_f_b1, hy_f_freq1, hy_f_w2, hy_f_b2, hy_f_freq2, hy_f_w3, hy_bias, br_s5, br_rw, br_hy, out_w, ffn_wg, ffn_wu, ffn_wd, moe_router, moe_wg, moe_wu, moe_wd):
    silu_c = jax.nn.silu(c)
    silu_cc = jax.nn.silu(c_ctx)
    for i in range(DEPTH):
        last = i == DEPTH - 1
        p = {
            'in_w': in_w[i],
            's5_lam_re': s5_lam_re[i], 's5_lam_im': s5_lam_im[i], 's5_log_step': s5_log_step[i],
            's5_b_re': s5_b_re[i], 's5_b_im': s5_b_im[i], 's5_c_re': s5_c_re[i], 's5_c_im': s5_c_im[i],
            's5_d': s5_d[i], 's5_glu_w': s5_glu_w[i],
            'rw_conv_w': rw_conv_w[i], 'rw_w0': rw_w0[i], 'rw_w2': rw_w2[i], 'rw_a0': rw_a0[i], 'rw_a2': rw_a2[i],
            'rw_g2': rw_g2[i], 'rw_kk': rw_kk[i], 'rw_ka': rw_ka[i], 'rw_rk': rw_rk[i],
            'rw_ln_w': rw_ln_w[i], 'rw_ln_b': rw_ln_b[i],
            'hy_conv_w': hy_conv_w[i], 'hy_conv_b': hy_conv_b[i], 'hy_f_w1': hy_f_w1[i], 'hy_f_b1': hy_f_b1[i],
            'hy_f_freq1': hy_f_freq1[i], 'hy_f_w2': hy_f_w2[i], 'hy_f_b2': hy_f_b2[i], 'hy_f_freq2': hy_f_freq2[i],
            'hy_f_w3': hy_f_w3[i], 'hy_bias': hy_bias[i],
            'br_s5': br_s5[i], 'br_rw': br_rw[i], 'br_hy': br_hy[i], 'out_w': out_w[i],
        }
        ml = jnp.split((silu_c @ mod_w[i] + mod_b[i])[:, None, :], 6, axis=-1)
        mc = jnp.split(silu_cc @ mod_w[i] + mod_b[i], 6, axis=-1)
        g_pre_m, g_post_m, g_pre_f, g_post_f = norm_g[i]
        if i % 2 == 0:
            def channel_mix(t, m, j=i // 2):
                return ffn_block(t, m, norm_g[i, 2:4], ffn_wg[j], ffn_wu[j], ffn_wd[j])
        else:
            def channel_mix(t, m, j=i // 2):
                h = rms_norm(t, g_pre_f) * (1 + m[:, 1:2]) + m[:, 0:1]
                y = moe_swiglu(h, moe_router[j], moe_wg[j], moe_wu[j], moe_wd[j])
                return t + m[:, 2:3] * rms_norm(y, g_post_f)
        x, ctx = token_mixer(x, ctx, jnp.concatenate(ml[0:3], axis=1), jnp.stack(mc[0:3])[None],
                             g_pre_m, g_post_m, p, not last)
        x = channel_mix(x, jnp.concatenate(ml[3:6], axis=1))
        if not last:
            ctx = channel_mix(ctx, jnp.stack(mc[3:6])[None])
    return x
```

```python
import math
import functools
import numpy as np
import jax
import jax.numpy as jnp
from jax import lax
from jax.experimental import pallas as pl
from jax.experimental.pallas import tpu as pltpu

D_MODEL = 1024
BATCH = 8
SEQ = 4096
DEPTH = 2
GRID_W = 64
CTX_LEN = 256
NORM_EPS = 1e-6
S5_WIDTH = D_MODEL // 4
S5_GROUP = 16
S5_GROUPS = S5_WIDTH // S5_GROUP
S5_STATE = 64
RW_WIDTH = D_MODEL // 2
RW_HEAD = 64
RW_HEADS = RW_WIDTH // RW_HEAD
RW_DECAY_LORA = 64
RW_ICLR_LORA = 64
RW_GATE_LORA = 128
RW_GN_EPS = 64e-5
HY_WIDTH = D_MODEL // 4
HY_ORDER = 2
HY_POS_DIM = 33
HY_FILTER_HIDDEN = 64
HY_DECAY_TARGET = 1e-2
HY_DECAY_PCT_SHORT = 0.3
HY_DECAY_PCT_LONG = 1.5
SHORT_CONV = 3
N_BRANCH = 3
FFN_HIDDEN = 2816
N_EXPERTS = 8
TOP_K = 2
EXPERT_HIDDEN = 3584
MOE_BLOCK = 512
IN_S5 = S5_WIDTH
IN_RW = 3 * RW_WIDTH
IN_LORA = 2 * RW_DECAY_LORA + 2 * RW_ICLR_LORA + RW_GATE_LORA
IN_HY = (HY_ORDER + 1) * HY_WIDTH
IN_GATE = N_BRANCH * D_MODEL
IN_COLS = IN_S5 + IN_RW + IN_LORA + IN_HY + IN_GATE

VMEM_LIMIT = 56 * 1024 * 1024
S5_CHUNK = 64
RW_CHUNK = 64
RW_HG = 4
RW_LANES = RW_HG * RW_HEAD
RW_BLOCK_TOKENS = 256
RW_BATCH_PER_STEP = 2
MOE_HIDDEN_TILES = 2
FFN_BLOCK_ROWS = 256
MIX_BLOCK_ROWS = 256


def rms_norm(x, gain):
    y = x * lax.rsqrt(jnp.mean(x * x, axis=-1, keepdims=True) + NORM_EPS)
    return y * gain


def s5_chunk_operators(lam_re, lam_im, log_step, b_re, b_im, c_re, c_im):
    T, G, P, GC = S5_CHUNK, S5_GROUPS, S5_STATE, S5_GROUP
    hp = lax.Precision.HIGHEST
    step = jnp.exp(log_step)[..., None]
    lr, li = lam_re, lam_im
    tau = jnp.arange(T + 1, dtype=jnp.float32)[:, None, None, None]
    mag = jnp.exp(lr * step * tau)
    ang = li * step * tau
    e_re, e_im = mag * jnp.cos(ang), mag * jnp.sin(ang)
    ab_re, ab_im = e_re[1], e_im[1]
    den = lr * lr + li * li
    nr, ni = ab_re - 1.0, ab_im
    q_re = (nr * lr + ni * li) / den
    q_im = (ni * lr - nr * li) / den
    bb_re = q_re[..., None] * b_re - q_im[..., None] * b_im
    bb_im = q_re[..., None] * b_im + q_im[..., None] * b_re
    eb_re = e_re[..., None] * bb_re - e_im[..., None] * bb_im
    eb_im = e_re[..., None] * bb_im + e_im[..., None] * bb_re
    ktau = (jnp.einsum('dgip,tdgpj->tdgij', c_re, eb_re[:T], precision=hp)
            - jnp.einsum('dgip,tdgpj->tdgij', c_im, eb_im[:T], precision=hp))
    t_idx = jnp.arange(T)
    k_f = jnp.transpose(ktau[:, 0], (1, 3, 0, 2))
    k_b = jnp.transpose(ktau[:, 1], (1, 3, 0, 2))
    table = jnp.concatenate([k_b[:, :, :0:-1], k_f[:, :, :1] + k_b[:, :, :1], k_f[:, :, 1:]], axis=2)
    table = table.astype(jnp.bfloat16).reshape(G, GC, (2 * T - 1) * GC)
    kt = jnp.stack([table[:, :, (T - 1 - s) * GC:(2 * T - 1 - s) * GC] for s in range(T)], axis=1)
    kt = kt.reshape(G, T * GC, T * GC)
    wf_re, wf_im = eb_re[T - 1 - t_idx, 0], eb_im[T - 1 - t_idx, 0]
    wb_re, wb_im = eb_re[t_idx, 1], eb_im[t_idx, 1]
    win = jnp.stack([wf_re, wf_im, wb_re, wb_im], axis=0)
    win = jnp.transpose(win, (2, 1, 4, 0, 3)).reshape(G, T * GC, 4 * P)
    ef_re, ef_im = e_re[t_idx + 1, 0], e_im[t_idx + 1, 0]
    eb2_re, eb2_im = e_re[T - t_idx, 1], e_im[T - t_idx, 1]

    def readout(cr, ci, er, ei):
        re = cr[None] * er[:, :, None, :] - ci[None] * ei[:, :, None, :]
        im = -(cr[None] * ei[:, :, None, :] + ci[None] * er[:, :, None, :])
        return re, im

    of_re, of_im = readout(c_re[0], c_im[0], ef_re, ef_im)
    ob_re, ob_im = readout(c_re[1], c_im[1], eb2_re, eb2_im)
    wout = jnp.stack([of_re, of_im, ob_re, ob_im], axis=0)
    wout = jnp.transpose(wout, (2, 0, 4, 1, 3)).reshape(G, 4 * P, T * GC)
    at_re, at_im = e_re[T], e_im[T]
    apow = jnp.concatenate([at_re[0], at_re[0], at_re[1], at_re[1]], axis=-1)[:, None, :]
    aimg = jnp.concatenate([-at_im[0], at_im[0], -at_im[1], at_im[1]], axis=-1)[:, None, :]
    return kt.astype(jnp.bfloat16), win.astype(jnp.bfloat16), wout.astype(jnp.bfloat16), apow, aimg


def _s5_kernel(x_ref, kt_ref, win_ref, wout_ref, apow_ref, aimg_ref, h0_ref, y_ref, hfin_ref, hin_ref, *,
               n_chunks, bsz):
    P2 = 2 * S5_STATE
    xb = x_ref[0].astype(jnp.bfloat16)
    hin_ref[...] = jnp.dot(xb, win_ref[0], preferred_element_type=jnp.float32)
    ap = apow_ref[0]
    ai = aimg_ref[0]
    apf, aif = ap[:, :P2], ai[:, :P2]
    apb, aib = ap[:, P2:], ai[:, P2:]

    def cmul(h, a_p, a_i):
        return h * a_p + pltpu.roll(h, S5_STATE, axis=1) * a_i

    def body(c, carry):
        hf, hb = carry
        rf = pl.ds(pl.multiple_of(c * bsz, bsz), bsz)
        rb = pl.ds(pl.multiple_of((n_chunks - 1 - c) * bsz, bsz), bsz)
        df = hin_ref[rf, :P2]
        db = hin_ref[rb, P2:]
        hin_ref[rf, :P2] = hf
        hin_ref[rb, P2:] = hb
        return cmul(hf, apf, aif) + df, cmul(hb, apb, aib) + db

    h0 = h0_ref[0]
    hf, hb = lax.fori_loop(0, n_chunks, body, (h0[:, :P2], h0[:, P2:]))
    hfin_ref[0, :, :P2] = hf
    hfin_ref[0, :, P2:] = hb
    y = jnp.dot(xb, kt_ref[0], preferred_element_type=jnp.float32)
    y = y + jnp.dot(hin_ref[...].astype(jnp.bfloat16), wout_ref[0], preferred_element_type=jnp.float32)
    y_ref[0] = y


def s5_scan_pallas(u, ops, h0):
    kt, win, wout, apow, aimg = ops
    bsz, n, _ = u.shape
    T, G, P, GC = S5_CHUNK, S5_GROUPS, S5_STATE, S5_GROUP
    nc = n // T
    rows = nc * bsz
    x = jnp.transpose(u.reshape(bsz, nc, T, G, GC), (3, 1, 0, 2, 4)).reshape(G, rows, T * GC)
    y, hfin = pl.pallas_call(
        functools.partial(_s5_kernel, n_chunks=nc, bsz=bsz),
        out_shape=(jax.ShapeDtypeStruct((G, rows, T * GC), jnp.float32),
                   jax.ShapeDtypeStruct((G, bsz, 4 * P), jnp.float32)),
        grid=(G,),
        in_specs=[pl.BlockSpec((1, rows, T * GC), lambda g: (g, 0, 0)),
                  pl.BlockSpec((1, T * GC, T * GC), lambda g: (g, 0, 0)),
                  pl.BlockSpec((1, T * GC, 4 * P), lambda g: (g, 0, 0)),
                  pl.BlockSpec((1, 4 * P, T * GC), lambda g: (g, 0, 0)),
                  pl.BlockSpec((1, 1, 4 * P), lambda g: (g, 0, 0)),
                  pl.BlockSpec((1, 1, 4 * P), lambda g: (g, 0, 0)),
                  pl.BlockSpec((1, bsz, 4 * P), lambda g: (g, 0, 0))],
        out_specs=(pl.BlockSpec((1, rows, T * GC), lambda g: (g, 0, 0)),
                   pl.BlockSpec((1, bsz, 4 * P), lambda g: (g, 0, 0))),
        scratch_shapes=[pltpu.VMEM((rows, 4 * P), jnp.float32)],
        compiler_params=pltpu.CompilerParams(dimension_semantics=("parallel",),
                                             vmem_limit_bytes=VMEM_LIMIT),
        name="s5_scan",
    )(x, kt, win, wout, apow, aimg, h0)
    y = jnp.transpose(y.reshape(G, nc, bsz, T, GC), (2, 1, 3, 0, 4)).reshape(bsz, n, G * GC)
    return y, hfin


def s5_branch(u, u_ctx, p):
    ops = s5_chunk_operators(p['s5_lam_re'], p['s5_lam_im'], p['s5_log_step'], p['s5_b_re'], p['s5_b_im'],
                             p['s5_c_re'], p['s5_c_im'])
    h0 = jnp.zeros((S5_GROUPS, u.shape[0], 4 * S5_STATE), jnp.float32)
    y_ctx, h_ctx = s5_scan_pallas(u_ctx, ops, h0)
    y_lat, _ = s5_scan_pallas(u, ops, h_ctx)
    return y_lat, y_ctx


_NT = (((1,), (1,)), ((), ()))

_M_SAME, _M_EYE = 0, 1
_W_STRICT, _W_INCL, _W_EYE, _W_LEVEL0 = 0, 2, 4, 5
_N_LEVELS = 6


def rwkv_masks():
    C, n = RW_CHUNK, RW_LANES
    row = np.arange(n)[:, None]
    col = np.arange(n)[None, :]
    square = np.stack([(row // C) == (col // C), row == col])
    tt, jj = np.arange(C)[:, None], np.arange(C)[None, :]
    wide = [jj < tt, jj > tt, jj <= tt, jj >= tt, jj == tt]
    for s in (1, 2, 4, 8, 16, 32):
        fwd = ((tt // s) % 2 == 1) & ((jj // s) == (tt // s) - 1)
        wide += [fwd, fwd.T]
    wide = np.stack([np.tile(m, (1, RW_HG)) for m in wide])
    tri = np.stack([jj <= tt, jj >= tt])
    return (jnp.asarray(square, jnp.float32), jnp.asarray(wide, jnp.float32), jnp.asarray(tri, jnp.bfloat16))


def _bdot(a, b, dims=None):
    a = a.astype(jnp.bfloat16)
    b = b.astype(jnp.bfloat16)
    if dims is None:
        return jnp.dot(a, b, preferred_element_type=jnp.float32)
    return lax.dot_general(a, b, dims, preferred_element_type=jnp.float32)


def _rwkv_chunk(S, r, lw, k, v, kk, b, m_ref, w_ref, same_b, tri, d):
    C = RW_CHUNK
    same = m_ref[_M_SAME]
    lw_hi = lw.astype(jnp.bfloat16)
    rem = lw - lw_hi.astype(jnp.float32)
    lw_mid = rem.astype(jnp.bfloat16)
    lw_lo = (rem - lw_mid.astype(jnp.float32)).astype(jnp.bfloat16)
    cl = (jnp.dot(tri, lw_hi, preferred_element_type=jnp.float32)
          + jnp.dot(tri, lw_mid, preferred_element_type=jnp.float32)
          + jnp.dot(tri, lw_lo, preferred_element_type=jnp.float32))
    yield
    tot = jnp.sum(lw, axis=0, keepdims=True)
    e_neg = jnp.exp(-cl)
    e_end = jnp.exp(tot - cl)
    a_t = -kk * jnp.exp(cl - lw)
    r_t = r * jnp.exp(cl)
    b_t = b * e_neg
    k_t = k * e_neg
    b_h = b * e_end
    k_h = k * e_end
    p_c = jnp.exp(tot)

    def tile(x):
        return jnp.concatenate([x] * RW_HG, axis=0)

    def stack(x):
        return tile(x.astype(jnp.bfloat16)) * same_b

    n = RW_LANES
    sa = stack(a_t)
    wide = _bdot(jnp.concatenate([a_t, r_t], axis=0), jnp.concatenate([stack(b_t), stack(k_t)], axis=0), _NT)
    yield
    a_ab = wide[:C, :n] * w_ref[_W_STRICT + d]
    a_ak = wide[:C, n:] * w_ref[_W_STRICT + d]
    a_rb = wide[C:, :n] * w_ref[_W_INCL + d]
    a_rk = wide[C:, n:] * w_ref[_W_INCL + d]
    dinv = w_ref[_W_EYE] + a_ab * w_ref[_W_LEVEL0 + d]
    for l in range(1, _N_LEVELS):
        inner = _bdot(a_ab * w_ref[_W_LEVEL0 + 2 * l + d], stack(dinv))
        yield
        dinv = dinv + _bdot(dinv, stack(inner))
        yield
    av = _bdot(jnp.concatenate([a_ak, a_rk], axis=0), stack(v))
    yield
    akv = av[:C]
    arkv = av[C:]
    mu = _bdot(dinv, jnp.concatenate([sa, stack(akv)], axis=1))
    yield
    m1 = mu[:, :n]
    u0 = mu[:, n:]
    my = _bdot(a_rb, jnp.concatenate([stack(m1), stack(u0)], axis=1))
    yield
    m2 = r_t + my[:, :n]
    y0 = my[:, n:] + arkv
    mut = jnp.concatenate([m1, u0], axis=1).T
    gh = _bdot(mut, b_h)
    yield
    g = m_ref[_M_EYE] * p_c + gh[:n] * same
    hmat = (gh[n:] + _bdot(v.T, k_h)) * same
    y = _bdot(m2, S, _NT) + y0
    s_new = _bdot(S, g) + hmat
    return s_new, y


def _lockstep(gens):
    results = [None] * len(gens)
    active = list(range(len(gens)))
    while active:
        for i in list(active):
            try:
                next(gens[i])
            except StopIteration as stop:
                results[i] = stop.value
                active.remove(i)
    return results


def _rwkv_kernel(m_ref, w_ref, tri_ref, rf_ref, rb_ref, vf_ref, vb_ref, kkf_ref, kkb_ref, lwf_ref, kf_ref, bf_ref,
                 lwb_ref, kb_ref, bb_ref, s0_ref, yf_ref, yb_ref, sfin_ref, s_scr, *, n_chunks, n_groups, n_batch):
    i = pl.program_id(1)
    C = RW_CHUNK

    @pl.when(i == 0)
    def _():
        s_scr[...] = s0_ref[...]

    tri_f = tri_ref[0]
    tri_b = tri_ref[1]
    same_b = m_ref[_M_SAME].astype(jnp.bfloat16)

    def body(c, carry):
        rf = pl.ds(pl.multiple_of(c * C, C), C)
        rb = pl.ds(pl.multiple_of((n_chunks - 1 - c) * C, C), C)
        gens = []
        probs = [(e, g) for e in range(n_batch) for g in range(n_groups)]
        for e, g in probs:
            ln = slice(g * RW_LANES, (g + 1) * RW_LANES)
            gens.append(_rwkv_chunk(s_scr[e, 0, g], rf_ref[e, rf, ln], lwf_ref[e, rf, ln], kf_ref[e, rf, ln],
                                    vf_ref[e, rf, ln], kkf_ref[e, rf, ln], bf_ref[e, rf, ln],
                                    m_ref, w_ref, same_b, tri_f, 0))
            gens.append(_rwkv_chunk(s_scr[e, 1, g], rb_ref[e, rb, ln], lwb_ref[e, rb, ln], kb_ref[e, rb, ln],
                                    vb_ref[e, rb, ln], kkb_ref[e, rb, ln], bb_ref[e, rb, ln],
                                    m_ref, w_ref, same_b, tri_b, 1))
        out = _lockstep(gens)
        for q, (e, g) in enumerate(probs):
            ln = slice(g * RW_LANES, (g + 1) * RW_LANES)
            (s_f, y_f), (s_b, y_b) = out[2 * q], out[2 * q + 1]
            s_scr[e, 0, g] = s_f
            s_scr[e, 1, g] = s_b
            yf_ref[e, rf, ln] = y_f
            yb_ref[e, rb, ln] = y_b
        return carry

    lax.fori_loop(0, n_chunks, body, 0)

    @pl.when(i == pl.num_programs(1) - 1)
    def _():
        sfin_ref[...] = s_scr[...]


def rwkv_scan_pallas(rkv, kk, lw, kd, bvec, s0, mask_set, block_tokens):
    bsz, n, width = kk.shape
    masks, wide, tri = mask_set
    ng = width // RW_LANES
    tb = block_tokens
    nb = n // tb
    eb = RW_BATCH_PER_STEP
    assert bsz % eb == 0
    fwd = pl.BlockSpec((eb, tb, width), lambda b, i: (b, i, 0))
    bwd = pl.BlockSpec((eb, tb, width), lambda b, i: (b, nb - 1 - i, 0))
    r_fwd, r_bwd = fwd, bwd
    v_fwd = pl.BlockSpec((eb, tb, width), lambda b, i: (b, i, 2))
    v_bwd = pl.BlockSpec((eb, tb, width), lambda b, i: (b, nb - 1 - i, 2))
    state_spec = pl.BlockSpec((eb, 2, ng, RW_LANES, RW_LANES), lambda b, i: (b, 0, 0, 0, 0))
    return pl.pallas_call(
        functools.partial(_rwkv_kernel, n_chunks=tb // RW_CHUNK, n_groups=ng, n_batch=eb),
        out_shape=(jax.ShapeDtypeStruct((bsz, n, width), jnp.float32),
                   jax.ShapeDtypeStruct((bsz, n, width), jnp.float32),
                   jax.ShapeDtypeStruct(s0.shape, jnp.float32)),
        grid=(bsz // eb, nb),
        in_specs=[pl.BlockSpec(masks.shape, lambda b, i: (0, 0, 0)),
                  pl.BlockSpec(wide.shape, lambda b, i: (0, 0, 0)),
                  pl.BlockSpec(tri.shape, lambda b, i: (0, 0, 0)),
                  r_fwd, r_bwd, v_fwd, v_bwd, fwd, bwd,
                  fwd, fwd, fwd, bwd, bwd, bwd, state_spec],
        out_specs=(fwd, bwd, state_spec),
        scratch_shapes=[pltpu.VMEM((eb, 2, ng, RW_LANES, RW_LANES), jnp.float32)],
        compiler_params=pltpu.CompilerParams(dimension_semantics=("parallel", "arbitrary"),
                                             vmem_limit_bytes=VMEM_LIMIT),
        name="rwkv_scan",
    )(masks, wide, tri, rkv, rkv, rkv, rkv, kk, kk, lw[0], kd[0], bvec[0], lw[1], kd[1], bvec[1], s0)


def _head_sum(t, hm):
    f32, bf16 = jnp.float32, jnp.bfloat16
    hi = t.astype(bf16)
    rem = t - hi.astype(f32)
    mid = rem.astype(bf16)
    lo = (rem - mid.astype(f32)).astype(bf16)
    return (jnp.dot(hi, hm, preferred_element_type=f32) + jnp.dot(mid, hm, preferred_element_type=f32)
            + jnp.dot(lo, hm, preferred_element_type=f32))


def _rwkv_prep_kernel(rkv_ref, lora_ref, vec_ref, w0_ref, a0_ref, w2_ref, a2_ref, hsum_ref,
                      kk_ref, lw0_ref, kd0_ref, b0_ref, lw1_ref, kd1_ref, b1_ref, bonus_ref):
    f32, bf16 = jnp.float32, jnp.bfloat16
    w = RW_WIDTH
    r = rkv_ref[:, 0:w]
    k = rkv_ref[:, w:2 * w]
    v = rkv_ref[:, 2 * w:3 * w]
    hm = hsum_ref[...]
    kk = k * vec_ref[0:1, :]
    kk = kk * lax.rsqrt(jnp.maximum(_head_sum(kk * kk, hm), 1e-24))
    kk_ref[...] = kk
    w_lo = jnp.tanh(lora_ref[:, 0:128]).astype(bf16)
    a_lo = lora_ref[:, 128:256].astype(bf16)
    rrk = r * vec_ref[2:3, :]
    bonus = None
    outs = ((lw0_ref, kd0_ref, b0_ref), (lw1_ref, kd1_ref, b1_ref))
    for d in range(2):
        x = w0_ref[d:d + 1, :] + jnp.dot(w_lo, w2_ref[d], preferred_element_type=f32)
        w_log = -(jnp.maximum(-x, 0.0) + jnp.log(1.0 + jnp.exp(-jnp.abs(x)))) - 0.5
        a = jax.nn.sigmoid(a0_ref[d:d + 1, :] + jnp.dot(a_lo, a2_ref[d], preferred_element_type=f32))
        kd = k * (1.0 + (a - 1.0) * vec_ref[1:2, :])
        outs[d][0][...] = -jnp.exp(w_log)
        outs[d][1][...] = kd
        outs[d][2][...] = kk * a
        term = _head_sum(rrk * kd, hm)
        bonus = term if bonus is None else bonus + term
    bonus_ref[...] = bonus * v


def rwkv_prep(rkv, lora, p):
    bsz, n, _ = rkv.shape
    rows = bsz * n
    tm = min(MIX_BLOCK_ROWS, n)
    w = RW_WIDTH
    hid = np.arange(w) // RW_HEAD
    hsum = jnp.asarray(hid[:, None] == hid[None, :], jnp.bfloat16)
    vec = jnp.stack([p['rw_kk'], p['rw_ka'], p['rw_rk']])
    zero = jnp.zeros((RW_DECAY_LORA, w), jnp.float32)
    w2 = jnp.stack([jnp.concatenate([p['rw_w2'][0], zero]), jnp.concatenate([zero, p['rw_w2'][1]])])
    a2 = jnp.stack([jnp.concatenate([p['rw_a2'][0], zero]), jnp.concatenate([zero, p['rw_a2'][1]])])
    consts = [vec, p['rw_w0'], p['rw_a0'], w2.astype(jnp.bfloat16), a2.astype(jnp.bfloat16), hsum]

    def const(a):
        return pl.BlockSpec(a.shape, lambda i: (0,) * a.ndim)

    outs = pl.pallas_call(
        _rwkv_prep_kernel,
        out_shape=tuple(jax.ShapeDtypeStruct((rows, w), jnp.float32) for _ in range(8)),
        grid=(rows // tm,),
        in_specs=[pl.BlockSpec((tm, 3 * w), lambda i: (i, 0)), pl.BlockSpec((tm, lora.shape[-1]), lambda i: (i, 0))]
                 + [const(c) for c in consts],
        out_specs=tuple(pl.BlockSpec((tm, w), lambda i: (i, 0)) for _ in range(8)),
        compiler_params=pltpu.CompilerParams(dimension_semantics=("parallel",), vmem_limit_bytes=VMEM_LIMIT),
        name="rwkv_prep",
    )(rkv.reshape(rows, 3 * w), lora.reshape(rows, -1), *consts)
    kk, lw0, kd0, b0, lw1, kd1, b1, bonus = [o.reshape(bsz, n, w) for o in outs]
    return kk, (lw0, lw1), (kd0, kd1), (b0, b1), bonus


def rwkv_branch(rkv, lora, p, s0, mask_set, block_tokens):
    kk, lw, kd, bvec, bonus = rwkv_prep(rkv, lora, p)
    y_f, y_b, s_fin = rwkv_scan_pallas(rkv, kk, lw, kd, bvec, s0, mask_set, block_tokens)
    return (y_f, y_b, bonus, lora[..., IN_LORA - RW_GATE_LORA:]), s_fin


def hyena_filters(n_tok, p):
    hp = lax.Precision.HIGHEST
    bands = (HY_POS_DIM - 1) // 2
    t = jnp.linspace(0.0, 1.0, n_tok, dtype=jnp.float32)[:, None]
    w = (2.0 * math.pi / n_tok) * jnp.arange(n_tok, dtype=jnp.float32)[:, None]
    f = jnp.linspace(1e-4, bands - 1, bands, dtype=jnp.float32)[None, :]
    feats = jnp.concatenate([t, jnp.cos(f * w), -jnp.sin(f * w)], axis=-1)
    h = jnp.sin(p['hy_f_freq1'] * (jnp.dot(feats, p['hy_f_w1'], precision=hp) + p['hy_f_b1']))
    h = jnp.sin(p['hy_f_freq2'] * (jnp.dot(h, p['hy_f_w2'], precision=hp) + p['hy_f_b2']))
    h = jnp.dot(h, p['hy_f_w3'], precision=hp).reshape(n_tok, HY_ORDER, 2, HY_WIDTH)
    rates = jnp.abs(jnp.linspace(math.log(HY_DECAY_TARGET) / HY_DECAY_PCT_SHORT,
                                 math.log(HY_DECAY_TARGET) / HY_DECAY_PCT_LONG, HY_WIDTH, dtype=jnp.float32))
    h = h * jnp.exp(-t * rates)[:, None, None, :]
    return h.at[0, :, 1].set(0.0)


HY_N = 2 * SEQ
HY_N1 = 64
HY_N2 = 128
HY_NH = HY_N1 // 2
HY_KH = HY_N1 // 2 + 1
HY_KP = 40
HY_PITCH = 88
HY_LANES = 128


def hyena_dft_tables():
    k1 = np.arange(HY_KH)[:, None]
    n1 = np.arange(HY_NH)[None, :]
    n2 = np.arange(HY_N2)
    ph = -2 * np.pi * (k1 * n1 / HY_N1)[None] - 2 * np.pi * (n2[:, None, None] * k1[None] / HY_N)
    lhs1 = np.zeros((HY_N2, 2 * HY_KP, HY_NH))
    lhs1[:, :HY_KH] = np.cos(ph)
    lhs1[:, HY_KP:HY_KP + HY_KH] = np.sin(ph)
    wgt = np.where((k1 == 0) | (k1 == HY_N1 // 2), 1.0, 2.0)[None] / HY_N
    lhs2 = np.zeros((HY_N2, HY_NH, 2 * HY_KP))
    lhs2[:, :, :HY_KH] = (wgt * np.cos(ph)).transpose(0, 2, 1)
    lhs2[:, :, HY_KP:HY_KP + HY_KH] = (wgt * np.sin(ph)).transpose(0, 2, 1)
    kk = np.arange(HY_N2)
    ang = -2 * np.pi * np.outer(kk, kk) / HY_N2
    cr, ci = np.cos(ang), np.sin(ang)
    f_fwd = np.block([[cr, -ci], [ci, cr]])
    f_inv = np.block([[cr, ci], [-ci, cr]])

    def as_bf(a):
        return jnp.asarray(a, jnp.float32).astype(jnp.bfloat16)

    def low_bf(a):
        a = jnp.asarray(a, jnp.float32)
        return (a - a.astype(jnp.bfloat16).astype(jnp.float32)).astype(jnp.bfloat16)

    return as_bf(lhs1), as_bf(lhs2), as_bf(f_fwd), as_bf(f_inv), low_bf(lhs1), low_bf(f_fwd)


def _hy_spec_kernel(h_ref, l1h_ref, l1l_ref, ffh_ref, ffl_ref, o_ref, a_ref):
    f32, bf16 = jnp.float32, jnp.bfloat16
    n2n, nh, pitch, kp = HY_N2, HY_NH, HY_PITCH, HY_KP

    def dot3(a_hi, a_lo, x):
        x_hi = x.astype(bf16)
        x_lo = (x - x_hi.astype(f32)).astype(bf16)
        return (jnp.dot(a_hi, x_hi, preferred_element_type=f32) + jnp.dot(a_hi, x_lo, preferred_element_type=f32)
                + jnp.dot(a_lo, x_hi, preferred_element_type=f32))

    def stage1(q, c):
        slab = h_ref[pl.ds(pl.multiple_of(q * nh, nh), nh), :]
        a_ref[pl.ds(pl.multiple_of(q * pitch, 8), 2 * kp), :] = dot3(l1h_ref[q], l1l_ref[q], slab)
        return c

    lax.fori_loop(0, n2n, stage1, 0, unroll=8)

    def stage2(k, c):
        a = jnp.concatenate([a_ref[pl.ds(k, n2n, stride=pitch), :],
                             a_ref[pl.ds(kp + k, n2n, stride=pitch), :]], axis=0)
        o_ref[0, pl.ds(pl.multiple_of(k * 2 * n2n, 2 * n2n), 2 * n2n), :] = dot3(ffh_ref[...], ffl_ref[...], a)
        return c

    lax.fori_loop(0, HY_KH, stage2, 0, unroll=3)


def hyena_filter_spectra_long(h, tables):
    n, orders, _, w = h.shape
    l1, _, ff, _, l1_lo, ff_lo = tables
    ch = orders * 2 * w
    sig = jnp.transpose(h.reshape(HY_NH, HY_N2, ch), (1, 0, 2)).reshape(n, ch)
    rows = HY_KH * 2 * HY_N2

    def const(a):
        return pl.BlockSpec(a.shape, lambda t: (0,) * a.ndim)

    spec = pl.pallas_call(
        _hy_spec_kernel,
        out_shape=jax.ShapeDtypeStruct((ch // HY_LANES, rows, HY_LANES), jnp.float32),
        grid=(ch // HY_LANES,),
        in_specs=[pl.BlockSpec((n, HY_LANES), lambda t: (0, t)), const(l1), const(l1_lo), const(ff), const(ff_lo)],
        out_specs=pl.BlockSpec((1, rows, HY_LANES), lambda t: (t, 0, 0)),
        scratch_shapes=[pltpu.VMEM((HY_N2 * HY_PITCH, HY_LANES), jnp.float32)],
        compiler_params=pltpu.CompilerParams(dimension_semantics=("parallel",), vmem_limit_bytes=VMEM_LIMIT),
        name="hyena_filter_spectrum",
    )(sig, l1, l1_lo, ff, ff_lo)
    spec = spec.reshape(orders, 2, w // HY_LANES, HY_KH, 2, HY_N2, HY_LANES)
    sign = jnp.asarray([1.0, -1.0], jnp.float32)[:, None, None]
    both = spec[:, 0] + sign * spec[:, 1]
    return both.reshape(orders, w // HY_LANES, rows, HY_LANES)


def _hy_kernel(z_ref, g_ref, k_ref, l1_ref, l2_ref, ff_ref, fi_ref, bias_ref, o_ref, a_ref):
    f32, bf16 = jnp.float32, jnp.bfloat16
    n2n, nh, pitch, kp = HY_N2, HY_NH, HY_PITCH, HY_KP

    def stage1(q, c):
        slab = z_ref[0, pl.ds(pl.multiple_of(q * nh, nh), nh), :].astype(bf16)
        a_ref[pl.ds(pl.multiple_of(q * pitch, 8), 2 * kp), :] = jnp.dot(l1_ref[q], slab, preferred_element_type=f32)
        return c

    lax.fori_loop(0, n2n, stage1, 0, unroll=8)

    def stage2(k, c):
        a = jnp.concatenate([a_ref[pl.ds(k, n2n, stride=pitch), :],
                             a_ref[pl.ds(kp + k, n2n, stride=pitch), :]], axis=0).astype(bf16)
        x = jnp.dot(ff_ref[...], a, preferred_element_type=f32)
        xr, xi = x[:n2n], x[n2n:]
        base = pl.multiple_of(k * 2 * n2n, 2 * n2n)
        kr = k_ref[0, pl.ds(base, n2n), :]
        ki = k_ref[0, pl.ds(base + n2n, n2n), :]
        y = jnp.concatenate([xr * kr - xi * ki, xr * ki + xi * kr], axis=0).astype(bf16)
        b = jnp.dot(fi_ref[...], y, preferred_element_type=f32)
        a_ref[pl.ds(k, n2n, stride=pitch), :] = b[:n2n]
        a_ref[pl.ds(kp + k, n2n, stride=pitch), :] = b[n2n:]
        return c

    lax.fori_loop(0, HY_KH, stage2, 0, unroll=3)

    def stage3(q, c):
        blk = a_ref[pl.ds(pl.multiple_of(q * pitch, 8), 2 * kp), :].astype(bf16)
        y = jnp.dot(l2_ref[q], blk, preferred_element_type=f32)
        rows = pl.ds(pl.multiple_of(q * nh, nh), nh)
        o_ref[0, rows, :] = g_ref[0, rows, :] * (y + z_ref[0, rows, :] * bias_ref[...])
        return c

    lax.fori_loop(0, n2n, stage3, 0, unroll=8)


def hyena_long_conv_gated(zt, gt, spec, bias, tables):
    bsz, n, w = zt.shape
    l1, l2, ff, fi = tables[:4]
    nt = w // HY_LANES
    tok = pl.BlockSpec((1, n, HY_LANES), lambda t, b: (b, 0, t))
    return pl.pallas_call(
        _hy_kernel,
        out_shape=jax.ShapeDtypeStruct((bsz, n, w), jnp.float32),
        grid=(nt, bsz),
        in_specs=[tok, tok,
                  pl.BlockSpec((1,) + spec.shape[1:], lambda t, b: (t, 0, 0)),
                  pl.BlockSpec(l1.shape, lambda t, b: (0, 0, 0)),
                  pl.BlockSpec(l2.shape, lambda t, b: (0, 0, 0)),
                  pl.BlockSpec(ff.shape, lambda t, b: (0, 0)),
                  pl.BlockSpec(fi.shape, lambda t, b: (0, 0)),
                  pl.BlockSpec((1, HY_LANES), lambda t, b: (0, t))],
        out_specs=tok,
        scratch_shapes=[pltpu.VMEM((HY_N2 * HY_PITCH, HY_LANES), jnp.float32)],
        compiler_params=pltpu.CompilerParams(dimension_semantics=("parallel", "parallel"),
                                             vmem_limit_bytes=VMEM_LIMIT),
        name="hyena_conv",
    )(zt, gt, spec, l1, l2, ff, fi, bias.reshape(1, w))


def hyena_branch_long(streams, p):
    bsz, n, w3 = streams.shape
    tables = hyena_dft_tables()
    spec = hyena_filter_spectra_long(hyena_filters(n, p), tables)
    st = jnp.transpose(streams.reshape(bsz, HY_NH, HY_N2, w3), (0, 2, 1, 3)).reshape(bsz, n, w3)
    z, x1, x2 = jnp.split(st, 3, axis=-1)
    for o, gate in enumerate((x1, x2)):
        z = hyena_long_conv_gated(z, gate, spec[o], p['hy_bias'][o], tables)
    w = z.shape[-1]
    return jnp.transpose(z.reshape(bsz, HY_N2, HY_NH, w), (0, 2, 1, 3)).reshape(bsz, n, w)


def _hy_ctx_kernel(z_ref, g_ref, k_ref, fd_ref, fi_ref, bias_ref, o_ref):
    f32, bf16 = jnp.float32, jnp.bfloat16
    z = z_ref[0]
    n = fd_ref.shape[0] // 2
    x = jnp.dot(fd_ref[...], z.astype(bf16), preferred_element_type=f32)
    xr, xi = x[:n], x[n:]
    kr, ki = k_ref[0, :n, :], k_ref[0, n:, :]
    y = jnp.concatenate([xr * kr - xi * ki, xr * ki + xi * kr], axis=0).astype(bf16)
    conv = jnp.dot(fi_ref[...], y, preferred_element_type=f32)
    o_ref[0] = g_ref[0] * (conv + z * bias_ref[...])


def hyena_branch_ctx(streams, p):
    bsz, n, w3 = streams.shape
    w = w3 // 3
    nn = 2 * n
    ang = 2 * np.pi * np.outer(np.arange(nn), np.arange(n)) / nn
    h = hyena_filters(n, p)
    hp = lax.Precision.HIGHEST
    spec = jnp.concatenate([
        jnp.einsum('kt,tow->kow', jnp.asarray(np.cos(ang), jnp.float32), h[:, :, 0] + h[:, :, 1], precision=hp),
        jnp.einsum('kt,tow->kow', jnp.asarray(-np.sin(ang), jnp.float32), h[:, :, 0] - h[:, :, 1], precision=hp)],
        axis=0)
    fd = jnp.asarray(np.concatenate([np.cos(ang), -np.sin(ang)], axis=0), jnp.float32).astype(jnp.bfloat16)
    fi = jnp.asarray(np.concatenate([np.cos(ang).T, -np.sin(ang).T], axis=1) / nn, jnp.float32).astype(jnp.bfloat16)
    nt = w // HY_LANES
    tok = pl.BlockSpec((1, n, HY_LANES), lambda t, b: (b, 0, t))
    z, x1, x2 = jnp.split(streams, 3, axis=-1)
    for o, gate in enumerate((x1, x2)):
        sp = jnp.transpose(spec[:, o].reshape(2 * nn, nt, HY_LANES), (1, 0, 2)).astype(jnp.float32)
        z = pl.pallas_call(
            _hy_ctx_kernel,
            out_shape=jax.ShapeDtypeStruct((bsz, n, w), jnp.float32),
            grid=(nt, bsz),
            in_specs=[tok, tok,
                      pl.BlockSpec((1, 2 * nn, HY_LANES), lambda t, b: (t, 0, 0)),
                      pl.BlockSpec(fd.shape, lambda t, b: (0, 0)),
                      pl.BlockSpec(fi.shape, lambda t, b: (0, 0)),
                      pl.BlockSpec((1, HY_LANES), lambda t, b: (0, t))],
            out_specs=tok,
            compiler_params=pltpu.CompilerParams(dimension_semantics=("parallel", "parallel"),
                                                 vmem_limit_bytes=VMEM_LIMIT),
            name="hyena_conv_ctx",
        )(z, gate, sp, fd, fi, p['hy_bias'][o].reshape(1, w))
    return z


def _rms(x, gain):
    return x * lax.rsqrt(jnp.mean(x * x, axis=-1, keepdims=True) + NORM_EPS) * gain


def _conv3(x, w_ref, period):
    rows = x.shape[0]
    pos = lax.broadcasted_iota(jnp.int32, (rows, 1), 0) % period
    prev = jnp.where(pos == 0, 0.0, pltpu.roll(x, 1, axis=0))
    nxt = jnp.where(pos == period - 1, 0.0, pltpu.roll(x, rows - 1, axis=0))
    return prev * w_ref[0:1, :] + x * w_ref[1:2, :] + nxt * w_ref[2:3, :]


def _inproj_kernel(x_ref, mod_ref, gain_ref, ws5_ref, wrw_ref, wlo_ref, why_ref, wgt_ref, cwr_ref, cwh_ref, cbh_ref,
                   os5_ref, orw_ref, olo_ref, ohy_ref, ogt_ref, *, period):
    f32 = jnp.float32
    shift, scale = mod_ref[0, 0:1, :], mod_ref[0, 1:2, :]
    h = (_rms(x_ref[...], gain_ref[...]) * (1.0 + scale) + shift).astype(jnp.bfloat16)
    os5_ref[...] = jnp.dot(h, ws5_ref[...], preferred_element_type=f32)
    olo_ref[...] = jnp.dot(h, wlo_ref[...], preferred_element_type=f32)
    ogt_ref[...] = jnp.dot(h, wgt_ref[...], preferred_element_type=f32)
    orw_ref[...] = _conv3(jnp.dot(h, wrw_ref[...], preferred_element_type=f32), cwr_ref, period)
    ohy_ref[...] = _conv3(jnp.dot(h, why_ref[...], preferred_element_type=f32), cwh_ref, period) + cbh_ref[...]


def input_projection(x, mod, gain, w_parts, rw_conv_w, hy_conv_w, hy_conv_b, period):
    bsz, n, d = x.shape
    tm = min(MIX_BLOCK_ROWS, n)
    per_b = n // tm
    rows = bsz * n
    bmap = (lambda i: (i // per_b, 0, 0)) if mod.shape[0] == bsz else (lambda i: (0, 0, 0))
    wb = [w.astype(jnp.bfloat16) for w in w_parts]

    def const(a):
        return pl.BlockSpec(a.shape, lambda i: (0,) * a.ndim)

    cbh = hy_conv_b.reshape(1, -1)
    outs = pl.pallas_call(
        functools.partial(_inproj_kernel, period=period),
        out_shape=tuple(jax.ShapeDtypeStruct((rows, w.shape[1]), jnp.float32) for w in wb),
        grid=(rows // tm,),
        in_specs=[pl.BlockSpec((tm, d), lambda i: (i, 0)),
                  pl.BlockSpec((1, 3, d), bmap),
                  pl.BlockSpec((1, d), lambda i: (0, 0))]
                 + [const(w) for w in wb] + [const(rw_conv_w), const(hy_conv_w), const(cbh)],
        out_specs=tuple(pl.BlockSpec((tm, w.shape[1]), lambda i: (i, 0)) for w in wb),
        compiler_params=pltpu.CompilerParams(dimension_semantics=("parallel",), vmem_limit_bytes=VMEM_LIMIT),
        name="input_projection",
    )(x.reshape(rows, d), mod, gain.reshape(1, d), *wb, rw_conv_w, hy_conv_w, cbh)
    return tuple(o.reshape(bsz, n, o.shape[-1]) for o in outs)


def _merge_kernel(x_ref, mod_ref, gain_ref, ys5_ref, us5_ref, wf_ref, wb_ref, bonus_ref, glo_ref, hy_ref, gt_ref,
                  s5d_ref, glu_ref, lnw_ref, lnb_ref, g2_ref, havg_ref, brs_ref, brr_ref, brh_ref, ow_ref, o_ref):
    f32, bf16 = jnp.float32, jnp.bfloat16
    y = ys5_ref[...] + s5d_ref[...] * us5_ref[...]
    y = 0.5 * y * (1.0 + lax.erf(y * (1.0 / math.sqrt(2.0))))
    s5 = y * jax.nn.sigmoid(jnp.dot(y.astype(bf16), glu_ref[...], preferred_element_type=f32))
    wkv = wf_ref[...] + wb_ref[...]

    def head_mean(t):
        hi = t.astype(bf16)
        rem = t - hi.astype(f32)
        mid = rem.astype(bf16)
        lo = (rem - mid.astype(f32)).astype(bf16)
        hm = havg_ref[...]
        tot = (jnp.dot(hi, hm, preferred_element_type=f32) + jnp.dot(mid, hm, preferred_element_type=f32)
               + jnp.dot(lo, hm, preferred_element_type=f32))
        return tot * (1.0 / RW_HEAD)

    mu = head_mean(wkv)
    cen = wkv - mu
    var = head_mean(cen * cen)
    o = cen * lax.rsqrt(var + RW_GN_EPS) * lnw_ref[...] + lnb_ref[...] + bonus_ref[...]
    g = jnp.dot(jax.nn.sigmoid(glo_ref[...]).astype(bf16), g2_ref[...], preferred_element_type=f32)
    rw = o * g
    gt = jax.nn.sigmoid(gt_ref[...])
    d = x_ref.shape[-1]
    m = (gt[:, 0:d] * jnp.dot(s5.astype(bf16), brs_ref[...], preferred_element_type=f32)
         + gt[:, d:2 * d] * jnp.dot(rw.astype(bf16), brr_ref[...], preferred_element_type=f32)
         + gt[:, 2 * d:3 * d] * jnp.dot(hy_ref[...].astype(bf16), brh_ref[...], preferred_element_type=f32))
    yl = jnp.dot(m.astype(bf16), ow_ref[...], preferred_element_type=f32)
    o_ref[...] = x_ref[...] + mod_ref[0, 2:3, :] * _rms(yl, gain_ref[...])


def merge_block(x, mod, gain, ys5, us5, rw_parts, hy, gates, p):
    bsz, n, d = x.shape
    tm = min(MIX_BLOCK_ROWS, n)
    per_b = n // tm
    rows = bsz * n
    bmap = (lambda i: (i // per_b, 0, 0)) if mod.shape[0] == bsz else (lambda i: (0, 0, 0))

    def bf(a):
        return a.astype(jnp.bfloat16)

    def row(a):
        return pl.BlockSpec((tm, a.shape[-1]), lambda i: (i, 0))

    def const(a):
        return pl.BlockSpec(a.shape, lambda i: (0,) * a.ndim)

    hid = np.arange(RW_WIDTH) // RW_HEAD
    havg = jnp.asarray(hid[:, None] == hid[None, :], jnp.bfloat16)
    acts = [a.reshape(rows, a.shape[-1]) for a in (ys5, us5) + tuple(rw_parts) + (hy, gates)]
    consts = [p['s5_d'].reshape(1, -1), bf(p['s5_glu_w']), p['rw_ln_w'].reshape(1, -1), p['rw_ln_b'].reshape(1, -1),
              bf(p['rw_g2']), havg, bf(p['br_s5']), bf(p['br_rw']), bf(p['br_hy']), bf(p['out_w'])]
    out = pl.pallas_call(
        _merge_kernel,
        out_shape=jax.ShapeDtypeStruct((rows, d), jnp.float32),
        grid=(rows // tm,),
        in_specs=[pl.BlockSpec((tm, d), lambda i: (i, 0)), pl.BlockSpec((1, 3, d), bmap),
                  pl.BlockSpec((1, d), lambda i: (0, 0))] + [row(a) for a in acts] + [const(c) for c in consts],
        out_specs=pl.BlockSpec((tm, d), lambda i: (i, 0)),
        compiler_params=pltpu.CompilerParams(dimension_semantics=("parallel",), vmem_limit_bytes=VMEM_LIMIT),
        name="merge_block",
    )(x.reshape(rows, d), mod, gain.reshape(1, d), *acts, *consts)
    return out.reshape(bsz, n, d)


def token_mixer(x, ctx, mod_l, mod_c, g_pre, g_post, p, with_ctx_out):
    bsz = x.shape[0]
    cuts = [IN_S5, IN_S5 + IN_RW, IN_S5 + IN_RW + IN_LORA, IN_S5 + IN_RW + IN_LORA + IN_HY]
    edges = [0] + cuts + [IN_COLS]
    w_parts = [p['in_w'][:, edges[i]:edges[i + 1]] for i in range(5)]
    conv = (p['rw_conv_w'], p['hy_conv_w'], p['hy_conv_b'])
    u_l, rkv_l, lora_l, hy_l, gate_l = input_projection(x, mod_l, g_pre, w_parts, *conv, GRID_W)
    u_c, rkv_c, lora_c, hy_c, gate_c = input_projection(ctx, mod_c, g_pre, w_parts, *conv, CTX_LEN)
    mask_set = rwkv_masks()
    zr = jnp.zeros((bsz, 2, RW_WIDTH // RW_LANES, RW_LANES, RW_LANES), jnp.float32)
    ys5_l, ys5_c = s5_branch(u_l, u_c, p)
    rw_c, rw_state = rwkv_branch(rkv_c, lora_c, p, zr, mask_set, CTX_LEN)
    rw_l, _ = rwkv_branch(rkv_l, lora_l, p, rw_state, mask_set, RW_BLOCK_TOKENS)
    hy_lat = hyena_branch_long(hy_l, p)
    x = merge_block(x, mod_l, g_post, ys5_l, u_l, rw_l, hy_lat, gate_l, p)
    if with_ctx_out:
        ctx = merge_block(ctx, mod_c, g_post, ys5_c, u_c, rw_c, hyena_branch_ctx(hy_c, p), gate_c, p)
    return x, ctx


def _ffn_kernel(x_ref, mod_ref, gain_ref, wg_ref, wu_ref, wd_ref, o_ref):
    x = x_ref[...]
    shift, scale, gate = mod_ref[0, 0:1, :], mod_ref[0, 1:2, :], mod_ref[0, 2:3, :]
    h = (_rms(x, gain_ref[0:1, :]) * (1.0 + scale) + shift).astype(jnp.bfloat16)
    g = jnp.dot(h, wg_ref[...], preferred_element_type=jnp.float32)
    u = jnp.dot(h, wu_ref[...], preferred_element_type=jnp.float32)
    a = (g * jax.nn.sigmoid(g) * u).astype(jnp.bfloat16)
    y = jnp.dot(a, wd_ref[...], preferred_element_type=jnp.float32)
    o_ref[...] = x + gate * _rms(y, gain_ref[1:2, :])


def ffn_block(x, mod, gains, wg, wu, wd):
    bsz, n, d = x.shape
    hid = wg.shape[1]
    tm = min(FFN_BLOCK_ROWS, n)
    per_b = n // tm
    xf = x.reshape(bsz * n, d)
    bmap = (lambda i: (i // per_b, 0, 0)) if mod.shape[0] == bsz else (lambda i: (0, 0, 0))
    out = pl.pallas_call(
        _ffn_kernel,
        out_shape=jax.ShapeDtypeStruct((bsz * n, d), jnp.float32),
        grid=(bsz * per_b,),
        in_specs=[pl.BlockSpec((tm, d), lambda i: (i, 0)),
                  pl.BlockSpec((1, 3, d), bmap),
                  pl.BlockSpec((2, d), lambda i: (0, 0)),
                  pl.BlockSpec((d, hid), lambda i: (0, 0)),
                  pl.BlockSpec((d, hid), lambda i: (0, 0)),
                  pl.BlockSpec((hid, d), lambda i: (0, 0))],
        out_specs=pl.BlockSpec((tm, d), lambda i: (i, 0)),
        compiler_params=pltpu.CompilerParams(dimension_semantics=("parallel",), vmem_limit_bytes=VMEM_LIMIT),
        name="ffn_block",
    )(xf, mod, gains, wg.astype(jnp.bfloat16), wu.astype(jnp.bfloat16), wd.astype(jnp.bfloat16))
    return out.reshape(bsz, n, d)


def _moe_kernel(be_ref, x_ref, gate_ref, wg_ref, wu_ref, wd_ref, o_ref):
    j = pl.program_id(1)
    xb = x_ref[...].astype(jnp.bfloat16)
    g = jnp.dot(xb, wg_ref[0], preferred_element_type=jnp.float32)
    u = jnp.dot(xb, wu_ref[0], preferred_element_type=jnp.float32)
    a = (g * jax.nn.sigmoid(g) * u).astype(jnp.bfloat16)
    y = jnp.dot(a, wd_ref[0], preferred_element_type=jnp.float32)

    @pl.when(j == 0)
    def _():
        o_ref[...] = y

    @pl.when(j > 0)
    def _():
        o_ref[...] += y

    @pl.when(j == pl.num_programs(1) - 1)
    def _():
        o_ref[...] *= gate_ref[...]


def moe_expert_blocks(xb, slot_gate, block_e, wg, wu, wd):
    cap, d = xb.shape
    n_blocks = cap // MOE_BLOCK
    hid = wg.shape[-1]
    nh = MOE_HIDDEN_TILES
    th = hid // nh

    def hidx(i, j):
        return j + (i % 2) * (nh - 1 - 2 * j)

    return pl.pallas_call(
        _moe_kernel,
        out_shape=jax.ShapeDtypeStruct((cap, d), jnp.float32),
        grid_spec=pltpu.PrefetchScalarGridSpec(
            num_scalar_prefetch=1,
            grid=(n_blocks, nh),
            in_specs=[pl.BlockSpec((MOE_BLOCK, d), lambda i, j, be: (i, 0)),
                      pl.BlockSpec((MOE_BLOCK, 1), lambda i, j, be: (i, 0)),
                      pl.BlockSpec((1, d, th), lambda i, j, be: (be[i], 0, hidx(i, j))),
                      pl.BlockSpec((1, d, th), lambda i, j, be: (be[i], 0, hidx(i, j))),
                      pl.BlockSpec((1, th, d), lambda i, j, be: (be[i], hidx(i, j), 0))],
            out_specs=pl.BlockSpec((MOE_BLOCK, d), lambda i, j, be: (i, 0))),
        compiler_params=pltpu.CompilerParams(dimension_semantics=("parallel", "arbitrary"),
                                             vmem_limit_bytes=VMEM_LIMIT),
        name="moe_experts",
    )(block_e, xb, slot_gate.reshape(cap, 1), wg.astype(jnp.bfloat16), wu.astype(jnp.bfloat16),
      wd.astype(jnp.bfloat16))


def moe_swiglu(h, router_w, wg, wu, wd):
    d_model = h.shape[-1]
    tok = h.reshape(-1, d_model)
    n = tok.shape[0]
    n_assign = n * TOP_K
    logits = jnp.dot(tok, router_w)
    top_logit, top_e = lax.top_k(logits, TOP_K)
    gate = jax.nn.softmax(top_logit, axis=-1).reshape(-1)
    flat_e = top_e.reshape(-1)
    order = jnp.argsort(flat_e)
    onehot = (flat_e[:, None] == jnp.arange(N_EXPERTS, dtype=flat_e.dtype)[None, :]).astype(jnp.int32)
    sizes = jnp.sum(onehot, axis=0)
    padded = (sizes + MOE_BLOCK - 1) // MOE_BLOCK * MOE_BLOCK
    pad_end = jnp.cumsum(padded)
    pad_start = pad_end - padded
    grp_start = jnp.cumsum(sizes) - sizes
    n_blocks = -(-n_assign // MOE_BLOCK) + N_EXPERTS
    cap = n_blocks * MOE_BLOCK
    block_start = jnp.arange(n_blocks, dtype=jnp.int32) * MOE_BLOCK
    block_e = jnp.minimum(jnp.sum(block_start[:, None] >= pad_end[None, :], axis=1), N_EXPERTS - 1)
    slot_e = jnp.repeat(block_e, MOE_BLOCK)
    within = jnp.arange(cap, dtype=jnp.int32) - pad_start[slot_e]
    valid = within < sizes[slot_e]
    assign = order[jnp.clip(grp_start[slot_e] + within, 0, n_assign - 1)]
    slot_tok = jnp.where(valid, (assign // TOP_K).astype(jnp.int32), 0)
    slot_gate = jnp.where(valid, gate[assign], 0.0)
    xb = tok[slot_tok]
    yb = moe_expert_blocks(xb, slot_gate, block_e.astype(jnp.int32), wg, wu, wd)
    rank = jnp.take_along_axis(jnp.cumsum(onehot, axis=0), flat_e[:, None], axis=1)[:, 0] - 1
    pos = (pad_start[flat_e] + rank).astype(jnp.int32).reshape(n, TOP_K)
    out = yb[pos[:, 0]]
    for kk in range(1, TOP_K):
        out = out + yb[pos[:, kk]]
    return out.reshape(h.shape)


def kernel(x, c, ctx, c_ctx, mod_w, mod_b, norm_g, in_w, s5_lam_re, s5_lam_im, s5_log_step, s5_b_re, s5_b_im, s5_c_re, s5_c_im, s5_d, s5_glu_w, rw_conv_w, rw_w0, rw_w2, rw_a0, rw_a2, rw_g2, rw_kk, rw_ka, rw_rk, rw_ln_w, rw_ln_b, hy_conv_w, hy_conv_b, hy_f_w1, hy_f_b1, hy_f_freq1, hy_f_w2, hy_f_b2, hy_f_freq2, hy_f_w3, hy_bias, br_s5, br_rw, br_hy, out_w, ffn_wg, ffn_wu, ffn_wd, moe_router, moe_wg, moe_wu, moe_wd):
    silu_c = jax.nn.silu(c)
    silu_cc = jax.nn.silu(c_ctx)
    for i in range(DEPTH):
        last = i == DEPTH - 1
        p = {
            'in_w': in_w[i],
            's5_lam_re': s5_lam_re[i], 's5_lam_im': s5_lam_im[i], 's5_log_step': s5_log_step[i],
            's5_b_re': s5_b_re[i], 's5_b_im': s5_b_im[i], 's5_c_re': s5_c_re[i], 's5_c_im': s5_c_im[i],
            's5_d': s5_d[i], 's5_glu_w': s5_glu_w[i],
            'rw_conv_w': rw_conv_w[i], 'rw_w0': rw_w0[i], 'rw_w2': rw_w2[i], 'rw_a0': rw_a0[i], 'rw_a2': rw_a2[i],
            'rw_g2': rw_g2[i], 'rw_kk': rw_kk[i], 'rw_ka': rw_ka[i], 'rw_rk': rw_rk[i],
            'rw_ln_w': rw_ln_w[i], 'rw_ln_b': rw_ln_b[i],
            'hy_conv_w': hy_conv_w[i], 'hy_conv_b': hy_conv_b[i], 'hy_f_w1': hy_f_w1[i], 'hy_f_b1': hy_f_b1[i],
            'hy_f_freq1': hy_f_freq1[i], 'hy_f_w2': hy_f_w2[i], 'hy_f_b2': hy_f_b2[i], 'hy_f_freq2': hy_f_freq2[i],
            'hy_f_w3': hy_f_w3[i], 'hy_bias': hy_bias[i],
            'br_s5': br_s5[i], 'br_rw': br_rw[i], 'br_hy': br_hy[i], 'out_w': out_w[i],
        }
        ml = jnp.split((silu_c @ mod_w[i] + mod_b[i])[:, None, :], 6, axis=-1)
        mc = jnp.split(silu_cc @ mod_w[i] + mod_b[i], 6, axis=-1)
        g_pre_m, g_post_m, g_pre_f, g_post_f = norm_g[i]
        if i % 2 == 0:
            def channel_mix(t, m, j=i // 2):
                return ffn_block(t, m, norm_g[i, 2:4], ffn_wg[j], ffn_wu[j], ffn_wd[j])
        else:
            def channel_mix(t, m, j=i // 2):
                h = rms_norm(t, g_pre_f) * (1 + m[:, 1:2]) + m[:, 0:1]
                y = moe_swiglu(h, moe_router[j], moe_wg[j], moe_wu[j], moe_wd[j])
                return t + m[:, 2:3] * rms_norm(y, g_post_f)
        x, ctx = token_mixer(x, ctx, jnp.concatenate(ml[0:3], axis=1), jnp.stack(mc[0:3])[None],
                             g_pre_m, g_post_m, p, not last)
        x = channel_mix(x, jnp.concatenate(ml[3:6], axis=1))
        if not last:
            ctx = channel_mix(ctx, jnp.stack(mc[3:6])[None])
    return x
```

```python
import math
import functools
import numpy as np
import jax
import jax.numpy as jnp
from jax import lax
from jax.experimental import pallas as pl
from jax.experimental.pallas import tpu as pltpu

D_MODEL = 1024
BATCH = 8
SEQ = 4096
DEPTH = 2
GRID_W = 64
CTX_LEN = 256
NORM_EPS = 1e-6
S5_WIDTH = D_MODEL // 4
S5_GROUP = 16
S5_GROUPS = S5_WIDTH // S5_GROUP
S5_STATE = 64
RW_WIDTH = D_MODEL // 2
RW_HEAD = 64
RW_HEADS = RW_WIDTH // RW_HEAD
RW_DECAY_LORA = 64
RW_ICLR_LORA = 64
RW_GATE_LORA = 128
RW_GN_EPS = 64e-5
HY_WIDTH = D_MODEL // 4
HY_ORDER = 2
HY_POS_DIM = 33
HY_FILTER_HIDDEN = 64
HY_DECAY_TARGET = 1e-2
HY_DECAY_PCT_SHORT = 0.3
HY_DECAY_PCT_LONG = 1.5
SHORT_CONV = 3
N_BRANCH = 3
FFN_HIDDEN = 2816
N_EXPERTS = 8
TOP_K = 2
EXPERT_HIDDEN = 3584
MOE_BLOCK = 512
IN_S5 = S5_WIDTH
IN_RW = 3 * RW_WIDTH
IN_LORA = 2 * RW_DECAY_LORA + 2 * RW_ICLR_LORA + RW_GATE_LORA
IN_HY = (HY_ORDER + 1) * HY_WIDTH
IN_GATE = N_BRANCH * D_MODEL
IN_COLS = IN_S5 + IN_RW + IN_LORA + IN_HY + IN_GATE

VMEM_LIMIT = 56 * 1024 * 1024
S5_CHUNK = 64
RW_CHUNK = 64
RW_HG = 4
RW_LANES = RW_HG * RW_HEAD
RW_BLOCK_TOKENS = 256
RW_BATCH_PER_STEP = 2
MOE_HIDDEN_TILES = 2
FFN_BLOCK_ROWS = 256
MIX_BLOCK_ROWS = 256


def rms_norm(x, gain):
    y = x * lax.rsqrt(jnp.mean(x * x, axis=-1, keepdims=True) + NORM_EPS)
    return y * gain


def s5_chunk_operators(lam_re, lam_im, log_step, b_re, b_im, c_re, c_im):
    T, G, P, GC = S5_CHUNK, S5_GROUPS, S5_STATE, S5_GROUP
    hp = lax.Precision.HIGHEST
    step = jnp.exp(log_step)[..., None]
    lr, li = lam_re, lam_im
    tau = jnp.arange(T + 1, dtype=jnp.float32)[:, None, None, None]
    mag = jnp.exp(lr * step * tau)
    ang = li * step * tau
    e_re, e_im = mag * jnp.cos(ang), mag * jnp.sin(ang)
    ab_re, ab_im = e_re[1], e_im[1]
    den = lr * lr + li * li
    nr, ni = ab_re - 1.0, ab_im
    q_re = (nr * lr + ni * li) / den
    q_im = (ni * lr - nr * li) / den
    bb_re = q_re[..., None] * b_re - q_im[..., None] * b_im
    bb_im = q_re[..., None] * b_im + q_im[..., None] * b_re
    eb_re = e_re[..., None] * bb_re - e_im[..., None] * bb_im
    eb_im = e_re[..., None] * bb_im + e_im[..., None] * bb_re
    ktau = (jnp.einsum('dgip,tdgpj->tdgij', c_re, eb_re[:T], precision=hp)
            - jnp.einsum('dgip,tdgpj->tdgij', c_im, eb_im[:T], precision=hp))
    t_idx = jnp.arange(T)
    k_f = jnp.transpose(ktau[:, 0], (1, 3, 0, 2))
    k_b = jnp.transpose(ktau[:, 1], (1, 3, 0, 2))
    table = jnp.concatenate([k_b[:, :, :0:-1], k_f[:, :, :1] + k_b[:, :, :1], k_f[:, :, 1:]], axis=2)
    table = table.astype(jnp.bfloat16).reshape(G, GC, (2 * T - 1) * GC)
    kt = jnp.stack([table[:, :, (T - 1 - s) * GC:(2 * T - 1 - s) * GC] for s in range(T)], axis=1)
    kt = kt.reshape(G, T * GC, T * GC)
    wf_re, wf_im = eb_re[T - 1 - t_idx, 0], eb_im[T - 1 - t_idx, 0]
    wb_re, wb_im = eb_re[t_idx, 1], eb_im[t_idx, 1]
    win = jnp.stack([wf_re, wf_im, wb_re, wb_im], axis=0)
    win = jnp.transpose(win, (2, 1, 4, 0, 3)).reshape(G, T * GC, 4 * P)
    ef_re, ef_im = e_re[t_idx + 1, 0], e_im[t_idx + 1, 0]
    eb2_re, eb2_im = e_re[T - t_idx, 1], e_im[T - t_idx, 1]

    def readout(cr, ci, er, ei):
        re = cr[None] * er[:, :, None, :] - ci[None] * ei[:, :, None, :]
        im = -(cr[None] * ei[:, :, None, :] + ci[None] * er[:, :, None, :])
        return re, im

    of_re, of_im = readout(c_re[0], c_im[0], ef_re, ef_im)
    ob_re, ob_im = readout(c_re[1], c_im[1], eb2_re, eb2_im)
    wout = jnp.stack([of_re, of_im, ob_re, ob_im], axis=0)
    wout = jnp.transpose(wout, (2, 0, 4, 1, 3)).reshape(G, 4 * P, T * GC)
    at_re, at_im = e_re[T], e_im[T]
    apow = jnp.concatenate([at_re[0], at_re[0], at_re[1], at_re[1]], axis=-1)[:, None, :]
    aimg = jnp.concatenate([-at_im[0], at_im[0], -at_im[1], at_im[1]], axis=-1)[:, None, :]
    return kt.astype(jnp.bfloat16), win.astype(jnp.bfloat16), wout.astype(jnp.bfloat16), apow, aimg


def _s5_kernel(x_ref, kt_ref, win_ref, wout_ref, apow_ref, aimg_ref, h0_ref, y_ref, hfin_ref, hin_ref, *,
               n_chunks, bsz):
    P2 = 2 * S5_STATE
    xb = x_ref[0]
    hin_ref[...] = jnp.dot(xb, win_ref[0], preferred_element_type=jnp.float32)
    ap = apow_ref[0]
    ai = aimg_ref[0]
    apf, aif = ap[:, :P2], ai[:, :P2]
    apb, aib = ap[:, P2:], ai[:, P2:]

    def cmul(h, a_p, a_i):
        return h * a_p + pltpu.roll(h, S5_STATE, axis=1) * a_i

    def body(c, carry):
        hf, hb = carry
        rf = pl.ds(pl.multiple_of(c * bsz, bsz), bsz)
        rb = pl.ds(pl.multiple_of((n_chunks - 1 - c) * bsz, bsz), bsz)
        df = hin_ref[rf, :P2]
        db = hin_ref[rb, P2:]
        hin_ref[rf, :P2] = hf
        hin_ref[rb, P2:] = hb
        return cmul(hf, apf, aif) + df, cmul(hb, apb, aib) + db

    h0 = h0_ref[0]
    hf, hb = lax.fori_loop(0, n_chunks, body, (h0[:, :P2], h0[:, P2:]))
    hfin_ref[0, :, :P2] = hf
    hfin_ref[0, :, P2:] = hb
    y = jnp.dot(xb, kt_ref[0], preferred_element_type=jnp.float32)
    y = y + jnp.dot(hin_ref[...].astype(jnp.bfloat16), wout_ref[0], preferred_element_type=jnp.float32)
    y_ref[0] = y


def s5_scan_pallas(u, ops, h0):
    kt, win, wout, apow, aimg = ops
    bsz, n, _ = u.shape
    T, G, P, GC = S5_CHUNK, S5_GROUPS, S5_STATE, S5_GROUP
    nc = n // T
    rows = nc * bsz
    x = jnp.transpose(u.astype(jnp.bfloat16).reshape(bsz, nc, T, G, GC), (3, 1, 0, 2, 4)).reshape(G, rows, T * GC)
    y, hfin = pl.pallas_call(
        functools.partial(_s5_kernel, n_chunks=nc, bsz=bsz),
        out_shape=(jax.ShapeDtypeStruct((G, rows, T * GC), jnp.float32),
                   jax.ShapeDtypeStruct((G, bsz, 4 * P), jnp.float32)),
        grid=(G,),
        in_specs=[pl.BlockSpec((1, rows, T * GC), lambda g: (g, 0, 0)),
                  pl.BlockSpec((1, T * GC, T * GC), lambda g: (g, 0, 0)),
                  pl.BlockSpec((1, T * GC, 4 * P), lambda g: (g, 0, 0)),
                  pl.BlockSpec((1, 4 * P, T * GC), lambda g: (g, 0, 0)),
                  pl.BlockSpec((1, 1, 4 * P), lambda g: (g, 0, 0)),
                  pl.BlockSpec((1, 1, 4 * P), lambda g: (g, 0, 0)),
                  pl.BlockSpec((1, bsz, 4 * P), lambda g: (g, 0, 0))],
        out_specs=(pl.BlockSpec((1, rows, T * GC), lambda g: (g, 0, 0)),
                   pl.BlockSpec((1, bsz, 4 * P), lambda g: (g, 0, 0))),
        scratch_shapes=[pltpu.VMEM((rows, 4 * P), jnp.float32)],
        compiler_params=pltpu.CompilerParams(dimension_semantics=("parallel",),
                                             vmem_limit_bytes=VMEM_LIMIT),
        name="s5_scan",
    )(x, kt, win, wout, apow, aimg, h0)
    y = jnp.transpose(y.reshape(G, nc, bsz, T, GC), (2, 1, 3, 0, 4)).reshape(bsz, n, G * GC)
    return y, hfin


def s5_branch(u, u_ctx, p):
    ops = s5_chunk_operators(p['s5_lam_re'], p['s5_lam_im'], p['s5_log_step'], p['s5_b_re'], p['s5_b_im'],
                             p['s5_c_re'], p['s5_c_im'])
    h0 = jnp.zeros((S5_GROUPS, u.shape[0], 4 * S5_STATE), jnp.float32)
    y_ctx, h_ctx = s5_scan_pallas(u_ctx, ops, h0)
    y_lat, _ = s5_scan_pallas(u, ops, h_ctx)
    return y_lat, y_ctx


_NT = (((1,), (1,)), ((), ()))

_M_SAME, _M_EYE = 0, 1
_W_STRICT, _W_INCL, _W_EYE, _W_LEVEL0 = 0, 2, 4, 5
_N_LEVELS = 6


def rwkv_masks():
    C, n = RW_CHUNK, RW_LANES
    row = np.arange(n)[:, None]
    col = np.arange(n)[None, :]
    square = np.stack([(row // C) == (col // C), row == col])
    tt, jj = np.arange(C)[:, None], np.arange(C)[None, :]
    wide = [jj < tt, jj > tt, jj <= tt, jj >= tt, jj == tt]
    for s in (1, 2, 4, 8, 16, 32):
        fwd = ((tt // s) % 2 == 1) & ((jj // s) == (tt // s) - 1)
        wide += [fwd, fwd.T]
    wide = np.stack([np.tile(m, (1, RW_HG)) for m in wide])
    tri = np.stack([jj <= tt, jj >= tt])
    return (jnp.asarray(square, jnp.float32), jnp.asarray(wide, jnp.float32), jnp.asarray(tri, jnp.bfloat16))


def _bdot(a, b, dims=None):
    a = a.astype(jnp.bfloat16)
    b = b.astype(jnp.bfloat16)
    if dims is None:
        return jnp.dot(a, b, preferred_element_type=jnp.float32)
    return lax.dot_general(a, b, dims, preferred_element_type=jnp.float32)


def _rwkv_chunk(S, r, lw, k, v, kk, b, m_ref, w_ref, same_b, tri, d):
    C = RW_CHUNK
    same = m_ref[_M_SAME]
    lw_hi = lw.astype(jnp.bfloat16)
    rem = lw - lw_hi.astype(jnp.float32)
    lw_mid = rem.astype(jnp.bfloat16)
    lw_lo = (rem - lw_mid.astype(jnp.float32)).astype(jnp.bfloat16)
    cl = (jnp.dot(tri, lw_hi, preferred_element_type=jnp.float32)
          + jnp.dot(tri, lw_mid, preferred_element_type=jnp.float32)
          + jnp.dot(tri, lw_lo, preferred_element_type=jnp.float32))
    yield
    tot = jnp.sum(lw, axis=0, keepdims=True)
    e_neg = jnp.exp(-cl)
    e_end = jnp.exp(tot - cl)
    a_t = -kk * jnp.exp(cl - lw)
    r_t = r * jnp.exp(cl)
    b_t = b * e_neg
    k_t = k * e_neg
    b_h = b * e_end
    k_h = k * e_end
    p_c = jnp.exp(tot)

    def tile(x):
        return jnp.concatenate([x] * RW_HG, axis=0)

    def stack(x):
        return tile(x.astype(jnp.bfloat16)) * same_b

    n = RW_LANES
    sa = stack(a_t)
    wide = _bdot(jnp.concatenate([a_t, r_t], axis=0), jnp.concatenate([stack(b_t), stack(k_t)], axis=0), _NT)
    yield
    a_ab = wide[:C, :n] * w_ref[_W_STRICT + d]
    a_ak = wide[:C, n:] * w_ref[_W_STRICT + d]
    a_rb = wide[C:, :n] * w_ref[_W_INCL + d]
    a_rk = wide[C:, n:] * w_ref[_W_INCL + d]
    dinv = w_ref[_W_EYE] + a_ab * w_ref[_W_LEVEL0 + d]
    for l in range(1, _N_LEVELS):
        inner = _bdot(a_ab * w_ref[_W_LEVEL0 + 2 * l + d], stack(dinv))
        yield
        dinv = dinv + _bdot(dinv, stack(inner))
        yield
    av = _bdot(jnp.concatenate([a_ak, a_rk], axis=0), stack(v))
    yield
    akv = av[:C]
    arkv = av[C:]
    mu = _bdot(dinv, jnp.concatenate([sa, stack(akv)], axis=1))
    yield
    m1 = mu[:, :n]
    u0 = mu[:, n:]
    my = _bdot(a_rb, jnp.concatenate([stack(m1), stack(u0)], axis=1))
    yield
    m2 = r_t + my[:, :n]
    y0 = my[:, n:] + arkv
    mut = jnp.concatenate([m1, u0], axis=1).T
    gh = _bdot(mut, b_h)
    yield
    g = m_ref[_M_EYE] * p_c + gh[:n] * same
    hmat = (gh[n:] + _bdot(v.T, k_h)) * same
    y = _bdot(m2, S, _NT) + y0
    s_new = _bdot(S, g) + hmat
    return s_new, y


def _lockstep(gens):
    results = [None] * len(gens)
    active = list(range(len(gens)))
    while active:
        for i in list(active):
            try:
                next(gens[i])
            except StopIteration as stop:
                results[i] = stop.value
                active.remove(i)
    return results


def _rwkv_kernel(m_ref, w_ref, tri_ref, rf_ref, rb_ref, vf_ref, vb_ref, kkf_ref, kkb_ref, lwf_ref, kf_ref, bf_ref,
                 lwb_ref, kb_ref, bb_ref, s0_ref, yf_ref, yb_ref, sfin_ref, s_scr, *, n_chunks, n_groups, n_batch):
    i = pl.program_id(1)
    C = RW_CHUNK

    @pl.when(i == 0)
    def _():
        s_scr[...] = s0_ref[...]

    tri_f = tri_ref[0]
    tri_b = tri_ref[1]
    same_b = m_ref[_M_SAME].astype(jnp.bfloat16)

    def body(c, carry):
        rf = pl.ds(pl.multiple_of(c * C, C), C)
        rb = pl.ds(pl.multiple_of((n_chunks - 1 - c) * C, C), C)
        gens = []
        probs = [(e, g) for e in range(n_batch) for g in range(n_groups)]
        for e, g in probs:
            ln = slice(g * RW_LANES, (g + 1) * RW_LANES)
            gens.append(_rwkv_chunk(s_scr[e, 0, g], rf_ref[e, rf, ln], lwf_ref[e, rf, ln], kf_ref[e, rf, ln],
                                    vf_ref[e, rf, ln], kkf_ref[e, rf, ln], bf_ref[e, rf, ln],
                                    m_ref, w_ref, same_b, tri_f, 0))
            gens.append(_rwkv_chunk(s_scr[e, 1, g], rb_ref[e, rb, ln], lwb_ref[e, rb, ln], kb_ref[e, rb, ln],
                                    vb_ref[e, rb, ln], kkb_ref[e, rb, ln], bb_ref[e, rb, ln],
                                    m_ref, w_ref, same_b, tri_b, 1))
        out = _lockstep(gens)
        for q, (e, g) in enumerate(probs):
            ln = slice(g * RW_LANES, (g + 1) * RW_LANES)
            (s_f, y_f), (s_b, y_b) = out[2 * q], out[2 * q + 1]
            s_scr[e, 0, g] = s_f
            s_scr[e, 1, g] = s_b
            yf_ref[e, rf, ln] = y_f
            yb_ref[e, rb, ln] = y_b
        return carry

    lax.fori_loop(0, n_chunks, body, 0)

    @pl.when(i == pl.num_programs(1) - 1)
    def _():
        sfin_ref[...] = s_scr[...]


def rwkv_scan_pallas(rkv, kk, lw, kd, bvec, s0, mask_set, block_tokens):
    bsz, n, width = kk.shape
    masks, wide, tri = mask_set
    ng = width // RW_LANES
    tb = block_tokens
    nb = n // tb
    eb = RW_BATCH_PER_STEP
    assert bsz % eb == 0
    fwd = pl.BlockSpec((eb, tb, width), lambda b, i: (b, i, 0))
    bwd = pl.BlockSpec((eb, tb, width), lambda b, i: (b, nb - 1 - i, 0))
    r_fwd, r_bwd = fwd, bwd
    v_fwd = pl.BlockSpec((eb, tb, width), lambda b, i: (b, i, 2))
    v_bwd = pl.BlockSpec((eb, tb, width), lambda b, i: (b, nb - 1 - i, 2))
    state_spec = pl.BlockSpec((eb, 2, ng, RW_LANES, RW_LANES), lambda b, i: (b, 0, 0, 0, 0))
    return pl.pallas_call(
        functools.partial(_rwkv_kernel, n_chunks=tb // RW_CHUNK, n_groups=ng, n_batch=eb),
        out_shape=(jax.ShapeDtypeStruct((bsz, n, width), jnp.float32),
                   jax.ShapeDtypeStruct((bsz, n, width), jnp.float32),
                   jax.ShapeDtypeStruct(s0.shape, jnp.float32)),
        grid=(bsz // eb, nb),
        in_specs=[pl.BlockSpec(masks.shape, lambda b, i: (0, 0, 0)),
                  pl.BlockSpec(wide.shape, lambda b, i: (0, 0, 0)),
                  pl.BlockSpec(tri.shape, lambda b, i: (0, 0, 0)),
                  r_fwd, r_bwd, v_fwd, v_bwd, fwd, bwd,
                  fwd, fwd, fwd, bwd, bwd, bwd, state_spec],
        out_specs=(fwd, bwd, state_spec),
        scratch_shapes=[pltpu.VMEM((eb, 2, ng, RW_LANES, RW_LANES), jnp.float32)],
        compiler_params=pltpu.CompilerParams(dimension_semantics=("parallel", "arbitrary"),
                                             vmem_limit_bytes=VMEM_LIMIT),
        name="rwkv_scan",
    )(masks, wide, tri, rkv, rkv, rkv, rkv, kk, kk, lw[0], kd[0], bvec[0], lw[1], kd[1], bvec[1], s0)


def _head_sum(t, hm):
    f32, bf16 = jnp.float32, jnp.bfloat16
    hi = t.astype(bf16)
    rem = t - hi.astype(f32)
    mid = rem.astype(bf16)
    lo = (rem - mid.astype(f32)).astype(bf16)
    return (jnp.dot(hi, hm, preferred_element_type=f32) + jnp.dot(mid, hm, preferred_element_type=f32)
            + jnp.dot(lo, hm, preferred_element_type=f32))


def _rwkv_prep_kernel(rkv_ref, lora_ref, vec_ref, w0_ref, a0_ref, w2_ref, a2_ref, hsum_ref,
                      kk_ref, lw0_ref, kd0_ref, b0_ref, lw1_ref, kd1_ref, b1_ref, bonus_ref):
    f32, bf16 = jnp.float32, jnp.bfloat16
    w = RW_WIDTH
    r = rkv_ref[:, 0:w]
    k = rkv_ref[:, w:2 * w]
    v = rkv_ref[:, 2 * w:3 * w]
    hm = hsum_ref[...]
    kk = k * vec_ref[0:1, :]
    kk = kk * lax.rsqrt(jnp.maximum(_head_sum(kk * kk, hm), 1e-24))
    kk_ref[...] = kk
    w_lo = jnp.tanh(lora_ref[:, 0:128]).astype(bf16)
    a_lo = lora_ref[:, 128:256].astype(bf16)
    rrk = r * vec_ref[2:3, :]
    bonus = None
    outs = ((lw0_ref, kd0_ref, b0_ref), (lw1_ref, kd1_ref, b1_ref))
    for d in range(2):
        x = w0_ref[d:d + 1, :] + jnp.dot(w_lo, w2_ref[d], preferred_element_type=f32)
        w_log = -(jnp.maximum(-x, 0.0) + jnp.log(1.0 + jnp.exp(-jnp.abs(x)))) - 0.5
        a = jax.nn.sigmoid(a0_ref[d:d + 1, :] + jnp.dot(a_lo, a2_ref[d], preferred_element_type=f32))
        kd = k * (1.0 + (a - 1.0) * vec_ref[1:2, :])
        outs[d][0][...] = -jnp.exp(w_log)
        outs[d][1][...] = kd
        outs[d][2][...] = kk * a
        term = _head_sum(rrk * kd, hm)
        bonus = term if bonus is None else bonus + term
    bonus_ref[...] = bonus * v


def rwkv_prep(rkv, lora, p):
    bsz, n, _ = rkv.shape
    rows = bsz * n
    tm = min(MIX_BLOCK_ROWS, n)
    w = RW_WIDTH
    hid = np.arange(w) // RW_HEAD
    hsum = jnp.asarray(hid[:, None] == hid[None, :], jnp.bfloat16)
    vec = jnp.stack([p['rw_kk'], p['rw_ka'], p['rw_rk']])
    zero = jnp.zeros((RW_DECAY_LORA, w), jnp.float32)
    w2 = jnp.stack([jnp.concatenate([p['rw_w2'][0], zero]), jnp.concatenate([zero, p['rw_w2'][1]])])
    a2 = jnp.stack([jnp.concatenate([p['rw_a2'][0], zero]), jnp.concatenate([zero, p['rw_a2'][1]])])
    consts = [vec, p['rw_w0'], p['rw_a0'], w2.astype(jnp.bfloat16), a2.astype(jnp.bfloat16), hsum]

    def const(a):
        return pl.BlockSpec(a.shape, lambda i: (0,) * a.ndim)

    outs = pl.pallas_call(
        _rwkv_prep_kernel,
        out_shape=tuple(jax.ShapeDtypeStruct((rows, w), jnp.float32) for _ in range(8)),
        grid=(rows // tm,),
        in_specs=[pl.BlockSpec((tm, 3 * w), lambda i: (i, 0)), pl.BlockSpec((tm, lora.shape[-1]), lambda i: (i, 0))]
                 + [const(c) for c in consts],
        out_specs=tuple(pl.BlockSpec((tm, w), lambda i: (i, 0)) for _ in range(8)),
        compiler_params=pltpu.CompilerParams(dimension_semantics=("parallel",), vmem_limit_bytes=VMEM_LIMIT),
        name="rwkv_prep",
    )(rkv.reshape(rows, 3 * w), lora.reshape(rows, -1), *consts)
    kk, lw0, kd0, b0, lw1, kd1, b1, bonus = [o.reshape(bsz, n, w) for o in outs]
    return kk, (lw0, lw1), (kd0, kd1), (b0, b1), bonus


def rwkv_branch(rkv, lora, p, s0, mask_set, block_tokens):
    kk, lw, kd, bvec, bonus = rwkv_prep(rkv, lora, p)
    y_f, y_b, s_fin = rwkv_scan_pallas(rkv, kk, lw, kd, bvec, s0, mask_set, block_tokens)
    return (y_f, y_b, bonus, lora[..., IN_LORA - RW_GATE_LORA:]), s_fin


def hyena_filters(n_tok, p):
    hp = lax.Precision.HIGHEST
    bands = (HY_POS_DIM - 1) // 2
    t = jnp.linspace(0.0, 1.0, n_tok, dtype=jnp.float32)[:, None]
    w = (2.0 * math.pi / n_tok) * jnp.arange(n_tok, dtype=jnp.float32)[:, None]
    f = jnp.linspace(1e-4, bands - 1, bands, dtype=jnp.float32)[None, :]
    feats = jnp.concatenate([t, jnp.cos(f * w), -jnp.sin(f * w)], axis=-1)
    h = jnp.sin(p['hy_f_freq1'] * (jnp.dot(feats, p['hy_f_w1'], precision=hp) + p['hy_f_b1']))
    h = jnp.sin(p['hy_f_freq2'] * (jnp.dot(h, p['hy_f_w2'], precision=hp) + p['hy_f_b2']))
    h = jnp.dot(h, p['hy_f_w3'], precision=hp).reshape(n_tok, HY_ORDER, 2, HY_WIDTH)
    rates = jnp.abs(jnp.linspace(math.log(HY_DECAY_TARGET) / HY_DECAY_PCT_SHORT,
                                 math.log(HY_DECAY_TARGET) / HY_DECAY_PCT_LONG, HY_WIDTH, dtype=jnp.float32))
    h = h * jnp.exp(-t * rates)[:, None, None, :]
    return h.at[0, :, 1].set(0.0)


HY_N = 2 * SEQ
HY_N1 = 64
HY_N2 = 128
HY_NH = HY_N1 // 2
HY_KH = HY_N1 // 2 + 1
HY_KP = 40
HY_PITCH = 88
HY_LANES = 128


def hyena_dft_tables():
    k1 = np.arange(HY_KH)[:, None]
    n1 = np.arange(HY_NH)[None, :]
    n2 = np.arange(HY_N2)
    ph = -2 * np.pi * (k1 * n1 / HY_N1)[None] - 2 * np.pi * (n2[:, None, None] * k1[None] / HY_N)
    lhs1 = np.zeros((HY_N2, 2 * HY_KP, HY_NH))
    lhs1[:, :HY_KH] = np.cos(ph)
    lhs1[:, HY_KP:HY_KP + HY_KH] = np.sin(ph)
    wgt = np.where((k1 == 0) | (k1 == HY_N1 // 2), 1.0, 2.0)[None] / HY_N
    lhs2 = np.zeros((HY_N2, HY_NH, 2 * HY_KP))
    lhs2[:, :, :HY_KH] = (wgt * np.cos(ph)).transpose(0, 2, 1)
    lhs2[:, :, HY_KP:HY_KP + HY_KH] = (wgt * np.sin(ph)).transpose(0, 2, 1)
    kk = np.arange(HY_N2)
    ang = -2 * np.pi * np.outer(kk, kk) / HY_N2
    cr, ci = np.cos(ang), np.sin(ang)
    f_fwd = np.block([[cr, -ci], [ci, cr]])
    f_inv = np.block([[cr, ci], [-ci, cr]])

    def as_bf(a):
        return jnp.asarray(a, jnp.float32).astype(jnp.bfloat16)

    def low_bf(a):
        a = jnp.asarray(a, jnp.float32)
        return (a - a.astype(jnp.bfloat16).astype(jnp.float32)).astype(jnp.bfloat16)

    return as_bf(lhs1), as_bf(lhs2), as_bf(f_fwd), as_bf(f_inv), low_bf(lhs1), low_bf(f_fwd)


def _hy_spec_kernel(h_ref, l1h_ref, l1l_ref, ffh_ref, ffl_ref, o_ref, a_ref):
    f32, bf16 = jnp.float32, jnp.bfloat16
    n2n, nh, pitch, kp = HY_N2, HY_NH, HY_PITCH, HY_KP

    def dot3(a_hi, a_lo, x):
        x_hi = x.astype(bf16)
        x_lo = (x - x_hi.astype(f32)).astype(bf16)
        return (jnp.dot(a_hi, x_hi, preferred_element_type=f32) + jnp.dot(a_hi, x_lo, preferred_element_type=f32)
                + jnp.dot(a_lo, x_hi, preferred_element_type=f32))

    def stage1(q, c):
        slab = h_ref[pl.ds(pl.multiple_of(q * nh, nh), nh), :]
        a_ref[pl.ds(pl.multiple_of(q * pitch, 8), 2 * kp), :] = dot3(l1h_ref[q], l1l_ref[q], slab)
        return c

    lax.fori_loop(0, n2n, stage1, 0, unroll=8)

    def stage2(k, c):
        a = jnp.concatenate([a_ref[pl.ds(k, n2n, stride=pitch), :],
                             a_ref[pl.ds(kp + k, n2n, stride=pitch), :]], axis=0)
        o_ref[0, pl.ds(pl.multiple_of(k * 2 * n2n, 2 * n2n), 2 * n2n), :] = dot3(ffh_ref[...], ffl_ref[...], a)
        return c

    lax.fori_loop(0, HY_KH, stage2, 0, unroll=3)


def hyena_filter_spectra_long(h, tables):
    n, orders, _, w = h.shape
    l1, _, ff, _, l1_lo, ff_lo = tables
    ch = orders * 2 * w
    sig = jnp.transpose(h.reshape(HY_NH, HY_N2, ch), (1, 0, 2)).reshape(n, ch)
    rows = HY_KH * 2 * HY_N2

    def const(a):
        return pl.BlockSpec(a.shape, lambda t: (0,) * a.ndim)

    spec = pl.pallas_call(
        _hy_spec_kernel,
        out_shape=jax.ShapeDtypeStruct((ch // HY_LANES, rows, HY_LANES), jnp.float32),
        grid=(ch // HY_LANES,),
        in_specs=[pl.BlockSpec((n, HY_LANES), lambda t: (0, t)), const(l1), const(l1_lo), const(ff), const(ff_lo)],
        out_specs=pl.BlockSpec((1, rows, HY_LANES), lambda t: (t, 0, 0)),
        scratch_shapes=[pltpu.VMEM((HY_N2 * HY_PITCH, HY_LANES), jnp.float32)],
        compiler_params=pltpu.CompilerParams(dimension_semantics=("parallel",), vmem_limit_bytes=VMEM_LIMIT),
        name="hyena_filter_spectrum",
    )(sig, l1, l1_lo, ff, ff_lo)
    spec = spec.reshape(orders, 2, w // HY_LANES, HY_KH, 2, HY_N2, HY_LANES)
    sign = jnp.asarray([1.0, -1.0], jnp.float32)[:, None, None]
    both = spec[:, 0] + sign * spec[:, 1]
    return both.reshape(orders, w // HY_LANES, rows, HY_LANES)


def _hy_kernel(z_ref, g_ref, k_ref, l1_ref, l2_ref, ff_ref, fi_ref, bias_ref, o_ref, a_ref):
    f32, bf16 = jnp.float32, jnp.bfloat16
    n2n, nh, pitch, kp = HY_N2, HY_NH, HY_PITCH, HY_KP

    def stage1(q, c):
        slab = z_ref[0, pl.ds(pl.multiple_of(q * nh, nh), nh), :].astype(bf16)
        a_ref[pl.ds(pl.multiple_of(q * pitch, 8), 2 * kp), :] = jnp.dot(l1_ref[q], slab, preferred_element_type=f32)
        return c

    lax.fori_loop(0, n2n, stage1, 0, unroll=8)

    def stage2(k, c):
        a = jnp.concatenate([a_ref[pl.ds(k, n2n, stride=pitch), :],
                             a_ref[pl.ds(kp + k, n2n, stride=pitch), :]], axis=0).astype(bf16)
        x = jnp.dot(ff_ref[...], a, preferred_element_type=f32)
        xr, xi = x[:n2n], x[n2n:]
        base = pl.multiple_of(k * 2 * n2n, 2 * n2n)
        kr = k_ref[0, pl.ds(base, n2n), :]
        ki = k_ref[0, pl.ds(base + n2n, n2n), :]
        y = jnp.concatenate([xr * kr - xi * ki, xr * ki + xi * kr], axis=0).astype(bf16)
        b = jnp.dot(fi_ref[...], y, preferred_element_type=f32)
        a_ref[pl.ds(k, n2n, stride=pitch), :] = b[:n2n]
        a_ref[pl.ds(kp + k, n2n, stride=pitch), :] = b[n2n:]
        return c

    lax.fori_loop(0, HY_KH, stage2, 0, unroll=3)

    def stage3(q, c):
        blk = a_ref[pl.ds(pl.multiple_of(q * pitch, 8), 2 * kp), :].astype(bf16)
        y = jnp.dot(l2_ref[q], blk, preferred_element_type=f32)
        rows = pl.ds(pl.multiple_of(q * nh, nh), nh)
        o_ref[0, rows, :] = g_ref[0, rows, :] * (y + z_ref[0, rows, :] * bias_ref[...])
        return c

    lax.fori_loop(0, n2n, stage3, 0, unroll=8)


def hyena_long_conv_gated(zt, gt, spec, bias, tables):
    bsz, n, w = zt.shape
    l1, l2, ff, fi = tables[:4]
    nt = w // HY_LANES
    tok = pl.BlockSpec((1, n, HY_LANES), lambda t, b: (b, 0, t))
    return pl.pallas_call(
        _hy_kernel,
        out_shape=jax.ShapeDtypeStruct((bsz, n, w), jnp.float32),
        grid=(nt, bsz),
        in_specs=[tok, tok,
                  pl.BlockSpec((1,) + spec.shape[1:], lambda t, b: (t, 0, 0)),
                  pl.BlockSpec(l1.shape, lambda t, b: (0, 0, 0)),
                  pl.BlockSpec(l2.shape, lambda t, b: (0, 0, 0)),
                  pl.BlockSpec(ff.shape, lambda t, b: (0, 0)),
                  pl.BlockSpec(fi.shape, lambda t, b: (0, 0)),
                  pl.BlockSpec((1, HY_LANES), lambda t, b: (0, t))],
        out_specs=tok,
        scratch_shapes=[pltpu.VMEM((HY_N2 * HY_PITCH, HY_LANES), jnp.float32)],
        compiler_params=pltpu.CompilerParams(dimension_semantics=("parallel", "parallel"),
                                             vmem_limit_bytes=VMEM_LIMIT),
        name="hyena_conv",
    )(zt, gt, spec, l1, l2, ff, fi, bias.reshape(1, w))


def hyena_branch_long(streams, p):
    bsz, n, w3 = streams.shape
    tables = hyena_dft_tables()
    spec = hyena_filter_spectra_long(hyena_filters(n, p), tables)
    st = jnp.transpose(streams.reshape(bsz, HY_NH, HY_N2, w3), (0, 2, 1, 3)).reshape(bsz, n, w3)
    z, x1, x2 = jnp.split(st, 3, axis=-1)
    for o, gate in enumerate((x1, x2)):
        z = hyena_long_conv_gated(z, gate, spec[o], p['hy_bias'][o], tables)
    w = z.shape[-1]
    return jnp.transpose(z.reshape(bsz, HY_N2, HY_NH, w), (0, 2, 1, 3)).reshape(bsz, n, w)


def _hy_ctx_kernel(z_ref, g_ref, k_ref, fd_ref, fi_ref, bias_ref, o_ref):
    f32, bf16 = jnp.float32, jnp.bfloat16
    z = z_ref[0]
    n = fd_ref.shape[0] // 2
    x = jnp.dot(fd_ref[...], z.astype(bf16), preferred_element_type=f32)
    xr, xi = x[:n], x[n:]
    kr, ki = k_ref[0, :n, :], k_ref[0, n:, :]
    y = jnp.concatenate([xr * kr - xi * ki, xr * ki + xi * kr], axis=0).astype(bf16)
    conv = jnp.dot(fi_ref[...], y, preferred_element_type=f32)
    o_ref[0] = g_ref[0] * (conv + z * bias_ref[...])


def hyena_branch_ctx(streams, p):
    bsz, n, w3 = streams.shape
    w = w3 // 3
    nn = 2 * n
    ang = 2 * np.pi * np.outer(np.arange(nn), np.arange(n)) / nn
    h = hyena_filters(n, p)
    hp = lax.Precision.HIGHEST
    spec = jnp.concatenate([
        jnp.einsum('kt,tow->kow', jnp.asarray(np.cos(ang), jnp.float32), h[:, :, 0] + h[:, :, 1], precision=hp),
        jnp.einsum('kt,tow->kow', jnp.asarray(-np.sin(ang), jnp.float32), h[:, :, 0] - h[:, :, 1], precision=hp)],
        axis=0)
    fd = jnp.asarray(np.concatenate([np.cos(ang), -np.sin(ang)], axis=0), jnp.float32).astype(jnp.bfloat16)
    fi = jnp.asarray(np.concatenate([np.cos(ang).T, -np.sin(ang).T], axis=1) / nn, jnp.float32).astype(jnp.bfloat16)
    nt = w // HY_LANES
    tok = pl.BlockSpec((1, n, HY_LANES), lambda t, b: (b, 0, t))
    z, x1, x2 = jnp.split(streams, 3, axis=-1)
    for o, gate in enumerate((x1, x2)):
        sp = jnp.transpose(spec[:, o].reshape(2 * nn, nt, HY_LANES), (1, 0, 2)).astype(jnp.float32)
        z = pl.pallas_call(
            _hy_ctx_kernel,
            out_shape=jax.ShapeDtypeStruct((bsz, n, w), jnp.float32),
            grid=(nt, bsz),
            in_specs=[tok, tok,
                      pl.BlockSpec((1, 2 * nn, HY_LANES), lambda t, b: (t, 0, 0)),
                      pl.BlockSpec(fd.shape, lambda t, b: (0, 0)),
                      pl.BlockSpec(fi.shape, lambda t, b: (0, 0)),
                      pl.BlockSpec((1, HY_LANES), lambda t, b: (0, t))],
            out_specs=tok,
            compiler_params=pltpu.CompilerParams(dimension_semantics=("parallel", "parallel"),
                                                 vmem_limit_bytes=VMEM_LIMIT),
            name="hyena_conv_ctx",
        )(z, gate, sp, fd, fi, p['hy_bias'][o].reshape(1, w))
    return z


def _rms(x, gain):
    return x * lax.rsqrt(jnp.mean(x * x, axis=-1, keepdims=True) + NORM_EPS) * gain


def _conv3(x, w_ref, period):
    rows = x.shape[0]
    pos = lax.broadcasted_iota(jnp.int32, (rows, 1), 0) % period
    prev = jnp.where(pos == 0, 0.0, pltpu.roll(x, 1, axis=0))
    nxt = jnp.where(pos == period - 1, 0.0, pltpu.roll(x, rows - 1, axis=0))
    return prev * w_ref[0:1, :] + x * w_ref[1:2, :] + nxt * w_ref[2:3, :]


def _inproj_kernel(x_ref, mod_ref, gain_ref, ws5_ref, wrw_ref, wlo_ref, why_ref, wgt_ref, cwr_ref, cwh_ref, cbh_ref,
                   os5_ref, orw_ref, olo_ref, ohy_ref, ogt_ref, *, period):
    f32 = jnp.float32
    shift, scale = mod_ref[0, 0:1, :], mod_ref[0, 1:2, :]
    h = (_rms(x_ref[...], gain_ref[...]) * (1.0 + scale) + shift).astype(jnp.bfloat16)
    os5_ref[...] = jnp.dot(h, ws5_ref[...], preferred_element_type=f32)
    olo_ref[...] = jnp.dot(h, wlo_ref[...], preferred_element_type=f32)
    ogt_ref[...] = jnp.dot(h, wgt_ref[...], preferred_element_type=f32)
    orw_ref[...] = _conv3(jnp.dot(h, wrw_ref[...], preferred_element_type=f32), cwr_ref, period)
    ohy_ref[...] = _conv3(jnp.dot(h, why_ref[...], preferred_element_type=f32), cwh_ref, period) + cbh_ref[...]


def input_projection(x, mod, gain, w_parts, rw_conv_w, hy_conv_w, hy_conv_b, period):
    bsz, n, d = x.shape
    tm = min(MIX_BLOCK_ROWS, n)
    per_b = n // tm
    rows = bsz * n
    bmap = (lambda i: (i // per_b, 0, 0)) if mod.shape[0] == bsz else (lambda i: (0, 0, 0))
    wb = [w.astype(jnp.bfloat16) for w in w_parts]

    def const(a):
        return pl.BlockSpec(a.shape, lambda i: (0,) * a.ndim)

    cbh = hy_conv_b.reshape(1, -1)
    outs = pl.pallas_call(
        functools.partial(_inproj_kernel, period=period),
        out_shape=tuple(jax.ShapeDtypeStruct((rows, w.shape[1]), jnp.float32) for w in wb),
        grid=(rows // tm,),
        in_specs=[pl.BlockSpec((tm, d), lambda i: (i, 0)),
                  pl.BlockSpec((1, 3, d), bmap),
                  pl.BlockSpec((1, d), lambda i: (0, 0))]
                 + [const(w) for w in wb] + [const(rw_conv_w), const(hy_conv_w), const(cbh)],
        out_specs=tuple(pl.BlockSpec((tm, w.shape[1]), lambda i: (i, 0)) for w in wb),
        compiler_params=pltpu.CompilerParams(dimension_semantics=("parallel",), vmem_limit_bytes=VMEM_LIMIT),
        name="input_projection",
    )(x.reshape(rows, d), mod, gain.reshape(1, d), *wb, rw_conv_w, hy_conv_w, cbh)
    return tuple(o.reshape(bsz, n, o.shape[-1]) for o in outs)


def _merge_kernel(x_ref, mod_ref, gain_ref, ys5_ref, us5_ref, wf_ref, wb_ref, bonus_ref, glo_ref, hy_ref, gt_ref,
                  s5d_ref, glu_ref, lnw_ref, lnb_ref, g2_ref, havg_ref, brs_ref, brr_ref, brh_ref, ow_ref, o_ref):
    f32, bf16 = jnp.float32, jnp.bfloat16
    y = ys5_ref[...] + s5d_ref[...] * us5_ref[...]
    y = 0.5 * y * (1.0 + lax.erf(y * (1.0 / math.sqrt(2.0))))
    s5 = y * jax.nn.sigmoid(jnp.dot(y.astype(bf16), glu_ref[...], preferred_element_type=f32))
    wkv = wf_ref[...] + wb_ref[...]

    def head_mean(t):
        hi = t.astype(bf16)
        rem = t - hi.astype(f32)
        mid = rem.astype(bf16)
        lo = (rem - mid.astype(f32)).astype(bf16)
        hm = havg_ref[...]
        tot = (jnp.dot(hi, hm, preferred_element_type=f32) + jnp.dot(mid, hm, preferred_element_type=f32)
               + jnp.dot(lo, hm, preferred_element_type=f32))
        return tot * (1.0 / RW_HEAD)

    mu = head_mean(wkv)
    cen = wkv - mu
    var = head_mean(cen * cen)
    o = cen * lax.rsqrt(var + RW_GN_EPS) * lnw_ref[...] + lnb_ref[...] + bonus_ref[...]
    g = jnp.dot(jax.nn.sigmoid(glo_ref[...]).astype(bf16), g2_ref[...], preferred_element_type=f32)
    rw = o * g
    gt = jax.nn.sigmoid(gt_ref[...])
    d = x_ref.shape[-1]
    m = (gt[:, 0:d] * jnp.dot(s5.astype(bf16), brs_ref[...], preferred_element_type=f32)
         + gt[:, d:2 * d] * jnp.dot(rw.astype(bf16), brr_ref[...], preferred_element_type=f32)
         + gt[:, 2 * d:3 * d] * jnp.dot(hy_ref[...].astype(bf16), brh_ref[...], preferred_element_type=f32))
    yl = jnp.dot(m.astype(bf16), ow_ref[...], preferred_element_type=f32)
    o_ref[...] = x_ref[...] + mod_ref[0, 2:3, :] * _rms(yl, gain_ref[...])


def merge_block(x, mod, gain, ys5, us5, rw_parts, hy, gates, p):
    bsz, n, d = x.shape
    tm = min(MIX_BLOCK_ROWS, n)
    per_b = n // tm
    rows = bsz * n
    bmap = (lambda i: (i // per_b, 0, 0)) if mod.shape[0] == bsz else (lambda i: (0, 0, 0))

    def bf(a):
        return a.astype(jnp.bfloat16)

    def row(a):
        return pl.BlockSpec((tm, a.shape[-1]), lambda i: (i, 0))

    def const(a):
        return pl.BlockSpec(a.shape, lambda i: (0,) * a.ndim)

    hid = np.arange(RW_WIDTH) // RW_HEAD
    havg = jnp.asarray(hid[:, None] == hid[None, :], jnp.bfloat16)
    acts = [a.reshape(rows, a.shape[-1]) for a in (ys5, us5) + tuple(rw_parts) + (hy, gates)]
    consts = [p['s5_d'].reshape(1, -1), bf(p['s5_glu_w']), p['rw_ln_w'].reshape(1, -1), p['rw_ln_b'].reshape(1, -1),
              bf(p['rw_g2']), havg, bf(p['br_s5']), bf(p['br_rw']), bf(p['br_hy']), bf(p['out_w'])]
    out = pl.pallas_call(
        _merge_kernel,
        out_shape=jax.ShapeDtypeStruct((rows, d), jnp.float32),
        grid=(rows // tm,),
        in_specs=[pl.BlockSpec((tm, d), lambda i: (i, 0)), pl.BlockSpec((1, 3, d), bmap),
                  pl.BlockSpec((1, d), lambda i: (0, 0))] + [row(a) for a in acts] + [const(c) for c in consts],
        out_specs=pl.BlockSpec((tm, d), lambda i: (i, 0)),
        compiler_params=pltpu.CompilerParams(dimension_semantics=("parallel",), vmem_limit_bytes=VMEM_LIMIT),
        name="merge_block",
    )(x.reshape(rows, d), mod, gain.reshape(1, d), *acts, *consts)
    return out.reshape(bsz, n, d)


def token_mixer(x, ctx, mod_l, mod_c, g_pre, g_post, p, with_ctx_out):
    bsz = x.shape[0]
    cuts = [IN_S5, IN_S5 + IN_RW, IN_S5 + IN_RW + IN_LORA, IN_S5 + IN_RW + IN_LORA + IN_HY]
    edges = [0] + cuts + [IN_COLS]
    w_parts = [p['in_w'][:, edges[i]:edges[i + 1]] for i in range(5)]
    conv = (p['rw_conv_w'], p['hy_conv_w'], p['hy_conv_b'])
    u_l, rkv_l, lora_l, hy_l, gate_l = input_projection(x, mod_l, g_pre, w_parts, *conv, GRID_W)
    u_c, rkv_c, lora_c, hy_c, gate_c = input_projection(ctx, mod_c, g_pre, w_parts, *conv, CTX_LEN)
    mask_set = rwkv_masks()
    zr = jnp.zeros((bsz, 2, RW_WIDTH // RW_LANES, RW_LANES, RW_LANES), jnp.float32)
    ys5_l, ys5_c = s5_branch(u_l, u_c, p)
    rw_c, rw_state = rwkv_branch(rkv_c, lora_c, p, zr, mask_set, CTX_LEN)
    rw_l, _ = rwkv_branch(rkv_l, lora_l, p, rw_state, mask_set, RW_BLOCK_TOKENS)
    hy_lat = hyena_branch_long(hy_l, p)
    x = merge_block(x, mod_l, g_post, ys5_l, u_l, rw_l, hy_lat, gate_l, p)
    if with_ctx_out:
        ctx = merge_block(ctx, mod_c, g_post, ys5_c, u_c, rw_c, hyena_branch_ctx(hy_c, p), gate_c, p)
    return x, ctx


def _ffn_kernel(x_ref, mod_ref, gain_ref, wg_ref, wu_ref, wd_ref, o_ref):
    x = x_ref[...]
    shift, scale, gate = mod_ref[0, 0:1, :], mod_ref[0, 1:2, :], mod_ref[0, 2:3, :]
    h = (_rms(x, gain_ref[0:1, :]) * (1.0 + scale) + shift).astype(jnp.bfloat16)
    g = jnp.dot(h, wg_ref[...], preferred_element_type=jnp.float32)
    u = jnp.dot(h, wu_ref[...], preferred_element_type=jnp.float32)
    a = (g * jax.nn.sigmoid(g) * u).astype(jnp.bfloat16)
    y = jnp.dot(a, wd_ref[...], preferred_element_type=jnp.float32)
    o_ref[...] = x + gate * _rms(y, gain_ref[1:2, :])


def ffn_block(x, mod, gains, wg, wu, wd):
    bsz, n, d = x.shape
    hid = wg.shape[1]
    tm = min(FFN_BLOCK_ROWS, n)
    per_b = n // tm
    xf = x.reshape(bsz * n, d)
    bmap = (lambda i: (i // per_b, 0, 0)) if mod.shape[0] == bsz else (lambda i: (0, 0, 0))
    out = pl.pallas_call(
        _ffn_kernel,
        out_shape=jax.ShapeDtypeStruct((bsz * n, d), jnp.float32),
        grid=(bsz * per_b,),
        in_specs=[pl.BlockSpec((tm, d), lambda i: (i, 0)),
                  pl.BlockSpec((1, 3, d), bmap),
                  pl.BlockSpec((2, d), lambda i: (0, 0)),
                  pl.BlockSpec((d, hid), lambda i: (0, 0)),
                  pl.BlockSpec((d, hid), lambda i: (0, 0)),
                  pl.BlockSpec((hid, d), lambda i: (0, 0))],
        out_specs=pl.BlockSpec((tm, d), lambda i: (i, 0)),
        compiler_params=pltpu.CompilerParams(dimension_semantics=("parallel",), vmem_limit_bytes=VMEM_LIMIT),
        name="ffn_block",
    )(xf, mod, gains, wg.astype(jnp.bfloat16), wu.astype(jnp.bfloat16), wd.astype(jnp.bfloat16))
    return out.reshape(bsz, n, d)


def _moe_kernel(be_ref, x_ref, gate_ref, wg_ref, wu_ref, wd_ref, o_ref):
    j = pl.program_id(1)
    xb = x_ref[...]
    g = jnp.dot(xb, wg_ref[0], preferred_element_type=jnp.float32)
    u = jnp.dot(xb, wu_ref[0], preferred_element_type=jnp.float32)
    a = (g * jax.nn.sigmoid(g) * u).astype(jnp.bfloat16)
    y = jnp.dot(a, wd_ref[0], preferred_element_type=jnp.float32)

    @pl.when(j == 0)
    def _():
        o_ref[...] = y

    @pl.when(j > 0)
    def _():
        o_ref[...] += y

    @pl.when(j == pl.num_programs(1) - 1)
    def _():
        o_ref[...] *= gate_ref[...]


def moe_expert_blocks(xb, slot_gate, block_e, wg, wu, wd):
    cap, d = xb.shape
    n_blocks = cap // MOE_BLOCK
    hid = wg.shape[-1]
    nh = MOE_HIDDEN_TILES
    th = hid // nh

    def hidx(i, j):
        return j + (i % 2) * (nh - 1 - 2 * j)

    return pl.pallas_call(
        _moe_kernel,
        out_shape=jax.ShapeDtypeStruct((cap, d), jnp.float32),
        grid_spec=pltpu.PrefetchScalarGridSpec(
            num_scalar_prefetch=1,
            grid=(n_blocks, nh),
            in_specs=[pl.BlockSpec((MOE_BLOCK, d), lambda i, j, be: (i, 0)),
                      pl.BlockSpec((MOE_BLOCK, 1), lambda i, j, be: (i, 0)),
                      pl.BlockSpec((1, d, th), lambda i, j, be: (be[i], 0, hidx(i, j))),
                      pl.BlockSpec((1, d, th), lambda i, j, be: (be[i], 0, hidx(i, j))),
                      pl.BlockSpec((1, th, d), lambda i, j, be: (be[i], hidx(i, j), 0))],
            out_specs=pl.BlockSpec((MOE_BLOCK, d), lambda i, j, be: (i, 0))),
        compiler_params=pltpu.CompilerParams(dimension_semantics=("parallel", "arbitrary"),
                                             vmem_limit_bytes=VMEM_LIMIT),
        name="moe_experts",
    )(block_e, xb, slot_gate.reshape(cap, 1), wg.astype(jnp.bfloat16), wu.astype(jnp.bfloat16),
      wd.astype(jnp.bfloat16))


def moe_swiglu(h, router_w, wg, wu, wd):
    d_model = h.shape[-1]
    tok = h.reshape(-1, d_model)
    n = tok.shape[0]
    n_assign = n * TOP_K
    logits = jnp.dot(tok, router_w)
    top_logit, top_e = lax.top_k(logits, TOP_K)
    gate = jax.nn.softmax(top_logit, axis=-1).reshape(-1)
    flat_e = top_e.reshape(-1)
    order = jnp.argsort(flat_e)
    onehot = (flat_e[:, None] == jnp.arange(N_EXPERTS, dtype=flat_e.dtype)[None, :]).astype(jnp.int32)
    sizes = jnp.sum(onehot, axis=0)
    padded = (sizes + MOE_BLOCK - 1) // MOE_BLOCK * MOE_BLOCK
    pad_end = jnp.cumsum(padded)
    pad_start = pad_end - padded
    grp_start = jnp.cumsum(sizes) - sizes
    n_blocks = -(-n_assign // MOE_BLOCK) + N_EXPERTS
    cap = n_blocks * MOE_BLOCK
    block_start = jnp.arange(n_blocks, dtype=jnp.int32) * MOE_BLOCK
    block_e = jnp.minimum(jnp.sum(block_start[:, None] >= pad_end[None, :], axis=1), N_EXPERTS - 1)
    slot_e = jnp.repeat(block_e, MOE_BLOCK)
    within = jnp.arange(cap, dtype=jnp.int32) - pad_start[slot_e]
    valid = within < sizes[slot_e]
    assign = order[jnp.clip(grp_start[slot_e] + within, 0, n_assign - 1)]
    slot_tok = jnp.where(valid, (assign // TOP_K).astype(jnp.int32), 0)
    slot_gate = jnp.where(valid, gate[assign], 0.0)
    xb = tok.astype(jnp.bfloat16)[slot_tok]
    yb = moe_expert_blocks(xb, slot_gate, block_e.astype(jnp.int32), wg, wu, wd)
    rank = jnp.take_along_axis(jnp.cumsum(onehot, axis=0), flat_e[:, None], axis=1)[:, 0] - 1
    pos = (pad_start[flat_e] + rank).astype(jnp.int32).reshape(n, TOP_K)
    out = yb[pos[:, 0]]
    for kk in range(1, TOP_K):
        out = out + yb[pos[:, kk]]
    return out.reshape(h.shape)


def kernel(x, c, ctx, c_ctx, mod_w, mod_b, norm_g, in_w, s5_lam_re, s5_lam_im, s5_log_step, s5_b_re, s5_b_im, s5_c_re, s5_c_im, s5_d, s5_glu_w, rw_conv_w, rw_w0, rw_w2, rw_a0, rw_a2, rw_g2, rw_kk, rw_ka, rw_rk, rw_ln_w, rw_ln_b, hy_conv_w, hy_conv_b, hy_f_w1, hy_f_b1, hy_f_freq1, hy_f_w2, hy_f_b2, hy_f_freq2, hy_f_w3, hy_bias, br_s5, br_rw, br_hy, out_w, ffn_wg, ffn_wu, ffn_wd, moe_router, moe_wg, moe_wu, moe_wd):
    silu_c = jax.nn.silu(c)
    silu_cc = jax.nn.silu(c_ctx)
    for i in range(DEPTH):
        last = i == DEPTH - 1
        p = {
            'in_w': in_w[i],
            's5_lam_re': s5_lam_re[i], 's5_lam_im': s5_lam_im[i], 's5_log_step': s5_log_step[i],
            's5_b_re': s5_b_re[i], 's5_b_im': s5_b_im[i], 's5_c_re': s5_c_re[i], 's5_c_im': s5_c_im[i],
            's5_d': s5_d[i], 's5_glu_w': s5_glu_w[i],
            'rw_conv_w': rw_conv_w[i], 'rw_w0': rw_w0[i], 'rw_w2': rw_w2[i], 'rw_a0': rw_a0[i], 'rw_a2': rw_a2[i],
            'rw_g2': rw_g2[i], 'rw_kk': rw_kk[i], 'rw_ka': rw_ka[i], 'rw_rk': rw_rk[i],
            'rw_ln_w': rw_ln_w[i], 'rw_ln_b': rw_ln_b[i],
            'hy_conv_w': hy_conv_w[i], 'hy_conv_b': hy_conv_b[i], 'hy_f_w1': hy_f_w1[i], 'hy_f_b1': hy_f_b1[i],
            'hy_f_freq1': hy_f_freq1[i], 'hy_f_w2': hy_f_w2[i], 'hy_f_b2': hy_f_b2[i], 'hy_f_freq2': hy_f_freq2[i],
            'hy_f_w3': hy_f_w3[i], 'hy_bias': hy_bias[i],
            'br_s5': br_s5[i], 'br_rw': br_rw[i], 'br_hy': br_hy[i], 'out_w': out_w[i],
        }
        ml = jnp.split((silu_c @ mod_w[i] + mod_b[i])[:, None, :], 6, axis=-1)
        mc = jnp.split(silu_cc @ mod_w[i] + mod_b[i], 6, axis=-1)
        g_pre_m, g_post_m, g_pre_f, g_post_f = norm_g[i]
        if i % 2 == 0:
            def channel_mix(t, m, j=i // 2):
                return ffn_block(t, m, norm_g[i, 2:4], ffn_wg[j], ffn_wu[j], ffn_wd[j])
        else:
            def channel_mix(t, m, j=i // 2):
                h = rms_norm(t, g_pre_f) * (1 + m[:, 1:2]) + m[:, 0:1]
                y = moe_swiglu(h, moe_router[j], moe_wg[j], moe_wu[j], moe_wd[j])
                return t + m[:, 2:3] * rms_norm(y, g_post_f)
        x, ctx = token_mixer(x, ctx, jnp.concatenate(ml[0:3], axis=1), jnp.stack(mc[0:3])[None],
                             g_pre_m, g_post_m, p, not last)
        x = channel_mix(x, jnp.concatenate(ml[3:6], axis=1))
        if not last:
            ctx = channel_mix(ctx, jnp.stack(mc[3:6])[None])
    return x
```

```python
import math
import functools
import numpy as np
import jax
import jax.numpy as jnp
from jax import lax
from jax.experimental import pallas as pl
from jax.experimental.pallas import tpu as pltpu

D_MODEL = 1024
BATCH = 8
SEQ = 4096
DEPTH = 2
GRID_W = 64
CTX_LEN = 256
NORM_EPS = 1e-6
S5_WIDTH = D_MODEL // 4
S5_GROUP = 16
S5_GROUPS = S5_WIDTH // S5_GROUP
S5_STATE = 64
RW_WIDTH = D_MODEL // 2
RW_HEAD = 64
RW_HEADS = RW_WIDTH // RW_HEAD
RW_DECAY_LORA = 64
RW_ICLR_LORA = 64
RW_GATE_LORA = 128
RW_GN_EPS = 64e-5
HY_WIDTH = D_MODEL // 4
HY_ORDER = 2
HY_POS_DIM = 33
HY_FILTER_HIDDEN = 64
HY_DECAY_TARGET = 1e-2
HY_DECAY_PCT_SHORT = 0.3
HY_DECAY_PCT_LONG = 1.5
SHORT_CONV = 3
N_BRANCH = 3
FFN_HIDDEN = 2816
N_EXPERTS = 8
TOP_K = 2
EXPERT_HIDDEN = 3584
MOE_BLOCK = 512
IN_S5 = S5_WIDTH
IN_RW = 3 * RW_WIDTH
IN_LORA = 2 * RW_DECAY_LORA + 2 * RW_ICLR_LORA + RW_GATE_LORA
IN_HY = (HY_ORDER + 1) * HY_WIDTH
IN_GATE = N_BRANCH * D_MODEL
IN_COLS = IN_S5 + IN_RW + IN_LORA + IN_HY + IN_GATE

VMEM_LIMIT = 56 * 1024 * 1024
S5_CHUNK = 64
RW_CHUNK = 64
RW_HG = 4
RW_LANES = RW_HG * RW_HEAD
RW_BLOCK_TOKENS = 256
RW_BATCH_PER_STEP = 2
MOE_HIDDEN_TILES = 2
FFN_BLOCK_ROWS = 256
MIX_BLOCK_ROWS = 256


def rms_norm(x, gain):
    y = x * lax.rsqrt(jnp.mean(x * x, axis=-1, keepdims=True) + NORM_EPS)
    return y * gain


def s5_chunk_operators(lam_re, lam_im, log_step, b_re, b_im, c_re, c_im):
    T, G, P, GC = S5_CHUNK, S5_GROUPS, S5_STATE, S5_GROUP
    hp = lax.Precision.HIGHEST
    step = jnp.exp(log_step)[..., None]
    lr, li = lam_re, lam_im
    tau = jnp.arange(T + 1, dtype=jnp.float32)[:, None, None, None]
    mag = jnp.exp(lr * step * tau)
    ang = li * step * tau
    e_re, e_im = mag * jnp.cos(ang), mag * jnp.sin(ang)
    ab_re, ab_im = e_re[1], e_im[1]
    den = lr * lr + li * li
    nr, ni = ab_re - 1.0, ab_im
    q_re = (nr * lr + ni * li) / den
    q_im = (ni * lr - nr * li) / den
    bb_re = q_re[..., None] * b_re - q_im[..., None] * b_im
    bb_im = q_re[..., None] * b_im + q_im[..., None] * b_re
    eb_re = e_re[..., None] * bb_re - e_im[..., None] * bb_im
    eb_im = e_re[..., None] * bb_im + e_im[..., None] * bb_re
    ktau = (jnp.einsum('dgip,tdgpj->tdgij', c_re, eb_re[:T], precision=hp)
            - jnp.einsum('dgip,tdgpj->tdgij', c_im, eb_im[:T], precision=hp))
    t_idx = jnp.arange(T)
    k_f = jnp.transpose(ktau[:, 0], (1, 3, 0, 2))
    k_b = jnp.transpose(ktau[:, 1], (1, 3, 0, 2))
    table = jnp.concatenate([k_b[:, :, :0:-1], k_f[:, :, :1] + k_b[:, :, :1], k_f[:, :, 1:]], axis=2)
    table = table.astype(jnp.bfloat16).reshape(G, GC, (2 * T - 1) * GC)
    kt = jnp.stack([table[:, :, (T - 1 - s) * GC:(2 * T - 1 - s) * GC] for s in range(T)], axis=1)
    kt = kt.reshape(G, T * GC, T * GC)
    wf_re, wf_im = eb_re[T - 1 - t_idx, 0], eb_im[T - 1 - t_idx, 0]
    wb_re, wb_im = eb_re[t_idx, 1], eb_im[t_idx, 1]
    win = jnp.stack([wf_re, wf_im, wb_re, wb_im], axis=0)
    win = jnp.transpose(win, (2, 1, 4, 0, 3)).reshape(G, T * GC, 4 * P)
    ef_re, ef_im = e_re[t_idx + 1, 0], e_im[t_idx + 1, 0]
    eb2_re, eb2_im = e_re[T - t_idx, 1], e_im[T - t_idx, 1]

    def readout(cr, ci, er, ei):
        re = cr[None] * er[:, :, None, :] - ci[None] * ei[:, :, None, :]
        im = -(cr[None] * ei[:, :, None, :] + ci[None] * er[:, :, None, :])
        return re, im

    of_re, of_im = readout(c_re[0], c_im[0], ef_re, ef_im)
    ob_re, ob_im = readout(c_re[1], c_im[1], eb2_re, eb2_im)
    wout = jnp.stack([of_re, of_im, ob_re, ob_im], axis=0)
    wout = jnp.transpose(wout, (2, 0, 4, 1, 3)).reshape(G, 4 * P, T * GC)
    at_re, at_im = e_re[T], e_im[T]
    apow = jnp.concatenate([at_re[0], at_re[0], at_re[1], at_re[1]], axis=-1)[:, None, :]
    aimg = jnp.concatenate([-at_im[0], at_im[0], -at_im[1], at_im[1]], axis=-1)[:, None, :]
    return kt.astype(jnp.bfloat16), win.astype(jnp.bfloat16), wout.astype(jnp.bfloat16), apow, aimg


def _s5_kernel(x_ref, kt_ref, win_ref, wout_ref, apow_ref, aimg_ref, h0_ref, y_ref, hfin_ref, hin_ref, *,
               n_chunks, bsz):
    P2 = 2 * S5_STATE
    xb = x_ref[0]
    hin_ref[...] = jnp.dot(xb, win_ref[0], preferred_element_type=jnp.float32)
    ap = apow_ref[0]
    ai = aimg_ref[0]
    apf, aif = ap[:, :P2], ai[:, :P2]
    apb, aib = ap[:, P2:], ai[:, P2:]

    def cmul(h, a_p, a_i):
        return h * a_p + pltpu.roll(h, S5_STATE, axis=1) * a_i

    def body(c, carry):
        hf, hb = carry
        rf = pl.ds(pl.multiple_of(c * bsz, bsz), bsz)
        rb = pl.ds(pl.multiple_of((n_chunks - 1 - c) * bsz, bsz), bsz)
        df = hin_ref[rf, :P2]
        db = hin_ref[rb, P2:]
        hin_ref[rf, :P2] = hf
        hin_ref[rb, P2:] = hb
        return cmul(hf, apf, aif) + df, cmul(hb, apb, aib) + db

    h0 = h0_ref[0]
    hf, hb = lax.fori_loop(0, n_chunks, body, (h0[:, :P2], h0[:, P2:]))
    hfin_ref[0, :, :P2] = hf
    hfin_ref[0, :, P2:] = hb
    y = jnp.dot(xb, kt_ref[0], preferred_element_type=jnp.float32)
    y = y + jnp.dot(hin_ref[...].astype(jnp.bfloat16), wout_ref[0], preferred_element_type=jnp.float32)
    y_ref[0] = y


def s5_scan_pallas(u, ops, h0):
    kt, win, wout, apow, aimg = ops
    bsz, n, _ = u.shape
    T, G, P, GC = S5_CHUNK, S5_GROUPS, S5_STATE, S5_GROUP
    nc = n // T
    rows = nc * bsz
    x = jnp.transpose(u.astype(jnp.bfloat16).reshape(bsz, nc, T, G, GC), (3, 1, 0, 2, 4)).reshape(G, rows, T * GC)
    y, hfin = pl.pallas_call(
        functools.partial(_s5_kernel, n_chunks=nc, bsz=bsz),
        out_shape=(jax.ShapeDtypeStruct((G, rows, T * GC), jnp.float32),
                   jax.ShapeDtypeStruct((G, bsz, 4 * P), jnp.float32)),
        grid=(G,),
        in_specs=[pl.BlockSpec((1, rows, T * GC), lambda g: (g, 0, 0)),
                  pl.BlockSpec((1, T * GC, T * GC), lambda g: (g, 0, 0)),
                  pl.BlockSpec((1, T * GC, 4 * P), lambda g: (g, 0, 0)),
                  pl.BlockSpec((1, 4 * P, T * GC), lambda g: (g, 0, 0)),
                  pl.BlockSpec((1, 1, 4 * P), lambda g: (g, 0, 0)),
                  pl.BlockSpec((1, 1, 4 * P), lambda g: (g, 0, 0)),
                  pl.BlockSpec((1, bsz, 4 * P), lambda g: (g, 0, 0))],
        out_specs=(pl.BlockSpec((1, rows, T * GC), lambda g: (g, 0, 0)),
                   pl.BlockSpec((1, bsz, 4 * P), lambda g: (g, 0, 0))),
        scratch_shapes=[pltpu.VMEM((rows, 4 * P), jnp.float32)],
        compiler_params=pltpu.CompilerParams(dimension_semantics=("parallel",),
                                             vmem_limit_bytes=VMEM_LIMIT),
        name="s5_scan",
    )(x, kt, win, wout, apow, aimg, h0)
    y = jnp.transpose(y.reshape(G, nc, bsz, T, GC), (2, 1, 3, 0, 4)).reshape(bsz, n, G * GC)
    return y, hfin


def s5_branch(u, u_ctx, p):
    ops = s5_chunk_operators(p['s5_lam_re'], p['s5_lam_im'], p['s5_log_step'], p['s5_b_re'], p['s5_b_im'],
                             p['s5_c_re'], p['s5_c_im'])
    h0 = jnp.zeros((S5_GROUPS, u.shape[0], 4 * S5_STATE), jnp.float32)
    y_ctx, h_ctx = s5_scan_pallas(u_ctx, ops, h0)
    y_lat, _ = s5_scan_pallas(u, ops, h_ctx)
    return y_lat, y_ctx


_NT = (((1,), (1,)), ((), ()))

_M_SAME, _M_EYE = 0, 1
_W_STRICT, _W_INCL, _W_EYE, _W_LEVEL0 = 0, 2, 4, 5
_N_LEVELS = 6


def rwkv_masks():
    C, n = RW_CHUNK, RW_LANES
    row = np.arange(n)[:, None]
    col = np.arange(n)[None, :]
    square = np.stack([(row // C) == (col // C), row == col])
    tt, jj = np.arange(C)[:, None], np.arange(C)[None, :]
    wide = [jj < tt, jj > tt, jj <= tt, jj >= tt, jj == tt]
    for s in (1, 2, 4, 8, 16, 32):
        fwd = ((tt // s) % 2 == 1) & ((jj // s) == (tt // s) - 1)
        wide += [fwd, fwd.T]
    wide = np.stack([np.tile(m, (1, RW_HG)) for m in wide])
    tri = np.stack([jj <= tt, jj >= tt])
    return (jnp.asarray(square, jnp.float32), jnp.asarray(wide, jnp.float32), jnp.asarray(tri, jnp.bfloat16))


def _bdot(a, b, dims=None):
    a = a.astype(jnp.bfloat16)
    b = b.astype(jnp.bfloat16)
    if dims is None:
        return jnp.dot(a, b, preferred_element_type=jnp.float32)
    return lax.dot_general(a, b, dims, preferred_element_type=jnp.float32)


def _rwkv_chunk(S, r, lw, k, v, kk, b, m_ref, w_ref, same_b, tri, d):
    C = RW_CHUNK
    same = m_ref[_M_SAME]
    lw_hi = lw.astype(jnp.bfloat16)
    rem = lw - lw_hi.astype(jnp.float32)
    lw_mid = rem.astype(jnp.bfloat16)
    lw_lo = (rem - lw_mid.astype(jnp.float32)).astype(jnp.bfloat16)
    cl = (jnp.dot(tri, lw_hi, preferred_element_type=jnp.float32)
          + jnp.dot(tri, lw_mid, preferred_element_type=jnp.float32)
          + jnp.dot(tri, lw_lo, preferred_element_type=jnp.float32))
    yield
    tot = jnp.sum(lw, axis=0, keepdims=True)
    e_neg = jnp.exp(-cl)
    e_end = jnp.exp(tot - cl)
    a_t = -kk * jnp.exp(cl - lw)
    r_t = r * jnp.exp(cl)
    b_t = b * e_neg
    k_t = k * e_neg
    b_h = b * e_end
    k_h = k * e_end
    p_c = jnp.exp(tot)

    def tile(x):
        return jnp.concatenate([x] * RW_HG, axis=0)

    def stack(x):
        return tile(x.astype(jnp.bfloat16)) * same_b

    n = RW_LANES
    sa = stack(a_t)
    wide = _bdot(jnp.concatenate([a_t, r_t], axis=0), jnp.concatenate([stack(b_t), stack(k_t)], axis=0), _NT)
    yield
    a_ab = wide[:C, :n] * w_ref[_W_STRICT + d]
    a_ak = wide[:C, n:] * w_ref[_W_STRICT + d]
    a_rb = wide[C:, :n] * w_ref[_W_INCL + d]
    a_rk = wide[C:, n:] * w_ref[_W_INCL + d]
    dinv = w_ref[_W_EYE] + a_ab * w_ref[_W_LEVEL0 + d]
    for l in range(1, _N_LEVELS):
        inner = _bdot(a_ab * w_ref[_W_LEVEL0 + 2 * l + d], stack(dinv))
        yield
        dinv = dinv + _bdot(dinv, stack(inner))
        yield
    av = _bdot(jnp.concatenate([a_ak, a_rk], axis=0), stack(v))
    yield
    akv = av[:C]
    arkv = av[C:]
    mu = _bdot(dinv, jnp.concatenate([sa, stack(akv)], axis=1))
    yield
    m1 = mu[:, :n]
    u0 = mu[:, n:]
    my = _bdot(a_rb, jnp.concatenate([stack(m1), stack(u0)], axis=1))
    yield
    m2 = r_t + my[:, :n]
    y0 = my[:, n:] + arkv
    mut = jnp.concatenate([m1, u0], axis=1).T
    gh = _bdot(mut, b_h)
    yield
    g = m_ref[_M_EYE] * p_c + gh[:n] * same
    hmat = (gh[n:] + _bdot(v.T, k_h)) * same
    y = _bdot(m2, S, _NT) + y0
    s_new = _bdot(S, g) + hmat
    return s_new, y


def _lockstep(gens):
    results = [None] * len(gens)
    active = list(range(len(gens)))
    while active:
        for i in list(active):
            try:
                next(gens[i])
            except StopIteration as stop:
                results[i] = stop.value
                active.remove(i)
    return results


def _rwkv_kernel(m_ref, w_ref, tri_ref, rf_ref, rb_ref, vf_ref, vb_ref, kkf_ref, kkb_ref, lwf_ref, kf_ref, bf_ref,
                 lwb_ref, kb_ref, bb_ref, s0_ref, yf_ref, yb_ref, sfin_ref, s_scr, *, n_chunks, n_groups, n_batch):
    i = pl.program_id(1)
    C = RW_CHUNK

    @pl.when(i == 0)
    def _():
        s_scr[...] = s0_ref[...]

    tri_f = tri_ref[0]
    tri_b = tri_ref[1]
    same_b = m_ref[_M_SAME].astype(jnp.bfloat16)

    def body(c, carry):
        rf = pl.ds(pl.multiple_of(c * C, C), C)
        rb = pl.ds(pl.multiple_of((n_chunks - 1 - c) * C, C), C)
        gens = []
        probs = [(e, g) for e in range(n_batch) for g in range(n_groups)]
        for e, g in probs:
            ln = slice(g * RW_LANES, (g + 1) * RW_LANES)
            gens.append(_rwkv_chunk(s_scr[e, 0, g], rf_ref[e, rf, ln], lwf_ref[e, rf, ln], kf_ref[e, rf, ln],
                                    vf_ref[e, rf, ln], kkf_ref[e, rf, ln], bf_ref[e, rf, ln],
                                    m_ref, w_ref, same_b, tri_f, 0))
            gens.append(_rwkv_chunk(s_scr[e, 1, g], rb_ref[e, rb, ln], lwb_ref[e, rb, ln], kb_ref[e, rb, ln],
                                    vb_ref[e, rb, ln], kkb_ref[e, rb, ln], bb_ref[e, rb, ln],
                                    m_ref, w_ref, same_b, tri_b, 1))
        out = _lockstep(gens)
        for q, (e, g) in enumerate(probs):
            ln = slice(g * RW_LANES, (g + 1) * RW_LANES)
            (s_f, y_f), (s_b, y_b) = out[2 * q], out[2 * q + 1]
            s_scr[e, 0, g] = s_f
            s_scr[e, 1, g] = s_b
            yf_ref[e, rf, ln] = y_f
            yb_ref[e, rb, ln] = y_b
        return carry

    lax.fori_loop(0, n_chunks, body, 0)

    @pl.when(i == pl.num_programs(1) - 1)
    def _():
        sfin_ref[...] = s_scr[...]


def rwkv_scan_pallas(rkv, kk, lw, kd, bvec, s0, mask_set, block_tokens):
    bsz, n, width = kk.shape
    masks, wide, tri = mask_set
    ng = width // RW_LANES
    tb = block_tokens
    nb = n // tb
    eb = RW_BATCH_PER_STEP
    assert bsz % eb == 0
    fwd = pl.BlockSpec((eb, tb, width), lambda b, i: (b, i, 0))
    bwd = pl.BlockSpec((eb, tb, width), lambda b, i: (b, nb - 1 - i, 0))
    r_fwd, r_bwd = fwd, bwd
    v_fwd = pl.BlockSpec((eb, tb, width), lambda b, i: (b, i, 2))
    v_bwd = pl.BlockSpec((eb, tb, width), lambda b, i: (b, nb - 1 - i, 2))
    state_spec = pl.BlockSpec((eb, 2, ng, RW_LANES, RW_LANES), lambda b, i: (b, 0, 0, 0, 0))
    return pl.pallas_call(
        functools.partial(_rwkv_kernel, n_chunks=tb // RW_CHUNK, n_groups=ng, n_batch=eb),
        out_shape=(jax.ShapeDtypeStruct((bsz, n, width), jnp.float32),
                   jax.ShapeDtypeStruct((bsz, n, width), jnp.float32),
                   jax.ShapeDtypeStruct(s0.shape, jnp.float32)),
        grid=(bsz // eb, nb),
        in_specs=[pl.BlockSpec(masks.shape, lambda b, i: (0, 0, 0)),
                  pl.BlockSpec(wide.shape, lambda b, i: (0, 0, 0)),
                  pl.BlockSpec(tri.shape, lambda b, i: (0, 0, 0)),
                  r_fwd, r_bwd, v_fwd, v_bwd, fwd, bwd,
                  fwd, fwd, fwd, bwd, bwd, bwd, state_spec],
        out_specs=(fwd, bwd, state_spec),
        scratch_shapes=[pltpu.VMEM((eb, 2, ng, RW_LANES, RW_LANES), jnp.float32)],
        compiler_params=pltpu.CompilerParams(dimension_semantics=("parallel", "arbitrary"),
                                             vmem_limit_bytes=VMEM_LIMIT),
        name="rwkv_scan",
    )(masks, wide, tri, rkv, rkv, rkv, rkv, kk, kk, lw[0], kd[0], bvec[0], lw[1], kd[1], bvec[1], s0)


def _head_sum(t, hm):
    f32, bf16 = jnp.float32, jnp.bfloat16
    hi = t.astype(bf16)
    rem = t - hi.astype(f32)
    mid = rem.astype(bf16)
    lo = (rem - mid.astype(f32)).astype(bf16)
    return (jnp.dot(hi, hm, preferred_element_type=f32) + jnp.dot(mid, hm, preferred_element_type=f32)
            + jnp.dot(lo, hm, preferred_element_type=f32))


def _rwkv_prep_kernel(rkv_ref, lora_ref, vec_ref, w0_ref, a0_ref, w2_ref, a2_ref, hsum_ref,
                      kk_ref, lw0_ref, kd0_ref, b0_ref, lw1_ref, kd1_ref, b1_ref, bonus_ref):
    f32, bf16 = jnp.float32, jnp.bfloat16
    w = RW_WIDTH
    r = rkv_ref[:, 0:w]
    k = rkv_ref[:, w:2 * w]
    v = rkv_ref[:, 2 * w:3 * w]
    hm = hsum_ref[...]
    kk = k * vec_ref[0:1, :]
    kk = kk * lax.rsqrt(jnp.maximum(_head_sum(kk * kk, hm), 1e-24))
    kk_ref[...] = kk
    w_lo = jnp.tanh(lora_ref[:, 0:128]).astype(bf16)
    a_lo = lora_ref[:, 128:256].astype(bf16)
    rrk = r * vec_ref[2:3, :]
    bonus = None
    outs = ((lw0_ref, kd0_ref, b0_ref), (lw1_ref, kd1_ref, b1_ref))
    for d in range(2):
        x = w0_ref[d:d + 1, :] + jnp.dot(w_lo, w2_ref[d], preferred_element_type=f32)
        w_log = -(jnp.maximum(-x, 0.0) + jnp.log(1.0 + jnp.exp(-jnp.abs(x)))) - 0.5
        a = jax.nn.sigmoid(a0_ref[d:d + 1, :] + jnp.dot(a_lo, a2_ref[d], preferred_element_type=f32))
        kd = k * (1.0 + (a - 1.0) * vec_ref[1:2, :])
        outs[d][0][...] = -jnp.exp(w_log)
        outs[d][1][...] = kd
        outs[d][2][...] = kk * a
        term = _head_sum(rrk * kd, hm)
        bonus = term if bonus is None else bonus + term
    bonus_ref[...] = bonus * v


def rwkv_prep(rkv, lora, p):
    bsz, n, _ = rkv.shape
    rows = bsz * n
    tm = min(MIX_BLOCK_ROWS, n)
    w = RW_WIDTH
    hid = np.arange(w) // RW_HEAD
    hsum = jnp.asarray(hid[:, None] == hid[None, :], jnp.bfloat16)
    vec = jnp.stack([p['rw_kk'], p['rw_ka'], p['rw_rk']])
    zero = jnp.zeros((RW_DECAY_LORA, w), jnp.float32)
    w2 = jnp.stack([jnp.concatenate([p['rw_w2'][0], zero]), jnp.concatenate([zero, p['rw_w2'][1]])])
    a2 = jnp.stack([jnp.concatenate([p['rw_a2'][0], zero]), jnp.concatenate([zero, p['rw_a2'][1]])])
    consts = [vec, p['rw_w0'], p['rw_a0'], w2.astype(jnp.bfloat16), a2.astype(jnp.bfloat16), hsum]

    def const(a):
        return pl.BlockSpec(a.shape, lambda i: (0,) * a.ndim)

    outs = pl.pallas_call(
        _rwkv_prep_kernel,
        out_shape=tuple(jax.ShapeDtypeStruct((rows, w), jnp.float32) for _ in range(8)),
        grid=(rows // tm,),
        in_specs=[pl.BlockSpec((tm, 3 * w), lambda i: (i, 0)), pl.BlockSpec((tm, lora.shape[-1]), lambda i: (i, 0))]
                 + [const(c) for c in consts],
        out_specs=tuple(pl.BlockSpec((tm, w), lambda i: (i, 0)) for _ in range(8)),
        compiler_params=pltpu.CompilerParams(dimension_semantics=("parallel",), vmem_limit_bytes=VMEM_LIMIT),
        name="rwkv_prep",
    )(rkv.reshape(rows, 3 * w), lora.reshape(rows, -1), *consts)
    kk, lw0, kd0, b0, lw1, kd1, b1, bonus = [o.reshape(bsz, n, w) for o in outs]
    return kk, (lw0, lw1), (kd0, kd1), (b0, b1), bonus


def rwkv_branch(rkv, lora, p, s0, mask_set, block_tokens):
    kk, lw, kd, bvec, bonus = rwkv_prep(rkv, lora, p)
    y_f, y_b, s_fin = rwkv_scan_pallas(rkv, kk, lw, kd, bvec, s0, mask_set, block_tokens)
    return (y_f, y_b, bonus, lora[..., IN_LORA - RW_GATE_LORA:]), s_fin


def hyena_filters(n_tok, p):
    hp = lax.Precision.HIGHEST
    bands = (HY_POS_DIM - 1) // 2
    t = jnp.linspace(0.0, 1.0, n_tok, dtype=jnp.float32)[:, None]
    w = (2.0 * math.pi / n_tok) * jnp.arange(n_tok, dtype=jnp.float32)[:, None]
    f = jnp.linspace(1e-4, bands - 1, bands, dtype=jnp.float32)[None, :]
    feats = jnp.concatenate([t, jnp.cos(f * w), -jnp.sin(f * w)], axis=-1)
    h = jnp.sin(p['hy_f_freq1'] * (jnp.dot(feats, p['hy_f_w1'], precision=hp) + p['hy_f_b1']))
    h = jnp.sin(p['hy_f_freq2'] * (jnp.dot(h, p['hy_f_w2'], precision=hp) + p['hy_f_b2']))
    h = jnp.dot(h, p['hy_f_w3'], precision=hp).reshape(n_tok, HY_ORDER, 2, HY_WIDTH)
    rates = jnp.abs(jnp.linspace(math.log(HY_DECAY_TARGET) / HY_DECAY_PCT_SHORT,
                                 math.log(HY_DECAY_TARGET) / HY_DECAY_PCT_LONG, HY_WIDTH, dtype=jnp.float32))
    h = h * jnp.exp(-t * rates)[:, None, None, :]
    return h.at[0, :, 1].set(0.0)


HY_N = 2 * SEQ
HY_N1 = 64
HY_N2 = 128
HY_NH = HY_N1 // 2
HY_KH = HY_N1 // 2 + 1
HY_KP = 40
HY_PITCH = 88
HY_LANES = 128


def hyena_dft_tables():
    k1 = np.arange(HY_KH)[:, None]
    n1 = np.arange(HY_NH)[None, :]
    n2 = np.arange(HY_N2)
    ph = -2 * np.pi * (k1 * n1 / HY_N1)[None] - 2 * np.pi * (n2[:, None, None] * k1[None] / HY_N)
    lhs1 = np.zeros((HY_N2, 2 * HY_KP, HY_NH))
    lhs1[:, :HY_KH] = np.cos(ph)
    lhs1[:, HY_KP:HY_KP + HY_KH] = np.sin(ph)
    wgt = np.where((k1 == 0) | (k1 == HY_N1 // 2), 1.0, 2.0)[None] / HY_N
    lhs2 = np.zeros((HY_N2, HY_NH, 2 * HY_KP))
    lhs2[:, :, :HY_KH] = (wgt * np.cos(ph)).transpose(0, 2, 1)
    lhs2[:, :, HY_KP:HY_KP + HY_KH] = (wgt * np.sin(ph)).transpose(0, 2, 1)
    kk = np.arange(HY_N2)
    ang = -2 * np.pi * np.outer(kk, kk) / HY_N2
    cr, ci = np.cos(ang), np.sin(ang)
    f_fwd = np.block([[cr, -ci], [ci, cr]])
    f_inv = np.block([[cr, ci], [-ci, cr]])

    def as_bf(a):
        return jnp.asarray(a, jnp.float32).astype(jnp.bfloat16)

    def low_bf(a):
        a = jnp.asarray(a, jnp.float32)
        return (a - a.astype(jnp.bfloat16).astype(jnp.float32)).astype(jnp.bfloat16)

    return as_bf(lhs1), as_bf(lhs2), as_bf(f_fwd), as_bf(f_inv), low_bf(lhs1), low_bf(f_fwd)


def _hy_spec_kernel(h_ref, l1h_ref, l1l_ref, ffh_ref, ffl_ref, o_ref, a_ref):
    f32, bf16 = jnp.float32, jnp.bfloat16
    n2n, nh, pitch, kp = HY_N2, HY_NH, HY_PITCH, HY_KP

    def dot3(a_hi, a_lo, x):
        x_hi = x.astype(bf16)
        x_lo = (x - x_hi.astype(f32)).astype(bf16)
        return (jnp.dot(a_hi, x_hi, preferred_element_type=f32) + jnp.dot(a_hi, x_lo, preferred_element_type=f32)
                + jnp.dot(a_lo, x_hi, preferred_element_type=f32))

    def stage1(q, c):
        slab = h_ref[pl.ds(pl.multiple_of(q * nh, nh), nh), :]
        a_ref[pl.ds(pl.multiple_of(q * pitch, 8), 2 * kp), :] = dot3(l1h_ref[q], l1l_ref[q], slab)
        return c

    lax.fori_loop(0, n2n, stage1, 0, unroll=8)

    def stage2(k, c):
        a = jnp.concatenate([a_ref[pl.ds(k, n2n, stride=pitch), :],
                             a_ref[pl.ds(kp + k, n2n, stride=pitch), :]], axis=0)
        o_ref[0, pl.ds(pl.multiple_of(k * 2 * n2n, 2 * n2n), 2 * n2n), :] = dot3(ffh_ref[...], ffl_ref[...], a)
        return c

    lax.fori_loop(0, HY_KH, stage2, 0, unroll=3)


def hyena_filter_spectra_long(h, tables):
    n, orders, _, w = h.shape
    l1, _, ff, _, l1_lo, ff_lo = tables
    ch = orders * 2 * w
    sig = jnp.transpose(h.reshape(HY_NH, HY_N2, ch), (1, 0, 2)).reshape(n, ch)
    rows = HY_KH * 2 * HY_N2

    def const(a):
        return pl.BlockSpec(a.shape, lambda t: (0,) * a.ndim)

    spec = pl.pallas_call(
        _hy_spec_kernel,
        out_shape=jax.ShapeDtypeStruct((ch // HY_LANES, rows, HY_LANES), jnp.float32),
        grid=(ch // HY_LANES,),
        in_specs=[pl.BlockSpec((n, HY_LANES), lambda t: (0, t)), const(l1), const(l1_lo), const(ff), const(ff_lo)],
        out_specs=pl.BlockSpec((1, rows, HY_LANES), lambda t: (t, 0, 0)),
        scratch_shapes=[pltpu.VMEM((HY_N2 * HY_PITCH, HY_LANES), jnp.float32)],
        compiler_params=pltpu.CompilerParams(dimension_semantics=("parallel",), vmem_limit_bytes=VMEM_LIMIT),
        name="hyena_filter_spectrum",
    )(sig, l1, l1_lo, ff, ff_lo)
    spec = spec.reshape(orders, 2, w // HY_LANES, HY_KH, 2, HY_N2, HY_LANES)
    sign = jnp.asarray([1.0, -1.0], jnp.float32)[:, None, None]
    both = spec[:, 0] + sign * spec[:, 1]
    return both.reshape(orders, w // HY_LANES, rows, HY_LANES)


def _hy_kernel(z_ref, g_ref, k_ref, l1_ref, l2_ref, ff_ref, fi_ref, bias_ref, o_ref, a_ref):
    f32, bf16 = jnp.float32, jnp.bfloat16
    n2n, nh, pitch, kp = HY_N2, HY_NH, HY_PITCH, HY_KP

    def stage1(q, c):
        slab = z_ref[0, pl.ds(pl.multiple_of(q * nh, nh), nh), :].astype(bf16)
        a_ref[pl.ds(pl.multiple_of(q * pitch, 8), 2 * kp), :] = jnp.dot(l1_ref[q], slab, preferred_element_type=f32)
        return c

    lax.fori_loop(0, n2n, stage1, 0, unroll=16)

    def stage2(k, c):
        a = jnp.concatenate([a_ref[pl.ds(k, n2n, stride=pitch), :],
                             a_ref[pl.ds(kp + k, n2n, stride=pitch), :]], axis=0).astype(bf16)
        x = jnp.dot(ff_ref[...], a, preferred_element_type=f32)
        xr, xi = x[:n2n], x[n2n:]
        base = pl.multiple_of(k * 2 * n2n, 2 * n2n)
        kr = k_ref[0, pl.ds(base, n2n), :]
        ki = k_ref[0, pl.ds(base + n2n, n2n), :]
        y = jnp.concatenate([xr * kr - xi * ki, xr * ki + xi * kr], axis=0).astype(bf16)
        b = jnp.dot(fi_ref[...], y, preferred_element_type=f32)
        a_ref[pl.ds(k, n2n, stride=pitch), :] = b[:n2n]
        a_ref[pl.ds(kp + k, n2n, stride=pitch), :] = b[n2n:]
        return c

    lax.fori_loop(0, HY_KH, stage2, 0, unroll=11)

    def stage3(q, c):
        blk = a_ref[pl.ds(pl.multiple_of(q * pitch, 8), 2 * kp), :].astype(bf16)
        y = jnp.dot(l2_ref[q], blk, preferred_element_type=f32)
        rows = pl.ds(pl.multiple_of(q * nh, nh), nh)
        o_ref[0, rows, :] = g_ref[0, rows, :] * (y + z_ref[0, rows, :] * bias_ref[...])
        return c

    lax.fori_loop(0, n2n, stage3, 0, unroll=16)


def hyena_long_conv_gated(zt, gt, spec, bias, tables):
    bsz, n, w = zt.shape
    l1, l2, ff, fi = tables[:4]
    nt = w // HY_LANES
    tok = pl.BlockSpec((1, n, HY_LANES), lambda t, b: (b, 0, t))
    return pl.pallas_call(
        _hy_kernel,
        out_shape=jax.ShapeDtypeStruct((bsz, n, w), jnp.float32),
        grid=(nt, bsz),
        in_specs=[tok, tok,
                  pl.BlockSpec((1,) + spec.shape[1:], lambda t, b: (t, 0, 0)),
                  pl.BlockSpec(l1.shape, lambda t, b: (0, 0, 0)),
                  pl.BlockSpec(l2.shape, lambda t, b: (0, 0, 0)),
                  pl.BlockSpec(ff.shape, lambda t, b: (0, 0)),
                  pl.BlockSpec(fi.shape, lambda t, b: (0, 0)),
                  pl.BlockSpec((1, HY_LANES), lambda t, b: (0, t))],
        out_specs=tok,
        scratch_shapes=[pltpu.VMEM((HY_N2 * HY_PITCH, HY_LANES), jnp.float32)],
        compiler_params=pltpu.CompilerParams(dimension_semantics=("parallel", "parallel"),
                                             vmem_limit_bytes=VMEM_LIMIT),
        name="hyena_conv",
    )(zt, gt, spec, l1, l2, ff, fi, bias.reshape(1, w))


def hyena_branch_long(streams, p):
    bsz, n, w3 = streams.shape
    tables = hyena_dft_tables()
    spec = hyena_filter_spectra_long(hyena_filters(n, p), tables)
    st = jnp.transpose(streams.reshape(bsz, HY_NH, HY_N2, w3), (0, 2, 1, 3)).reshape(bsz, n, w3)
    z, x1, x2 = jnp.split(st, 3, axis=-1)
    for o, gate in enumerate((x1, x2)):
        z = hyena_long_conv_gated(z, gate, spec[o], p['hy_bias'][o], tables)
    w = z.shape[-1]
    return jnp.transpose(z.reshape(bsz, HY_N2, HY_NH, w), (0, 2, 1, 3)).reshape(bsz, n, w)


def _hy_ctx_kernel(z_ref, g_ref, k_ref, fd_ref, fi_ref, bias_ref, o_ref):
    f32, bf16 = jnp.float32, jnp.bfloat16
    z = z_ref[0]
    n = fd_ref.shape[0] // 2
    x = jnp.dot(fd_ref[...], z.astype(bf16), preferred_element_type=f32)
    xr, xi = x[:n], x[n:]
    kr, ki = k_ref[0, :n, :], k_ref[0, n:, :]
    y = jnp.concatenate([xr * kr - xi * ki, xr * ki + xi * kr], axis=0).astype(bf16)
    conv = jnp.dot(fi_ref[...], y, preferred_element_type=f32)
    o_ref[0] = g_ref[0] * (conv + z * bias_ref[...])


def hyena_branch_ctx(streams, p):
    bsz, n, w3 = streams.shape
    w = w3 // 3
    nn = 2 * n
    ang = 2 * np.pi * np.outer(np.arange(nn), np.arange(n)) / nn
    h = hyena_filters(n, p)
    hp = lax.Precision.HIGHEST
    spec = jnp.concatenate([
        jnp.einsum('kt,tow->kow', jnp.asarray(np.cos(ang), jnp.float32), h[:, :, 0] + h[:, :, 1], precision=hp),
        jnp.einsum('kt,tow->kow', jnp.asarray(-np.sin(ang), jnp.float32), h[:, :, 0] - h[:, :, 1], precision=hp)],
        axis=0)
    fd = jnp.asarray(np.concatenate([np.cos(ang), -np.sin(ang)], axis=0), jnp.float32).astype(jnp.bfloat16)
    fi = jnp.asarray(np.concatenate([np.cos(ang).T, -np.sin(ang).T], axis=1) / nn, jnp.float32).astype(jnp.bfloat16)
    nt = w // HY_LANES
    tok = pl.BlockSpec((1, n, HY_LANES), lambda t, b: (b, 0, t))
    z, x1, x2 = jnp.split(streams, 3, axis=-1)
    for o, gate in enumerate((x1, x2)):
        sp = jnp.transpose(spec[:, o].reshape(2 * nn, nt, HY_LANES), (1, 0, 2)).astype(jnp.float32)
        z = pl.pallas_call(
            _hy_ctx_kernel,
            out_shape=jax.ShapeDtypeStruct((bsz, n, w), jnp.float32),
            grid=(nt, bsz),
            in_specs=[tok, tok,
                      pl.BlockSpec((1, 2 * nn, HY_LANES), lambda t, b: (t, 0, 0)),
                      pl.BlockSpec(fd.shape, lambda t, b: (0, 0)),
                      pl.BlockSpec(fi.shape, lambda t, b: (0, 0)),
                      pl.BlockSpec((1, HY_LANES), lambda t, b: (0, t))],
            out_specs=tok,
            compiler_params=pltpu.CompilerParams(dimension_semantics=("parallel", "parallel"),
                                                 vmem_limit_bytes=VMEM_LIMIT),
            name="hyena_conv_ctx",
        )(z, gate, sp, fd, fi, p['hy_bias'][o].reshape(1, w))
    return z


def _rms(x, gain):
    return x * lax.rsqrt(jnp.mean(x * x, axis=-1, keepdims=True) + NORM_EPS) * gain


def _conv3(x, w_ref, period):
    rows = x.shape[0]
    pos = lax.broadcasted_iota(jnp.int32, (rows, 1), 0) % period
    prev = jnp.where(pos == 0, 0.0, pltpu.roll(x, 1, axis=0))
    nxt = jnp.where(pos == period - 1, 0.0, pltpu.roll(x, rows - 1, axis=0))
    return prev * w_ref[0:1, :] + x * w_ref[1:2, :] + nxt * w_ref[2:3, :]


def _inproj_kernel(x_ref, mod_ref, gain_ref, ws5_ref, wrw_ref, wlo_ref, why_ref, wgt_ref, cwr_ref, cwh_ref, cbh_ref,
                   os5_ref, orw_ref, olo_ref, ohy_ref, ogt_ref, *, period):
    f32 = jnp.float32
    shift, scale = mod_ref[0, 0:1, :], mod_ref[0, 1:2, :]
    h = (_rms(x_ref[...], gain_ref[...]) * (1.0 + scale) + shift).astype(jnp.bfloat16)
    os5_ref[...] = jnp.dot(h, ws5_ref[...], preferred_element_type=f32)
    olo_ref[...] = jnp.dot(h, wlo_ref[...], preferred_element_type=f32)
    ogt_ref[...] = jnp.dot(h, wgt_ref[...], preferred_element_type=f32)
    orw_ref[...] = _conv3(jnp.dot(h, wrw_ref[...], preferred_element_type=f32), cwr_ref, period)
    ohy_ref[...] = _conv3(jnp.dot(h, why_ref[...], preferred_element_type=f32), cwh_ref, period) + cbh_ref[...]


def input_projection(x, mod, gain, w_parts, rw_conv_w, hy_conv_w, hy_conv_b, period):
    bsz, n, d = x.shape
    tm = min(MIX_BLOCK_ROWS, n)
    per_b = n // tm
    rows = bsz * n
    bmap = (lambda i: (i // per_b, 0, 0)) if mod.shape[0] == bsz else (lambda i: (0, 0, 0))
    wb = [w.astype(jnp.bfloat16) for w in w_parts]

    def const(a):
        return pl.BlockSpec(a.shape, lambda i: (0,) * a.ndim)

    cbh = hy_conv_b.reshape(1, -1)
    outs = pl.pallas_call(
        functools.partial(_inproj_kernel, period=period),
        out_shape=tuple(jax.ShapeDtypeStruct((rows, w.shape[1]), jnp.float32) for w in wb),
        grid=(rows // tm,),
        in_specs=[pl.BlockSpec((tm, d), lambda i: (i, 0)),
                  pl.BlockSpec((1, 3, d), bmap),
                  pl.BlockSpec((1, d), lambda i: (0, 0))]
                 + [const(w) for w in wb] + [const(rw_conv_w), const(hy_conv_w), const(cbh)],
        out_specs=tuple(pl.BlockSpec((tm, w.shape[1]), lambda i: (i, 0)) for w in wb),
        compiler_params=pltpu.CompilerParams(dimension_semantics=("parallel",), vmem_limit_bytes=VMEM_LIMIT),
        name="input_projection",
    )(x.reshape(rows, d), mod, gain.reshape(1, d), *wb, rw_conv_w, hy_conv_w, cbh)
    return tuple(o.reshape(bsz, n, o.shape[-1]) for o in outs)


def _merge_kernel(x_ref, mod_ref, gain_ref, ys5_ref, us5_ref, wf_ref, wb_ref, bonus_ref, glo_ref, hy_ref, gt_ref,
                  s5d_ref, glu_ref, lnw_ref, lnb_ref, g2_ref, havg_ref, brs_ref, brr_ref, brh_ref, ow_ref, o_ref):
    f32, bf16 = jnp.float32, jnp.bfloat16
    y = ys5_ref[...] + s5d_ref[...] * us5_ref[...]
    y = 0.5 * y * (1.0 + lax.erf(y * (1.0 / math.sqrt(2.0))))
    s5 = y * jax.nn.sigmoid(jnp.dot(y.astype(bf16), glu_ref[...], preferred_element_type=f32))
    wkv = wf_ref[...] + wb_ref[...]

    def head_mean(t):
        hi = t.astype(bf16)
        rem = t - hi.astype(f32)
        mid = rem.astype(bf16)
        lo = (rem - mid.astype(f32)).astype(bf16)
        hm = havg_ref[...]
        tot = (jnp.dot(hi, hm, preferred_element_type=f32) + jnp.dot(mid, hm, preferred_element_type=f32)
               + jnp.dot(lo, hm, preferred_element_type=f32))
        return tot * (1.0 / RW_HEAD)

    mu = head_mean(wkv)
    cen = wkv - mu
    var = head_mean(cen * cen)
    o = cen * lax.rsqrt(var + RW_GN_EPS) * lnw_ref[...] + lnb_ref[...] + bonus_ref[...]
    g = jnp.dot(jax.nn.sigmoid(glo_ref[...]).astype(bf16), g2_ref[...], preferred_element_type=f32)
    rw = o * g
    gt = jax.nn.sigmoid(gt_ref[...])
    d = x_ref.shape[-1]
    m = (gt[:, 0:d] * jnp.dot(s5.astype(bf16), brs_ref[...], preferred_element_type=f32)
         + gt[:, d:2 * d] * jnp.dot(rw.astype(bf16), brr_ref[...], preferred_element_type=f32)
         + gt[:, 2 * d:3 * d] * jnp.dot(hy_ref[...].astype(bf16), brh_ref[...], preferred_element_type=f32))
    yl = jnp.dot(m.astype(bf16), ow_ref[...], preferred_element_type=f32)
    o_ref[...] = x_ref[...] + mod_ref[0, 2:3, :] * _rms(yl, gain_ref[...])


def merge_block(x, mod, gain, ys5, us5, rw_parts, hy, gates, p):
    bsz, n, d = x.shape
    tm = min(MIX_BLOCK_ROWS, n)
    per_b = n // tm
    rows = bsz * n
    bmap = (lambda i: (i // per_b, 0, 0)) if mod.shape[0] == bsz else (lambda i: (0, 0, 0))

    def bf(a):
        return a.astype(jnp.bfloat16)

    def row(a):
        return pl.BlockSpec((tm, a.shape[-1]), lambda i: (i, 0))

    def const(a):
        return pl.BlockSpec(a.shape, lambda i: (0,) * a.ndim)

    hid = np.arange(RW_WIDTH) // RW_HEAD
    havg = jnp.asarray(hid[:, None] == hid[None, :], jnp.bfloat16)
    acts = [a.reshape(rows, a.shape[-1]) for a in (ys5, us5) + tuple(rw_parts) + (hy, gates)]
    consts = [p['s5_d'].reshape(1, -1), bf(p['s5_glu_w']), p['rw_ln_w'].reshape(1, -1), p['rw_ln_b'].reshape(1, -1),
              bf(p['rw_g2']), havg, bf(p['br_s5']), bf(p['br_rw']), bf(p['br_hy']), bf(p['out_w'])]
    out = pl.pallas_call(
        _merge_kernel,
        out_shape=jax.ShapeDtypeStruct((rows, d), jnp.float32),
        grid=(rows // tm,),
        in_specs=[pl.BlockSpec((tm, d), lambda i: (i, 0)), pl.BlockSpec((1, 3, d), bmap),
                  pl.BlockSpec((1, d), lambda i: (0, 0))] + [row(a) for a in acts] + [const(c) for c in consts],
        out_specs=pl.BlockSpec((tm, d), lambda i: (i, 0)),
        compiler_params=pltpu.CompilerParams(dimension_semantics=("parallel",), vmem_limit_bytes=VMEM_LIMIT),
        name="merge_block",
    )(x.reshape(rows, d), mod, gain.reshape(1, d), *acts, *consts)
    return out.reshape(bsz, n, d)


def token_mixer(x, ctx, mod_l, mod_c, g_pre, g_post, p, with_ctx_out):
    bsz = x.shape[0]
    cuts = [IN_S5, IN_S5 + IN_RW, IN_S5 + IN_RW + IN_LORA, IN_S5 + IN_RW + IN_LORA + IN_HY]
    edges = [0] + cuts + [IN_COLS]
    w_parts = [p['in_w'][:, edges[i]:edges[i + 1]] for i in range(5)]
    conv = (p['rw_conv_w'], p['hy_conv_w'], p['hy_conv_b'])
    u_l, rkv_l, lora_l, hy_l, gate_l = input_projection(x, mod_l, g_pre, w_parts, *conv, GRID_W)
    u_c, rkv_c, lora_c, hy_c, gate_c = input_projection(ctx, mod_c, g_pre, w_parts, *conv, CTX_LEN)
    mask_set = rwkv_masks()
    zr = jnp.zeros((bsz, 2, RW_WIDTH // RW_LANES, RW_LANES, RW_LANES), jnp.float32)
    ys5_l, ys5_c = s5_branch(u_l, u_c, p)
    rw_c, rw_state = rwkv_branch(rkv_c, lora_c, p, zr, mask_set, CTX_LEN)
    rw_l, _ = rwkv_branch(rkv_l, lora_l, p, rw_state, mask_set, RW_BLOCK_TOKENS)
    hy_lat = hyena_branch_long(hy_l, p)
    x = merge_block(x, mod_l, g_post, ys5_l, u_l, rw_l, hy_lat, gate_l, p)
    if with_ctx_out:
        ctx = merge_block(ctx, mod_c, g_post, ys5_c, u_c, rw_c, hyena_branch_ctx(hy_c, p), gate_c, p)
    return x, ctx


def _ffn_kernel(x_ref, mod_ref, gain_ref, wg_ref, wu_ref, wd_ref, o_ref):
    x = x_ref[...]
    shift, scale, gate = mod_ref[0, 0:1, :], mod_ref[0, 1:2, :], mod_ref[0, 2:3, :]
    h = (_rms(x, gain_ref[0:1, :]) * (1.0 + scale) + shift).astype(jnp.bfloat16)
    g = jnp.dot(h, wg_ref[...], preferred_element_type=jnp.float32)
    u = jnp.dot(h, wu_ref[...], preferred_element_type=jnp.float32)
    a = (g * jax.nn.sigmoid(g) * u).astype(jnp.bfloat16)
    y = jnp.dot(a, wd_ref[...], preferred_element_type=jnp.float32)
    o_ref[...] = x + gate * _rms(y, gain_ref[1:2, :])


def ffn_block(x, mod, gains, wg, wu, wd):
    bsz, n, d = x.shape
    hid = wg.shape[1]
    tm = min(FFN_BLOCK_ROWS, n)
    per_b = n // tm
    xf = x.reshape(bsz * n, d)
    bmap = (lambda i: (i // per_b, 0, 0)) if mod.shape[0] == bsz else (lambda i: (0, 0, 0))
    out = pl.pallas_call(
        _ffn_kernel,
        out_shape=jax.ShapeDtypeStruct((bsz * n, d), jnp.float32),
        grid=(bsz * per_b,),
        in_specs=[pl.BlockSpec((tm, d), lambda i: (i, 0)),
                  pl.BlockSpec((1, 3, d), bmap),
                  pl.BlockSpec((2, d), lambda i: (0, 0)),
                  pl.BlockSpec((d, hid), lambda i: (0, 0)),
                  pl.BlockSpec((d, hid), lambda i: (0, 0)),
                  pl.BlockSpec((hid, d), lambda i: (0, 0))],
        out_specs=pl.BlockSpec((tm, d), lambda i: (i, 0)),
        compiler_params=pltpu.CompilerParams(dimension_semantics=("parallel",), vmem_limit_bytes=VMEM_LIMIT),
        name="ffn_block",
    )(xf, mod, gains, wg.astype(jnp.bfloat16), wu.astype(jnp.bfloat16), wd.astype(jnp.bfloat16))
    return out.reshape(bsz, n, d)


def _moe_kernel(be_ref, x_ref, gate_ref, wg_ref, wu_ref, wd_ref, o_ref):
    j = pl.program_id(1)
    xb = x_ref[...]
    g = jnp.dot(xb, wg_ref[0], preferred_element_type=jnp.float32)
    u = jnp.dot(xb, wu_ref[0], preferred_element_type=jnp.float32)
    a = (g * jax.nn.sigmoid(g) * u).astype(jnp.bfloat16)
    y = jnp.dot(a, wd_ref[0], preferred_element_type=jnp.float32)

    @pl.when(j == 0)
    def _():
        o_ref[...] = y

    @pl.when(j > 0)
    def _():
        o_ref[...] += y

    @pl.when(j == pl.num_programs(1) - 1)
    def _():
        o_ref[...] *= gate_ref[...]


def moe_expert_blocks(xb, slot_gate, block_e, wg, wu, wd):
    cap, d = xb.shape
    n_blocks = cap // MOE_BLOCK
    hid = wg.shape[-1]
    nh = MOE_HIDDEN_TILES
    th = hid // nh

    def hidx(i, j):
        return j + (i % 2) * (nh - 1 - 2 * j)

    return pl.pallas_call(
        _moe_kernel,
        out_shape=jax.ShapeDtypeStruct((cap, d), jnp.float32),
        grid_spec=pltpu.PrefetchScalarGridSpec(
            num_scalar_prefetch=1,
            grid=(n_blocks, nh),
            in_specs=[pl.BlockSpec((MOE_BLOCK, d), lambda i, j, be: (i, 0)),
                      pl.BlockSpec((MOE_BLOCK, 1), lambda i, j, be: (i, 0)),
                      pl.BlockSpec((1, d, th), lambda i, j, be: (be[i], 0, hidx(i, j))),
                      pl.BlockSpec((1, d, th), lambda i, j, be: (be[i], 0, hidx(i, j))),
                      pl.BlockSpec((1, th, d), lambda i, j, be: (be[i], hidx(i, j), 0))],
            out_specs=pl.BlockSpec((MOE_BLOCK, d), lambda i, j, be: (i, 0))),
        compiler_params=pltpu.CompilerParams(dimension_semantics=("parallel", "arbitrary"),
                                             vmem_limit_bytes=VMEM_LIMIT),
        name="moe_experts",
    )(block_e, xb, slot_gate.reshape(cap, 1), wg.astype(jnp.bfloat16), wu.astype(jnp.bfloat16),
      wd.astype(jnp.bfloat16))


def moe_swiglu(h, router_w, wg, wu, wd):
    d_model = h.shape[-1]
    tok = h.reshape(-1, d_model)
    n = tok.shape[0]
    n_assign = n * TOP_K
    logits = jnp.dot(tok, router_w)
    top_logit, top_e = lax.top_k(logits, TOP_K)
    gate = jax.nn.softmax(top_logit, axis=-1).reshape(-1)
    flat_e = top_e.reshape(-1)
    order = jnp.argsort(flat_e)
    onehot = (flat_e[:, None] == jnp.arange(N_EXPERTS, dtype=flat_e.dtype)[None, :]).astype(jnp.int32)
    sizes = jnp.sum(onehot, axis=0)
    padded = (sizes + MOE_BLOCK - 1) // MOE_BLOCK * MOE_BLOCK
    pad_end = jnp.cumsum(padded)
    pad_start = pad_end - padded
    grp_start = jnp.cumsum(sizes) - sizes
    n_blocks = -(-n_assign // MOE_BLOCK) + N_EXPERTS
    cap = n_blocks * MOE_BLOCK
    block_start = jnp.arange(n_blocks, dtype=jnp.int32) * MOE_BLOCK
    block_e = jnp.minimum(jnp.sum(block_start[:, None] >= pad_end[None, :], axis=1), N_EXPERTS - 1)
    slot_e = jnp.repeat(block_e, MOE_BLOCK)
    within = jnp.arange(cap, dtype=jnp.int32) - pad_start[slot_e]
    valid = within < sizes[slot_e]
    assign = order[jnp.clip(grp_start[slot_e] + within, 0, n_assign - 1)]
    slot_tok = jnp.where(valid, (assign // TOP_K).astype(jnp.int32), 0)
    slot_gate = jnp.where(valid, gate[assign], 0.0)
    xb = tok.astype(jnp.bfloat16)[slot_tok]
    yb = moe_expert_blocks(xb, slot_gate, block_e.astype(jnp.int32), wg, wu, wd)
    rank = jnp.take_along_axis(jnp.cumsum(onehot, axis=0), flat_e[:, None], axis=1)[:, 0] - 1
    pos = (pad_start[flat_e] + rank).astype(jnp.int32).reshape(n, TOP_K)
    out = yb[pos[:, 0]]
    for kk in range(1, TOP_K):
        out = out + yb[pos[:, kk]]
    return out.reshape(h.shape)


def kernel(x, c, ctx, c_ctx, mod_w, mod_b, norm_g, in_w, s5_lam_re, s5_lam_im, s5_log_step, s5_b_re, s5_b_im, s5_c_re, s5_c_im, s5_d, s5_glu_w, rw_conv_w, rw_w0, rw_w2, rw_a0, rw_a2, rw_g2, rw_kk, rw_ka, rw_rk, rw_ln_w, rw_ln_b, hy_conv_w, hy_conv_b, hy_f_w1, hy_f_b1, hy_f_freq1, hy_f_w2, hy_f_b2, hy_f_freq2, hy_f_w3, hy_bias, br_s5, br_rw, br_hy, out_w, ffn_wg, ffn_wu, ffn_wd, moe_router, moe_wg, moe_wu, moe_wd):
    silu_c = jax.nn.silu(c)
    silu_cc = jax.nn.silu(c_ctx)
    for i in range(DEPTH):
        last = i == DEPTH - 1
        p = {
            'in_w': in_w[i],
            's5_lam_re': s5_lam_re[i], 's5_lam_im': s5_lam_im[i], 's5_log_step': s5_log_step[i],
            's5_b_re': s5_b_re[i], 's5_b_im': s5_b_im[i], 's5_c_re': s5_c_re[i], 's5_c_im': s5_c_im[i],
            's5_d': s5_d[i], 's5_glu_w': s5_glu_w[i],
            'rw_conv_w': rw_conv_w[i], 'rw_w0': rw_w0[i], 'rw_w2': rw_w2[i], 'rw_a0': rw_a0[i], 'rw_a2': rw_a2[i],
            'rw_g2': rw_g2[i], 'rw_kk': rw_kk[i], 'rw_ka': rw_ka[i], 'rw_rk': rw_rk[i],
            'rw_ln_w': rw_ln_w[i], 'rw_ln_b': rw_ln_b[i],
            'hy_conv_w': hy_conv_w[i], 'hy_conv_b': hy_conv_b[i], 'hy_f_w1': hy_f_w1[i], 'hy_f_b1': hy_f_b1[i],
            'hy_f_freq1': hy_f_freq1[i], 'hy_f_w2': hy_f_w2[i], 'hy_f_b2': hy_f_b2[i], 'hy_f_freq2': hy_f_freq2[i],
            'hy_f_w3': hy_f_w3[i], 'hy_bias': hy_bias[i],
            'br_s5': br_s5[i], 'br_rw': br_rw[i], 'br_hy': br_hy[i], 'out_w': out_w[i],
        }
        ml = jnp.split((silu_c @ mod_w[i] + mod_b[i])[:, None, :], 6, axis=-1)
        mc = jnp.split(silu_cc @ mod_w[i] + mod_b[i], 6, axis=-1)
        g_pre_m, g_post_m, g_pre_f, g_post_f = norm_g[i]
        if i % 2 == 0:
            def channel_mix(t, m, j=i // 2):
                return ffn_block(t, m, norm_g[i, 2:4], ffn_wg[j], ffn_wu[j], ffn_wd[j])
        else:
            def channel_mix(t, m, j=i // 2):
                h = rms_norm(t, g_pre_f) * (1 + m[:, 1:2]) + m[:, 0:1]
                y = moe_swiglu(h, moe_router[j], moe_wg[j], moe_wu[j], moe_wd[j])
                return t + m[:, 2:3] * rms_norm(y, g_post_f)
        x, ctx = token_mixer(x, ctx, jnp.concatenate(ml[0:3], axis=1), jnp.stack(mc[0:3])[None],
                             g_pre_m, g_post_m, p, not last)
        x = channel_mix(x, jnp.concatenate(ml[3:6], axis=1))
        if not last:
            ctx = channel_mix(ctx, jnp.stack(mc[3:6])[None])
    return x
```

```python
import math
import functools
import numpy as np
import jax
import jax.numpy as jnp
from jax import lax
from jax.experimental import pallas as pl
from jax.experimental.pallas import tpu as pltpu

D_MODEL = 1024
BATCH = 8
SEQ = 4096
DEPTH = 2
GRID_W = 64
CTX_LEN = 256
NORM_EPS = 1e-6
S5_WIDTH = D_MODEL // 4
S5_GROUP = 16
S5_GROUPS = S5_WIDTH // S5_GROUP
S5_STATE = 64
RW_WIDTH = D_MODEL // 2
RW_HEAD = 64
RW_HEADS = RW_WIDTH // RW_HEAD
RW_DECAY_LORA = 64
RW_ICLR_LORA = 64
RW_GATE_LORA = 128
RW_GN_EPS = 64e-5
HY_WIDTH = D_MODEL // 4
HY_ORDER = 2
HY_POS_DIM = 33
HY_FILTER_HIDDEN = 64
HY_DECAY_TARGET = 1e-2
HY_DECAY_PCT_SHORT = 0.3
HY_DECAY_PCT_LONG = 1.5
SHORT_CONV = 3
N_BRANCH = 3
FFN_HIDDEN = 2816
N_EXPERTS = 8
TOP_K = 2
EXPERT_HIDDEN = 3584
MOE_BLOCK = 512
IN_S5 = S5_WIDTH
IN_RW = 3 * RW_WIDTH
IN_LORA = 2 * RW_DECAY_LORA + 2 * RW_ICLR_LORA + RW_GATE_LORA
IN_HY = (HY_ORDER + 1) * HY_WIDTH
IN_GATE = N_BRANCH * D_MODEL
IN_COLS = IN_S5 + IN_RW + IN_LORA + IN_HY + IN_GATE

VMEM_LIMIT = 56 * 1024 * 1024
S5_CHUNK = 64
RW_CHUNK = 64
RW_HG = 4
RW_LANES = RW_HG * RW_HEAD
RW_BLOCK_TOKENS = 256
RW_BATCH_PER_STEP = 2
MOE_HIDDEN_TILES = 2
FFN_BLOCK_ROWS = 256
MIX_BLOCK_ROWS = 256


def rms_norm(x, gain):
    y = x * lax.rsqrt(jnp.mean(x * x, axis=-1, keepdims=True) + NORM_EPS)
    return y * gain


def s5_chunk_operators(lam_re, lam_im, log_step, b_re, b_im, c_re, c_im):
    T, G, P, GC = S5_CHUNK, S5_GROUPS, S5_STATE, S5_GROUP
    hp = lax.Precision.HIGHEST
    step = jnp.exp(log_step)[..., None]
    lr, li = lam_re, lam_im
    tau = jnp.arange(T + 1, dtype=jnp.float32)[:, None, None, None]
    mag = jnp.exp(lr * step * tau)
    ang = li * step * tau
    e_re, e_im = mag * jnp.cos(ang), mag * jnp.sin(ang)
    ab_re, ab_im = e_re[1], e_im[1]
    den = lr * lr + li * li
    nr, ni = ab_re - 1.0, ab_im
    q_re = (nr * lr + ni * li) / den
    q_im = (ni * lr - nr * li) / den
    bb_re = q_re[..., None] * b_re - q_im[..., None] * b_im
    bb_im = q_re[..., None] * b_im + q_im[..., None] * b_re
    eb_re = e_re[..., None] * bb_re - e_im[..., None] * bb_im
    eb_im = e_re[..., None] * bb_im + e_im[..., None] * bb_re
    ktau = (jnp.einsum('dgip,tdgpj->tdgij', c_re, eb_re[:T], precision=hp)
            - jnp.einsum('dgip,tdgpj->tdgij', c_im, eb_im[:T], precision=hp))
    t_idx = jnp.arange(T)
    k_f = jnp.transpose(ktau[:, 0], (1, 3, 0, 2))
    k_b = jnp.transpose(ktau[:, 1], (1, 3, 0, 2))
    table = jnp.concatenate([k_b[:, :, :0:-1], k_f[:, :, :1] + k_b[:, :, :1], k_f[:, :, 1:]], axis=2)
    table = table.astype(jnp.bfloat16).reshape(G, GC, (2 * T - 1) * GC)
    kt = jnp.stack([table[:, :, (T - 1 - s) * GC:(2 * T - 1 - s) * GC] for s in range(T)], axis=1)
    kt = kt.reshape(G, T * GC, T * GC)
    wf_re, wf_im = eb_re[T - 1 - t_idx, 0], eb_im[T - 1 - t_idx, 0]
    wb_re, wb_im = eb_re[t_idx, 1], eb_im[t_idx, 1]
    win = jnp.stack([wf_re, wf_im, wb_re, wb_im], axis=0)
    win = jnp.transpose(win, (2, 1, 4, 0, 3)).reshape(G, T * GC, 4 * P)
    ef_re, ef_im = e_re[t_idx + 1, 0], e_im[t_idx + 1, 0]
    eb2_re, eb2_im = e_re[T - t_idx, 1], e_im[T - t_idx, 1]

    def readout(cr, ci, er, ei):
        re = cr[None] * er[:, :, None, :] - ci[None] * ei[:, :, None, :]
        im = -(cr[None] * ei[:, :, None, :] + ci[None] * er[:, :, None, :])
        return re, im

    of_re, of_im = readout(c_re[0], c_im[0], ef_re, ef_im)
    ob_re, ob_im = readout(c_re[1], c_im[1], eb2_re, eb2_im)
    wout = jnp.stack([of_re, of_im, ob_re, ob_im], axis=0)
    wout = jnp.transpose(wout, (2, 0, 4, 1, 3)).reshape(G, 4 * P, T * GC)
    at_re, at_im = e_re[T], e_im[T]
    apow = jnp.concatenate([at_re[0], at_re[0], at_re[1], at_re[1]], axis=-1)[:, None, :]
    aimg = jnp.concatenate([-at_im[0], at_im[0], -at_im[1], at_im[1]], axis=-1)[:, None, :]
    return kt.astype(jnp.bfloat16), win.astype(jnp.bfloat16), wout.astype(jnp.bfloat16), apow, aimg


def _s5_kernel(x_ref, kt_ref, win_ref, wout_ref, apow_ref, aimg_ref, h0_ref, y_ref, hfin_ref, hin_ref, *,
               n_chunks, bsz):
    P2 = 2 * S5_STATE
    xb = x_ref[0]
    hin_ref[...] = jnp.dot(xb, win_ref[0], preferred_element_type=jnp.float32)
    ap = apow_ref[0]
    ai = aimg_ref[0]
    apf, aif = ap[:, :P2], ai[:, :P2]
    apb, aib = ap[:, P2:], ai[:, P2:]

    def cmul(h, a_p, a_i):
        return h * a_p + pltpu.roll(h, S5_STATE, axis=1) * a_i

    def body(c, carry):
        hf, hb = carry
        rf = pl.ds(pl.multiple_of(c * bsz, bsz), bsz)
        rb = pl.ds(pl.multiple_of((n_chunks - 1 - c) * bsz, bsz), bsz)
        df = hin_ref[rf, :P2]
        db = hin_ref[rb, P2:]
        hin_ref[rf, :P2] = hf
        hin_ref[rb, P2:] = hb
        return cmul(hf, apf, aif) + df, cmul(hb, apb, aib) + db

    h0 = h0_ref[0]
    hf, hb = lax.fori_loop(0, n_chunks, body, (h0[:, :P2], h0[:, P2:]), unroll=4)
    hfin_ref[0, :, :P2] = hf
    hfin_ref[0, :, P2:] = hb
    y = jnp.dot(xb, kt_ref[0], preferred_element_type=jnp.float32)
    y = y + jnp.dot(hin_ref[...].astype(jnp.bfloat16), wout_ref[0], preferred_element_type=jnp.float32)
    y_ref[0] = y


def s5_scan_pallas(u, ops, h0):
    kt, win, wout, apow, aimg = ops
    bsz, n, _ = u.shape
    T, G, P, GC = S5_CHUNK, S5_GROUPS, S5_STATE, S5_GROUP
    nc = n // T
    rows = nc * bsz
    x = jnp.transpose(u.astype(jnp.bfloat16).reshape(bsz, nc, T, G, GC), (3, 1, 0, 2, 4)).reshape(G, rows, T * GC)
    y, hfin = pl.pallas_call(
        functools.partial(_s5_kernel, n_chunks=nc, bsz=bsz),
        out_shape=(jax.ShapeDtypeStruct((G, rows, T * GC), jnp.float32),
                   jax.ShapeDtypeStruct((G, bsz, 4 * P), jnp.float32)),
        grid=(G,),
        in_specs=[pl.BlockSpec((1, rows, T * GC), lambda g: (g, 0, 0)),
                  pl.BlockSpec((1, T * GC, T * GC), lambda g: (g, 0, 0)),
                  pl.BlockSpec((1, T * GC, 4 * P), lambda g: (g, 0, 0)),
                  pl.BlockSpec((1, 4 * P, T * GC), lambda g: (g, 0, 0)),
                  pl.BlockSpec((1, 1, 4 * P), lambda g: (g, 0, 0)),
                  pl.BlockSpec((1, 1, 4 * P), lambda g: (g, 0, 0)),
                  pl.BlockSpec((1, bsz, 4 * P), lambda g: (g, 0, 0))],
        out_specs=(pl.BlockSpec((1, rows, T * GC), lambda g: (g, 0, 0)),
                   pl.BlockSpec((1, bsz, 4 * P), lambda g: (g, 0, 0))),
        scratch_shapes=[pltpu.VMEM((rows, 4 * P), jnp.float32)],
        compiler_params=pltpu.CompilerParams(dimension_semantics=("parallel",),
                                             vmem_limit_bytes=VMEM_LIMIT),
        name="s5_scan",
    )(x, kt, win, wout, apow, aimg, h0)
    y = jnp.transpose(y.reshape(G, nc, bsz, T, GC), (2, 1, 3, 0, 4)).reshape(bsz, n, G * GC)
    return y, hfin


def s5_branch(u, u_ctx, p):
    ops = s5_chunk_operators(p['s5_lam_re'], p['s5_lam_im'], p['s5_log_step'], p['s5_b_re'], p['s5_b_im'],
                             p['s5_c_re'], p['s5_c_im'])
    h0 = jnp.zeros((S5_GROUPS, u.shape[0], 4 * S5_STATE), jnp.float32)
    y_ctx, h_ctx = s5_scan_pallas(u_ctx, ops, h0)
    y_lat, _ = s5_scan_pallas(u, ops, h_ctx)
    return y_lat, y_ctx


_NT = (((1,), (1,)), ((), ()))

_M_SAME, _M_EYE = 0, 1
_W_STRICT, _W_INCL, _W_EYE, _W_LEVEL0 = 0, 2, 4, 5
_N_LEVELS = 6


def rwkv_masks():
    C, n = RW_CHUNK, RW_LANES
    row = np.arange(n)[:, None]
    col = np.arange(n)[None, :]
    square = np.stack([(row // C) == (col // C), row == col])
    tt, jj = np.arange(C)[:, None], np.arange(C)[None, :]
    wide = [jj < tt, jj > tt, jj <= tt, jj >= tt, jj == tt]
    for s in (1, 2, 4, 8, 16, 32):
        fwd = ((tt // s) % 2 == 1) & ((jj // s) == (tt // s) - 1)
        wide += [fwd, fwd.T]
    wide = np.stack([np.tile(m, (1, RW_HG)) for m in wide])
    tri = np.stack([jj <= tt, jj >= tt])
    return (jnp.asarray(square, jnp.float32), jnp.asarray(wide, jnp.float32), jnp.asarray(tri, jnp.bfloat16))


def _bdot(a, b, dims=None):
    a = a.astype(jnp.bfloat16)
    b = b.astype(jnp.bfloat16)
    if dims is None:
        return jnp.dot(a, b, preferred_element_type=jnp.float32)
    return lax.dot_general(a, b, dims, preferred_element_type=jnp.float32)


def _rwkv_chunk(S, r, lw, k, v, kk, b, m_ref, w_ref, same_b, tri, d):
    C = RW_CHUNK
    same = m_ref[_M_SAME]
    lw_hi = lw.astype(jnp.bfloat16)
    rem = lw - lw_hi.astype(jnp.float32)
    lw_mid = rem.astype(jnp.bfloat16)
    lw_lo = (rem - lw_mid.astype(jnp.float32)).astype(jnp.bfloat16)
    cl = (jnp.dot(tri, lw_hi, preferred_element_type=jnp.float32)
          + jnp.dot(tri, lw_mid, preferred_element_type=jnp.float32)
          + jnp.dot(tri, lw_lo, preferred_element_type=jnp.float32))
    yield
    tot = jnp.sum(lw, axis=0, keepdims=True)
    e_neg = jnp.exp(-cl)
    e_end = jnp.exp(tot - cl)
    a_t = -kk * jnp.exp(cl - lw)
    r_t = r * jnp.exp(cl)
    b_t = b * e_neg
    k_t = k * e_neg
    b_h = b * e_end
    k_h = k * e_end
    p_c = jnp.exp(tot)

    def tile(x):
        return jnp.concatenate([x] * RW_HG, axis=0)

    def stack(x):
        return tile(x.astype(jnp.bfloat16)) * same_b

    n = RW_LANES
    sa = stack(a_t)
    wide = _bdot(jnp.concatenate([a_t, r_t], axis=0), jnp.concatenate([stack(b_t), stack(k_t)], axis=0), _NT)
    yield
    a_ab = wide[:C, :n] * w_ref[_W_STRICT + d]
    a_ak = wide[:C, n:] * w_ref[_W_STRICT + d]
    a_rb = wide[C:, :n] * w_ref[_W_INCL + d]
    a_rk = wide[C:, n:] * w_ref[_W_INCL + d]
    dinv = w_ref[_W_EYE] + a_ab * w_ref[_W_LEVEL0 + d]
    for l in range(1, _N_LEVELS):
        inner = _bdot(a_ab * w_ref[_W_LEVEL0 + 2 * l + d], stack(dinv))
        yield
        dinv = dinv + _bdot(dinv, stack(inner))
        yield
    av = _bdot(jnp.concatenate([a_ak, a_rk], axis=0), stack(v))
    yield
    akv = av[:C]
    arkv = av[C:]
    mu = _bdot(dinv, jnp.concatenate([sa, stack(akv)], axis=1))
    yield
    m1 = mu[:, :n]
    u0 = mu[:, n:]
    my = _bdot(a_rb, jnp.concatenate([stack(m1), stack(u0)], axis=1))
    yield
    m2 = r_t + my[:, :n]
    y0 = my[:, n:] + arkv
    mut = jnp.concatenate([m1, u0], axis=1).T
    gh = _bdot(mut, b_h)
    yield
    g = m_ref[_M_EYE] * p_c + gh[:n] * same
    hmat = (gh[n:] + _bdot(v.T, k_h)) * same
    y = _bdot(m2, S, _NT) + y0
    s_new = _bdot(S, g) + hmat
    return s_new, y


def _lockstep(gens):
    results = [None] * len(gens)
    active = list(range(len(gens)))
    while active:
        for i in list(active):
            try:
                next(gens[i])
            except StopIteration as stop:
                results[i] = stop.value
                active.remove(i)
    return results


def _rwkv_kernel(m_ref, w_ref, tri_ref, rf_ref, rb_ref, vf_ref, vb_ref, kkf_ref, kkb_ref, lwf_ref, kf_ref, bf_ref,
                 lwb_ref, kb_ref, bb_ref, s0_ref, yf_ref, yb_ref, sfin_ref, s_scr, *, n_chunks, n_groups, n_batch):
    i = pl.program_id(1)
    C = RW_CHUNK

    @pl.when(i == 0)
    def _():
        s_scr[...] = s0_ref[...]

    tri_f = tri_ref[0]
    tri_b = tri_ref[1]
    same_b = m_ref[_M_SAME].astype(jnp.bfloat16)

    def body(c, carry):
        rf = pl.ds(pl.multiple_of(c * C, C), C)
        rb = pl.ds(pl.multiple_of((n_chunks - 1 - c) * C, C), C)
        gens = []
        probs = [(e, g) for e in range(n_batch) for g in range(n_groups)]
        for e, g in probs:
            ln = slice(g * RW_LANES, (g + 1) * RW_LANES)
            gens.append(_rwkv_chunk(s_scr[e, 0, g], rf_ref[e, rf, ln], lwf_ref[e, rf, ln], kf_ref[e, rf, ln],
                                    vf_ref[e, rf, ln], kkf_ref[e, rf, ln], bf_ref[e, rf, ln],
                                    m_ref, w_ref, same_b, tri_f, 0))
            gens.append(_rwkv_chunk(s_scr[e, 1, g], rb_ref[e, rb, ln], lwb_ref[e, rb, ln], kb_ref[e, rb, ln],
                                    vb_ref[e, rb, ln], kkb_ref[e, rb, ln], bb_ref[e, rb, ln],
                                    m_ref, w_ref, same_b, tri_b, 1))
        out = _lockstep(gens)
        for q, (e, g) in enumerate(probs):
            ln = slice(g * RW_LANES, (g + 1) * RW_LANES)
            (s_f, y_f), (s_b, y_b) = out[2 * q], out[2 * q + 1]
            s_scr[e, 0, g] = s_f
            s_scr[e, 1, g] = s_b
            yf_ref[e, rf, ln] = y_f
            yb_ref[e, rb, ln] = y_b
        return carry

    lax.fori_loop(0, n_chunks, body, 0)

    @pl.when(i == pl.num_programs(1) - 1)
    def _():
        sfin_ref[...] = s_scr[...]


def rwkv_scan_pallas(rkv, kk, lw, kd, bvec, s0, mask_set, block_tokens):
    bsz, n, width = kk.shape
    masks, wide, tri = mask_set
    ng = width // RW_LANES
    tb = block_tokens
    nb = n // tb
    eb = RW_BATCH_PER_STEP
    assert bsz % eb == 0
    fwd = pl.BlockSpec((eb, tb, width), lambda b, i: (b, i, 0))
    bwd = pl.BlockSpec((eb, tb, width), lambda b, i: (b, nb - 1 - i, 0))
    r_fwd, r_bwd = fwd, bwd
    v_fwd = pl.BlockSpec((eb, tb, width), lambda b, i: (b, i, 2))
    v_bwd = pl.BlockSpec((eb, tb, width), lambda b, i: (b, nb - 1 - i, 2))
    state_spec = pl.BlockSpec((eb, 2, ng, RW_LANES, RW_LANES), lambda b, i: (b, 0, 0, 0, 0))
    return pl.pallas_call(
        functools.partial(_rwkv_kernel, n_chunks=tb // RW_CHUNK, n_groups=ng, n_batch=eb),
        out_shape=(jax.ShapeDtypeStruct((bsz, n, width), jnp.float32),
                   jax.ShapeDtypeStruct((bsz, n, width), jnp.float32),
                   jax.ShapeDtypeStruct(s0.shape, jnp.float32)),
        grid=(bsz // eb, nb),
        in_specs=[pl.BlockSpec(masks.shape, lambda b, i: (0, 0, 0)),
                  pl.BlockSpec(wide.shape, lambda b, i: (0, 0, 0)),
                  pl.BlockSpec(tri.shape, lambda b, i: (0, 0, 0)),
                  r_fwd, r_bwd, v_fwd, v_bwd, fwd, bwd,
                  fwd, fwd, fwd, bwd, bwd, bwd, state_spec],
        out_specs=(fwd, bwd, state_spec),
        scratch_shapes=[pltpu.VMEM((eb, 2, ng, RW_LANES, RW_LANES), jnp.float32)],
        compiler_params=pltpu.CompilerParams(dimension_semantics=("parallel", "arbitrary"),
                                             vmem_limit_bytes=VMEM_LIMIT),
        name="rwkv_scan",
    )(masks, wide, tri, rkv, rkv, rkv, rkv, kk, kk, lw[0], kd[0], bvec[0], lw[1], kd[1], bvec[1], s0)


def _head_sum(t, hm):
    f32, bf16 = jnp.float32, jnp.bfloat16
    hi = t.astype(bf16)
    rem = t - hi.astype(f32)
    mid = rem.astype(bf16)
    lo = (rem - mid.astype(f32)).astype(bf16)
    return (jnp.dot(hi, hm, preferred_element_type=f32) + jnp.dot(mid, hm, preferred_element_type=f32)
            + jnp.dot(lo, hm, preferred_element_type=f32))


def _rwkv_prep_kernel(rkv_ref, lora_ref, vec_ref, w0_ref, a0_ref, w2_ref, a2_ref, hsum_ref,
                      kk_ref, lw0_ref, kd0_ref, b0_ref, lw1_ref, kd1_ref, b1_ref, bonus_ref):
    f32, bf16 = jnp.float32, jnp.bfloat16
    w = RW_WIDTH
    r = rkv_ref[:, 0:w]
    k = rkv_ref[:, w:2 * w]
    v = rkv_ref[:, 2 * w:3 * w]
    hm = hsum_ref[...]
    kk = k * vec_ref[0:1, :]
    kk = kk * lax.rsqrt(jnp.maximum(_head_sum(kk * kk, hm), 1e-24))
    kk_ref[...] = kk
    w_lo = jnp.tanh(lora_ref[:, 0:128]).astype(bf16)
    a_lo = lora_ref[:, 128:256].astype(bf16)
    rrk = r * vec_ref[2:3, :]
    bonus = None
    outs = ((lw0_ref, kd0_ref, b0_ref), (lw1_ref, kd1_ref, b1_ref))
    for d in range(2):
        x = w0_ref[d:d + 1, :] + jnp.dot(w_lo, w2_ref[d], preferred_element_type=f32)
        w_log = -(jnp.maximum(-x, 0.0) + jnp.log(1.0 + jnp.exp(-jnp.abs(x)))) - 0.5
        a = jax.nn.sigmoid(a0_ref[d:d + 1, :] + jnp.dot(a_lo, a2_ref[d], preferred_element_type=f32))
        kd = k * (1.0 + (a - 1.0) * vec_ref[1:2, :])
        outs[d][0][...] = -jnp.exp(w_log)
        outs[d][1][...] = kd
        outs[d][2][...] = kk * a
        term = _head_sum(rrk * kd, hm)
        bonus = term if bonus is None else bonus + term
    bonus_ref[...] = bonus * v


def rwkv_prep(rkv, lora, p):
    bsz, n, _ = rkv.shape
    rows = bsz * n
    tm = min(MIX_BLOCK_ROWS, n)
    w = RW_WIDTH
    hid = np.arange(w) // RW_HEAD
    hsum = jnp.asarray(hid[:, None] == hid[None, :], jnp.bfloat16)
    vec = jnp.stack([p['rw_kk'], p['rw_ka'], p['rw_rk']])
    zero = jnp.zeros((RW_DECAY_LORA, w), jnp.float32)
    w2 = jnp.stack([jnp.concatenate([p['rw_w2'][0], zero]), jnp.concatenate([zero, p['rw_w2'][1]])])
    a2 = jnp.stack([jnp.concatenate([p['rw_a2'][0], zero]), jnp.concatenate([zero, p['rw_a2'][1]])])
    consts = [vec, p['rw_w0'], p['rw_a0'], w2.astype(jnp.bfloat16), a2.astype(jnp.bfloat16), hsum]

    def const(a):
        return pl.BlockSpec(a.shape, lambda i: (0,) * a.ndim)

    outs = pl.pallas_call(
        _rwkv_prep_kernel,
        out_shape=tuple(jax.ShapeDtypeStruct((rows, w), jnp.float32) for _ in range(8)),
        grid=(rows // tm,),
        in_specs=[pl.BlockSpec((tm, 3 * w), lambda i: (i, 0)), pl.BlockSpec((tm, lora.shape[-1]), lambda i: (i, 0))]
                 + [const(c) for c in consts],
        out_specs=tuple(pl.BlockSpec((tm, w), lambda i: (i, 0)) for _ in range(8)),
        compiler_params=pltpu.CompilerParams(dimension_semantics=("parallel",), vmem_limit_bytes=VMEM_LIMIT),
        name="rwkv_prep",
    )(rkv.reshape(rows, 3 * w), lora.reshape(rows, -1), *consts)
    kk, lw0, kd0, b0, lw1, kd1, b1, bonus = [o.reshape(bsz, n, w) for o in outs]
    return kk, (lw0, lw1), (kd0, kd1), (b0, b1), bonus


def rwkv_branch(rkv, lora, p, s0, mask_set, block_tokens):
    kk, lw, kd, bvec, bonus = rwkv_prep(rkv, lora, p)
    y_f, y_b, s_fin = rwkv_scan_pallas(rkv, kk, lw, kd, bvec, s0, mask_set, block_tokens)
    return (y_f, y_b, bonus, lora[..., IN_LORA - RW_GATE_LORA:]), s_fin


def hyena_filters(n_tok, p):
    hp = lax.Precision.HIGHEST
    bands = (HY_POS_DIM - 1) // 2
    t = jnp.linspace(0.0, 1.0, n_tok, dtype=jnp.float32)[:, None]
    w = (2.0 * math.pi / n_tok) * jnp.arange(n_tok, dtype=jnp.float32)[:, None]
    f = jnp.linspace(1e-4, bands - 1, bands, dtype=jnp.float32)[None, :]
    feats = jnp.concatenate([t, jnp.cos(f * w), -jnp.sin(f * w)], axis=-1)
    h = jnp.sin(p['hy_f_freq1'] * (jnp.dot(feats, p['hy_f_w1'], precision=hp) + p['hy_f_b1']))
    h = jnp.sin(p['hy_f_freq2'] * (jnp.dot(h, p['hy_f_w2'], precision=hp) + p['hy_f_b2']))
    h = jnp.dot(h, p['hy_f_w3'], precision=hp).reshape(n_tok, HY_ORDER, 2, HY_WIDTH)
    rates = jnp.abs(jnp.linspace(math.log(HY_DECAY_TARGET) / HY_DECAY_PCT_SHORT,
                                 math.log(HY_DECAY_TARGET) / HY_DECAY_PCT_LONG, HY_WIDTH, dtype=jnp.float32))
    h = h * jnp.exp(-t * rates)[:, None, None, :]
    return h.at[0, :, 1].set(0.0)


HY_N = 2 * SEQ
HY_N1 = 64
HY_N2 = 128
HY_NH = HY_N1 // 2
HY_KH = HY_N1 // 2 + 1
HY_KP = 40
HY_PITCH = 88
HY_LANES = 128


def hyena_dft_tables():
    k1 = np.arange(HY_KH)[:, None]
    n1 = np.arange(HY_NH)[None, :]
    n2 = np.arange(HY_N2)
    ph = -2 * np.pi * (k1 * n1 / HY_N1)[None] - 2 * np.pi * (n2[:, None, None] * k1[None] / HY_N)
    lhs1 = np.zeros((HY_N2, 2 * HY_KP, HY_NH))
    lhs1[:, :HY_KH] = np.cos(ph)
    lhs1[:, HY_KP:HY_KP + HY_KH] = np.sin(ph)
    wgt = np.where((k1 == 0) | (k1 == HY_N1 // 2), 1.0, 2.0)[None] / HY_N
    lhs2 = np.zeros((HY_N2, HY_NH, 2 * HY_KP))
    lhs2[:, :, :HY_KH] = (wgt * np.cos(ph)).transpose(0, 2, 1)
    lhs2[:, :, HY_KP:HY_KP + HY_KH] = (wgt * np.sin(ph)).transpose(0, 2, 1)
    kk = np.arange(HY_N2)
    ang = -2 * np.pi * np.outer(kk, kk) / HY_N2
    cr, ci = np.cos(ang), np.sin(ang)
    f_fwd = np.block([[cr, -ci], [ci, cr]])
    f_inv = np.block([[cr, ci], [-ci, cr]])

    def as_bf(a):
        return jnp.asarray(a, jnp.float32).astype(jnp.bfloat16)

    def low_bf(a):
        a = jnp.asarray(a, jnp.float32)
        return (a - a.astype(jnp.bfloat16).astype(jnp.float32)).astype(jnp.bfloat16)

    return as_bf(lhs1), as_bf(lhs2), as_bf(f_fwd), as_bf(f_inv), low_bf(lhs1), low_bf(f_fwd)


def _hy_spec_kernel(h_ref, l1h_ref, l1l_ref, ffh_ref, ffl_ref, o_ref, a_ref):
    f32, bf16 = jnp.float32, jnp.bfloat16
    n2n, nh, pitch, kp = HY_N2, HY_NH, HY_PITCH, HY_KP

    def dot3(a_hi, a_lo, x):
        x_hi = x.astype(bf16)
        x_lo = (x - x_hi.astype(f32)).astype(bf16)
        return (jnp.dot(a_hi, x_hi, preferred_element_type=f32) + jnp.dot(a_hi, x_lo, preferred_element_type=f32)
                + jnp.dot(a_lo, x_hi, preferred_element_type=f32))

    def stage1(q, c):
        slab = h_ref[pl.ds(pl.multiple_of(q * nh, nh), nh), :]
        a_ref[pl.ds(pl.multiple_of(q * pitch, 8), 2 * kp), :] = dot3(l1h_ref[q], l1l_ref[q], slab)
        return c

    lax.fori_loop(0, n2n, stage1, 0, unroll=16)

    def stage2(k, c):
        a = jnp.concatenate([a_ref[pl.ds(k, n2n, stride=pitch), :],
                             a_ref[pl.ds(kp + k, n2n, stride=pitch), :]], axis=0)
        o_ref[0, pl.ds(pl.multiple_of(k * 2 * n2n, 2 * n2n), 2 * n2n), :] = dot3(ffh_ref[...], ffl_ref[...], a)
        return c

    lax.fori_loop(0, HY_KH, stage2, 0, unroll=11)


def hyena_filter_spectra_long(h, tables):
    n, orders, _, w = h.shape
    l1, _, ff, _, l1_lo, ff_lo = tables
    ch = orders * 2 * w
    sig = jnp.transpose(h.reshape(HY_NH, HY_N2, ch), (1, 0, 2)).reshape(n, ch)
    rows = HY_KH * 2 * HY_N2

    def const(a):
        return pl.BlockSpec(a.shape, lambda t: (0,) * a.ndim)

    spec = pl.pallas_call(
        _hy_spec_kernel,
        out_shape=jax.ShapeDtypeStruct((ch // HY_LANES, rows, HY_LANES), jnp.float32),
        grid=(ch // HY_LANES,),
        in_specs=[pl.BlockSpec((n, HY_LANES), lambda t: (0, t)), const(l1), const(l1_lo), const(ff), const(ff_lo)],
        out_specs=pl.BlockSpec((1, rows, HY_LANES), lambda t: (t, 0, 0)),
        scratch_shapes=[pltpu.VMEM((HY_N2 * HY_PITCH, HY_LANES), jnp.float32)],
        compiler_params=pltpu.CompilerParams(dimension_semantics=("parallel",), vmem_limit_bytes=VMEM_LIMIT),
        name="hyena_filter_spectrum",
    )(sig, l1, l1_lo, ff, ff_lo)
    spec = spec.reshape(orders, 2, w // HY_LANES, HY_KH, 2, HY_N2, HY_LANES)
    sign = jnp.asarray([1.0, -1.0], jnp.float32)[:, None, None]
    both = spec[:, 0] + sign * spec[:, 1]
    return both.reshape(orders, w // HY_LANES, rows, HY_LANES)


def _hy_kernel(z_ref, g_ref, k_ref, l1_ref, l2_ref, ff_ref, fi_ref, bias_ref, o_ref, a_ref):
    f32, bf16 = jnp.float32, jnp.bfloat16
    n2n, nh, pitch, kp = HY_N2, HY_NH, HY_PITCH, HY_KP

    def stage1(q, c):
        slab = z_ref[0, pl.ds(pl.multiple_of(q * nh, nh), nh), :].astype(bf16)
        a_ref[pl.ds(pl.multiple_of(q * pitch, 8), 2 * kp), :] = jnp.dot(l1_ref[q], slab, preferred_element_type=f32)
        return c

    lax.fori_loop(0, n2n, stage1, 0, unroll=16)

    def stage2(k, c):
        a = jnp.concatenate([a_ref[pl.ds(k, n2n, stride=pitch), :],
                             a_ref[pl.ds(kp + k, n2n, stride=pitch), :]], axis=0).astype(bf16)
        x = jnp.dot(ff_ref[...], a, preferred_element_type=f32)
        xr, xi = x[:n2n], x[n2n:]
        base = pl.multiple_of(k * 2 * n2n, 2 * n2n)
        kr = k_ref[0, pl.ds(base, n2n), :]
        ki = k_ref[0, pl.ds(base + n2n, n2n), :]
        y = jnp.concatenate([xr * kr - xi * ki, xr * ki + xi * kr], axis=0).astype(bf16)
        b = jnp.dot(fi_ref[...], y, preferred_element_type=f32)
        a_ref[pl.ds(k, n2n, stride=pitch), :] = b[:n2n]
        a_ref[pl.ds(kp + k, n2n, stride=pitch), :] = b[n2n:]
        return c

    lax.fori_loop(0, HY_KH, stage2, 0, unroll=11)

    def stage3(q, c):
        blk = a_ref[pl.ds(pl.multiple_of(q * pitch, 8), 2 * kp), :].astype(bf16)
        y = jnp.dot(l2_ref[q], blk, preferred_element_type=f32)
        rows = pl.ds(pl.multiple_of(q * nh, nh), nh)
        o_ref[0, rows, :] = g_ref[0, rows, :] * (y + z_ref[0, rows, :] * bias_ref[...])
        return c

    lax.fori_loop(0, n2n, stage3, 0, unroll=16)


def hyena_long_conv_gated(zt, gt, spec, bias, tables):
    bsz, n, w = zt.shape
    l1, l2, ff, fi = tables[:4]
    nt = w // HY_LANES
    tok = pl.BlockSpec((1, n, HY_LANES), lambda t, b: (b, 0, t))
    return pl.pallas_call(
        _hy_kernel,
        out_shape=jax.ShapeDtypeStruct((bsz, n, w), jnp.float32),
        grid=(nt, bsz),
        in_specs=[tok, tok,
                  pl.BlockSpec((1,) + spec.shape[1:], lambda t, b: (t, 0, 0)),
                  pl.BlockSpec(l1.shape, lambda t, b: (0, 0, 0)),
                  pl.BlockSpec(l2.shape, lambda t, b: (0, 0, 0)),
                  pl.BlockSpec(ff.shape, lambda t, b: (0, 0)),
                  pl.BlockSpec(fi.shape, lambda t, b: (0, 0)),
                  pl.BlockSpec((1, HY_LANES), lambda t, b: (0, t))],
        out_specs=tok,
        scratch_shapes=[pltpu.VMEM((HY_N2 * HY_PITCH, HY_LANES), jnp.float32)],
        compiler_params=pltpu.CompilerParams(dimension_semantics=("parallel", "parallel"),
                                             vmem_limit_bytes=VMEM_LIMIT),
        name="hyena_conv",
    )(zt, gt, spec, l1, l2, ff, fi, bias.reshape(1, w))


def hyena_branch_long(streams, p):
    bsz, n, w3 = streams.shape
    tables = hyena_dft_tables()
    spec = hyena_filter_spectra_long(hyena_filters(n, p), tables)
    st = jnp.transpose(streams.reshape(bsz, HY_NH, HY_N2, w3), (0, 2, 1, 3)).reshape(bsz, n, w3)
    z, x1, x2 = jnp.split(st, 3, axis=-1)
    for o, gate in enumerate((x1, x2)):
        z = hyena_long_conv_gated(z, gate, spec[o], p['hy_bias'][o], tables)
    w = z.shape[-1]
    return jnp.transpose(z.reshape(bsz, HY_N2, HY_NH, w), (0, 2, 1, 3)).reshape(bsz, n, w)


def _hy_ctx_kernel(z_ref, g_ref, k_ref, fd_ref, fi_ref, bias_ref, o_ref):
    f32, bf16 = jnp.float32, jnp.bfloat16
    z = z_ref[0]
    n = fd_ref.shape[0] // 2
    x = jnp.dot(fd_ref[...], z.astype(bf16), preferred_element_type=f32)
    xr, xi = x[:n], x[n:]
    kr, ki = k_ref[0, :n, :], k_ref[0, n:, :]
    y = jnp.concatenate([xr * kr - xi * ki, xr * ki + xi * kr], axis=0).astype(bf16)
    conv = jnp.dot(fi_ref[...], y, preferred_element_type=f32)
    o_ref[0] = g_ref[0] * (conv + z * bias_ref[...])


def hyena_branch_ctx(streams, p):
    bsz, n, w3 = streams.shape
    w = w3 // 3
    nn = 2 * n
    ang = 2 * np.pi * np.outer(np.arange(nn), np.arange(n)) / nn
    h = hyena_filters(n, p)
    hp = lax.Precision.HIGHEST
    spec = jnp.concatenate([
        jnp.einsum('kt,tow->kow', jnp.asarray(np.cos(ang), jnp.float32), h[:, :, 0] + h[:, :, 1], precision=hp),
        jnp.einsum('kt,tow->kow', jnp.asarray(-np.sin(ang), jnp.float32), h[:, :, 0] - h[:, :, 1], precision=hp)],
        axis=0)
    fd = jnp.asarray(np.concatenate([np.cos(ang), -np.sin(ang)], axis=0), jnp.float32).astype(jnp.bfloat16)
    fi = jnp.asarray(np.concatenate([np.cos(ang).T, -np.sin(ang).T], axis=1) / nn, jnp.float32).astype(jnp.bfloat16)
    nt = w // HY_LANES
    tok = pl.BlockSpec((1, n, HY_LANES), lambda t, b: (b, 0, t))
    z, x1, x2 = jnp.split(streams, 3, axis=-1)
    for o, gate in enumerate((x1, x2)):
        sp = jnp.transpose(spec[:, o].reshape(2 * nn, nt, HY_LANES), (1, 0, 2)).astype(jnp.float32)
        z = pl.pallas_call(
            _hy_ctx_kernel,
            out_shape=jax.ShapeDtypeStruct((bsz, n, w), jnp.float32),
            grid=(nt, bsz),
            in_specs=[tok, tok,
                      pl.BlockSpec((1, 2 * nn, HY_LANES), lambda t, b: (t, 0, 0)),
                      pl.BlockSpec(fd.shape, lambda t, b: (0, 0)),
                      pl.BlockSpec(fi.shape, lambda t, b: (0, 0)),
                      pl.BlockSpec((1, HY_LANES), lambda t, b: (0, t))],
            out_specs=tok,
            compiler_params=pltpu.CompilerParams(dimension_semantics=("parallel", "parallel"),
                                                 vmem_limit_bytes=VMEM_LIMIT),
            name="hyena_conv_ctx",
        )(z, gate, sp, fd, fi, p['hy_bias'][o].reshape(1, w))
    return z


def _rms(x, gain):
    return x * lax.rsqrt(jnp.mean(x * x, axis=-1, keepdims=True) + NORM_EPS) * gain


def _conv3(x, w_ref, period):
    rows = x.shape[0]
    pos = lax.broadcasted_iota(jnp.int32, (rows, 1), 0) % period
    prev = jnp.where(pos == 0, 0.0, pltpu.roll(x, 1, axis=0))
    nxt = jnp.where(pos == period - 1, 0.0, pltpu.roll(x, rows - 1, axis=0))
    return prev * w_ref[0:1, :] + x * w_ref[1:2, :] + nxt * w_ref[2:3, :]


def _inproj_kernel(x_ref, mod_ref, gain_ref, ws5_ref, wrw_ref, wlo_ref, why_ref, wgt_ref, cwr_ref, cwh_ref, cbh_ref,
                   os5_ref, orw_ref, olo_ref, ohy_ref, ogt_ref, *, period):
    f32 = jnp.float32
    shift, scale = mod_ref[0, 0:1, :], mod_ref[0, 1:2, :]
    h = (_rms(x_ref[...], gain_ref[...]) * (1.0 + scale) + shift).astype(jnp.bfloat16)
    os5_ref[...] = jnp.dot(h, ws5_ref[...], preferred_element_type=f32)
    olo_ref[...] = jnp.dot(h, wlo_ref[...], preferred_element_type=f32)
    ogt_ref[...] = jnp.dot(h, wgt_ref[...], preferred_element_type=f32)
    orw_ref[...] = _conv3(jnp.dot(h, wrw_ref[...], preferred_element_type=f32), cwr_ref, period)
    ohy_ref[...] = _conv3(jnp.dot(h, why_ref[...], preferred_element_type=f32), cwh_ref, period) + cbh_ref[...]


def input_projection(x, mod, gain, w_parts, rw_conv_w, hy_conv_w, hy_conv_b, period):
    bsz, n, d = x.shape
    tm = min(MIX_BLOCK_ROWS, n)
    per_b = n // tm
    rows = bsz * n
    bmap = (lambda i: (i // per_b, 0, 0)) if mod.shape[0] == bsz else (lambda i: (0, 0, 0))
    wb = [w.astype(jnp.bfloat16) for w in w_parts]

    def const(a):
        return pl.BlockSpec(a.shape, lambda i: (0,) * a.ndim)

    cbh = hy_conv_b.reshape(1, -1)
    outs = pl.pallas_call(
        functools.partial(_inproj_kernel, period=period),
        out_shape=tuple(jax.ShapeDtypeStruct((rows, w.shape[1]), jnp.float32) for w in wb),
        grid=(rows // tm,),
        in_specs=[pl.BlockSpec((tm, d), lambda i: (i, 0)),
                  pl.BlockSpec((1, 3, d), bmap),
                  pl.BlockSpec((1, d), lambda i: (0, 0))]
                 + [const(w) for w in wb] + [const(rw_conv_w), const(hy_conv_w), const(cbh)],
        out_specs=tuple(pl.BlockSpec((tm, w.shape[1]), lambda i: (i, 0)) for w in wb),
        compiler_params=pltpu.CompilerParams(dimension_semantics=("parallel",), vmem_limit_bytes=VMEM_LIMIT),
        name="input_projection",
    )(x.reshape(rows, d), mod, gain.reshape(1, d), *wb, rw_conv_w, hy_conv_w, cbh)
    return tuple(o.reshape(bsz, n, o.shape[-1]) for o in outs)


def _merge_kernel(x_ref, mod_ref, gain_ref, ys5_ref, us5_ref, wf_ref, wb_ref, bonus_ref, glo_ref, hy_ref, gt_ref,
                  s5d_ref, glu_ref, lnw_ref, lnb_ref, g2_ref, havg_ref, brs_ref, brr_ref, brh_ref, ow_ref, o_ref):
    f32, bf16 = jnp.float32, jnp.bfloat16
    y = ys5_ref[...] + s5d_ref[...] * us5_ref[...]
    y = 0.5 * y * (1.0 + lax.erf(y * (1.0 / math.sqrt(2.0))))
    s5 = y * jax.nn.sigmoid(jnp.dot(y.astype(bf16), glu_ref[...], preferred_element_type=f32))
    wkv = wf_ref[...] + wb_ref[...]

    def head_mean(t):
        hi = t.astype(bf16)
        rem = t - hi.astype(f32)
        mid = rem.astype(bf16)
        lo = (rem - mid.astype(f32)).astype(bf16)
        hm = havg_ref[...]
        tot = (jnp.dot(hi, hm, preferred_element_type=f32) + jnp.dot(mid, hm, preferred_element_type=f32)
               + jnp.dot(lo, hm, preferred_element_type=f32))
        return tot * (1.0 / RW_HEAD)

    mu = head_mean(wkv)
    cen = wkv - mu
    var = head_mean(cen * cen)
    o = cen * lax.rsqrt(var + RW_GN_EPS) * lnw_ref[...] + lnb_ref[...] + bonus_ref[...]
    g = jnp.dot(jax.nn.sigmoid(glo_ref[...]).astype(bf16), g2_ref[...], preferred_element_type=f32)
    rw = o * g
    gt = jax.nn.sigmoid(gt_ref[...])
    d = x_ref.shape[-1]
    m = (gt[:, 0:d] * jnp.dot(s5.astype(bf16), brs_ref[...], preferred_element_type=f32)
         + gt[:, d:2 * d] * jnp.dot(rw.astype(bf16), brr_ref[...], preferred_element_type=f32)
         + gt[:, 2 * d:3 * d] * jnp.dot(hy_ref[...].astype(bf16), brh_ref[...], preferred_element_type=f32))
    yl = jnp.dot(m.astype(bf16), ow_ref[...], preferred_element_type=f32)
    o_ref[...] = x_ref[...] + mod_ref[0, 2:3, :] * _rms(yl, gain_ref[...])


def merge_block(x, mod, gain, ys5, us5, rw_parts, hy, gates, p):
    bsz, n, d = x.shape
    tm = min(MIX_BLOCK_ROWS, n)
    per_b = n // tm
    rows = bsz * n
    bmap = (lambda i: (i // per_b, 0, 0)) if mod.shape[0] == bsz else (lambda i: (0, 0, 0))

    def bf(a):
        return a.astype(jnp.bfloat16)

    def row(a):
        return pl.BlockSpec((tm, a.shape[-1]), lambda i: (i, 0))

    def const(a):
        return pl.BlockSpec(a.shape, lambda i: (0,) * a.ndim)

    hid = np.arange(RW_WIDTH) // RW_HEAD
    havg = jnp.asarray(hid[:, None] == hid[None, :], jnp.bfloat16)
    acts = [a.reshape(rows, a.shape[-1]) for a in (ys5, us5) + tuple(rw_parts) + (hy, gates)]
    consts = [p['s5_d'].reshape(1, -1), bf(p['s5_glu_w']), p['rw_ln_w'].reshape(1, -1), p['rw_ln_b'].reshape(1, -1),
              bf(p['rw_g2']), havg, bf(p['br_s5']), bf(p['br_rw']), bf(p['br_hy']), bf(p['out_w'])]
    out = pl.pallas_call(
        _merge_kernel,
        out_shape=jax.ShapeDtypeStruct((rows, d), jnp.float32),
        grid=(rows // tm,),
        in_specs=[pl.BlockSpec((tm, d), lambda i: (i, 0)), pl.BlockSpec((1, 3, d), bmap),
                  pl.BlockSpec((1, d), lambda i: (0, 0))] + [row(a) for a in acts] + [const(c) for c in consts],
        out_specs=pl.BlockSpec((tm, d), lambda i: (i, 0)),
        compiler_params=pltpu.CompilerParams(dimension_semantics=("parallel",), vmem_limit_bytes=VMEM_LIMIT),
        name="merge_block",
    )(x.reshape(rows, d), mod, gain.reshape(1, d), *acts, *consts)
    return out.reshape(bsz, n, d)


def token_mixer(x, ctx, mod_l, mod_c, g_pre, g_post, p, with_ctx_out):
    bsz = x.shape[0]
    cuts = [IN_S5, IN_S5 + IN_RW, IN_S5 + IN_RW + IN_LORA, IN_S5 + IN_RW + IN_LORA + IN_HY]
    edges = [0] + cuts + [IN_COLS]
    w_parts = [p['in_w'][:, edges[i]:edges[i + 1]] for i in range(5)]
    conv = (p['rw_conv_w'], p['hy_conv_w'], p['hy_conv_b'])
    u_l, rkv_l, lora_l, hy_l, gate_l = input_projection(x, mod_l, g_pre, w_parts, *conv, GRID_W)
    u_c, rkv_c, lora_c, hy_c, gate_c = input_projection(ctx, mod_c, g_pre, w_parts, *conv, CTX_LEN)
    mask_set = rwkv_masks()
    zr = jnp.zeros((bsz, 2, RW_WIDTH // RW_LANES, RW_LANES, RW_LANES), jnp.float32)
    ys5_l, ys5_c = s5_branch(u_l, u_c, p)
    rw_c, rw_state = rwkv_branch(rkv_c, lora_c, p, zr, mask_set, CTX_LEN)
    rw_l, _ = rwkv_branch(rkv_l, lora_l, p, rw_state, mask_set, RW_BLOCK_TOKENS)
    hy_lat = hyena_branch_long(hy_l, p)
    x = merge_block(x, mod_l, g_post, ys5_l, u_l, rw_l, hy_lat, gate_l, p)
    if with_ctx_out:
        ctx = merge_block(ctx, mod_c, g_post, ys5_c, u_c, rw_c, hyena_branch_ctx(hy_c, p), gate_c, p)
    return x, ctx


def _ffn_kernel(x_ref, mod_ref, gain_ref, wg_ref, wu_ref, wd_ref, o_ref):
    x = x_ref[...]
    shift, scale, gate = mod_ref[0, 0:1, :], mod_ref[0, 1:2, :], mod_ref[0, 2:3, :]
    h = (_rms(x, gain_ref[0:1, :]) * (1.0 + scale) + shift).astype(jnp.bfloat16)
    g = jnp.dot(h, wg_ref[...], preferred_element_type=jnp.float32)
    u = jnp.dot(h, wu_ref[...], preferred_element_type=jnp.float32)
    a = (g * jax.nn.sigmoid(g) * u).astype(jnp.bfloat16)
    y = jnp.dot(a, wd_ref[...], preferred_element_type=jnp.float32)
    o_ref[...] = x + gate * _rms(y, gain_ref[1:2, :])


def ffn_block(x, mod, gains, wg, wu, wd):
    bsz, n, d = x.shape
    hid = wg.shape[1]
    tm = min(FFN_BLOCK_ROWS, n)
    per_b = n // tm
    xf = x.reshape(bsz * n, d)
    bmap = (lambda i: (i // per_b, 0, 0)) if mod.shape[0] == bsz else (lambda i: (0, 0, 0))
    out = pl.pallas_call(
        _ffn_kernel,
        out_shape=jax.ShapeDtypeStruct((bsz * n, d), jnp.float32),
        grid=(bsz * per_b,),
        in_specs=[pl.BlockSpec((tm, d), lambda i: (i, 0)),
                  pl.BlockSpec((1, 3, d), bmap),
                  pl.BlockSpec((2, d), lambda i: (0, 0)),
                  pl.BlockSpec((d, hid), lambda i: (0, 0)),
                  pl.BlockSpec((d, hid), lambda i: (0, 0)),
                  pl.BlockSpec((hid, d), lambda i: (0, 0))],
        out_specs=pl.BlockSpec((tm, d), lambda i: (i, 0)),
        compiler_params=pltpu.CompilerParams(dimension_semantics=("parallel",), vmem_limit_bytes=VMEM_LIMIT),
        name="ffn_block",
    )(xf, mod, gains, wg.astype(jnp.bfloat16), wu.astype(jnp.bfloat16), wd.astype(jnp.bfloat16))
    return out.reshape(bsz, n, d)


def _moe_kernel(be_ref, x_ref, gate_ref, wg_ref, wu_ref, wd_ref, o_ref):
    j = pl.program_id(1)
    xb = x_ref[...]
    g = jnp.dot(xb, wg_ref[0], preferred_element_type=jnp.float32)
    u = jnp.dot(xb, wu_ref[0], preferred_element_type=jnp.float32)
    a = (g * jax.nn.sigmoid(g) * u).astype(jnp.bfloat16)
    y = jnp.dot(a, wd_ref[0], preferred_element_type=jnp.float32)

    @pl.when(j == 0)
    def _():
        o_ref[...] = y

    @pl.when(j > 0)
    def _():
        o_ref[...] += y

    @pl.when(j == pl.num_programs(1) - 1)
    def _():
        o_ref[...] *= gate_ref[...]


def moe_expert_blocks(xb, slot_gate, block_e, wg, wu, wd):
    cap, d = xb.shape
    n_blocks = cap // MOE_BLOCK
    hid = wg.shape[-1]
    nh = MOE_HIDDEN_TILES
    th = hid // nh

    def hidx(i, j):
        return j + (i % 2) * (nh - 1 - 2 * j)

    return pl.pallas_call(
        _moe_kernel,
        out_shape=jax.ShapeDtypeStruct((cap, d), jnp.float32),
        grid_spec=pltpu.PrefetchScalarGridSpec(
            num_scalar_prefetch=1,
            grid=(n_blocks, nh),
            in_specs=[pl.BlockSpec((MOE_BLOCK, d), lambda i, j, be: (i, 0)),
                      pl.BlockSpec((MOE_BLOCK, 1), lambda i, j, be: (i, 0)),
                      pl.BlockSpec((1, d, th), lambda i, j, be: (be[i], 0, hidx(i, j))),
                      pl.BlockSpec((1, d, th), lambda i, j, be: (be[i], 0, hidx(i, j))),
                      pl.BlockSpec((1, th, d), lambda i, j, be: (be[i], hidx(i, j), 0))],
            out_specs=pl.BlockSpec((MOE_BLOCK, d), lambda i, j, be: (i, 0))),
        compiler_params=pltpu.CompilerParams(dimension_semantics=("parallel", "arbitrary"),
                                             vmem_limit_bytes=VMEM_LIMIT),
        name="moe_experts",
    )(block_e, xb, slot_gate.reshape(cap, 1), wg.astype(jnp.bfloat16), wu.astype(jnp.bfloat16),
      wd.astype(jnp.bfloat16))


def moe_swiglu(h, router_w, wg, wu, wd):
    d_model = h.shape[-1]
    tok = h.reshape(-1, d_model)
    n = tok.shape[0]
    n_assign = n * TOP_K
    logits = jnp.dot(tok, router_w)
    top_logit, top_e = lax.top_k(logits, TOP_K)
    gate = jax.nn.softmax(top_logit, axis=-1).reshape(-1)
    flat_e = top_e.reshape(-1)
    order = jnp.argsort(flat_e)
    onehot = (flat_e[:, None] == jnp.arange(N_EXPERTS, dtype=flat_e.dtype)[None, :]).astype(jnp.int32)
    sizes = jnp.sum(onehot, axis=0)
    padded = (sizes + MOE_BLOCK - 1) // MOE_BLOCK * MOE_BLOCK
    pad_end = jnp.cumsum(padded)
    pad_start = pad_end - padded
    grp_start = jnp.cumsum(sizes) - sizes
    n_blocks = -(-n_assign // MOE_BLOCK) + N_EXPERTS
    cap = n_blocks * MOE_BLOCK
    block_start = jnp.arange(n_blocks, dtype=jnp.int32) * MOE_BLOCK
    block_e = jnp.minimum(jnp.sum(block_start[:, None] >= pad_end[None, :], axis=1), N_EXPERTS - 1)
    slot_e = jnp.repeat(block_e, MOE_BLOCK)
    within = jnp.arange(cap, dtype=jnp.int32) - pad_start[slot_e]
    valid = within < sizes[slot_e]
    assign = order[jnp.clip(grp_start[slot_e] + within, 0, n_assign - 1)]
    slot_tok = jnp.where(valid, (assign // TOP_K).astype(jnp.int32), 0)
    slot_gate = jnp.where(valid, gate[assign], 0.0)
    xb = tok.astype(jnp.bfloat16)[slot_tok]
    yb = moe_expert_blocks(xb, slot_gate, block_e.astype(jnp.int32), wg, wu, wd)
    rank = jnp.take_along_axis(jnp.cumsum(onehot, axis=0), flat_e[:, None], axis=1)[:, 0] - 1
    pos = (pad_start[flat_e] + rank).astype(jnp.int32).reshape(n, TOP_K)
    out = yb[pos[:, 0]]
    for kk in range(1, TOP_K):
        out = out + yb[pos[:, kk]]
    return out.reshape(h.shape)


def kernel(x, c, ctx, c_ctx, mod_w, mod_b, norm_g, in_w, s5_lam_re, s5_lam_im, s5_log_step, s5_b_re, s5_b_im, s5_c_re, s5_c_im, s5_d, s5_glu_w, rw_conv_w, rw_w0, rw_w2, rw_a0, rw_a2, rw_g2, rw_kk, rw_ka, rw_rk, rw_ln_w, rw_ln_b, hy_conv_w, hy_conv_b, hy_f_w1, hy_f_b1, hy_f_freq1, hy_f_w2, hy_f_b2, hy_f_freq2, hy_f_w3, hy_bias, br_s5, br_rw, br_hy, out_w, ffn_wg, ffn_wu, ffn_wd, moe_router, moe_wg, moe_wu, moe_wd):
    silu_c = jax.nn.silu(c)
    silu_cc = jax.nn.silu(c_ctx)
    for i in range(DEPTH):
        last = i == DEPTH - 1
        p = {
            'in_w': in_w[i],
            's5_lam_re': s5_lam_re[i], 's5_lam_im': s5_lam_im[i], 's5_log_step': s5_log_step[i],
            's5_b_re': s5_b_re[i], 's5_b_im': s5_b_im[i], 's5_c_re': s5_c_re[i], 's5_c_im': s5_c_im[i],
            's5_d': s5_d[i], 's5_glu_w': s5_glu_w[i],
            'rw_conv_w': rw_conv_w[i], 'rw_w0': rw_w0[i], 'rw_w2': rw_w2[i], 'rw_a0': rw_a0[i], 'rw_a2': rw_a2[i],
            'rw_g2': rw_g2[i], 'rw_kk': rw_kk[i], 'rw_ka': rw_ka[i], 'rw_rk': rw_rk[i],
            'rw_ln_w': rw_ln_w[i], 'rw_ln_b': rw_ln_b[i],
            'hy_conv_w': hy_conv_w[i], 'hy_conv_b': hy_conv_b[i], 'hy_f_w1': hy_f_w1[i], 'hy_f_b1': hy_f_b1[i],
            'hy_f_freq1': hy_f_freq1[i], 'hy_f_w2': hy_f_w2[i], 'hy_f_b2': hy_f_b2[i], 'hy_f_freq2': hy_f_freq2[i],
            'hy_f_w3': hy_f_w3[i], 'hy_bias': hy_bias[i],
            'br_s5': br_s5[i], 'br_rw': br_rw[i], 'br_hy': br_hy[i], 'out_w': out_w[i],
        }
        ml = jnp.split((silu_c @ mod_w[i] + mod_b[i])[:, None, :], 6, axis=-1)
        mc = jnp.split(silu_cc @ mod_w[i] + mod_b[i], 6, axis=-1)
        g_pre_m, g_post_m, g_pre_f, g_post_f = norm_g[i]
        if i % 2 == 0:
            def channel_mix(t, m, j=i // 2):
                return ffn_block(t, m, norm_g[i, 2:4], ffn_wg[j], ffn_wu[j], ffn_wd[j])
        else:
            def channel_mix(t, m, j=i // 2):
                h = rms_norm(t, g_pre_f) * (1 + m[:, 1:2]) + m[:, 0:1]
                y = moe_swiglu(h, moe_router[j], moe_wg[j], moe_wu[j], moe_wd[j])
                return t + m[:, 2:3] * rms_norm(y, g_post_f)
        x, ctx = token_mixer(x, ctx, jnp.concatenate(ml[0:3], axis=1), jnp.stack(mc[0:3])[None],
                             g_pre_m, g_post_m, p, not last)
        x = channel_mix(x, jnp.concatenate(ml[3:6], axis=1))
        if not last:
            ctx = channel_mix(ctx, jnp.stack(mc[3:6])[None])
    return x
```
